```python
import math
import jax
import jax.numpy as jnp
from jax import lax
import numpy as np

D_MODEL = 1024
BATCH = 8
SEQ = 4096
DEPTH = 2

CTX_LEN = 256
GRID_W = 64
HEAD_DIM = 64
H_RET = 4
H_GDN = 4
H_NA = 8
D_RET = H_RET * HEAD_DIM
D_GDN = H_GDN * HEAD_DIM
D_NA = H_NA * HEAD_DIM
D_MIX = D_RET + D_GDN + D_NA
RET_COLS = 4 * D_RET
GDN_COLS = 4 * D_GDN + 4 * H_GDN
NA_COLS = 3 * D_NA
D_IN = RET_COLS + GDN_COLS + NA_COLS
RET_CHUNK = 128
GDN_CHUNK = 64
SHORT_CONV = 5
NA_KH = 8
NA_KW = 16
NA_QBLK = 16
NA_KBLK = 32
N_COLBLK = GRID_W // NA_QBLK
ROPE_BASE = 10000.0
N_FREQ = HEAD_DIM // 4
D_FF = 2816
N_EXPERTS = 8
TOP_K = 2
MOE_BLOCK = 128
EPS = 1e-6
NEG_INF = -1e30

kernel_name = 'hybrid_dit_retention_gdn_natten_moe'


def rms_norm(x, g):
    xf = x.astype(jnp.float32)
    y = xf * lax.rsqrt(jnp.mean(xf * xf, axis=-1, keepdims=True) + EPS)
    return (y * g.astype(jnp.float32)).astype(x.dtype)


def modulate(h, shift, scale):
    return h * (1 + scale) + shift


def head_rms(o):
    return o * lax.rsqrt(jnp.mean(o * o, axis=-1, keepdims=True) + EPS)


def l2_normalise(t):
    return t * lax.rsqrt(jnp.sum(t * t, axis=-1, keepdims=True) + EPS)


def split_heads(t, n_heads):
    return t.reshape(t.shape[:-1] + (n_heads, HEAD_DIM))


def merge_heads(t):
    return t.reshape(t.shape[:-2] + (-1,))


def axial_rope_tables(n_tok):
    t = jnp.arange(n_tok, dtype=jnp.int32)
    row = (t // GRID_W).astype(jnp.float32)
    col = (t % GRID_W).astype(jnp.float32)
    inv_freq = ROPE_BASE ** (-jnp.arange(N_FREQ, dtype=jnp.float32) / N_FREQ)
    ang = jnp.concatenate([row[:, None] * inv_freq, col[:, None] * inv_freq], axis=-1)
    return jnp.cos(ang)[:, None, :], jnp.sin(ang)[:, None, :]


def apply_rope(t, cos, sin):
    t1, t2 = jnp.split(t, 2, axis=-1)
    return jnp.concatenate([t1 * cos - t2 * sin, t1 * sin + t2 * cos], axis=-1)


def centred_depthwise_conv(t, w):
    k_w, ch = w.shape
    return lax.conv_general_dilated(t, w[:, None, :].astype(t.dtype), window_strides=(1,),
                                    padding=[(k_w // 2, k_w // 2)],
                                    dimension_numbers=('NWC', 'WIO', 'NWC'), feature_group_count=ch)


def to_chunks(t, chunk):
    b, l, h = t.shape[:3]
    t = t.reshape((b, l // chunk, chunk, h) + t.shape[3:])
    return jnp.moveaxis(t, (1, 3), (0, 2))


def from_chunks(t):
    t = jnp.moveaxis(t, (0, 2), (1, 3))
    return t.reshape((t.shape[0], t.shape[1] * t.shape[2]) + t.shape[3:])


def retention_scan(q, k, v, log_gamma, s0):
    pos = jnp.arange(RET_CHUNK, dtype=jnp.float32)
    diff = pos[:, None] - pos[None, :]
    intra_decay = jnp.where(diff >= 0, jnp.exp(log_gamma[:, None, None] * jnp.maximum(diff, 0.0)), 0.0)
    q_decay = jnp.exp(log_gamma[:, None] * (pos + 1.0))[..., None]
    k_decay = jnp.exp(log_gamma[:, None] * (RET_CHUNK - 1.0 - pos))[..., None]
    chunk_decay = jnp.exp(log_gamma * RET_CHUNK)[:, None, None]

    def step(s, inp):
        qi, ki, vi = inp
        scores = jnp.einsum('bhid,bhjd->bhij', qi, ki) * intra_decay
        o = jnp.einsum('bhij,bhje->bhie', scores, vi) + jnp.einsum('bhid,bhde->bhie', qi * q_decay, s)
        s = s * chunk_decay + jnp.einsum('bhjd,bhje->bhde', ki * k_decay, vi)
        return s, o

    s_fin, o = lax.scan(step, s0, (to_chunks(q, RET_CHUNK), to_chunks(k, RET_CHUNK), to_chunks(v, RET_CHUNK)))
    return from_chunks(o), s_fin


def gated_delta_scan(q, k, v, g, beta, s0):
    c = GDN_CHUNK
    pos = jnp.arange(c)
    incl = pos[:, None] >= pos[None, :]
    strict = pos[:, None] > pos[None, :]

    def step(s, inp):
        qi, ki, vi, gi, bi = inp
        diff = gi[..., :, None] - gi[..., None, :]
        decay = jnp.where(incl, jnp.exp(jnp.where(incl, diff, 0.0)), 0.0)
        kb = ki * bi[..., None]
        a = jnp.where(strict, jnp.einsum('bhid,bhjd->bhij', kb, ki) * decay, 0.0)
        rhs = jnp.concatenate([vi * bi[..., None], kb * jnp.exp(gi)[..., None]], axis=-1)
        sol = lax.linalg.triangular_solve(a, rhs, left_side=True, lower=True, unit_diagonal=True)
        u, w = jnp.split(sol, 2, axis=-1)
        v_new = u - jnp.einsum('bhcd,bhde->bhce', w, s)
        attn = jnp.einsum('bhid,bhjd->bhij', qi, ki) * decay
        o = (jnp.einsum('bhid,bhde->bhie', qi * jnp.exp(gi)[..., None], s)
             + jnp.einsum('bhij,bhje->bhie', attn, v_new))
        g_last = gi[..., -1:]
        s = (s * jnp.exp(g_last)[..., None]
             + jnp.einsum('bhjd,bhje->bhde', ki * jnp.exp(g_last - gi)[..., None], v_new))
        return s, o

    g_cum = jnp.cumsum(to_chunks(g, c), axis=-1)
    s_fin, o = lax.scan(step, s0, (to_chunks(q, c), to_chunks(k, c), to_chunks(v, c), g_cum, to_chunks(beta, c)))
    return from_chunks(o), s_fin


def bidirectional(scan_fw, scan_bw, lat, ctx, s0):
    rev = lambda ts: tuple(jnp.flip(t, axis=1) for t in ts)
    oc_f, sc_f = scan_fw(ctx, s0)
    oc_b, sc_b = scan_bw(rev(ctx), s0)
    ol_f, _ = scan_fw(lat, sc_f)
    ol_b, _ = scan_bw(rev(lat), sc_b)
    return ol_f + jnp.flip(ol_b, axis=1), oc_f + jnp.flip(oc_b, axis=1)


def retention_group(pl, pc, decay_param, cos, sin, need_ctx):
    log_gamma = jnp.log1p(-jnp.exp2(-decay_param.astype(jnp.float32)))

    def prep(p, rotate):
        q, k, v, gate = [split_heads(t.astype(jnp.float32), H_RET) for t in jnp.split(p, 4, axis=-1)]
        if rotate:
            q, k = apply_rope(q, cos, sin), apply_rope(k, cos, sin)
        return (q, k * HEAD_DIM ** -0.5, v), gate

    lat, gate_l = prep(pl, True)
    ctx, gate_c = prep(pc, False)
    s0 = jnp.zeros((pl.shape[0], H_RET, HEAD_DIM, HEAD_DIM), jnp.float32)
    o_l, o_c = bidirectional(lambda seq, s: retention_scan(*seq, log_gamma[0], s),
                             lambda seq, s: retention_scan(*seq, log_gamma[1], s), lat, ctx, s0)
    finish = lambda o, gate: merge_heads(head_rms(o) * jax.nn.silu(gate))
    out_c = finish(o_c, gate_c).astype(pc.dtype) if need_ctx else None
    return finish(o_l, gate_l).astype(pl.dtype), out_c


def gdn_group(pl, pc, conv_w, a_log, dt_bias, norm_g, need_ctx):
    a_log = a_log.astype(jnp.float32)
    dt_bias = dt_bias.astype(jnp.float32)

    def prep(p):
        p = p.astype(jnp.float32)
        qkv = jax.nn.silu(centred_depthwise_conv(p[..., :3 * D_GDN], conv_w))
        q, k, v = [split_heads(t, H_GDN) for t in jnp.split(qkv, 3, axis=-1)]
        gate = split_heads(p[..., 3 * D_GDN:4 * D_GDN], H_GDN)
        ab = p[..., 4 * D_GDN:].reshape(p.shape[:2] + (2, 2, H_GDN))
        g = -jnp.exp(a_log) * jax.nn.softplus(ab[:, :, 0] + dt_bias)
        beta = jax.nn.sigmoid(ab[:, :, 1])
        return (l2_normalise(q) * HEAD_DIM ** -0.5, l2_normalise(k), v, g, beta), gate

    lat, gate_l = prep(pl)
    ctx, gate_c = prep(pc)
    s0 = jnp.zeros((pl.shape[0], H_GDN, HEAD_DIM, HEAD_DIM), jnp.float32)

    def scan_dir(dr):
        return lambda seq, s: gated_delta_scan(seq[0], seq[1], seq[2], seq[3][:, :, dr], seq[4][:, :, dr], s)

    o_l, o_c = bidirectional(scan_dir(0), scan_dir(1), lat, ctx, s0)
    finish = lambda o, gate: merge_heads(head_rms(o) * norm_g.astype(jnp.float32) * jax.nn.silu(gate))
    out_c = finish(o_c, gate_c).astype(pc.dtype) if need_ctx else None
    return finish(o_l, gate_l).astype(pl.dtype), out_c


def neighbourhood_attention(ql, kl, vl, kc, vc, rpb):
    b, l, h, d = ql.shape
    rows = l // GRID_W
    kh = min(NA_KH, rows)
    scale = d ** -0.5
    qg = ql.reshape(b, rows, GRID_W, h, d)
    kg = kl.reshape(b, rows, GRID_W, h, d)
    vg = vl.reshape(b, rows, GRID_W, h, d)
    kcf = kc.astype(jnp.float32)
    vcf = vc.astype(jnp.float32)
    qcol = jnp.arange(GRID_W, dtype=jnp.int32).reshape(N_COLBLK, NA_QBLK)
    win_start = jnp.clip(qcol - NA_KW // 2, 0, GRID_W - NA_KW)
    band_start = jnp.clip(qcol[:, 0] - NA_KW // 2, 0, GRID_W - NA_KBLK)
    band_cols = band_start[:, None] + jnp.arange(NA_KBLK, dtype=jnp.int32)
    kcol = band_cols[:, None, :]
    col_ok = (kcol >= win_start[..., None]) & (kcol < win_start[..., None] + NA_KW)
    col_idx = jnp.clip(kcol - qcol[..., None] + NA_KW - 1, 0, 2 * NA_KW - 2)
    rpb_c = rpb.astype(jnp.float32)[:, :, col_idx]

    def one_row(r):
        r0 = jnp.clip(r - kh // 2, 0, rows - kh)
        qb = lax.dynamic_index_in_dim(qg, r, axis=1, keepdims=False)
        qb = qb.reshape(b, N_COLBLK, NA_QBLK, h, d).astype(jnp.float32)
        kr = lax.dynamic_slice_in_dim(kg, r0, kh, axis=1)[:, :, band_cols].astype(jnp.float32)
        vr = lax.dynamic_slice_in_dim(vg, r0, kh, axis=1)[:, :, band_cols].astype(jnp.float32)
        s_loc = jnp.einsum('bnqhd,binkhd->bhnqik', qb, kr) * scale
        row_idx = r0 + jnp.arange(kh, dtype=jnp.int32) - r + NA_KH - 1
        bias = jnp.transpose(rpb_c[:, row_idx], (0, 2, 3, 1, 4))
        s_loc = jnp.where(col_ok[:, :, None, :], s_loc + bias, NEG_INF)
        s_ctx = jnp.einsum('bnqhd,bchd->bhnqc', qb, kcf) * scale
        s = jnp.concatenate([s_loc.reshape(b, h, N_COLBLK, NA_QBLK, kh * NA_KBLK), s_ctx], axis=-1)
        p = jax.nn.softmax(s, axis=-1)
        p_loc = p[..., :kh * NA_KBLK].reshape(b, h, N_COLBLK, NA_QBLK, kh, NA_KBLK)
        p_ctx = p[..., kh * NA_KBLK:]
        o = jnp.einsum('bhnqik,binkhd->bnqhd', p_loc, vr) + jnp.einsum('bhnqc,bchd->bnqhd', p_ctx, vcf)
        return o.reshape(b, GRID_W, h, d)

    out = lax.map(one_row, jnp.arange(rows, dtype=jnp.int32))
    return jnp.transpose(out, (1, 0, 2, 3, 4)).reshape(b, l, h, d)


def context_attention(q, k, v):
    s = jnp.einsum('bqhd,bkhd->bhqk', q.astype(jnp.float32), k.astype(jnp.float32)) * HEAD_DIM ** -0.5
    p = jax.nn.softmax(s, axis=-1)
    return jnp.einsum('bhqk,bkhd->bqhd', p, v.astype(jnp.float32))


def na_group(pl, pc, rpb, need_ctx):
    ql, kl, vl = [split_heads(t, H_NA) for t in jnp.split(pl, 3, axis=-1)]
    qc, kc, vc = [split_heads(t, H_NA) for t in jnp.split(pc, 3, axis=-1)]
    out_l = merge_heads(neighbourhood_attention(ql, kl, vl, kc, vc, rpb)).astype(pl.dtype)
    out_c = merge_heads(context_attention(qc, kc, vc)).astype(pc.dtype) if need_ctx else None
    return out_l, out_c


def token_mixer(hl, hc, w_in, w_out, conv_w, ret_decay, gdn_a_log, gdn_dt_bias, gdn_norm_g, na_rpb,
                cos, sin, need_ctx):
    pl = hl @ w_in
    pc = hc @ w_in
    c1, c2 = RET_COLS, RET_COLS + GDN_COLS
    r_l, r_c = retention_group(pl[..., :c1], pc[..., :c1], ret_decay, cos, sin, need_ctx)
    g_l, g_c = gdn_group(pl[..., c1:c2], pc[..., c1:c2], conv_w, gdn_a_log, gdn_dt_bias, gdn_norm_g, need_ctx)
    n_l, n_c = na_group(pl[..., c2:], pc[..., c2:], na_rpb, need_ctx)
    yl = jnp.concatenate([r_l, g_l, n_l], axis=-1) @ w_out
    yc = jnp.concatenate([r_c, g_c, n_c], axis=-1) @ w_out if need_ctx else None
    return yl, yc


def swiglu(h, w1, w3, w2):
    return (jax.nn.silu(h @ w1) * (h @ w3)) @ w2


def moe_swiglu(h, w_router, w1, w3, w2):
    b, l, d = h.shape
    n_tok = b * l
    hf = h.reshape(n_tok, d)
    logits = jnp.dot(hf, w_router).astype(jnp.float32)
    top_logit, top_e = lax.top_k(logits, TOP_K)
    gates = jax.nn.softmax(top_logit, axis=-1).astype(h.dtype)
    e_flat = top_e.reshape(-1).astype(jnp.int32)
    tok_flat = jnp.repeat(jnp.arange(n_tok, dtype=jnp.int32), TOP_K)
    w_flat = gates.reshape(-1)
    order = jnp.argsort(e_flat)
    e_sorted, tok_sorted, w_sorted = e_flat[order], tok_flat[order], w_flat[order]
    counts = jax.ops.segment_sum(jnp.ones_like(e_flat), e_flat, num_segments=N_EXPERTS)
    starts = jnp.cumsum(counts) - counts
    padded = (counts + MOE_BLOCK - 1) // MOE_BLOCK * MOE_BLOCK
    pad_ends = jnp.cumsum(padded)
    pad_starts = pad_ends - padded
    dest = pad_starts[e_sorted] + (jnp.arange(n_tok * TOP_K, dtype=jnp.int32) - starts[e_sorted])
    n_blocks = (n_tok * TOP_K + MOE_BLOCK - 1) // MOE_BLOCK + N_EXPERTS
    n_rows = n_blocks * MOE_BLOCK
    buf_tok = jnp.zeros((n_rows,), jnp.int32).at[dest].set(tok_sorted)
    buf_w = jnp.zeros((n_rows,), h.dtype).at[dest].set(w_sorted)
    block_start = jnp.arange(n_blocks, dtype=jnp.int32) * MOE_BLOCK
    block_e = jnp.minimum(jnp.searchsorted(pad_ends, block_start, side='right'), N_EXPERTS - 1)

    def expert_block(args):
        idx, e, wt = args
        return swiglu(hf[idx], w1[e], w3[e], w2[e]) * wt[:, None]

    y = lax.map(expert_block, (buf_tok.reshape(n_blocks, MOE_BLOCK), block_e, buf_w.reshape(n_blocks, MOE_BLOCK)))
    out = jnp.zeros_like(hf).at[buf_tok].add(y.reshape(n_rows, d))
    return out.reshape(b, l, d)


def setup_inputs(seed: int = 0) -> dict:
    key = jax.random.key(seed)
    ks = jax.random.split(key, 24)
    f32 = jnp.float32
    n_dense = (DEPTH + 1) // 2
    n_moe = DEPTH // 2

    def nrm(k, shape, scale):
        return jax.random.normal(k, shape, f32) * scale

    dt = jnp.exp(jax.random.uniform(ks[13], (DEPTH, 2, H_GDN), f32, math.log(1e-3), math.log(1e-1)))
    return {
        'x': nrm(ks[0], (BATCH, SEQ, D_MODEL), 1.0),
        'c': nrm(ks[1], (BATCH, D_MODEL), 1.0),
        'ctx': nrm(ks[2], (BATCH, CTX_LEN, D_MODEL), 1.0),
        'c_ctx': nrm(ks[3], (D_MODEL,), 1.0),
        'ada_w': nrm(ks[4], (DEPTH, D_MODEL, 6 * D_MODEL), 0.5 * D_MODEL ** -0.5),
        'ada_b': nrm(ks[5], (DEPTH, 6 * D_MODEL), 0.02),
        'norm1_g': 1.0 + nrm(ks[6], (DEPTH, D_MODEL), 0.05),
        'norm2_g': 1.0 + nrm(ks[7], (DEPTH, D_MODEL), 0.05),
        'w_in': nrm(ks[8], (DEPTH, D_MODEL, D_IN), D_MODEL ** -0.5),
        'w_out': nrm(ks[9], (DEPTH, D_MIX, D_MODEL), D_MIX ** -0.5),
        'conv_w': nrm(ks[10], (DEPTH, SHORT_CONV, 3 * D_GDN), SHORT_CONV ** -0.5),
        'ret_decay': 5.0 + jnp.arange(H_RET, dtype=f32) + nrm(ks[11], (DEPTH, 2, H_RET), 0.1),
        'gdn_a_log': jnp.log(jax.random.uniform(ks[12], (DEPTH, 2, H_GDN), f32, 1.0, 16.0)),
        'gdn_dt_bias': dt + jnp.log(-jnp.expm1(-dt)),
        'gdn_norm_g': 1.0 + nrm(ks[14], (DEPTH, HEAD_DIM), 0.05),
        'na_rpb': nrm(ks[15], (DEPTH, H_NA, 2 * NA_KH - 1, 2 * NA_KW - 1), 0.1),
        'ffn_w1': nrm(ks[16], (n_dense, D_MODEL, D_FF), D_MODEL ** -0.5),
        'ffn_w3': nrm(ks[17], (n_dense, D_MODEL, D_FF), D_MODEL ** -0.5),
        'ffn_w2': nrm(ks[18], (n_dense, D_FF, D_MODEL), D_FF ** -0.5),
        'moe_router': nrm(ks[19], (n_moe, D_MODEL, N_EXPERTS), D_MODEL ** -0.5),
        'moe_w1': nrm(ks[20], (n_moe, N_EXPERTS, D_MODEL, D_FF), D_MODEL ** -0.5),
        'moe_w3': nrm(ks[21], (n_moe, N_EXPERTS, D_MODEL, D_FF), D_MODEL ** -0.5),
        'moe_w2': nrm(ks[22], (n_moe, N_EXPERTS, D_FF, D_MODEL), D_FF ** -0.5),
        'final_g': 1.0 + nrm(ks[23], (D_MODEL,), 0.05),
    }


def reference(x, c, ctx, c_ctx, ada_w, ada_b, norm1_g, norm2_g, w_in, w_out, conv_w, ret_decay,
              gdn_a_log, gdn_dt_bias, gdn_norm_g, na_rpb, ffn_w1, ffn_w3, ffn_w2,
              moe_router, moe_w1, moe_w3, moe_w2, final_g):
    cos, sin = axial_rope_tables(x.shape[1])
    y = ctx
    sc = jax.nn.silu(c)
    scc = jax.nn.silu(c_ctx)
    for l in range(DEPTH):
        need_ctx = l < DEPTH - 1
        ml = [m[:, None, :] for m in jnp.split(sc @ ada_w[l] + ada_b[l], 6, axis=-1)]
        mc = jnp.split(scc @ ada_w[l] + ada_b[l], 6, axis=-1)
        hl = modulate(rms_norm(x, norm1_g[l]), ml[0], ml[1])
        hc = modulate(rms_norm(y, norm1_g[l]), mc[0], mc[1])
        ol, oc = token_mixer(hl, hc, w_in[l], w_out[l], conv_w[l], ret_decay[l], gdn_a_log[l], gdn_dt_bias[l],
                             gdn_norm_g[l], na_rpb[l], cos, sin, need_ctx)
        x = x + ml[2] * ol
        if need_ctx:
            y = y + mc[2] * oc
        j = l // 2
        if l % 2 == 0:
            ffn = lambda h: swiglu(h, ffn_w1[j], ffn_w3[j], ffn_w2[j])
        else:
            ffn = lambda h: moe_swiglu(h, moe_router[j], moe_w1[j], moe_w3[j], moe_w2[j])
        x = x + ml[5] * ffn(modulate(rms_norm(x, norm2_g[l]), ml[3], ml[4]))
        if need_ctx:
            y = y + mc[5] * ffn(modulate(rms_norm(y, norm2_g[l]), mc[3], mc[4]))
    return rms_norm(x, final_g)
```

```python
import functools

import numpy as np
import jax
import jax.numpy as jnp
from jax import lax
from jax.experimental import pallas as pl
from jax.experimental.pallas import tpu as pltpu

F32 = jnp.float32
BF16 = jnp.bfloat16
HIGHEST = lax.Precision.HIGHEST

LANES = 128
HEAD_DIM = 64
PAIR = 2 * HEAD_DIM
GRID_W = 64
H_RET, H_GDN, H_NA = 4, 4, 8
D_RET, D_GDN, D_NA = H_RET * HEAD_DIM, H_GDN * HEAD_DIM, H_NA * HEAD_DIM
RET_CHUNK = 128
GDN_CHUNK = 64
SHORT_CONV = 5
NA_KH, NA_KW = 8, 16
NA_QROWS = 4
NA_KROWS = NA_QROWS + NA_KH
N_FREQ = HEAD_DIM // 4
ROPE_BASE = 10000.0
N_EXPERTS = 8
TOP_K = 2
MOE_ROWS = 512
FF_CHUNK = 256
EPS = 1e-6
NEG_INF = -1e30
VMEM_LIMIT = 56 * 1024 * 1024

C_RET = 0
C_GDN = C_RET + 4 * D_RET
C_AB = C_GDN + 4 * D_GDN
C_NA = C_AB + 2 * LANES
C_END = C_NA + 3 * D_NA


def _cparams(sem, vmem=None):
    return pltpu.CompilerParams(dimension_semantics=sem, vmem_limit_bytes=vmem)


def _silu(x):
    return x * jax.nn.sigmoid(x)


def _dot(a, b):
    return jnp.dot(a, b, preferred_element_type=F32)


def _dot_nt(a, b):
    return lax.dot_general(a, b, (((1,), (1,)), ((), ())), preferred_element_type=F32)


def _dot_tn(a, b):
    return lax.dot_general(a, b, (((0,), (0,)), ((), ())), preferred_element_type=F32)


def _dot_hi(a, b):
    return jnp.dot(a, b, preferred_element_type=F32, precision=HIGHEST)


def _ada_body(c_ref, w_ref, b_ref, o_ref):
    s = _silu(c_ref[...])
    o_ref[0] = _dot_hi(s, w_ref[0]) + b_ref[0]


def _ada_vectors(c_all, ada_w, ada_b):
    depth, d, d6 = ada_w.shape
    rows = c_all.shape[0]
    tn = 1024
    return pl.pallas_call(
        _ada_body,
        grid=(depth, d6 // tn),
        in_specs=[pl.BlockSpec((rows, d), lambda l, j: (0, 0)),
                  pl.BlockSpec((1, d, tn), lambda l, j: (l, 0, j)),
                  pl.BlockSpec((1, 1, tn), lambda l, j: (l, 0, j))],
        out_specs=pl.BlockSpec((1, rows, tn), lambda l, j: (l, 0, j)),
        out_shape=jax.ShapeDtypeStruct((depth, rows, d6), F32),
        compiler_params=_cparams(("arbitrary", "arbitrary")),
        name="ada_vectors",
    )(c_all, ada_w, ada_b.reshape(depth, 1, d6))


def _norm_mod(x, g, shift, scale):
    ms = jnp.mean(x * x, axis=-1, keepdims=True)
    return (x * lax.rsqrt(ms + EPS) * g) * (1.0 + scale) + shift


def _rope_slab(t, cosf, sins):
    lane = lax.broadcasted_iota(jnp.int32, t.shape, 1)
    first = (lane % HEAD_DIM) < (HEAD_DIM // 2)
    partner = jnp.where(first, pltpu.roll(t, LANES - HEAD_DIM // 2, 1), pltpu.roll(t, HEAD_DIM // 2, 1))
    return t * cosf + partner * sins


def _inproj_body(x_ref, mod_ref, g_ref, cos_ref, sin_ref, w_ref, ret_ref, gdn_ref, ab_ref, na_ref, *, rope):
    h = _norm_mod(x_ref[0], g_ref[...], mod_ref[0, 0:1, :], mod_ref[0, 1:2, :]).astype(BF16)
    qk = _dot(h, w_ref[:, C_RET:C_RET + 2 * D_RET])
    slabs = []
    for s in range(2 * D_RET // LANES):
        t = qk[:, s * LANES:(s + 1) * LANES]
        if rope:
            t = _rope_slab(t, cos_ref[...], sin_ref[...])
        if s >= D_RET // LANES:
            t = t * HEAD_DIM ** -0.5
        slabs.append(t)
    ret_ref[0, :, 0:2 * D_RET] = jnp.concatenate(slabs, axis=1).astype(BF16)
    ret_ref[0, :, 2 * D_RET:] = _dot(h, w_ref[:, C_RET + 2 * D_RET:C_GDN]).astype(BF16)
    for j in range(2):
        gdn_ref[0, :, 512 * j:512 * (j + 1)] = _dot(h, w_ref[:, C_GDN + 512 * j:C_GDN + 512 * (j + 1)]).astype(BF16)
    ab_ref[0] = _dot(h, w_ref[:, C_AB:C_NA])
    for j in range(3):
        na_ref[0, :, 512 * j:512 * (j + 1)] = _dot(h, w_ref[:, C_NA + 512 * j:C_NA + 512 * (j + 1)]).astype(BF16)


def _in_projection(x, mod, g, cosf, sins, w, rope):
    b, l, d = x.shape
    tm = min(512, l)
    body = functools.partial(_inproj_body, rope=rope)
    return pl.pallas_call(
        body,
        grid=(b, l // tm),
        in_specs=[pl.BlockSpec((1, tm, d), lambda i, j: (i, j, 0)),
                  pl.BlockSpec((1, 6, d), lambda i, j: (i, 0, 0)),
                  pl.BlockSpec((1, d), lambda i, j: (0, 0)),
                  pl.BlockSpec((tm, LANES), lambda i, j: (j, 0)),
                  pl.BlockSpec((tm, LANES), lambda i, j: (j, 0)),
                  pl.BlockSpec((d, C_END), lambda i, j: (0, 0))],
        out_specs=[pl.BlockSpec((1, tm, 4 * D_RET), lambda i, j: (i, j, 0)),
                   pl.BlockSpec((1, tm, 4 * D_GDN), lambda i, j: (i, j, 0)),
                   pl.BlockSpec((1, tm, 2 * LANES), lambda i, j: (i, j, 0)),
                   pl.BlockSpec((1, tm, 3 * D_NA), lambda i, j: (i, j, 0))],
        out_shape=[jax.ShapeDtypeStruct((b, l, 4 * D_RET), BF16),
                   jax.ShapeDtypeStruct((b, l, 4 * D_GDN), BF16),
                   jax.ShapeDtypeStruct((b, l, 2 * LANES), F32),
                   jax.ShapeDtypeStruct((b, l, 3 * D_NA), BF16)],
        compiler_params=_cparams(("arbitrary", "arbitrary"), VMEM_LIMIT),
        name="in_projection",
    )(x, mod, g, cosf, sins, w)


def _pack_w_in(w_in, conv_cols=None):
    d = w_in.shape[0]
    c1 = 4 * D_RET
    c2 = c1 + 4 * D_GDN
    ab = w_in[:, c2:c2 + 4 * H_GDN]
    ab = ab.reshape(d, 2, 2, H_GDN // 2, 2)
    ab = jnp.transpose(ab, (0, 3, 1, 2, 4)).reshape(d, 2, 8)
    ab = jnp.pad(ab, ((0, 0), (0, 0), (0, LANES - 8))).reshape(d, 2 * LANES)
    return jnp.concatenate([w_in[:, :c2], ab, w_in[:, c2 + 4 * H_GDN:]], axis=1).astype(BF16)


def _head_masks(shape):
    lane = lax.broadcasted_iota(jnp.int32, shape, len(shape) - 1)
    return lane < HEAD_DIM, lane >= HEAD_DIM


def _per_head(lo, hi, shape):
    m0, _ = _head_masks(shape)
    return jnp.where(m0, lo, hi)


def _head_sumsq(o):
    m0, m1 = _head_masks(o.shape)
    sq = o * o
    s0 = jnp.sum(jnp.where(m0, sq, 0.0), axis=-1, keepdims=True)
    s1 = jnp.sum(jnp.where(m1, sq, 0.0), axis=-1, keepdims=True)
    return jnp.where(m0, s0, s1)


def _ret_body(lg_ref, ql, kl, vl, gl, qc, kc, vc, gc, r_ref, rc_ref, sb_ref):
    c = RET_CHUNK
    p = pl.program_id(1)
    lane_shape = (c, PAIR)
    pos = lax.broadcasted_iota(jnp.int32, lane_shape, 0).astype(F32)
    lgf = _per_head(lg_ref[0, 2 * p], lg_ref[0, 2 * p + 1], lane_shape)
    lgb = _per_head(lg_ref[1, 2 * p], lg_ref[1, 2 * p + 1], lane_shape)
    qdf = jnp.exp(lgf * (pos + 1.0))
    kdf = jnp.exp(lgf * (c - 1.0 - pos))
    qdb = jnp.exp(lgb * (c - pos))
    kdb = jnp.exp(lgb * pos)
    cdf = jnp.exp(lgf[0:1] * c)
    cdb = jnp.exp(lgb[0:1] * c)
    ii = lax.broadcasted_iota(jnp.int32, (c, c), 0)
    jj = lax.broadcasted_iota(jnp.int32, (c, c), 1)
    diff = (ii - jj).astype(F32)
    dmats = [jnp.where(diff > 0, jnp.exp(lg_ref[0, 2 * p + hh] * diff),
                       jnp.where(diff < 0, jnp.exp(-lg_ref[1, 2 * p + hh] * diff), 2.0)) for hh in range(2)]
    m0, m1 = _head_masks(lane_shape)
    masks = (m0, m1)
    bi = lax.broadcasted_iota(jnp.int32, (PAIR, PAIR), 0) // HEAD_DIM
    bj = lax.broadcasted_iota(jnp.int32, (PAIR, PAIR), 1) // HEAD_DIM
    bd = bi == bj

    def sweep(q_ref, k_ref, v_ref, g_ref, o_ref, n, sf0, sb0):
        def bstep(t, sb):
            ci = n - 1 - t
            sl = pl.ds(pl.multiple_of(ci * c, c), c)
            sb_ref[ci] = sb
            kd = (k_ref[0, sl, :].astype(F32) * kdb).astype(BF16)
            return sb * cdb + jnp.where(bd, _dot_tn(kd, v_ref[0, sl, :]), 0.0)

        sb_fin = lax.fori_loop(0, n, bstep, sb0)

        def fstep(ci, sf):
            sl = pl.ds(pl.multiple_of(ci * c, c), c)
            q = q_ref[0, sl, :]
            k = k_ref[0, sl, :]
            v = v_ref[0, sl, :]
            qf = q.astype(F32)
            o = _dot((qf * qdf).astype(BF16), sf.astype(BF16)) + _dot((qf * qdb).astype(BF16), sb_ref[ci].astype(BF16))
            for hh in range(2):
                qm = jnp.where(masks[hh], q, jnp.zeros_like(q))
                pm = (_dot_nt(qm, k) * dmats[hh]).astype(BF16)
                o = o + jnp.where(masks[hh], _dot(pm, v), 0.0)
            on = o * lax.rsqrt(_head_sumsq(o) * (1.0 / HEAD_DIM) + EPS)
            o_ref[0, sl, :] = (on * _silu(g_ref[0, sl, :].astype(F32))).astype(o_ref.dtype)
            kd = (k.astype(F32) * kdf).astype(BF16)
            return sf * cdf + jnp.where(bd, _dot_tn(kd, v), 0.0)

        sf_fin = lax.fori_loop(0, n, fstep, sf0)
        return sf_fin, sb_fin

    z = jnp.zeros((PAIR, PAIR), F32)
    sfc, sbc = sweep(qc, kc, vc, gc, rc_ref, qc.shape[1] // c, z, z)
    sweep(ql, kl, vl, gl, r_ref, ql.shape[1] // c, sfc, sbc)


def _retention(ret_l, ret_c, log_gamma):
    b, l, _ = ret_l.shape
    lc = ret_c.shape[1]
    npair = D_RET // PAIR

    def col(k):
        return lambda i, p: (i, 0, k * npair + p)

    lat = [pl.BlockSpec((1, l, PAIR), col(k)) for k in range(4)]
    ctx = [pl.BlockSpec((1, lc, PAIR), col(k)) for k in range(4)]
    return pl.pallas_call(
        _ret_body,
        grid=(b, npair),
        in_specs=[pl.BlockSpec(memory_space=pltpu.SMEM)] + lat + ctx,
        out_specs=[pl.BlockSpec((1, l, PAIR), lambda i, p: (i, 0, p)),
                   pl.BlockSpec((1, lc, PAIR), lambda i, p: (i, 0, p))],
        out_shape=[jax.ShapeDtypeStruct((b, l, D_RET), BF16), jax.ShapeDtypeStruct((b, lc, D_RET), BF16)],
        scratch_shapes=[pltpu.VMEM((max(l, lc) // RET_CHUNK, PAIR, PAIR), F32)],
        compiler_params=_cparams(("arbitrary", "arbitrary"), VMEM_LIMIT),
        name="retention",
    )(log_gamma, ret_l, ret_l, ret_l, ret_l, ret_c, ret_c, ret_c, ret_c)


def _stack_heads(t):
    m0, m1 = _head_masks(t.shape)
    z = jnp.zeros_like(t)
    return jnp.concatenate([jnp.where(m0, t, z), jnp.where(m1, t, z)], axis=0)


def _gdn_prep(x_ref, cw_ref, o_ref, *, l2, scale):
    x = x_ref[0].astype(F32)
    n = x.shape[0]
    row = lax.broadcasted_iota(jnp.int32, x.shape, 0)
    acc = x * cw_ref[SHORT_CONV // 2:SHORT_CONV // 2 + 1, :]
    for j in range(SHORT_CONV):
        s = j - SHORT_CONV // 2
        if s == 0:
            continue
        sh = pltpu.roll(x, (-s) % n, 0)
        ok = (row + s >= 0) & (row + s < n)
        acc = acc + jnp.where(ok, sh, 0.0) * cw_ref[j:j + 1, :]
    y = _silu(acc)
    if l2:
        y = y * lax.rsqrt(_head_sumsq(y) + EPS)
    if scale != 1.0:
        y = y * scale
    o_ref[...] = y.astype(o_ref.dtype)


def _gdn_body(qx, kx, vx, gx, ab, abt, qxc, kxc, vxc, gxc, abc, abtc, cwq, cwk, cwv, cst, cstt, ng,
              o_ref, oc_ref, qn, kn, vn, qnc, knc, vnc, of, ob):
    c = GDN_CHUNK
    c2 = 2 * c
    nea, dtb = cst[0, 0:1, :], cst[0, 1:2, :]
    neat, dtbt = cstt[0, 0], cstt[0, 1]
    ii = lax.broadcasted_iota(jnp.int32, (c2, c2), 0)
    jj = lax.broadcasted_iota(jnp.int32, (c2, c2), 1)
    same = (ii // c) == (jj // c)
    eye = (ii == jj).astype(F32)
    ti = lax.broadcasted_iota(jnp.int32, (c, c), 0)
    tj = lax.broadcasted_iota(jnp.int32, (c, c), 1)
    lower = (ti >= tj).astype(F32)
    upper = (ti <= tj).astype(F32)
    dirs = ((same & (ii >= jj), same & (ii > jj), lower, upper, c - 1),
            (same & (ii <= jj), same & (ii < jj), upper, lower, 0))

    def chunk(d, q_s, k_s, v_s, ab_ref, abt_ref, o_s, ci, st):
        incl, strict, tri_col, tri_row, last = dirs[d]
        sl = pl.ds(pl.multiple_of(ci * c, c), c)
        abv = ab_ref[0, sl, :]
        gall = nea * jax.nn.softplus(abv + dtb)
        ball = jax.nn.sigmoid(abv)
        shape = (c, PAIR)
        gl = _per_head(gall[:, 2 * d:2 * d + 1], gall[:, 2 * d + 1:2 * d + 2], shape)
        bl = _per_head(ball[:, 4 + 2 * d:5 + 2 * d], ball[:, 5 + 2 * d:6 + 2 * d], shape)
        gcum = _dot_hi(tri_col, gl)
        gt = neat[:, 0:c] * jax.nn.softplus(abt_ref[0, 0, ci] + dtbt[:, 0:c])
        gtc = _dot_hi(gt, tri_row)
        grow = jnp.concatenate([gtc[2 * d:2 * d + 1, :], gtc[2 * d + 1:2 * d + 2, :]], axis=1)
        gcol = jnp.concatenate([gcum[:, 0:1], gcum[:, HEAD_DIM:HEAD_DIM + 1]], axis=0)
        dec = jnp.where(incl, jnp.exp(jnp.where(incl, gcol - grow, 0.0)), 0.0)
        dec = jnp.where(ii == jj, 1.0, dec)
        q = q_s[sl, :]
        k = k_s[sl, :]
        v = v_s[sl, :]
        kf = k.astype(F32)
        eg = jnp.exp(gcum)
        kb = kf * bl
        k_st = _stack_heads(k)
        a = jnp.where(strict, _dot_nt(_stack_heads(kb.astype(BF16)), k_st) * dec, 0.0)
        attn = (_dot_nt(_stack_heads(q), k_st) * dec).astype(BF16)
        x = -a
        t = eye + x
        for _ in range(5):
            xb = x.astype(BF16)
            x = _dot(xb, xb)
            t = t + _dot(t.astype(BF16), x.astype(BF16))
        rhs = jnp.concatenate([_stack_heads((v.astype(F32) * bl).astype(BF16)),
                               _stack_heads((kb * eg).astype(BF16))], axis=1)
        sol = _dot(t.astype(BF16), rhs)
        stb = st.astype(BF16)
        v_new = (sol[:, :PAIR] - _dot(sol[:, PAIR:].astype(BF16), stb)).astype(BF16)
        o_st = _dot(_stack_heads((q.astype(F32) * eg).astype(BF16)), stb) + _dot(attn, v_new)
        o_s[sl, :] = o_st[:c] + o_st[c:]
        glast = gcum[last:last + 1, :]
        kdec = _stack_heads((kf * jnp.exp(glast - gcum)).astype(BF16))
        return st * jnp.exp(glast) + _dot_tn(kdec, v_new)

    def run(q_s, k_s, v_s, ab_ref, abt_ref, n, sf0, sb0):
        def step(t, carry):
            sf, sb = carry
            sf = chunk(0, q_s, k_s, v_s, ab_ref, abt_ref, of, t, sf)
            sb = chunk(1, q_s, k_s, v_s, ab_ref, abt_ref, ob, n - 1 - t, sb)
            return sf, sb
        return lax.fori_loop(0, n, step, (sf0, sb0))

    def finish(gate_ref, out_ref, n):
        o = of[0:n, :] + ob[0:n, :]
        on = o * lax.rsqrt(_head_sumsq(o) * (1.0 / HEAD_DIM) + EPS)
        out_ref[0] = (on * ng[...] * _silu(gate_ref[0].astype(F32))).astype(out_ref.dtype)

    lc = qxc.shape[1]
    l = qx.shape[1]
    _gdn_prep(qxc, cwq, qnc, l2=True, scale=HEAD_DIM ** -0.5)
    _gdn_prep(kxc, cwk, knc, l2=True, scale=1.0)
    _gdn_prep(vxc, cwv, vnc, l2=False, scale=1.0)
    z = jnp.zeros((PAIR, PAIR), F32)
    sfc, sbc = run(qnc, knc, vnc, abc, abtc, lc // c, z, z)
    finish(gxc, oc_ref, lc)
    _gdn_prep(qx, cwq, qn, l2=True, scale=HEAD_DIM ** -0.5)
    _gdn_prep(kx, cwk, kn, l2=True, scale=1.0)
    _gdn_prep(vx, cwv, vn, l2=False, scale=1.0)
    run(qn, kn, vn, ab, abt, l // c, sfc, sbc)
    finish(gx, o_ref, l)


def _gdn(gdn_l, ab_l, gdn_c, ab_c, conv_w, a_log, dt_bias, norm_g):
    b, l, _ = gdn_l.shape
    lc = gdn_c.shape[1]
    npair = D_GDN // PAIR

    def col(k):
        return lambda i, p: (i, 0, k * npair + p)

    def tr(ab):
        x = jnp.stack([ab[:, :, 0:8], ab[:, :, LANES:LANES + 8]], axis=1)
        x = x.reshape(x.shape[0], npair, x.shape[2] // GDN_CHUNK, GDN_CHUNK, 8)
        return jnp.transpose(x, (0, 1, 2, 4, 3))

    nea = -jnp.exp(a_log.astype(F32)).reshape(2, npair, 2)
    dtb = dt_bias.astype(F32).reshape(2, npair, 2)
    rows = jnp.stack([jnp.transpose(nea, (1, 0, 2)).reshape(npair, 4), jnp.transpose(dtb, (1, 0, 2)).reshape(npair, 4)], axis=1)
    cst = jnp.pad(rows, ((0, 0), (0, 6), (0, LANES - 4)))
    cstt = jnp.broadcast_to(jnp.pad(rows, ((0, 0), (0, 0), (0, 4)))[..., None], (npair, 2, 8, LANES))
    cw = jnp.pad(conv_w.astype(F32), ((0, 8 - SHORT_CONV), (0, 0)))
    ng = jnp.tile(norm_g.astype(F32), 2).reshape(1, PAIR)
    lat = [pl.BlockSpec((1, l, PAIR), col(k)) for k in range(4)]
    ctx = [pl.BlockSpec((1, lc, PAIR), col(k)) for k in range(4)]
    return pl.pallas_call(
        _gdn_body,
        grid=(b, npair),
        in_specs=lat + [pl.BlockSpec((1, l, LANES), lambda i, p: (i, 0, p)),
                        pl.BlockSpec((1, 1, l // GDN_CHUNK, 8, GDN_CHUNK), lambda i, p: (i, p, 0, 0, 0))]
        + ctx + [pl.BlockSpec((1, lc, LANES), lambda i, p: (i, 0, p)),
                 pl.BlockSpec((1, 1, lc // GDN_CHUNK, 8, GDN_CHUNK), lambda i, p: (i, p, 0, 0, 0))]
        + [pl.BlockSpec((8, PAIR), lambda i, p, k=k: (0, k * npair + p)) for k in range(3)]
        + [pl.BlockSpec((1, 8, LANES), lambda i, p: (p, 0, 0)),
           pl.BlockSpec((1, 2, 8, LANES), lambda i, p: (p, 0, 0, 0)),
           pl.BlockSpec((1, PAIR), lambda i, p: (0, 0))],
        out_specs=[pl.BlockSpec((1, l, PAIR), lambda i, p: (i, 0, p)),
                   pl.BlockSpec((1, lc, PAIR), lambda i, p: (i, 0, p))],
        out_shape=[jax.ShapeDtypeStruct((b, l, D_GDN), BF16), jax.ShapeDtypeStruct((b, lc, D_GDN), BF16)],
        scratch_shapes=[pltpu.VMEM((l, PAIR), BF16)] * 3 + [pltpu.VMEM((lc, PAIR), BF16)] * 3
        + [pltpu.VMEM((max(l, lc), PAIR), F32)] * 2,
        compiler_params=_cparams(("arbitrary", "arbitrary"), VMEM_LIMIT),
        name="gated_deltanet",
    )(gdn_l, gdn_l, gdn_l, gdn_l, ab_l, tr(ab_l), gdn_c, gdn_c, gdn_c, gdn_c, ab_c, tr(ab_c),
      cw, cw, cw, cst, cstt, ng)


def _na_bias_tiles(rpb, rows):
    w = GRID_W
    ext = jnp.pad(rpb.astype(F32), ((0, 0), (0, 0), (w - NA_KW, w - NA_KW)))
    tb = jnp.stack([ext[:, :, w - 1 - cq:2 * w - 1 - cq] for cq in range(w)], axis=2)
    cq = np.arange(w)[:, None]
    ck = np.arange(w)[None, :]
    ws = np.clip(cq - NA_KW // 2, 0, w - NA_KW)
    col_ok = (ck >= ws) & (ck < ws + NA_KW)
    tb = jnp.where(jnp.asarray(col_ok)[None, None], tb, NEG_INF)
    tb = jnp.concatenate([tb, jnp.full((tb.shape[0], 1, w, w), NEG_INF, F32)], axis=1)
    nt = rows // NA_QROWS
    idx = np.zeros((3, NA_QROWS, NA_KROWS), np.int32)
    for cls, t in enumerate((0, 1, nt - 1)):
        ks = int(np.clip(NA_QROWS * t - NA_KH // 2, 0, rows - NA_KROWS))
        for rl in range(NA_QROWS):
            r = NA_QROWS * t + rl
            r0 = int(np.clip(r - NA_KH // 2, 0, rows - NA_KH))
            for j in range(NA_KROWS):
                kr = ks + j
                idx[cls, rl, j] = kr - r + NA_KH - 1 if r0 <= kr < r0 + NA_KH else 2 * NA_KH - 1
    tiles = tb[:, jnp.asarray(idx)]
    tiles = jnp.transpose(tiles, (1, 0, 2, 4, 3, 5))
    return tiles.reshape(3, tb.shape[0], NA_QROWS * w, NA_KROWS * w)


def _softmax_pv(s_parts, v_parts):
    m = functools.reduce(jnp.maximum, [jnp.max(s, axis=-1, keepdims=True) for s in s_parts])
    ps = [jnp.exp(s - m) for s in s_parts]
    den = functools.reduce(lambda x, y: x + y, [jnp.sum(p, axis=-1, keepdims=True) for p in ps])
    o = functools.reduce(lambda x, y: x + y, [_dot(p.astype(BF16), v) for p, v in zip(ps, v_parts)])
    return o / den


def _na_body(q_ref, k_ref, v_ref, qc_ref, kc_ref, vc_ref, bias_ref, o_ref, oc_ref, *, rows):
    w = GRID_W
    tq = NA_QROWS * w
    tk = NA_KROWS * w
    nt = rows // NA_QROWS
    scale = HEAD_DIM ** -0.5
    kc = kc_ref[0]
    vc = vc_ref[0]
    m0, m1 = _head_masks((tq, PAIR))
    masks = (m0, m1)

    def tile(t, _):
        cls = jnp.where(t > 0, 1, 0) + jnp.where(t == nt - 1, 1, 0)
        ks = jnp.clip(NA_QROWS * t - NA_KH // 2, 0, rows - NA_KROWS) * w
        ksl = pl.ds(pl.multiple_of(ks, w), tk)
        qsl = pl.ds(pl.multiple_of(t * tq, tq), tq)
        q = q_ref[0, qsl, :] * scale
        kb = k_ref[0, ksl, :]
        vb = v_ref[0, ksl, :]
        acc = jnp.zeros((tq, PAIR), F32)
        for hh in range(2):
            qm = jnp.where(masks[hh], q, jnp.zeros_like(q))
            s_loc = _dot_nt(qm, kb) + bias_ref[cls, hh]
            s_ctx = _dot_nt(qm, kc)
            acc = acc + jnp.where(masks[hh], _softmax_pv([s_loc, s_ctx], [vb, vc]), 0.0)
        o_ref[0, qsl, :] = acc.astype(o_ref.dtype)
        return 0

    lax.fori_loop(0, nt, tile, 0)
    qc = qc_ref[0] * scale
    mc0, mc1 = _head_masks(qc.shape)
    accc = jnp.zeros(qc.shape, F32)
    for hh, mk in enumerate((mc0, mc1)):
        qm = jnp.where(mk, qc, jnp.zeros_like(qc))
        accc = accc + jnp.where(mk, _softmax_pv([_dot_nt(qm, kc)], [vc]), 0.0)
    oc_ref[0] = accc.astype(oc_ref.dtype)


def _neighbourhood_attention(na_l, na_c, rpb):
    b, l, _ = na_l.shape
    lc = na_c.shape[1]
    rows = l // GRID_W
    npair = D_NA // PAIR
    bias = _na_bias_tiles(rpb, rows)

    def col(k):
        return lambda p, i: (i, 0, k * npair + p)

    lat = [pl.BlockSpec((1, l, PAIR), col(k)) for k in range(3)]
    ctx = [pl.BlockSpec((1, lc, PAIR), col(k)) for k in range(3)]
    return pl.pallas_call(
        functools.partial(_na_body, rows=rows),
        grid=(npair, b),
        in_specs=lat + ctx + [pl.BlockSpec((3, 2, NA_QROWS * GRID_W, NA_KROWS * GRID_W), lambda p, i: (0, p, 0, 0))],
        out_specs=[pl.BlockSpec((1, l, PAIR), lambda p, i: (i, 0, p)),
                   pl.BlockSpec((1, lc, PAIR), lambda p, i: (i, 0, p))],
        out_shape=[jax.ShapeDtypeStruct((b, l, D_NA), BF16), jax.ShapeDtypeStruct((b, lc, D_NA), BF16)],
        compiler_params=_cparams(("arbitrary", "arbitrary"), VMEM_LIMIT),
        name="neighbourhood_attention",
    )(na_l, na_l, na_l, na_c, na_c, na_c, bias)


def _outproj_residual(r_ref, g_ref, n_ref, x_ref, mod_ref, wo_ref):
    mix = jnp.concatenate([r_ref[0], g_ref[0], n_ref[0]], axis=1)
    return x_ref[0] + mod_ref[0, 2:3, :] * _dot(mix, wo_ref[...])


def _swiglu_chunks(hb, w1_ref, w3_ref, w2_ref, acc_ref):
    acc_ref[...] = jnp.zeros_like(acc_ref)

    def step(j, _):
        t = (_silu(_dot(hb, w1_ref[j])) * _dot(hb, w3_ref[j])).astype(BF16)
        acc_ref[...] += _dot(t, w2_ref[j])
        return 0

    lax.fori_loop(0, w1_ref.shape[0], step, 0)


def _dense_body(r_ref, g_ref, n_ref, x_ref, mod_ref, g2_ref, wo_ref, w1_ref, w3_ref, w2_ref, o_ref, acc_ref):
    x1 = _outproj_residual(r_ref, g_ref, n_ref, x_ref, mod_ref, wo_ref)
    hb = _norm_mod(x1, g2_ref[...], mod_ref[0, 3:4, :], mod_ref[0, 4:5, :]).astype(BF16)
    _swiglu_chunks(hb, w1_ref, w3_ref, w2_ref, acc_ref)
    o_ref[0] = x1 + mod_ref[0, 5:6, :] * acc_ref[...]


def _router_body(r_ref, g_ref, n_ref, x_ref, mod_ref, g2_ref, wo_ref, wr_ref, x1_ref, h_ref, lg_ref):
    x1 = _outproj_residual(r_ref, g_ref, n_ref, x_ref, mod_ref, wo_ref)
    h = _norm_mod(x1, g2_ref[...], mod_ref[0, 3:4, :], mod_ref[0, 4:5, :])
    x1_ref[0] = x1
    h_ref[0] = h
    lg_ref[0] = _dot_hi(h, wr_ref[...])


def _resident(shape):
    nd = len(shape)
    return pl.BlockSpec(shape, lambda i, j: (0,) * nd, pipeline_mode=pl.Buffered(1))


def _mixer_specs(tm, d):
    return [pl.BlockSpec((1, tm, D_RET), lambda i, j: (i, j, 0)),
            pl.BlockSpec((1, tm, D_GDN), lambda i, j: (i, j, 0)),
            pl.BlockSpec((1, tm, D_NA), lambda i, j: (i, j, 0)),
            pl.BlockSpec((1, tm, d), lambda i, j: (i, j, 0)),
            pl.BlockSpec((1, 6, d), lambda i, j: (i, 0, 0)),
            pl.BlockSpec((1, d), lambda i, j: (0, 0))]


def _chunk_ffn_weights(w1, w3, w2):
    d, dff = w1.shape
    nch = dff // FF_CHUNK
    w1c = jnp.transpose(w1.reshape(d, nch, FF_CHUNK), (1, 0, 2)).astype(BF16)
    w3c = jnp.transpose(w3.reshape(d, nch, FF_CHUNK), (1, 0, 2)).astype(BF16)
    w2c = w2.reshape(nch, FF_CHUNK, d).astype(BF16)
    return w1c, w3c, w2c


def _dense_block(r, g, n, x, mod, g2, wo, w1c, w3c, w2c):
    b, l, d = x.shape
    tm = min(512, l)
    nch = w1c.shape[0]
    return pl.pallas_call(
        _dense_body,
        grid=(b, l // tm),
        in_specs=_mixer_specs(tm, d) + [_resident((d, d)), _resident((nch, d, FF_CHUNK)),
                                        _resident((nch, d, FF_CHUNK)), _resident((nch, FF_CHUNK, d))],
        out_specs=pl.BlockSpec((1, tm, d), lambda i, j: (i, j, 0)),
        out_shape=jax.ShapeDtypeStruct((b, l, d), F32),
        scratch_shapes=[pltpu.VMEM((tm, d), F32)],
        compiler_params=_cparams(("arbitrary", "arbitrary"), VMEM_LIMIT),
        name="outproj_swiglu",
    )(r, g, n, x, mod, g2, wo, w1c, w3c, w2c)


def _router_block(r, g, n, x, mod, g2, wo, wr):
    b, l, d = x.shape
    tm = min(512, l)
    blk = pl.BlockSpec((1, tm, d), lambda i, j: (i, j, 0))
    return pl.pallas_call(
        _router_body,
        grid=(b, l // tm),
        in_specs=_mixer_specs(tm, d) + [_resident((d, d)), _resident((d, LANES))],
        out_specs=[blk, blk, pl.BlockSpec((1, tm, LANES), lambda i, j: (i, j, 0))],
        out_shape=[jax.ShapeDtypeStruct((b, l, d), F32), jax.ShapeDtypeStruct((b, l, d), F32),
                   jax.ShapeDtypeStruct((b, l, LANES), F32)],
        compiler_params=_cparams(("arbitrary", "arbitrary"), VMEM_LIMIT),
        name="outproj_router",
    )(r, g, n, x, mod, g2, wo, wr)


def _gather_rows(idx_ref, src_hbm, dst_ref, sem, n):
    def issue(r, _):
        pltpu.make_async_copy(src_hbm.at[pl.ds(idx_ref[0, 0, r], 1), :], dst_ref.at[pl.ds(r, 1), :], sem).start()
        return 0

    lax.fori_loop(0, n, issue, 0)
    pltpu.make_async_copy(src_hbm.at[pl.ds(0, n), :], dst_ref, sem).wait()


def _moe_body(bexp_ref, bval_ref, tok_ref, h_hbm, w1_ref, w3_ref, w2_ref, y_ref, xbuf, acc_ref, sem):
    i = pl.program_id(0)

    @pl.when(bval_ref[i] == 1)
    def _():
        _gather_rows(tok_ref, h_hbm, xbuf, sem, xbuf.shape[0])
        _swiglu_chunks(xbuf[...].astype(BF16), w1_ref.at[0], w3_ref.at[0], w2_ref.at[0], acc_ref)
        y_ref[...] = acc_ref[...]

    @pl.when(bval_ref[i] == 0)
    def _():
        y_ref[...] = jnp.zeros_like(y_ref)


def _moe_experts(h_rows, row_tok, block_expert, block_valid, w1c, w3c, w2c):
    n_tok, d = h_rows.shape
    nb = block_expert.shape[0]
    nch = w1c.shape[1]
    tb = MOE_ROWS
    grid_spec = pltpu.PrefetchScalarGridSpec(
        num_scalar_prefetch=2,
        grid=(nb,),
        in_specs=[pl.BlockSpec((1, 1, tb), lambda i, be, bv: (i, 0, 0), memory_space=pltpu.SMEM),
                  pl.BlockSpec(memory_space=pl.ANY),
                  pl.BlockSpec((1, nch, d, FF_CHUNK), lambda i, be, bv: (be[i], 0, 0, 0)),
                  pl.BlockSpec((1, nch, d, FF_CHUNK), lambda i, be, bv: (be[i], 0, 0, 0)),
                  pl.BlockSpec((1, nch, FF_CHUNK, d), lambda i, be, bv: (be[i], 0, 0, 0))],
        out_specs=pl.BlockSpec((tb, d), lambda i, be, bv: (i, 0)),
        scratch_shapes=[pltpu.VMEM((tb, d), F32), pltpu.VMEM((tb, d), F32), pltpu.SemaphoreType.DMA(())],
    )
    return pl.pallas_call(
        _moe_body,
        grid_spec=grid_spec,
        out_shape=jax.ShapeDtypeStruct((nb * tb, d), F32),
        compiler_params=_cparams(("arbitrary",), VMEM_LIMIT),
        name="moe_experts",
    )(block_expert, block_valid, row_tok.reshape(nb, 1, tb), h_rows, w1c, w3c, w2c)


def _combine_body(pos_ref, y_hbm, x1_ref, gate_ref, gm_ref, fg_ref, o_ref, ybuf, sem, *, final):
    tm = x1_ref.shape[0]
    _gather_rows(pos_ref, y_hbm, ybuf, sem, 2 * tm)
    moe = gate_ref[:, 0:1] * ybuf[0:tm, :] + gate_ref[:, 1:2] * ybuf[tm:, :]
    x2 = x1_ref[...] + gm_ref[0] * moe
    if final:
        ms = jnp.mean(x2 * x2, axis=-1, keepdims=True)
        x2 = x2 * lax.rsqrt(ms + EPS) * fg_ref[...]
    o_ref[...] = x2


def _moe_combine(y_rows, pos, x1, gates, gate_mlp, final_g, final):
    b, l, d = x1.shape
    n_tok = b * l
    tm = 256
    nt = n_tok // tm
    per_b = l // tm
    pos_t = jnp.transpose(pos.reshape(nt, tm, 2), (0, 2, 1)).reshape(nt, 1, 2 * tm)
    out = pl.pallas_call(
        functools.partial(_combine_body, final=final),
        grid=(nt,),
        in_specs=[pl.BlockSpec((1, 1, 2 * tm), lambda i: (i, 0, 0), memory_space=pltpu.SMEM),
                  pl.BlockSpec(memory_space=pl.ANY),
                  pl.BlockSpec((tm, d), lambda i: (i, 0)),
                  pl.BlockSpec((tm, 2), lambda i: (i, 0)),
                  pl.BlockSpec((1, 1, d), lambda i: (i // per_b, 0, 0)),
                  pl.BlockSpec((1, d), lambda i: (0, 0))],
        out_specs=pl.BlockSpec((tm, d), lambda i: (i, 0)),
        out_shape=jax.ShapeDtypeStruct((n_tok, d), F32),
        scratch_shapes=[pltpu.VMEM((2 * tm, d), F32), pltpu.SemaphoreType.DMA(())],
        compiler_params=_cparams(("arbitrary",), VMEM_LIMIT),
        name="moe_combine_final_norm",
    )(pos_t, y_rows, x1.reshape(n_tok, d), gates, gate_mlp, final_g.reshape(1, d))
    return out.reshape(b, l, d)


def _route(logits):
    n_tok = logits.shape[0]
    tb = MOE_ROWS
    top_logit, top_e = lax.top_k(logits[:, :N_EXPERTS], TOP_K)
    gates = jax.nn.softmax(top_logit, axis=-1)
    e_flat = top_e.reshape(-1).astype(jnp.int32)
    onehot = (e_flat[:, None] == jnp.arange(N_EXPERTS, dtype=jnp.int32)[None, :]).astype(jnp.int32)
    rank = jnp.sum((jnp.cumsum(onehot, axis=0) - onehot) * onehot, axis=1)
    counts = jnp.sum(onehot, axis=0)
    padded = (counts + tb - 1) // tb * tb
    pad_ends = jnp.cumsum(padded)
    pad_starts = pad_ends - padded
    dest = pad_starts[e_flat] + rank
    nb = (n_tok * TOP_K + tb - 1) // tb + N_EXPERTS
    tok_flat = jnp.repeat(jnp.arange(n_tok, dtype=jnp.int32), TOP_K)
    row_tok = jnp.zeros((nb * tb,), jnp.int32).at[dest].set(tok_flat)
    block_start = jnp.arange(nb, dtype=jnp.int32) * tb
    block_expert = jnp.minimum(jnp.searchsorted(pad_ends, block_start, side='right'), N_EXPERTS - 1).astype(jnp.int32)
    block_valid = (block_start < pad_ends[-1]).astype(jnp.int32)
    last_e = block_expert[jnp.maximum(pad_ends[-1] // tb - 1, 0)]
    block_expert = jnp.where(block_valid == 1, block_expert, last_e)
    return row_tok, dest.reshape(n_tok, TOP_K), gates, block_expert, block_valid


def _final_norm_body(x_ref, g_ref, o_ref):
    x = x_ref[...]
    ms = jnp.mean(x * x, axis=-1, keepdims=True)
    o_ref[...] = x * lax.rsqrt(ms + EPS) * g_ref[...]


def _final_norm(x, g):
    b, l, d = x.shape
    n = b * l
    tm = min(512, n)
    out = pl.pallas_call(
        _final_norm_body,
        grid=(n // tm,),
        in_specs=[pl.BlockSpec((tm, d), lambda i: (i, 0)), pl.BlockSpec((1, d), lambda i: (0, 0))],
        out_specs=pl.BlockSpec((tm, d), lambda i: (i, 0)),
        out_shape=jax.ShapeDtypeStruct((n, d), F32),
        compiler_params=_cparams(("arbitrary",)),
        name="final_norm",
    )(x.reshape(n, d), g.reshape(1, d))
    return out.reshape(b, l, d)


def _rope_tables(n_tok):
    t = jnp.arange(n_tok, dtype=jnp.int32)
    row = (t // GRID_W).astype(F32)
    col = (t % GRID_W).astype(F32)
    inv_freq = ROPE_BASE ** (-jnp.arange(N_FREQ, dtype=F32) / N_FREQ)
    ang = jnp.concatenate([row[:, None] * inv_freq, col[:, None] * inv_freq], axis=-1)
    cos, sin = jnp.cos(ang), jnp.sin(ang)
    cosf = jnp.concatenate([cos, cos, cos, cos], axis=-1)
    sins = jnp.concatenate([-sin, sin, -sin, sin], axis=-1)
    return cosf, sins


def kernel(x, c, ctx, c_ctx, ada_w, ada_b, norm1_g, norm2_g, w_in, w_out, conv_w, ret_decay, gdn_a_log, gdn_dt_bias,
           gdn_norm_g, na_rpb, ffn_w1, ffn_w3, ffn_w2, moe_router, moe_w1, moe_w3, moe_w2, final_g):
    b, l, d = x.shape
    lc = ctx.shape[1]
    depth = ada_w.shape[0]
    cosf, sins = _rope_tables(l)
    ones_c = jnp.ones((lc, LANES), F32)
    zeros_c = jnp.zeros((lc, LANES), F32)

    rows = ((b + 1 + 7) // 8) * 8
    c_all = jnp.zeros((rows, d), F32).at[:b].set(c).at[b].set(c_ctx)
    mod = _ada_vectors(c_all, ada_w, ada_b).reshape(depth, rows, 6, d)

    y = ctx
    for layer in range(depth):
        need_ctx = layer < depth - 1
        mod_l = mod[layer, :b]
        mod_c = jnp.broadcast_to(mod[layer, b][None], (b, 6, d))
        w1p = _pack_w_in(w_in[layer])
        g1 = norm1_g[layer].reshape(1, d)
        ret_l, gdn_l, ab_l, na_l = _in_projection(x, mod_l, g1, cosf, sins, w1p, rope=True)
        ret_c, gdn_c, ab_c, na_c = _in_projection(y, mod_c, g1, ones_c, zeros_c, w1p, rope=False)

        log_gamma = jnp.log1p(-jnp.exp2(-ret_decay[layer].astype(F32)))
        r_l, r_c = _retention(ret_l, ret_c, log_gamma)
        g_l, g_c = _gdn(gdn_l, ab_l, gdn_c, ab_c, conv_w[layer], gdn_a_log[layer], gdn_dt_bias[layer], gdn_norm_g[layer])
        n_l, n_c = _neighbourhood_attention(na_l, na_c, na_rpb[layer])

        wo = w_out[layer].astype(BF16)
        g2 = norm2_g[layer].reshape(1, d)
        j = layer // 2
        if layer % 2 == 0:
            w1c, w3c, w2c = _chunk_ffn_weights(ffn_w1[j], ffn_w3[j], ffn_w2[j])
            x = _dense_block(r_l, g_l, n_l, x, mod_l, g2, wo, w1c, w3c, w2c)
            if need_ctx:
                y = _dense_block(r_c, g_c, n_c, y, mod_c, g2, wo, w1c, w3c, w2c)
            if layer == depth - 1:
                x = _final_norm(x, final_g)
        else:
            wr = jnp.pad(moe_router[j].astype(F32), ((0, 0), (0, LANES - N_EXPERTS)))
            ws = [_chunk_ffn_weights(moe_w1[j][e], moe_w3[j][e], moe_w2[j][e]) for e in range(N_EXPERTS)]
            w1c, w3c, w2c = (jnp.stack([w[k] for w in ws]) for k in range(3))

            def moe_ffn(r, g, n, xin, m, last):
                bb, ll, _ = xin.shape
                x1, h, logits = _router_block(r, g, n, xin, m, g2, wo, wr)
                row_tok, pos, gates, bexp, bval = _route(logits.reshape(bb * ll, LANES))
                y_rows = _moe_experts(h.reshape(bb * ll, d), row_tok, bexp, bval, w1c, w3c, w2c)
                return _moe_combine(y_rows, pos, x1, gates, m[:, 5:6, :], final_g, last)

            x = moe_ffn(r_l, g_l, n_l, x, mod_l, layer == depth - 1)
            if need_ctx:
                y = moe_ffn(r_c, g_c, n_c, y, mod_c, False)
    return x
```

```python
import functools

import numpy as np
import jax
import jax.numpy as jnp
from jax import lax
from jax.experimental import pallas as pl
from jax.experimental.pallas import tpu as pltpu

F32 = jnp.float32
BF16 = jnp.bfloat16
HIGHEST = lax.Precision.HIGHEST

LANES = 128
HEAD_DIM = 64
PAIR = 2 * HEAD_DIM
GRID_W = 64
H_RET, H_GDN, H_NA = 4, 4, 8
D_RET, D_GDN, D_NA = H_RET * HEAD_DIM, H_GDN * HEAD_DIM, H_NA * HEAD_DIM
RET_CHUNK = 128
GDN_CHUNK = 64
GDN_GROUP = 4
GDN_PREP_ROWS = 512
SHORT_CONV = 5
NA_KH, NA_KW = 8, 16
NA_QROWS = 4
NA_KROWS = NA_QROWS + NA_KH
N_FREQ = HEAD_DIM // 4
ROPE_BASE = 10000.0
N_EXPERTS = 8
TOP_K = 2
MOE_ROWS = 512
FF_CHUNK = 256
EPS = 1e-6
NEG_INF = -1e30
VMEM_LIMIT = 56 * 1024 * 1024

C_RET = 0
C_GDN = C_RET + 4 * D_RET
C_AB = C_GDN + 4 * D_GDN
C_NA = C_AB + 2 * LANES
C_END = C_NA + 3 * D_NA


def _cparams(sem, vmem=None):
    return pltpu.CompilerParams(dimension_semantics=sem, vmem_limit_bytes=vmem)


def _silu(x):
    return x * jax.nn.sigmoid(x)


def _dot(a, b):
    return jnp.dot(a, b, preferred_element_type=F32)


def _dot_nt(a, b):
    return lax.dot_general(a, b, (((1,), (1,)), ((), ())), preferred_element_type=F32)


def _dot_tn(a, b):
    return lax.dot_general(a, b, (((0,), (0,)), ((), ())), preferred_element_type=F32)


def _dot_hi(a, b):
    return jnp.dot(a, b, preferred_element_type=F32, precision=HIGHEST)


def _ada_body(c_ref, w_ref, b_ref, o_ref):
    s = _silu(c_ref[...])
    o_ref[0] = _dot_hi(s, w_ref[0]) + b_ref[0]


def _ada_vectors(c_all, ada_w, ada_b):
    depth, d, d6 = ada_w.shape
    rows = c_all.shape[0]
    tn = 1024
    return pl.pallas_call(
        _ada_body,
        grid=(depth, d6 // tn),
        in_specs=[pl.BlockSpec((rows, d), lambda l, j: (0, 0)),
                  pl.BlockSpec((1, d, tn), lambda l, j: (l, 0, j)),
                  pl.BlockSpec((1, 1, tn), lambda l, j: (l, 0, j))],
        out_specs=pl.BlockSpec((1, rows, tn), lambda l, j: (l, 0, j)),
        out_shape=jax.ShapeDtypeStruct((depth, rows, d6), F32),
        compiler_params=_cparams(("arbitrary", "arbitrary")),
        name="ada_vectors",
    )(c_all, ada_w, ada_b.reshape(depth, 1, d6))


def _norm_mod(x, g, shift, scale):
    ms = jnp.mean(x * x, axis=-1, keepdims=True)
    return (x * lax.rsqrt(ms + EPS) * g) * (1.0 + scale) + shift


def _rope_slab(t, cosf, sins):
    lane = lax.broadcasted_iota(jnp.int32, t.shape, 1)
    first = (lane % HEAD_DIM) < (HEAD_DIM // 2)
    partner = jnp.where(first, pltpu.roll(t, LANES - HEAD_DIM // 2, 1), pltpu.roll(t, HEAD_DIM // 2, 1))
    return t * cosf + partner * sins


def _inproj_body(x_ref, mod_ref, g_ref, cos_ref, sin_ref, w_ref, ret_ref, gdn_ref, ab_ref, na_ref, *, rope):
    h = _norm_mod(x_ref[0], g_ref[...], mod_ref[0, 0:1, :], mod_ref[0, 1:2, :]).astype(BF16)
    qk = _dot(h, w_ref[:, C_RET:C_RET + 2 * D_RET])
    slabs = []
    for s in range(2 * D_RET // LANES):
        t = qk[:, s * LANES:(s + 1) * LANES]
        if rope:
            t = _rope_slab(t, cos_ref[...], sin_ref[...])
        if s >= D_RET // LANES:
            t = t * HEAD_DIM ** -0.5
        slabs.append(t)
    ret_ref[0, :, 0:2 * D_RET] = jnp.concatenate(slabs, axis=1).astype(BF16)
    ret_ref[0, :, 2 * D_RET:] = _dot(h, w_ref[:, C_RET + 2 * D_RET:C_GDN]).astype(BF16)
    for j in range(2):
        gdn_ref[0, :, 512 * j:512 * (j + 1)] = _dot(h, w_ref[:, C_GDN + 512 * j:C_GDN + 512 * (j + 1)]).astype(BF16)
    ab_ref[0] = _dot(h, w_ref[:, C_AB:C_NA])
    for j in range(3):
        na_ref[0, :, 512 * j:512 * (j + 1)] = _dot(h, w_ref[:, C_NA + 512 * j:C_NA + 512 * (j + 1)]).astype(BF16)


def _in_projection(x, mod, g, cosf, sins, w, rope):
    b, l, d = x.shape
    tm = min(512, l)
    body = functools.partial(_inproj_body, rope=rope)
    return pl.pallas_call(
        body,
        grid=(b, l // tm),
        in_specs=[pl.BlockSpec((1, tm, d), lambda i, j: (i, j, 0)),
                  pl.BlockSpec((1, 6, d), lambda i, j: (i, 0, 0)),
                  pl.BlockSpec((1, d), lambda i, j: (0, 0)),
                  pl.BlockSpec((tm, LANES), lambda i, j: (j, 0)),
                  pl.BlockSpec((tm, LANES), lambda i, j: (j, 0)),
                  pl.BlockSpec((d, C_END), lambda i, j: (0, 0))],
        out_specs=[pl.BlockSpec((1, tm, 4 * D_RET), lambda i, j: (i, j, 0)),
                   pl.BlockSpec((1, tm, 4 * D_GDN), lambda i, j: (i, j, 0)),
                   pl.BlockSpec((1, tm, 2 * LANES), lambda i, j: (i, j, 0)),
                   pl.BlockSpec((1, tm, 3 * D_NA), lambda i, j: (i, j, 0))],
        out_shape=[jax.ShapeDtypeStruct((b, l, 4 * D_RET), BF16),
                   jax.ShapeDtypeStruct((b, l, 4 * D_GDN), BF16),
                   jax.ShapeDtypeStruct((b, l, 2 * LANES), F32),
                   jax.ShapeDtypeStruct((b, l, 3 * D_NA), BF16)],
        compiler_params=_cparams(("arbitrary", "arbitrary"), VMEM_LIMIT),
        name="in_projection",
    )(x, mod, g, cosf, sins, w)


def _pack_w_in(w_in, conv_cols=None):
    d = w_in.shape[0]
    c1 = 4 * D_RET
    c2 = c1 + 4 * D_GDN
    ab = w_in[:, c2:c2 + 4 * H_GDN]
    ab = ab.reshape(d, 2, 2, H_GDN // 2, 2)
    ab = jnp.transpose(ab, (0, 3, 1, 2, 4)).reshape(d, 2, 8)
    ab = jnp.pad(ab, ((0, 0), (0, 0), (0, LANES - 8))).reshape(d, 2 * LANES)
    return jnp.concatenate([w_in[:, :c2], ab, w_in[:, c2 + 4 * H_GDN:]], axis=1).astype(BF16)


def _head_masks(shape):
    lane = lax.broadcasted_iota(jnp.int32, shape, len(shape) - 1)
    return lane < HEAD_DIM, lane >= HEAD_DIM


def _per_head(lo, hi, shape):
    m0, _ = _head_masks(shape)
    return jnp.where(m0, lo, hi)


def _head_sumsq(o):
    m0, m1 = _head_masks(o.shape)
    sq = o * o
    s0 = jnp.sum(jnp.where(m0, sq, 0.0), axis=-1, keepdims=True)
    s1 = jnp.sum(jnp.where(m1, sq, 0.0), axis=-1, keepdims=True)
    return jnp.where(m0, s0, s1)


def _ret_body(lg_ref, ql, kl, vl, gl, qc, kc, vc, gc, r_ref, rc_ref, sb_ref):
    c = RET_CHUNK
    p = pl.program_id(1)
    lane_shape = (c, PAIR)
    pos = lax.broadcasted_iota(jnp.int32, lane_shape, 0).astype(F32)
    lgf = _per_head(lg_ref[0, 2 * p], lg_ref[0, 2 * p + 1], lane_shape)
    lgb = _per_head(lg_ref[1, 2 * p], lg_ref[1, 2 * p + 1], lane_shape)
    qdf = jnp.exp(lgf * (pos + 1.0))
    kdf = jnp.exp(lgf * (c - 1.0 - pos))
    qdb = jnp.exp(lgb * (c - pos))
    kdb = jnp.exp(lgb * pos)
    cdf = jnp.exp(lgf[0:1] * c)
    cdb = jnp.exp(lgb[0:1] * c)
    ii = lax.broadcasted_iota(jnp.int32, (c, c), 0)
    jj = lax.broadcasted_iota(jnp.int32, (c, c), 1)
    diff = (ii - jj).astype(F32)
    dmats = [jnp.where(diff > 0, jnp.exp(lg_ref[0, 2 * p + hh] * diff),
                       jnp.where(diff < 0, jnp.exp(-lg_ref[1, 2 * p + hh] * diff), 2.0)) for hh in range(2)]
    m0, m1 = _head_masks(lane_shape)
    masks = (m0, m1)
    bi = lax.broadcasted_iota(jnp.int32, (PAIR, PAIR), 0) // HEAD_DIM
    bj = lax.broadcasted_iota(jnp.int32, (PAIR, PAIR), 1) // HEAD_DIM
    bd = bi == bj

    def sweep(q_ref, k_ref, v_ref, g_ref, o_ref, n, sf0, sb0):
        def bstep(t, sb):
            ci = n - 1 - t
            sl = pl.ds(pl.multiple_of(ci * c, c), c)
            sb_ref[ci] = sb
            kd = (k_ref[0, sl, :].astype(F32) * kdb).astype(BF16)
            return sb * cdb + jnp.where(bd, _dot_tn(kd, v_ref[0, sl, :]), 0.0)

        sb_fin = lax.fori_loop(0, n, bstep, sb0)

        def fstep(ci, sf):
            sl = pl.ds(pl.multiple_of(ci * c, c), c)
            q = q_ref[0, sl, :]
            k = k_ref[0, sl, :]
            v = v_ref[0, sl, :]
            qf = q.astype(F32)
            o = _dot((qf * qdf).astype(BF16), sf.astype(BF16)) + _dot((qf * qdb).astype(BF16), sb_ref[ci].astype(BF16))
            for hh in range(2):
                qm = jnp.where(masks[hh], q, jnp.zeros_like(q))
                pm = (_dot_nt(qm, k) * dmats[hh]).astype(BF16)
                o = o + jnp.where(masks[hh], _dot(pm, v), 0.0)
            on = o * lax.rsqrt(_head_sumsq(o) * (1.0 / HEAD_DIM) + EPS)
            o_ref[0, sl, :] = (on * _silu(g_ref[0, sl, :].astype(F32))).astype(o_ref.dtype)
            kd = (k.astype(F32) * kdf).astype(BF16)
            return sf * cdf + jnp.where(bd, _dot_tn(kd, v), 0.0)

        sf_fin = lax.fori_loop(0, n, fstep, sf0)
        return sf_fin, sb_fin

    z = jnp.zeros((PAIR, PAIR), F32)
    sfc, sbc = sweep(qc, kc, vc, gc, rc_ref, qc.shape[1] // c, z, z)
    sweep(ql, kl, vl, gl, r_ref, ql.shape[1] // c, sfc, sbc)


def _retention(ret_l, ret_c, log_gamma):
    b, l, _ = ret_l.shape
    lc = ret_c.shape[1]
    npair = D_RET // PAIR

    def col(k):
        return lambda i, p: (i, 0, k * npair + p)

    lat = [pl.BlockSpec((1, l, PAIR), col(k)) for k in range(4)]
    ctx = [pl.BlockSpec((1, lc, PAIR), col(k)) for k in range(4)]
    return pl.pallas_call(
        _ret_body,
        grid=(b, npair),
        in_specs=[pl.BlockSpec(memory_space=pltpu.SMEM)] + lat + ctx,
        out_specs=[pl.BlockSpec((1, l, PAIR), lambda i, p: (i, 0, p)),
                   pl.BlockSpec((1, lc, PAIR), lambda i, p: (i, 0, p))],
        out_shape=[jax.ShapeDtypeStruct((b, l, D_RET), BF16), jax.ShapeDtypeStruct((b, lc, D_RET), BF16)],
        scratch_shapes=[pltpu.VMEM((max(l, lc) // RET_CHUNK, PAIR, PAIR), F32)],
        compiler_params=_cparams(("arbitrary", "arbitrary"), VMEM_LIMIT),
        name="retention",
    )(log_gamma, ret_l, ret_l, ret_l, ret_l, ret_c, ret_c, ret_c, ret_c)


def _stack_heads(t):
    m0, m1 = _head_masks(t.shape)
    z = jnp.zeros_like(t)
    return jnp.concatenate([jnp.where(m0, t, z), jnp.where(m1, t, z)], axis=0)


def _gdn_prep(src_ref, col, cw, dst_ref, r0, *, l2, scale):
    n = src_ref.shape[1]
    blk = min(GDN_PREP_ROWS, n)
    halo = 16
    lanes = slice(col % D_GDN, col % D_GDN + LANES)

    def block(i, _):
        b0 = pl.multiple_of(i * blk, blk)
        x = src_ref[0, pl.ds(b0, blk), col:col + LANES].astype(F32)
        lo = src_ref[0, pl.ds(pl.multiple_of(jnp.maximum(b0 - halo, 0), halo), halo), col:col + LANES].astype(F32)
        hi = src_ref[0, pl.ds(pl.multiple_of(jnp.minimum(b0 + blk, n - halo), halo), halo), col:col + LANES].astype(F32)
        lo = jnp.where(b0 > 0, lo, 0.0)
        hi = jnp.where(b0 + blk < n, hi, 0.0)
        ext = jnp.concatenate([lo, x, hi], axis=0)
        acc = x * cw[SHORT_CONV // 2:SHORT_CONV // 2 + 1, :]
        for j in range(SHORT_CONV):
            s = j - SHORT_CONV // 2
            if s != 0:
                sh = pltpu.roll(ext, (-s) % (blk + 2 * halo), 0)
                acc = acc + sh[halo:halo + blk] * cw[j:j + 1, :]
        y = _silu(acc)
        if l2:
            y = y * lax.rsqrt(_head_sumsq(y) + EPS)
        if scale != 1.0:
            y = y * scale
        dst_ref[pl.ds(pl.multiple_of(r0 + b0, halo), blk), lanes] = y.astype(dst_ref.dtype)
        return 0

    lax.fori_loop(0, n // blk, block, 0)


def _gdn_body(x_ref, xc_ref, ab_ref, abt_ref, cw_ref, cst_ref, cstt_ref, ng_ref,
              o_ref, oc_ref, qn, kn, vn, oacc, st_s, x_s, t_s, rhs_s, u_s, w_s, at_s, qg_s, kd_s, egl_s):
    c = GDN_CHUNK
    c2 = 2 * c
    grp = GDN_GROUP
    npair = D_GDN // PAIR
    lc = xc_ref.shape[1]
    l = x_ref.shape[1]
    nc = lc // c
    nt = (lc + l) // c
    ii = lax.broadcasted_iota(jnp.int32, (c2, c2), 0)
    jj = lax.broadcasted_iota(jnp.int32, (c2, c2), 1)
    same = (ii // c) == (jj // c)
    eye = (ii == jj).astype(F32)
    ti = lax.broadcasted_iota(jnp.int32, (c, c), 0)
    tj = lax.broadcasted_iota(jnp.int32, (c, c), 1)
    lower = (ti >= tj).astype(F32)
    upper = (ti <= tj).astype(F32)
    dirs = ((same & (ii >= jj), same & (ii > jj), lower, upper, c - 1),
            (same & (ii <= jj), same & (ii < jj), upper, lower, 0))

    def chains(g, slot):
        out = []
        for gi in range(grp):
            t = g * grp + gi
            cf = t
            cb = jnp.where(t < nc, nc - 1 - t, nt + nc - 1 - t)
            for p in range(npair):
                for d in range(2):
                    out.append((((slot * grp + gi) * 2 + d) * npair + p, p, d, cf if d == 0 else cb))
        return out

    def stage_inputs(idx, p, d, ci):
        incl, strict, tri_col, tri_row, last = dirs[d]
        lanes = slice(p * PAIR, (p + 1) * PAIR)
        nea, dtb = cst_ref[p, 0:1, :], cst_ref[p, 1:2, :]
        neat, dtbt = cstt_ref[p, 0], cstt_ref[p, 1]
        sl = pl.ds(pl.multiple_of(ci * c, c), c)
        abv = ab_ref[0, sl, p * LANES:(p + 1) * LANES]
        gall = nea * jax.nn.softplus(abv + dtb)
        ball = jax.nn.sigmoid(abv)
        shape = (c, PAIR)
        gl = _per_head(gall[:, 2 * d:2 * d + 1], gall[:, 2 * d + 1:2 * d + 2], shape)
        bl = _per_head(ball[:, 4 + 2 * d:5 + 2 * d], ball[:, 5 + 2 * d:6 + 2 * d], shape)
        gcum = _dot_hi(tri_col, gl)
        gt = neat[:, 0:c] * jax.nn.softplus(abt_ref[0, p, ci] + dtbt[:, 0:c])
        gtc = _dot_hi(gt, tri_row)
        grow = jnp.concatenate([gtc[2 * d:2 * d + 1, :], gtc[2 * d + 1:2 * d + 2, :]], axis=1)
        gcol = jnp.concatenate([gcum[:, 0:1], gcum[:, HEAD_DIM:HEAD_DIM + 1]], axis=0)
        dec = jnp.where(incl, jnp.exp(jnp.where(incl, gcol - grow, 0.0)), 0.0)
        dec = jnp.where(ii == jj, 1.0, dec)
        q = qn[sl, lanes]
        k = kn[sl, lanes]
        v = vn[sl, lanes]
        kf = k.astype(F32)
        eg = jnp.exp(gcum)
        kb = kf * bl
        k_st = _stack_heads(k)
        a = jnp.where(strict, _dot_nt(_stack_heads(kb.astype(BF16)), k_st) * dec, 0.0)
        at_s[idx] = (_dot_nt(_stack_heads(q), k_st) * dec).astype(BF16)
        x_s[idx] = (-a).astype(BF16)
        t_s[idx] = eye - a
        rhs_s[idx] = jnp.concatenate([_stack_heads((v.astype(F32) * bl).astype(BF16)),
                                      _stack_heads((kb * eg).astype(BF16))], axis=1)
        qg_s[idx] = _stack_heads((q.astype(F32) * eg).astype(BF16))
        glast = gcum[last:last + 1, :]
        kd_s[idx] = _stack_heads(kf * jnp.exp(glast - gcum)).T.astype(BF16)
        egl_s[idx] = jnp.broadcast_to(jnp.exp(glast), (8, PAIR))

    def stage_group(g, slot):
        todo = chains(g, slot)
        for idx, p, d, ci in todo:
            stage_inputs(idx, p, d, ci)
        for _ in range(5):
            for idx, _, _, _ in todo:
                xb = x_s[idx]
                x_s[idx] = _dot(xb, xb).astype(BF16)
            for idx, _, _, _ in todo:
                t = t_s[idx]
                t_s[idx] = t + _dot(t.astype(BF16), x_s[idx])
        for idx, _, _, _ in todo:
            sol = _dot(t_s[idx].astype(BF16), rhs_s[idx])
            u_s[idx] = sol[:, :PAIR]
            w_s[idx] = sol[:, PAIR:].astype(BF16)

    def recur_group(g, slot):
        sts = {(p, d): st_s[d * npair + p] for p in range(npair) for d in range(2)}
        for idx, p, d, ci in chains(g, slot):
            sl = pl.ds(pl.multiple_of(ci * c, c), c)
            st = sts[(p, d)]
            stb = st.astype(BF16)
            v_new = (u_s[idx] - _dot(w_s[idx], stb)).astype(BF16)
            o_st = _dot(qg_s[idx], stb) + _dot(at_s[idx], v_new)
            oacc[sl, p * PAIR:(p + 1) * PAIR] += o_st[:c] + o_st[c:]
            sts[(p, d)] = st * egl_s[idx][0:1, :] + _dot(kd_s[idx], v_new)
        for (p, d), st in sts.items():
            st_s[d * npair + p] = st

    def prep(src_ref, r0):
        for s in range(D_GDN // LANES):
            for k, (dst, l2, scale) in enumerate(((qn, True, HEAD_DIM ** -0.5), (kn, True, 1.0), (vn, False, 1.0))):
                col = k * D_GDN + s * LANES
                _gdn_prep(src_ref, col, cw_ref[:, col:col + LANES], dst, r0, l2=l2, scale=scale)

    def finish(src_ref, out_ref, r0, n):
        for p in range(npair):
            lanes = slice(p * PAIR, (p + 1) * PAIR)
            o = oacc[r0:r0 + n, lanes]
            on = o * lax.rsqrt(_head_sumsq(o) * (1.0 / HEAD_DIM) + EPS)
            gate = src_ref[0, :, 3 * D_GDN + p * PAIR:3 * D_GDN + (p + 1) * PAIR].astype(F32)
            out_ref[0, :, lanes] = (on * ng_ref[...] * _silu(gate)).astype(out_ref.dtype)

    st_s[...] = jnp.zeros_like(st_s)
    oacc[...] = jnp.zeros_like(oacc)
    prep(xc_ref, 0)
    prep(x_ref, lc)

    ng = nt // grp
    stage_group(0, 0)

    def two_groups(kk, _):
        g = 2 * kk
        recur_group(g, 0)
        stage_group(g + 1, 1)
        recur_group(g + 1, 1)
        stage_group(g + 2, 0)
        return 0

    lax.fori_loop(0, (ng - 1) // 2, two_groups, 0)
    if ng % 2 == 0:
        recur_group(ng - 2, 0)
        stage_group(ng - 1, 1)
        recur_group(ng - 1, 1)
    else:
        recur_group(ng - 1, 0)
    finish(xc_ref, oc_ref, 0, lc)
    finish(x_ref, o_ref, lc, l)


def _gdn(gdn_l, ab_l, gdn_c, ab_c, conv_w, a_log, dt_bias, norm_g):
    b, l, _ = gdn_l.shape
    lc = gdn_c.shape[1]
    npair = D_GDN // PAIR
    lt = l + lc
    assert (lt // GDN_CHUNK) % GDN_GROUP == 0

    def tr(ab):
        x = jnp.stack([ab[:, :, 0:8], ab[:, :, LANES:LANES + 8]], axis=1)
        x = x.reshape(x.shape[0], npair, x.shape[2] // GDN_CHUNK, GDN_CHUNK, 8)
        return jnp.transpose(x, (0, 1, 2, 4, 3))

    nea = -jnp.exp(a_log.astype(F32)).reshape(2, npair, 2)
    dtb = dt_bias.astype(F32).reshape(2, npair, 2)
    rows = jnp.stack([jnp.transpose(nea, (1, 0, 2)).reshape(npair, 4), jnp.transpose(dtb, (1, 0, 2)).reshape(npair, 4)], axis=1)
    cst = jnp.pad(rows, ((0, 0), (0, 6), (0, LANES - 4)))
    cstt = jnp.broadcast_to(jnp.pad(rows, ((0, 0), (0, 0), (0, 4)))[..., None], (npair, 2, 8, LANES))
    cw = jnp.pad(conv_w.astype(F32), ((0, 8 - SHORT_CONV), (0, 0)))
    ng = jnp.tile(norm_g.astype(F32), 2).reshape(1, PAIR)
    nstage = 2 * GDN_GROUP * 2 * npair

    def whole(shape):
        nd = len(shape)
        return pl.BlockSpec(shape, lambda i: (0,) * nd)

    def per_batch(shape, single=False):
        nd = len(shape)
        return pl.BlockSpec((1,) + shape, lambda i: (i,) + (0,) * nd, pipeline_mode=pl.Buffered(1) if single else None)

    sq = (nstage, PAIR, PAIR)
    return pl.pallas_call(
        _gdn_body,
        grid=(b,),
        in_specs=[per_batch((l, 4 * D_GDN), True), per_batch((lc, 4 * D_GDN)), per_batch((lt, 2 * LANES), True),
                  per_batch((npair, lt // GDN_CHUNK, 8, GDN_CHUNK)),
                  whole((8, 3 * D_GDN)), whole((npair, 8, LANES)), whole((npair, 2, 8, LANES)), whole((1, PAIR))],
        out_specs=[per_batch((l, D_GDN)), per_batch((lc, D_GDN))],
        out_shape=[jax.ShapeDtypeStruct((b, l, D_GDN), BF16), jax.ShapeDtypeStruct((b, lc, D_GDN), BF16)],
        scratch_shapes=[pltpu.VMEM((lt, D_GDN), BF16)] * 3 + [pltpu.VMEM((lt, D_GDN), F32)]
        + [pltpu.VMEM((2 * npair, PAIR, PAIR), F32), pltpu.VMEM(sq, BF16), pltpu.VMEM(sq, F32),
           pltpu.VMEM((nstage, PAIR, 2 * PAIR), BF16), pltpu.VMEM(sq, F32)]
        + [pltpu.VMEM(sq, BF16)] * 4 + [pltpu.VMEM((nstage, 8, PAIR), F32)],
        compiler_params=_cparams(("arbitrary",), VMEM_LIMIT),
        name="gated_deltanet",
    )(gdn_l, gdn_c, jnp.concatenate([ab_c, ab_l], axis=1), jnp.concatenate([tr(ab_c), tr(ab_l)], axis=2), cw, cst, cstt, ng)


def _na_bias_tiles(rpb, rows):
    w = GRID_W
    ext = jnp.pad(rpb.astype(F32), ((0, 0), (0, 0), (w - NA_KW, w - NA_KW)))
    tb = jnp.stack([ext[:, :, w - 1 - cq:2 * w - 1 - cq] for cq in range(w)], axis=2)
    cq = np.arange(w)[:, None]
    ck = np.arange(w)[None, :]
    ws = np.clip(cq - NA_KW // 2, 0, w - NA_KW)
    col_ok = (ck >= ws) & (ck < ws + NA_KW)
    tb = jnp.where(jnp.asarray(col_ok)[None, None], tb, NEG_INF)
    tb = jnp.concatenate([tb, jnp.full((tb.shape[0], 1, w, w), NEG_INF, F32)], axis=1)
    nt = rows // NA_QROWS
    idx = np.zeros((3, NA_QROWS, NA_KROWS), np.int32)
    for cls, t in enumerate((0, 1, nt - 1)):
        ks = int(np.clip(NA_QROWS * t - NA_KH // 2, 0, rows - NA_KROWS))
        for rl in range(NA_QROWS):
            r = NA_QROWS * t + rl
            r0 = int(np.clip(r - NA_KH // 2, 0, rows - NA_KH))
            for j in range(NA_KROWS):
                kr = ks + j
                idx[cls, rl, j] = kr - r + NA_KH - 1 if r0 <= kr < r0 + NA_KH else 2 * NA_KH - 1
    tiles = tb[:, jnp.asarray(idx)]
    tiles = jnp.transpose(tiles, (1, 0, 2, 4, 3, 5))
    return tiles.reshape(3, tb.shape[0], NA_QROWS * w, NA_KROWS * w)


def _softmax_pv(s_parts, v_parts):
    m = functools.reduce(jnp.maximum, [jnp.max(s, axis=-1, keepdims=True) for s in s_parts])
    ps = [jnp.exp(s - m) for s in s_parts]
    den = functools.reduce(lambda x, y: x + y, [jnp.sum(p, axis=-1, keepdims=True) for p in ps])
    o = functools.reduce(lambda x, y: x + y, [_dot(p.astype(BF16), v) for p, v in zip(ps, v_parts)])
    return o / den


def _na_body(q_ref, k_ref, v_ref, qc_ref, kc_ref, vc_ref, bias_ref, o_ref, oc_ref, *, rows):
    w = GRID_W
    tq = NA_QROWS * w
    tk = NA_KROWS * w
    nt = rows // NA_QROWS
    scale = HEAD_DIM ** -0.5
    kc = kc_ref[0]
    vc = vc_ref[0]
    m0, m1 = _head_masks((tq, PAIR))
    masks = (m0, m1)

    def tile(t, _):
        cls = jnp.where(t > 0, 1, 0) + jnp.where(t == nt - 1, 1, 0)
        ks = jnp.clip(NA_QROWS * t - NA_KH // 2, 0, rows - NA_KROWS) * w
        ksl = pl.ds(pl.multiple_of(ks, w), tk)
        qsl = pl.ds(pl.multiple_of(t * tq, tq), tq)
        q = q_ref[0, qsl, :] * scale
        kb = k_ref[0, ksl, :]
        vb = v_ref[0, ksl, :]
        acc = jnp.zeros((tq, PAIR), F32)
        for hh in range(2):
            qm = jnp.where(masks[hh], q, jnp.zeros_like(q))
            s_loc = _dot_nt(qm, kb) + bias_ref[cls, hh]
            s_ctx = _dot_nt(qm, kc)
            acc = acc + jnp.where(masks[hh], _softmax_pv([s_loc, s_ctx], [vb, vc]), 0.0)
        o_ref[0, qsl, :] = acc.astype(o_ref.dtype)
        return 0

    lax.fori_loop(0, nt, tile, 0)
    qc = qc_ref[0] * scale
    mc0, mc1 = _head_masks(qc.shape)
    accc = jnp.zeros(qc.shape, F32)
    for hh, mk in enumerate((mc0, mc1)):
        qm = jnp.where(mk, qc, jnp.zeros_like(qc))
        accc = accc + jnp.where(mk, _softmax_pv([_dot_nt(qm, kc)], [vc]), 0.0)
    oc_ref[0] = accc.astype(oc_ref.dtype)


def _neighbourhood_attention(na_l, na_c, rpb):
    b, l, _ = na_l.shape
    lc = na_c.shape[1]
    rows = l // GRID_W
    npair = D_NA // PAIR
    bias = _na_bias_tiles(rpb, rows)

    def col(k):
        return lambda p, i: (i, 0, k * npair + p)

    lat = [pl.BlockSpec((1, l, PAIR), col(k)) for k in range(3)]
    ctx = [pl.BlockSpec((1, lc, PAIR), col(k)) for k in range(3)]
    return pl.pallas_call(
        functools.partial(_na_body, rows=rows),
        grid=(npair, b),
        in_specs=lat + ctx + [pl.BlockSpec((3, 2, NA_QROWS * GRID_W, NA_KROWS * GRID_W), lambda p, i: (0, p, 0, 0))],
        out_specs=[pl.BlockSpec((1, l, PAIR), lambda p, i: (i, 0, p)),
                   pl.BlockSpec((1, lc, PAIR), lambda p, i: (i, 0, p))],
        out_shape=[jax.ShapeDtypeStruct((b, l, D_NA), BF16), jax.ShapeDtypeStruct((b, lc, D_NA), BF16)],
        compiler_params=_cparams(("arbitrary", "arbitrary"), VMEM_LIMIT),
        name="neighbourhood_attention",
    )(na_l, na_l, na_l, na_c, na_c, na_c, bias)


def _outproj_residual(r_ref, g_ref, n_ref, x_ref, mod_ref, wo_ref):
    mix = jnp.concatenate([r_ref[0], g_ref[0], n_ref[0]], axis=1)
    return x_ref[0] + mod_ref[0, 2:3, :] * _dot(mix, wo_ref[...])


def _swiglu_chunks(hb, w1_ref, w3_ref, w2_ref, acc_ref):
    acc_ref[...] = jnp.zeros_like(acc_ref)

    def step(j, _):
        t = (_silu(_dot(hb, w1_ref[j])) * _dot(hb, w3_ref[j])).astype(BF16)
        acc_ref[...] += _dot(t, w2_ref[j])
        return 0

    lax.fori_loop(0, w1_ref.shape[0], step, 0)


def _dense_body(r_ref, g_ref, n_ref, x_ref, mod_ref, g2_ref, wo_ref, w1_ref, w3_ref, w2_ref, o_ref, acc_ref):
    x1 = _outproj_residual(r_ref, g_ref, n_ref, x_ref, mod_ref, wo_ref)
    hb = _norm_mod(x1, g2_ref[...], mod_ref[0, 3:4, :], mod_ref[0, 4:5, :]).astype(BF16)
    _swiglu_chunks(hb, w1_ref, w3_ref, w2_ref, acc_ref)
    o_ref[0] = x1 + mod_ref[0, 5:6, :] * acc_ref[...]


def _router_body(r_ref, g_ref, n_ref, x_ref, mod_ref, g2_ref, wo_ref, wr_ref, x1_ref, h_ref, lg_ref):
    x1 = _outproj_residual(r_ref, g_ref, n_ref, x_ref, mod_ref, wo_ref)
    h = _norm_mod(x1, g2_ref[...], mod_ref[0, 3:4, :], mod_ref[0, 4:5, :])
    x1_ref[0] = x1
    h_ref[0] = h
    lg_ref[0] = _dot_hi(h, wr_ref[...])


def _resident(shape):
    nd = len(shape)
    return pl.BlockSpec(shape, lambda i, j: (0,) * nd, pipeline_mode=pl.Buffered(1))


def _mixer_specs(tm, d):
    return [pl.BlockSpec((1, tm, D_RET), lambda i, j: (i, j, 0)),
            pl.BlockSpec((1, tm, D_GDN), lambda i, j: (i, j, 0)),
            pl.BlockSpec((1, tm, D_NA), lambda i, j: (i, j, 0)),
            pl.BlockSpec((1, tm, d), lambda i, j: (i, j, 0)),
            pl.BlockSpec((1, 6, d), lambda i, j: (i, 0, 0)),
            pl.BlockSpec((1, d), lambda i, j: (0, 0))]


def _chunk_ffn_weights(w1, w3, w2):
    d, dff = w1.shape
    nch = dff // FF_CHUNK
    w1c = jnp.transpose(w1.reshape(d, nch, FF_CHUNK), (1, 0, 2)).astype(BF16)
    w3c = jnp.transpose(w3.reshape(d, nch, FF_CHUNK), (1, 0, 2)).astype(BF16)
    w2c = w2.reshape(nch, FF_CHUNK, d).astype(BF16)
    return w1c, w3c, w2c


def _dense_block(r, g, n, x, mod, g2, wo, w1c, w3c, w2c):
    b, l, d = x.shape
    tm = min(512, l)
    nch = w1c.shape[0]
    return pl.pallas_call(
        _dense_body,
        grid=(b, l // tm),
        in_specs=_mixer_specs(tm, d) + [_resident((d, d)), _resident((nch, d, FF_CHUNK)),
                                        _resident((nch, d, FF_CHUNK)), _resident((nch, FF_CHUNK, d))],
        out_specs=pl.BlockSpec((1, tm, d), lambda i, j: (i, j, 0)),
        out_shape=jax.ShapeDtypeStruct((b, l, d), F32),
        scratch_shapes=[pltpu.VMEM((tm, d), F32)],
        compiler_params=_cparams(("arbitrary", "arbitrary"), VMEM_LIMIT),
        name="outproj_swiglu",
    )(r, g, n, x, mod, g2, wo, w1c, w3c, w2c)


def _router_block(r, g, n, x, mod, g2, wo, wr):
    b, l, d = x.shape
    tm = min(512, l)
    blk = pl.BlockSpec((1, tm, d), lambda i, j: (i, j, 0))
    return pl.pallas_call(
        _router_body,
        grid=(b, l // tm),
        in_specs=_mixer_specs(tm, d) + [_resident((d, d)), _resident((d, LANES))],
        out_specs=[blk, blk, pl.BlockSpec((1, tm, LANES), lambda i, j: (i, j, 0))],
        out_shape=[jax.ShapeDtypeStruct((b, l, d), F32), jax.ShapeDtypeStruct((b, l, d), F32),
                   jax.ShapeDtypeStruct((b, l, LANES), F32)],
        compiler_params=_cparams(("arbitrary", "arbitrary"), VMEM_LIMIT),
        name="outproj_router",
    )(r, g, n, x, mod, g2, wo, wr)


def _gather_rows(idx_ref, src_hbm, dst_ref, sem, n):
    def issue(r, _):
        pltpu.make_async_copy(src_hbm.at[pl.ds(idx_ref[0, 0, r], 1), :], dst_ref.at[pl.ds(r, 1), :], sem).start()
        return 0

    lax.fori_loop(0, n, issue, 0)
    pltpu.make_async_copy(src_hbm.at[pl.ds(0, n), :], dst_ref, sem).wait()


def _moe_body(bexp_ref, bval_ref, tok_ref, h_hbm, w1_ref, w3_ref, w2_ref, y_ref, xbuf, acc_ref, sem):
    i = pl.program_id(0)

    @pl.when(bval_ref[i] == 1)
    def _():
        _gather_rows(tok_ref, h_hbm, xbuf, sem, xbuf.shape[0])
        _swiglu_chunks(xbuf[...].astype(BF16), w1_ref.at[0], w3_ref.at[0], w2_ref.at[0], acc_ref)
        y_ref[...] = acc_ref[...]

    @pl.when(bval_ref[i] == 0)
    def _():
        y_ref[...] = jnp.zeros_like(y_ref)


def _moe_experts(h_rows, row_tok, block_expert, block_valid, w1c, w3c, w2c):
    n_tok, d = h_rows.shape
    nb = block_expert.shape[0]
    nch = w1c.shape[1]
    tb = MOE_ROWS
    grid_spec = pltpu.PrefetchScalarGridSpec(
        num_scalar_prefetch=2,
        grid=(nb,),
        in_specs=[pl.BlockSpec((1, 1, tb), lambda i, be, bv: (i, 0, 0), memory_space=pltpu.SMEM),
                  pl.BlockSpec(memory_space=pl.ANY),
                  pl.BlockSpec((1, nch, d, FF_CHUNK), lambda i, be, bv: (be[i], 0, 0, 0)),
                  pl.BlockSpec((1, nch, d, FF_CHUNK), lambda i, be, bv: (be[i], 0, 0, 0)),
                  pl.BlockSpec((1, nch, FF_CHUNK, d), lambda i, be, bv: (be[i], 0, 0, 0))],
        out_specs=pl.BlockSpec((tb, d), lambda i, be, bv: (i, 0)),
        scratch_shapes=[pltpu.VMEM((tb, d), F32), pltpu.VMEM((tb, d), F32), pltpu.SemaphoreType.DMA(())],
    )
    return pl.pallas_call(
        _moe_body,
        grid_spec=grid_spec,
        out_shape=jax.ShapeDtypeStruct((nb * tb, d), F32),
        compiler_params=_cparams(("arbitrary",), VMEM_LIMIT),
        name="moe_experts",
    )(block_expert, block_valid, row_tok.reshape(nb, 1, tb), h_rows, w1c, w3c, w2c)


def _combine_body(pos_ref, y_hbm, x1_ref, gate_ref, gm_ref, fg_ref, o_ref, ybuf, sem, *, final):
    tm = x1_ref.shape[0]
    _gather_rows(pos_ref, y_hbm, ybuf, sem, 2 * tm)
    moe = gate_ref[:, 0:1] * ybuf[0:tm, :] + gate_ref[:, 1:2] * ybuf[tm:, :]
    x2 = x1_ref[...] + gm_ref[0] * moe
    if final:
        ms = jnp.mean(x2 * x2, axis=-1, keepdims=True)
        x2 = x2 * lax.rsqrt(ms + EPS) * fg_ref[...]
    o_ref[...] = x2


def _moe_combine(y_rows, pos, x1, gates, gate_mlp, final_g, final):
    b, l, d = x1.shape
    n_tok = b * l
    tm = 256
    nt = n_tok // tm
    per_b = l // tm
    pos_t = jnp.transpose(pos.reshape(nt, tm, 2), (0, 2, 1)).reshape(nt, 1, 2 * tm)
    out = pl.pallas_call(
        functools.partial(_combine_body, final=final),
        grid=(nt,),
        in_specs=[pl.BlockSpec((1, 1, 2 * tm), lambda i: (i, 0, 0), memory_space=pltpu.SMEM),
                  pl.BlockSpec(memory_space=pl.ANY),
                  pl.BlockSpec((tm, d), lambda i: (i, 0)),
                  pl.BlockSpec((tm, 2), lambda i: (i, 0)),
                  pl.BlockSpec((1, 1, d), lambda i: (i // per_b, 0, 0)),
                  pl.BlockSpec((1, d), lambda i: (0, 0))],
        out_specs=pl.BlockSpec((tm, d), lambda i: (i, 0)),
        out_shape=jax.ShapeDtypeStruct((n_tok, d), F32),
        scratch_shapes=[pltpu.VMEM((2 * tm, d), F32), pltpu.SemaphoreType.DMA(())],
        compiler_params=_cparams(("arbitrary",), VMEM_LIMIT),
        name="moe_combine_final_norm",
    )(pos_t, y_rows, x1.reshape(n_tok, d), gates, gate_mlp, final_g.reshape(1, d))
    return out.reshape(b, l, d)


def _route(logits):
    n_tok = logits.shape[0]
    tb = MOE_ROWS
    top_logit, top_e = lax.top_k(logits[:, :N_EXPERTS], TOP_K)
    gates = jax.nn.softmax(top_logit, axis=-1)
    e_flat = top_e.reshape(-1).astype(jnp.int32)
    onehot = (e_flat[:, None] == jnp.arange(N_EXPERTS, dtype=jnp.int32)[None, :]).astype(jnp.int32)
    rank = jnp.sum((jnp.cumsum(onehot, axis=0) - onehot) * onehot, axis=1)
    counts = jnp.sum(onehot, axis=0)
    padded = (counts + tb - 1) // tb * tb
    pad_ends = jnp.cumsum(padded)
    pad_starts = pad_ends - padded
    dest = pad_starts[e_flat] + rank
    nb = (n_tok * TOP_K + tb - 1) // tb + N_EXPERTS
    tok_flat = jnp.repeat(jnp.arange(n_tok, dtype=jnp.int32), TOP_K)
    row_tok = jnp.zeros((nb * tb,), jnp.int32).at[dest].set(tok_flat)
    block_start = jnp.arange(nb, dtype=jnp.int32) * tb
    block_expert = jnp.minimum(jnp.searchsorted(pad_ends, block_start, side='right'), N_EXPERTS - 1).astype(jnp.int32)
    block_valid = (block_start < pad_ends[-1]).astype(jnp.int32)
    last_e = block_expert[jnp.maximum(pad_ends[-1] // tb - 1, 0)]
    block_expert = jnp.where(block_valid == 1, block_expert, last_e)
    return row_tok, dest.reshape(n_tok, TOP_K), gates, block_expert, block_valid


def _final_norm_body(x_ref, g_ref, o_ref):
    x = x_ref[...]
    ms = jnp.mean(x * x, axis=-1, keepdims=True)
    o_ref[...] = x * lax.rsqrt(ms + EPS) * g_ref[...]


def _final_norm(x, g):
    b, l, d = x.shape
    n = b * l
    tm = min(512, n)
    out = pl.pallas_call(
        _final_norm_body,
        grid=(n // tm,),
        in_specs=[pl.BlockSpec((tm, d), lambda i: (i, 0)), pl.BlockSpec((1, d), lambda i: (0, 0))],
        out_specs=pl.BlockSpec((tm, d), lambda i: (i, 0)),
        out_shape=jax.ShapeDtypeStruct((n, d), F32),
        compiler_params=_cparams(("arbitrary",)),
        name="final_norm",
    )(x.reshape(n, d), g.reshape(1, d))
    return out.reshape(b, l, d)


def _rope_tables(n_tok):
    t = jnp.arange(n_tok, dtype=jnp.int32)
    row = (t // GRID_W).astype(F32)
    col = (t % GRID_W).astype(F32)
    inv_freq = ROPE_BASE ** (-jnp.arange(N_FREQ, dtype=F32) / N_FREQ)
    ang = jnp.concatenate([row[:, None] * inv_freq, col[:, None] * inv_freq], axis=-1)
    cos, sin = jnp.cos(ang), jnp.sin(ang)
    cosf = jnp.concatenate([cos, cos, cos, cos], axis=-1)
    sins = jnp.concatenate([-sin, sin, -sin, sin], axis=-1)
    return cosf, sins


def kernel(x, c, ctx, c_ctx, ada_w, ada_b, norm1_g, norm2_g, w_in, w_out, conv_w, ret_decay, gdn_a_log, gdn_dt_bias,
           gdn_norm_g, na_rpb, ffn_w1, ffn_w3, ffn_w2, moe_router, moe_w1, moe_w3, moe_w2, final_g):
    b, l, d = x.shape
    lc = ctx.shape[1]
    depth = ada_w.shape[0]
    cosf, sins = _rope_tables(l)
    ones_c = jnp.ones((lc, LANES), F32)
    zeros_c = jnp.zeros((lc, LANES), F32)

    rows = ((b + 1 + 7) // 8) * 8
    c_all = jnp.zeros((rows, d), F32).at[:b].set(c).at[b].set(c_ctx)
    mod = _ada_vectors(c_all, ada_w, ada_b).reshape(depth, rows, 6, d)

    y = ctx
    for layer in range(depth):
        need_ctx = layer < depth - 1
        mod_l = mod[layer, :b]
        mod_c = jnp.broadcast_to(mod[layer, b][None], (b, 6, d))
        w1p = _pack_w_in(w_in[layer])
        g1 = norm1_g[layer].reshape(1, d)
        ret_l, gdn_l, ab_l, na_l = _in_projection(x, mod_l, g1, cosf, sins, w1p, rope=True)
        ret_c, gdn_c, ab_c, na_c = _in_projection(y, mod_c, g1, ones_c, zeros_c, w1p, rope=False)

        log_gamma = jnp.log1p(-jnp.exp2(-ret_decay[layer].astype(F32)))
        r_l, r_c = _retention(ret_l, ret_c, log_gamma)
        g_l, g_c = _gdn(gdn_l, ab_l, gdn_c, ab_c, conv_w[layer], gdn_a_log[layer], gdn_dt_bias[layer], gdn_norm_g[layer])
        n_l, n_c = _neighbourhood_attention(na_l, na_c, na_rpb[layer])

        wo = w_out[layer].astype(BF16)
        g2 = norm2_g[layer].reshape(1, d)
        j = layer // 2
        if layer % 2 == 0:
            w1c, w3c, w2c = _chunk_ffn_weights(ffn_w1[j], ffn_w3[j], ffn_w2[j])
            x = _dense_block(r_l, g_l, n_l, x, mod_l, g2, wo, w1c, w3c, w2c)
            if need_ctx:
                y = _dense_block(r_c, g_c, n_c, y, mod_c, g2, wo, w1c, w3c, w2c)
            if layer == depth - 1:
                x = _final_norm(x, final_g)
        else:
            wr = jnp.pad(moe_router[j].astype(F32), ((0, 0), (0, LANES - N_EXPERTS)))
            ws = [_chunk_ffn_weights(moe_w1[j][e], moe_w3[j][e], moe_w2[j][e]) for e in range(N_EXPERTS)]
            w1c, w3c, w2c = (jnp.stack([w[k] for w in ws]) for k in range(3))

            def moe_ffn(r, g, n, xin, m, last):
                bb, ll, _ = xin.shape
                x1, h, logits = _router_block(r, g, n, xin, m, g2, wo, wr)
                row_tok, pos, gates, bexp, bval = _route(logits.reshape(bb * ll, LANES))
                y_rows = _moe_experts(h.reshape(bb * ll, d), row_tok, bexp, bval, w1c, w3c, w2c)
                return _moe_combine(y_rows, pos, x1, gates, m[:, 5:6, :], final_g, last)

            x = moe_ffn(r_l, g_l, n_l, x, mod_l, layer == depth - 1)
            if need_ctx:
                y = moe_ffn(r_c, g_c, n_c, y, mod_c, False)
    return x
```

```python
import functools

import numpy as np
import jax
import jax.numpy as jnp
from jax import lax
from jax.experimental import pallas as pl
from jax.experimental.pallas import tpu as pltpu

F32 = jnp.float32
BF16 = jnp.bfloat16
HIGHEST = lax.Precision.HIGHEST

LANES = 128
HEAD_DIM = 64
PAIR = 2 * HEAD_DIM
GRID_W = 64
H_RET, H_GDN, H_NA = 4, 4, 8
D_RET, D_GDN, D_NA = H_RET * HEAD_DIM, H_GDN * HEAD_DIM, H_NA * HEAD_DIM
RET_CHUNK = 128
GDN_CHUNK = 64
GDN_GROUP = 4
GDN_PREP_ROWS = 512
SHORT_CONV = 5
NA_KH, NA_KW = 8, 16
NA_QROWS = 4
NA_KROWS = NA_QROWS + NA_KH
N_FREQ = HEAD_DIM // 4
ROPE_BASE = 10000.0
N_EXPERTS = 8
TOP_K = 2
MOE_ROWS = 512
FF_CHUNK = 256
EPS = 1e-6
NEG_INF = -1e30
VMEM_LIMIT = 56 * 1024 * 1024

C_RET = 0
C_GDN = C_RET + 4 * D_RET
C_AB = C_GDN + 4 * D_GDN
C_NA = C_AB + 2 * LANES
C_END = C_NA + 3 * D_NA


def _cparams(sem, vmem=None):
    return pltpu.CompilerParams(dimension_semantics=sem, vmem_limit_bytes=vmem)


def _silu(x):
    return x * jax.nn.sigmoid(x)


def _dot(a, b):
    return jnp.dot(a, b, preferred_element_type=F32)


def _dot_nt(a, b):
    return lax.dot_general(a, b, (((1,), (1,)), ((), ())), preferred_element_type=F32)


def _dot_tn(a, b):
    return lax.dot_general(a, b, (((0,), (0,)), ((), ())), preferred_element_type=F32)


def _dot_hi(a, b):
    return jnp.dot(a, b, preferred_element_type=F32, precision=HIGHEST)


def _ada_body(c_ref, w_ref, b_ref, o_ref):
    s = _silu(c_ref[...])
    o_ref[0] = _dot_hi(s, w_ref[0]) + b_ref[0]


def _ada_vectors(c_all, ada_w, ada_b):
    depth, d, d6 = ada_w.shape
    rows = c_all.shape[0]
    tn = 1024
    return pl.pallas_call(
        _ada_body,
        grid=(depth, d6 // tn),
        in_specs=[pl.BlockSpec((rows, d), lambda l, j: (0, 0)),
                  pl.BlockSpec((1, d, tn), lambda l, j: (l, 0, j)),
                  pl.BlockSpec((1, 1, tn), lambda l, j: (l, 0, j))],
        out_specs=pl.BlockSpec((1, rows, tn), lambda l, j: (l, 0, j)),
        out_shape=jax.ShapeDtypeStruct((depth, rows, d6), F32),
        compiler_params=_cparams(("arbitrary", "arbitrary")),
        name="ada_vectors",
    )(c_all, ada_w, ada_b.reshape(depth, 1, d6))


def _norm_mod(x, g, shift, scale):
    ms = jnp.mean(x * x, axis=-1, keepdims=True)
    return (x * lax.rsqrt(ms + EPS) * g) * (1.0 + scale) + shift


def _rope_slab(t, cosf, sins):
    lane = lax.broadcasted_iota(jnp.int32, t.shape, 1)
    first = (lane % HEAD_DIM) < (HEAD_DIM // 2)
    partner = jnp.where(first, pltpu.roll(t, LANES - HEAD_DIM // 2, 1), pltpu.roll(t, HEAD_DIM // 2, 1))
    return t * cosf + partner * sins


def _inproj_body(x_ref, mod_ref, g_ref, cos_ref, sin_ref, w_ref, ret_ref, gdn_ref, ab_ref, na_ref, *, rope):
    h = _norm_mod(x_ref[0], g_ref[...], mod_ref[0, 0:1, :], mod_ref[0, 1:2, :]).astype(BF16)
    qk = _dot(h, w_ref[:, C_RET:C_RET + 2 * D_RET])
    slabs = []
    for s in range(2 * D_RET // LANES):
        t = qk[:, s * LANES:(s + 1) * LANES]
        if rope:
            t = _rope_slab(t, cos_ref[...], sin_ref[...])
        if s >= D_RET // LANES:
            t = t * HEAD_DIM ** -0.5
        slabs.append(t)
    ret_ref[0, :, 0:2 * D_RET] = jnp.concatenate(slabs, axis=1).astype(BF16)
    ret_ref[0, :, 2 * D_RET:] = _dot(h, w_ref[:, C_RET + 2 * D_RET:C_GDN]).astype(BF16)
    for j in range(2):
        gdn_ref[0, :, 512 * j:512 * (j + 1)] = _dot(h, w_ref[:, C_GDN + 512 * j:C_GDN + 512 * (j + 1)]).astype(BF16)
    ab_ref[0] = _dot(h, w_ref[:, C_AB:C_NA])
    for j in range(3):
        na_ref[0, :, 512 * j:512 * (j + 1)] = _dot(h, w_ref[:, C_NA + 512 * j:C_NA + 512 * (j + 1)]).astype(BF16)


def _in_projection(x, mod, g, cosf, sins, w, rope):
    b, l, d = x.shape
    tm = min(512, l)
    body = functools.partial(_inproj_body, rope=rope)
    return pl.pallas_call(
        body,
        grid=(b, l // tm),
        in_specs=[pl.BlockSpec((1, tm, d), lambda i, j: (i, j, 0)),
                  pl.BlockSpec((1, 6, d), lambda i, j: (i, 0, 0)),
                  pl.BlockSpec((1, d), lambda i, j: (0, 0)),
                  pl.BlockSpec((tm, LANES), lambda i, j: (j, 0)),
                  pl.BlockSpec((tm, LANES), lambda i, j: (j, 0)),
                  pl.BlockSpec((d, C_END), lambda i, j: (0, 0))],
        out_specs=[pl.BlockSpec((1, tm, 4 * D_RET), lambda i, j: (i, j, 0)),
                   pl.BlockSpec((1, tm, 4 * D_GDN), lambda i, j: (i, j, 0)),
                   pl.BlockSpec((1, tm, 2 * LANES), lambda i, j: (i, j, 0)),
                   pl.BlockSpec((1, tm, 3 * D_NA), lambda i, j: (i, j, 0))],
        out_shape=[jax.ShapeDtypeStruct((b, l, 4 * D_RET), BF16),
                   jax.ShapeDtypeStruct((b, l, 4 * D_GDN), BF16),
                   jax.ShapeDtypeStruct((b, l, 2 * LANES), F32),
                   jax.ShapeDtypeStruct((b, l, 3 * D_NA), BF16)],
        compiler_params=_cparams(("arbitrary", "arbitrary"), VMEM_LIMIT),
        name="in_projection",
    )(x, mod, g, cosf, sins, w)


def _pack_w_in(w_in, conv_cols=None):
    d = w_in.shape[0]
    c1 = 4 * D_RET
    c2 = c1 + 4 * D_GDN
    ab = w_in[:, c2:c2 + 4 * H_GDN]
    ab = ab.reshape(d, 2, 2, H_GDN // 2, 2)
    ab = jnp.transpose(ab, (0, 3, 1, 2, 4)).reshape(d, 2, 8)
    ab = jnp.pad(ab, ((0, 0), (0, 0), (0, LANES - 8))).reshape(d, 2 * LANES)
    return jnp.concatenate([w_in[:, :c2], ab, w_in[:, c2 + 4 * H_GDN:]], axis=1).astype(BF16)


def _head_masks(shape):
    lane = lax.broadcasted_iota(jnp.int32, shape, len(shape) - 1)
    return lane < HEAD_DIM, lane >= HEAD_DIM


def _per_head(lo, hi, shape):
    m0, _ = _head_masks(shape)
    return jnp.where(m0, lo, hi)


def _head_sumsq(o):
    m0, m1 = _head_masks(o.shape)
    sq = o * o
    s0 = jnp.sum(jnp.where(m0, sq, 0.0), axis=-1, keepdims=True)
    s1 = jnp.sum(jnp.where(m1, sq, 0.0), axis=-1, keepdims=True)
    return jnp.where(m0, s0, s1)


def _ret_body(lg_ref, ql, kl, vl, gl, qc, kc, vc, gc, r_ref, rc_ref, sb_ref):
    c = RET_CHUNK
    p = pl.program_id(1)
    lane_shape = (c, PAIR)
    pos = lax.broadcasted_iota(jnp.int32, lane_shape, 0).astype(F32)
    lgf = _per_head(lg_ref[0, 2 * p], lg_ref[0, 2 * p + 1], lane_shape)
    lgb = _per_head(lg_ref[1, 2 * p], lg_ref[1, 2 * p + 1], lane_shape)
    qdf = jnp.exp(lgf * (pos + 1.0))
    kdf = jnp.exp(lgf * (c - 1.0 - pos))
    qdb = jnp.exp(lgb * (c - pos))
    kdb = jnp.exp(lgb * pos)
    cdf = jnp.exp(lgf[0:1] * c)
    cdb = jnp.exp(lgb[0:1] * c)
    ii = lax.broadcasted_iota(jnp.int32, (c, c), 0)
    jj = lax.broadcasted_iota(jnp.int32, (c, c), 1)
    diff = (ii - jj).astype(F32)
    dmats = [jnp.where(diff > 0, jnp.exp(lg_ref[0, 2 * p + hh] * diff),
                       jnp.where(diff < 0, jnp.exp(-lg_ref[1, 2 * p + hh] * diff), 2.0)) for hh in range(2)]
    m0, m1 = _head_masks(lane_shape)
    masks = (m0, m1)
    bi = lax.broadcasted_iota(jnp.int32, (PAIR, PAIR), 0) // HEAD_DIM
    bj = lax.broadcasted_iota(jnp.int32, (PAIR, PAIR), 1) // HEAD_DIM
    bd = bi == bj

    def sweep(q_ref, k_ref, v_ref, g_ref, o_ref, n, sf0, sb0):
        def bstep(t, sb):
            ci = n - 1 - t
            sl = pl.ds(pl.multiple_of(ci * c, c), c)
            sb_ref[ci] = sb
            kd = (k_ref[0, sl, :].astype(F32) * kdb).astype(BF16)
            return sb * cdb + jnp.where(bd, _dot_tn(kd, v_ref[0, sl, :]), 0.0)

        sb_fin = lax.fori_loop(0, n, bstep, sb0)

        def fstep(ci, sf):
            sl = pl.ds(pl.multiple_of(ci * c, c), c)
            q = q_ref[0, sl, :]
            k = k_ref[0, sl, :]
            v = v_ref[0, sl, :]
            qf = q.astype(F32)
            o = _dot((qf * qdf).astype(BF16), sf.astype(BF16)) + _dot((qf * qdb).astype(BF16), sb_ref[ci].astype(BF16))
            for hh in range(2):
                qm = jnp.where(masks[hh], q, jnp.zeros_like(q))
                pm = (_dot_nt(qm, k) * dmats[hh]).astype(BF16)
                o = o + jnp.where(masks[hh], _dot(pm, v), 0.0)
            on = o * lax.rsqrt(_head_sumsq(o) * (1.0 / HEAD_DIM) + EPS)
            o_ref[0, sl, :] = (on * _silu(g_ref[0, sl, :].astype(F32))).astype(o_ref.dtype)
            kd = (k.astype(F32) * kdf).astype(BF16)
            return sf * cdf + jnp.where(bd, _dot_tn(kd, v), 0.0)

        sf_fin = lax.fori_loop(0, n, fstep, sf0)
        return sf_fin, sb_fin

    z = jnp.zeros((PAIR, PAIR), F32)
    sfc, sbc = sweep(qc, kc, vc, gc, rc_ref, qc.shape[1] // c, z, z)
    sweep(ql, kl, vl, gl, r_ref, ql.shape[1] // c, sfc, sbc)


def _retention(ret_l, ret_c, log_gamma):
    b, l, _ = ret_l.shape
    lc = ret_c.shape[1]
    npair = D_RET // PAIR

    def col(k):
        return lambda i, p: (i, 0, k * npair + p)

    lat = [pl.BlockSpec((1, l, PAIR), col(k)) for k in range(4)]
    ctx = [pl.BlockSpec((1, lc, PAIR), col(k)) for k in range(4)]
    return pl.pallas_call(
        _ret_body,
        grid=(b, npair),
        in_specs=[pl.BlockSpec(memory_space=pltpu.SMEM)] + lat + ctx,
        out_specs=[pl.BlockSpec((1, l, PAIR), lambda i, p: (i, 0, p)),
                   pl.BlockSpec((1, lc, PAIR), lambda i, p: (i, 0, p))],
        out_shape=[jax.ShapeDtypeStruct((b, l, D_RET), BF16), jax.ShapeDtypeStruct((b, lc, D_RET), BF16)],
        scratch_shapes=[pltpu.VMEM((max(l, lc) // RET_CHUNK, PAIR, PAIR), F32)],
        compiler_params=_cparams(("arbitrary", "arbitrary"), VMEM_LIMIT),
        name="retention",
    )(log_gamma, ret_l, ret_l, ret_l, ret_l, ret_c, ret_c, ret_c, ret_c)


def _stack_heads(t):
    m0, m1 = _head_masks(t.shape)
    z = jnp.zeros_like(t)
    return jnp.concatenate([jnp.where(m0, t, z), jnp.where(m1, t, z)], axis=0)


def _gdn_prep(src_ref, col, cw, dst_ref, r0, *, l2, scale):
    n = src_ref.shape[1]
    blk = min(GDN_PREP_ROWS, n)
    halo = 16
    lanes = slice(col % D_GDN, col % D_GDN + LANES)

    def block(i, _):
        b0 = pl.multiple_of(i * blk, blk)
        x = src_ref[0, pl.ds(b0, blk), col:col + LANES].astype(F32)
        lo = src_ref[0, pl.ds(pl.multiple_of(jnp.maximum(b0 - halo, 0), halo), halo), col:col + LANES].astype(F32)
        hi = src_ref[0, pl.ds(pl.multiple_of(jnp.minimum(b0 + blk, n - halo), halo), halo), col:col + LANES].astype(F32)
        lo = jnp.where(b0 > 0, lo, 0.0)
        hi = jnp.where(b0 + blk < n, hi, 0.0)
        ext = jnp.concatenate([lo, x, hi], axis=0)
        acc = x * cw[SHORT_CONV // 2:SHORT_CONV // 2 + 1, :]
        for j in range(SHORT_CONV):
            s = j - SHORT_CONV // 2
            if s != 0:
                sh = pltpu.roll(ext, (-s) % (blk + 2 * halo), 0)
                acc = acc + sh[halo:halo + blk] * cw[j:j + 1, :]
        y = _silu(acc)
        if l2:
            y = y * lax.rsqrt(_head_sumsq(y) + EPS)
        if scale != 1.0:
            y = y * scale
        dst_ref[pl.ds(pl.multiple_of(r0 + b0, halo), blk), lanes] = y.astype(dst_ref.dtype)
        return 0

    lax.fori_loop(0, n // blk, block, 0)


def _gdn_body(x_ref, xc_ref, ab_ref, abt_ref, cw_ref, cst_ref, cstt_ref, ng_ref,
              o_ref, oc_ref, qn, kn, vn, oacc, st_s, x_s, t_s, rhs_s, u_s, w_s, at_s, qg_s, kd_s, egl_s):
    c = GDN_CHUNK
    c2 = 2 * c
    grp = GDN_GROUP
    npair = D_GDN // PAIR
    lc = xc_ref.shape[1]
    l = x_ref.shape[1]
    nc = lc // c
    nt = (lc + l) // c
    ii = lax.broadcasted_iota(jnp.int32, (c2, c2), 0)
    jj = lax.broadcasted_iota(jnp.int32, (c2, c2), 1)
    same = (ii // c) == (jj // c)
    eye = (ii == jj).astype(F32)
    ti = lax.broadcasted_iota(jnp.int32, (c, c), 0)
    tj = lax.broadcasted_iota(jnp.int32, (c, c), 1)
    lower = (ti >= tj).astype(F32)
    upper = (ti <= tj).astype(F32)
    dirs = ((same & (ii >= jj), same & (ii > jj), lower, upper, c - 1),
            (same & (ii <= jj), same & (ii < jj), upper, lower, 0))

    def chains(g, slot):
        out = []
        for gi in range(grp):
            t = g * grp + gi
            cf = t
            cb = jnp.where(t < nc, nc - 1 - t, nt + nc - 1 - t)
            for p in range(npair):
                for d in range(2):
                    out.append((((slot * grp + gi) * 2 + d) * npair + p, p, d, cf if d == 0 else cb))
        return out

    def stage_inputs(idx, p, d, ci):
        incl, strict, tri_col, tri_row, last = dirs[d]
        lanes = slice(p * PAIR, (p + 1) * PAIR)
        nea, dtb = cst_ref[p, 0:1, :], cst_ref[p, 1:2, :]
        neat, dtbt = cstt_ref[p, 0], cstt_ref[p, 1]
        sl = pl.ds(pl.multiple_of(ci * c, c), c)
        abv = ab_ref[0, sl, p * LANES:(p + 1) * LANES]
        gall = nea * jax.nn.softplus(abv + dtb)
        ball = jax.nn.sigmoid(abv)
        shape = (c, PAIR)
        gl = _per_head(gall[:, 2 * d:2 * d + 1], gall[:, 2 * d + 1:2 * d + 2], shape)
        bl = _per_head(ball[:, 4 + 2 * d:5 + 2 * d], ball[:, 5 + 2 * d:6 + 2 * d], shape)
        gcum = _dot_hi(tri_col, gl)
        gt = neat[:, 0:c] * jax.nn.softplus(abt_ref[0, p, ci] + dtbt[:, 0:c])
        gtc = _dot_hi(gt, tri_row)
        grow = jnp.concatenate([gtc[2 * d:2 * d + 1, :], gtc[2 * d + 1:2 * d + 2, :]], axis=1)
        gcol = jnp.concatenate([gcum[:, 0:1], gcum[:, HEAD_DIM:HEAD_DIM + 1]], axis=0)
        dec = jnp.where(incl, jnp.exp(jnp.where(incl, gcol - grow, 0.0)), 0.0)
        dec = jnp.where(ii == jj, 1.0, dec)
        q = qn[sl, lanes]
        k = kn[sl, lanes]
        v = vn[sl, lanes]
        kf = k.astype(F32)
        eg = jnp.exp(gcum)
        kb = kf * bl
        k_st = _stack_heads(k)
        a = jnp.where(strict, _dot_nt(_stack_heads(kb.astype(BF16)), k_st) * dec, 0.0)
        at_s[idx] = (_dot_nt(_stack_heads(q), k_st) * dec).astype(BF16)
        x_s[idx] = (-a).astype(BF16)
        t_s[idx] = eye - a
        rhs_s[idx] = jnp.concatenate([_stack_heads((v.astype(F32) * bl).astype(BF16)),
                                      _stack_heads((kb * eg).astype(BF16))], axis=1)
        qg_s[idx] = _stack_heads((q.astype(F32) * eg).astype(BF16))
        glast = gcum[last:last + 1, :]
        kd_s[idx] = _stack_heads(kf * jnp.exp(glast - gcum)).T.astype(BF16)
        egl_s[idx] = jnp.broadcast_to(jnp.exp(glast), (8, PAIR))

    def stage_group(g, slot):
        todo = chains(g, slot)
        for idx, p, d, ci in todo:
            stage_inputs(idx, p, d, ci)
        for _ in range(5):
            for idx, _, _, _ in todo:
                xb = x_s[idx]
                x_s[idx] = _dot(xb, xb).astype(BF16)
            for idx, _, _, _ in todo:
                t = t_s[idx]
                t_s[idx] = t + _dot(t.astype(BF16), x_s[idx])
        for idx, _, _, _ in todo:
            sol = _dot(t_s[idx].astype(BF16), rhs_s[idx])
            u_s[idx] = sol[:, :PAIR]
            w_s[idx] = sol[:, PAIR:].astype(BF16)

    def recur_group(g, slot):
        sts = {(p, d): st_s[d * npair + p] for p in range(npair) for d in range(2)}
        for idx, p, d, ci in chains(g, slot):
            sl = pl.ds(pl.multiple_of(ci * c, c), c)
            st = sts[(p, d)]
            stb = st.astype(BF16)
            v_new = (u_s[idx] - _dot(w_s[idx], stb)).astype(BF16)
            o_st = _dot(qg_s[idx], stb) + _dot(at_s[idx], v_new)
            oacc[sl, p * PAIR:(p + 1) * PAIR] += o_st[:c] + o_st[c:]
            sts[(p, d)] = st * egl_s[idx][0:1, :] + _dot(kd_s[idx], v_new)
        for (p, d), st in sts.items():
            st_s[d * npair + p] = st

    def prep(src_ref, r0):
        for s in range(D_GDN // LANES):
            for k, (dst, l2, scale) in enumerate(((qn, True, HEAD_DIM ** -0.5), (kn, True, 1.0), (vn, False, 1.0))):
                col = k * D_GDN + s * LANES
                _gdn_prep(src_ref, col, cw_ref[:, col:col + LANES], dst, r0, l2=l2, scale=scale)

    def finish(src_ref, out_ref, r0, n):
        for p in range(npair):
            lanes = slice(p * PAIR, (p + 1) * PAIR)
            o = oacc[r0:r0 + n, lanes]
            on = o * lax.rsqrt(_head_sumsq(o) * (1.0 / HEAD_DIM) + EPS)
            gate = src_ref[0, :, 3 * D_GDN + p * PAIR:3 * D_GDN + (p + 1) * PAIR].astype(F32)
            out_ref[0, :, lanes] = (on * ng_ref[...] * _silu(gate)).astype(out_ref.dtype)

    st_s[...] = jnp.zeros_like(st_s)
    oacc[...] = jnp.zeros_like(oacc)
    prep(xc_ref, 0)
    prep(x_ref, lc)

    ng = nt // grp
    stage_group(0, 0)

    def two_groups(kk, _):
        g = 2 * kk
        recur_group(g, 0)
        stage_group(g + 1, 1)
        recur_group(g + 1, 1)
        stage_group(g + 2, 0)
        return 0

    lax.fori_loop(0, (ng - 1) // 2, two_groups, 0)
    if ng % 2 == 0:
        recur_group(ng - 2, 0)
        stage_group(ng - 1, 1)
        recur_group(ng - 1, 1)
    else:
        recur_group(ng - 1, 0)
    finish(xc_ref, oc_ref, 0, lc)
    finish(x_ref, o_ref, lc, l)


def _gdn(gdn_l, ab_l, gdn_c, ab_c, conv_w, a_log, dt_bias, norm_g):
    b, l, _ = gdn_l.shape
    lc = gdn_c.shape[1]
    npair = D_GDN // PAIR
    lt = l + lc
    assert (lt // GDN_CHUNK) % GDN_GROUP == 0

    def tr(ab):
        x = jnp.stack([ab[:, :, 0:8], ab[:, :, LANES:LANES + 8]], axis=1)
        x = x.reshape(x.shape[0], npair, x.shape[2] // GDN_CHUNK, GDN_CHUNK, 8)
        return jnp.transpose(x, (0, 1, 2, 4, 3))

    nea = -jnp.exp(a_log.astype(F32)).reshape(2, npair, 2)
    dtb = dt_bias.astype(F32).reshape(2, npair, 2)
    rows = jnp.stack([jnp.transpose(nea, (1, 0, 2)).reshape(npair, 4), jnp.transpose(dtb, (1, 0, 2)).reshape(npair, 4)], axis=1)
    cst = jnp.pad(rows, ((0, 0), (0, 6), (0, LANES - 4)))
    cstt = jnp.broadcast_to(jnp.pad(rows, ((0, 0), (0, 0), (0, 4)))[..., None], (npair, 2, 8, LANES))
    cw = jnp.pad(conv_w.astype(F32), ((0, 8 - SHORT_CONV), (0, 0)))
    ng = jnp.tile(norm_g.astype(F32), 2).reshape(1, PAIR)
    nstage = 2 * GDN_GROUP * 2 * npair

    def whole(shape):
        nd = len(shape)
        return pl.BlockSpec(shape, lambda i: (0,) * nd)

    def per_batch(shape, single=False):
        nd = len(shape)
        return pl.BlockSpec((1,) + shape, lambda i: (i,) + (0,) * nd, pipeline_mode=pl.Buffered(1) if single else None)

    sq = (nstage, PAIR, PAIR)
    return pl.pallas_call(
        _gdn_body,
        grid=(b,),
        in_specs=[per_batch((l, 4 * D_GDN), True), per_batch((lc, 4 * D_GDN)), per_batch((lt, 2 * LANES), True),
                  per_batch((npair, lt // GDN_CHUNK, 8, GDN_CHUNK)),
                  whole((8, 3 * D_GDN)), whole((npair, 8, LANES)), whole((npair, 2, 8, LANES)), whole((1, PAIR))],
        out_specs=[per_batch((l, D_GDN)), per_batch((lc, D_GDN))],
        out_shape=[jax.ShapeDtypeStruct((b, l, D_GDN), BF16), jax.ShapeDtypeStruct((b, lc, D_GDN), BF16)],
        scratch_shapes=[pltpu.VMEM((lt, D_GDN), BF16)] * 3 + [pltpu.VMEM((lt, D_GDN), F32)]
        + [pltpu.VMEM((2 * npair, PAIR, PAIR), F32), pltpu.VMEM(sq, BF16), pltpu.VMEM(sq, F32),
           pltpu.VMEM((nstage, PAIR, 2 * PAIR), BF16), pltpu.VMEM(sq, F32)]
        + [pltpu.VMEM(sq, BF16)] * 4 + [pltpu.VMEM((nstage, 8, PAIR), F32)],
        compiler_params=_cparams(("arbitrary",), VMEM_LIMIT),
        name="gated_deltanet",
    )(gdn_l, gdn_c, jnp.concatenate([ab_c, ab_l], axis=1), jnp.concatenate([tr(ab_c), tr(ab_l)], axis=2), cw, cst, cstt, ng)


def _na_bias_tiles(rpb, rows):
    w = GRID_W
    ext = jnp.pad(rpb.astype(F32), ((0, 0), (0, 0), (w - NA_KW, w - NA_KW)))
    tb = jnp.stack([ext[:, :, w - 1 - cq:2 * w - 1 - cq] for cq in range(w)], axis=2)
    cq = np.arange(w)[:, None]
    ck = np.arange(w)[None, :]
    ws = np.clip(cq - NA_KW // 2, 0, w - NA_KW)
    col_ok = (ck >= ws) & (ck < ws + NA_KW)
    tb = jnp.where(jnp.asarray(col_ok)[None, None], tb, NEG_INF)
    tb = jnp.concatenate([tb, jnp.full((tb.shape[0], 1, w, w), NEG_INF, F32)], axis=1)
    nt = rows // NA_QROWS
    idx = np.zeros((3, NA_QROWS, NA_KROWS), np.int32)
    for cls, t in enumerate((0, 1, nt - 1)):
        ks = int(np.clip(NA_QROWS * t - NA_KH // 2, 0, rows - NA_KROWS))
        for rl in range(NA_QROWS):
            r = NA_QROWS * t + rl
            r0 = int(np.clip(r - NA_KH // 2, 0, rows - NA_KH))
            for j in range(NA_KROWS):
                kr = ks + j
                idx[cls, rl, j] = kr - r + NA_KH - 1 if r0 <= kr < r0 + NA_KH else 2 * NA_KH - 1
    tiles = tb[:, jnp.asarray(idx)]
    tiles = jnp.transpose(tiles, (1, 0, 2, 4, 3, 5))
    return tiles.reshape(3, tb.shape[0], NA_QROWS * w, NA_KROWS * w)


def _softmax_pv(s_parts, v_parts):
    m = functools.reduce(jnp.maximum, [jnp.max(s, axis=-1, keepdims=True) for s in s_parts])
    ps = [jnp.exp(s - m) for s in s_parts]
    den = functools.reduce(lambda x, y: x + y, [jnp.sum(p, axis=-1, keepdims=True) for p in ps])
    o = functools.reduce(lambda x, y: x + y, [_dot(p.astype(BF16), v) for p, v in zip(ps, v_parts)])
    return o / den


def _na_body(q_ref, k_ref, v_ref, qc_ref, kc_ref, vc_ref, bias_ref, o_ref, oc_ref, *, rows):
    w = GRID_W
    tq = NA_QROWS * w
    tk = NA_KROWS * w
    nt = rows // NA_QROWS
    scale = HEAD_DIM ** -0.5
    kc = kc_ref[0]
    vc = vc_ref[0]
    m0, m1 = _head_masks((tq, PAIR))
    masks = (m0, m1)

    def tile(t, _):
        cls = jnp.where(t > 0, 1, 0) + jnp.where(t == nt - 1, 1, 0)
        ks = jnp.clip(NA_QROWS * t - NA_KH // 2, 0, rows - NA_KROWS) * w
        ksl = pl.ds(pl.multiple_of(ks, w), tk)
        qsl = pl.ds(pl.multiple_of(t * tq, tq), tq)
        q = q_ref[0, qsl, :] * scale
        kb = k_ref[0, ksl, :]
        vb = v_ref[0, ksl, :]
        acc = jnp.zeros((tq, PAIR), F32)
        for hh in range(2):
            qm = jnp.where(masks[hh], q, jnp.zeros_like(q))
            s_loc = _dot_nt(qm, kb) + bias_ref[cls, hh]
            s_ctx = _dot_nt(qm, kc)
            acc = acc + jnp.where(masks[hh], _softmax_pv([s_loc, s_ctx], [vb, vc]), 0.0)
        o_ref[0, qsl, :] = acc.astype(o_ref.dtype)
        return 0

    lax.fori_loop(0, nt, tile, 0)
    qc = qc_ref[0] * scale
    mc0, mc1 = _head_masks(qc.shape)
    accc = jnp.zeros(qc.shape, F32)
    for hh, mk in enumerate((mc0, mc1)):
        qm = jnp.where(mk, qc, jnp.zeros_like(qc))
        accc = accc + jnp.where(mk, _softmax_pv([_dot_nt(qm, kc)], [vc]), 0.0)
    oc_ref[0] = accc.astype(oc_ref.dtype)


def _neighbourhood_attention(na_l, na_c, rpb):
    b, l, _ = na_l.shape
    lc = na_c.shape[1]
    rows = l // GRID_W
    npair = D_NA // PAIR
    bias = _na_bias_tiles(rpb, rows)

    def col(k):
        return lambda p, i: (i, 0, k * npair + p)

    lat = [pl.BlockSpec((1, l, PAIR), col(k)) for k in range(3)]
    ctx = [pl.BlockSpec((1, lc, PAIR), col(k)) for k in range(3)]
    return pl.pallas_call(
        functools.partial(_na_body, rows=rows),
        grid=(npair, b),
        in_specs=lat + ctx + [pl.BlockSpec((3, 2, NA_QROWS * GRID_W, NA_KROWS * GRID_W), lambda p, i: (0, p, 0, 0))],
        out_specs=[pl.BlockSpec((1, l, PAIR), lambda p, i: (i, 0, p)),
                   pl.BlockSpec((1, lc, PAIR), lambda p, i: (i, 0, p))],
        out_shape=[jax.ShapeDtypeStruct((b, l, D_NA), BF16), jax.ShapeDtypeStruct((b, lc, D_NA), BF16)],
        compiler_params=_cparams(("arbitrary", "arbitrary"), VMEM_LIMIT),
        name="neighbourhood_attention",
    )(na_l, na_l, na_l, na_c, na_c, na_c, bias)


def _outproj_residual(r_ref, g_ref, n_ref, x_ref, mod_ref, wo_ref):
    mix = jnp.concatenate([r_ref[0], g_ref[0], n_ref[0]], axis=1)
    return x_ref[0] + mod_ref[0, 2:3, :] * _dot(mix, wo_ref[...])


def _swiglu_chunk(hb, w1_ref, w3_ref, w2_ref, j):
    cols = slice(j * FF_CHUNK, (j + 1) * FF_CHUNK)
    t = (_silu(_dot(hb, w1_ref[:, cols])) * _dot(hb, w3_ref[:, cols])).astype(BF16)
    return _dot(t, w2_ref[cols, :])


def _swiglu_chunks(hb, w1_ref, w3_ref, w2_ref, acc_ref):
    for j in range(w2_ref.shape[0] // FF_CHUNK):
        if j == 0:
            acc_ref[...] = _swiglu_chunk(hb, w1_ref, w3_ref, w2_ref, j)
        else:
            acc_ref[...] += _swiglu_chunk(hb, w1_ref, w3_ref, w2_ref, j)


def _dense_body(r_ref, g_ref, n_ref, x_ref, mod_ref, g2_ref, wo_ref, w1_ref, w3_ref, w2_ref, o_ref, acc_ref):
    x1 = _outproj_residual(r_ref, g_ref, n_ref, x_ref, mod_ref, wo_ref)
    hb = _norm_mod(x1, g2_ref[...], mod_ref[0, 3:4, :], mod_ref[0, 4:5, :]).astype(BF16)
    _swiglu_chunks(hb, w1_ref, w3_ref, w2_ref, acc_ref)
    o_ref[0] = x1 + mod_ref[0, 5:6, :] * acc_ref[...]


def _pack_bf16_pairs(h):
    m = h.shape[1] // 2
    bits = lax.bitcast_convert_type(h.astype(BF16).astype(F32), jnp.uint32)
    return (bits[:, :m] >> 16) | (bits[:, m:] & jnp.uint32(0xFFFF0000))


def _unpack_bf16_pairs(u):
    lo = lax.bitcast_convert_type(u << 16, F32)
    hi = lax.bitcast_convert_type(u & jnp.uint32(0xFFFF0000), F32)
    return jnp.concatenate([lo, hi], axis=1).astype(BF16)


def _router_body(r_ref, g_ref, n_ref, x_ref, mod_ref, g2_ref, wo_ref, wr_ref, x1_ref, h_ref, lg_ref):
    x1 = _outproj_residual(r_ref, g_ref, n_ref, x_ref, mod_ref, wo_ref)
    h = _norm_mod(x1, g2_ref[...], mod_ref[0, 3:4, :], mod_ref[0, 4:5, :])
    x1_ref[0] = x1
    h_ref[0] = _pack_bf16_pairs(h)
    lg_ref[0] = _dot_hi(h, wr_ref[...])


def _resident(shape):
    nd = len(shape)
    return pl.BlockSpec(shape, lambda i, j: (0,) * nd, pipeline_mode=pl.Buffered(1))


def _mixer_specs(tm, d):
    return [pl.BlockSpec((1, tm, D_RET), lambda i, j: (i, j, 0)),
            pl.BlockSpec((1, tm, D_GDN), lambda i, j: (i, j, 0)),
            pl.BlockSpec((1, tm, D_NA), lambda i, j: (i, j, 0)),
            pl.BlockSpec((1, tm, d), lambda i, j: (i, j, 0)),
            pl.BlockSpec((1, 6, d), lambda i, j: (i, 0, 0)),
            pl.BlockSpec((1, d), lambda i, j: (0, 0))]


def _dense_block(r, g, n, x, mod, g2, wo, w1, w3, w2):
    b, l, d = x.shape
    tm = min(512, l)
    dff = w1.shape[1]
    return pl.pallas_call(
        _dense_body,
        grid=(b, l // tm),
        in_specs=_mixer_specs(tm, d) + [_resident((d, d)), _resident((d, dff)), _resident((d, dff)), _resident((dff, d))],
        out_specs=pl.BlockSpec((1, tm, d), lambda i, j: (i, j, 0)),
        out_shape=jax.ShapeDtypeStruct((b, l, d), F32),
        scratch_shapes=[pltpu.VMEM((tm, d), F32)],
        compiler_params=_cparams(("arbitrary", "arbitrary"), VMEM_LIMIT),
        name="outproj_swiglu",
    )(r, g, n, x, mod, g2, wo, w1, w3, w2)


def _router_block(r, g, n, x, mod, g2, wo, wr):
    b, l, d = x.shape
    tm = min(512, l)
    blk = pl.BlockSpec((1, tm, d), lambda i, j: (i, j, 0))
    return pl.pallas_call(
        _router_body,
        grid=(b, l // tm),
        in_specs=_mixer_specs(tm, d) + [_resident((d, d)), _resident((d, LANES))],
        out_specs=[blk, pl.BlockSpec((1, tm, d // 2), lambda i, j: (i, j, 0)), pl.BlockSpec((1, tm, LANES), lambda i, j: (i, j, 0))],
        out_shape=[jax.ShapeDtypeStruct((b, l, d), F32), jax.ShapeDtypeStruct((b, l, d // 2), jnp.uint32),
                   jax.ShapeDtypeStruct((b, l, LANES), F32)],
        compiler_params=_cparams(("arbitrary", "arbitrary"), VMEM_LIMIT),
        name="outproj_router",
    )(r, g, n, x, mod, g2, wo, wr)


def _row_copy_in(idx_ref, r, src_hbm, dst_ref, sem):
    return pltpu.make_async_copy(src_hbm.at[pl.ds(idx_ref[0, 0, r], 1), :], dst_ref.at[pl.ds(r, 1), :], sem)


def _row_copy_out(idx_ref, r, src_ref, dst_hbm, sem):
    return pltpu.make_async_copy(src_ref.at[pl.ds(r, 1), :], dst_hbm.at[pl.ds(idx_ref[0, 0, r], 1), :], sem)


def _moe_body(bval_ref, bexp_ref, tok_ref, tokn_ref, slotp_ref, h_hbm, w1_ref, w3_ref, w2_ref, y_hbm,
              xbuf, ybuf, gsem, ssem, *, n_asg):
    i = pl.program_id(0)
    tb = xbuf.shape[1]
    nch = w2_ref.shape[1] // FF_CHUNK
    cur = i % 2
    per = -(-tb // nch)
    valid = bval_ref[i] == 1
    prev_valid = bval_ref[jnp.maximum(i - 1, 0)] == 1

    def wait_gather(slot):
        pltpu.make_async_copy(h_hbm.at[pl.ds(0, tb), :], xbuf.at[slot], gsem).wait()

    def wait_scatter(slot):
        pltpu.make_async_copy(ybuf.at[slot], y_hbm.at[pl.ds(0, tb), :], ssem).wait()

    @pl.when(i == 0)
    def _():
        ybuf[1] = jnp.zeros((tb, ybuf.shape[2]), F32)
        fills = [pltpu.make_async_copy(ybuf.at[1], y_hbm.at[pl.ds(r0, tb), :], ssem)
                 for r0 in range(n_asg, y_hbm.shape[0], tb)]
        for f in fills:
            f.start()
        for f in fills:
            f.wait()

        def issue(r, _):
            _row_copy_in(tok_ref, r, h_hbm, xbuf.at[0], gsem).start()
            return 0

        lax.fori_loop(0, tb, issue, 0)

    @pl.when(valid)
    def _():
        wait_gather(cur)
        hb = _unpack_bf16_pairs(xbuf[cur])
        acc = ybuf.at[cur]
        prv = ybuf.at[1 - cur]
        nxt = xbuf.at[1 - cur]
        for j in range(nch):
            for r in range(j * per, min(tb, (j + 1) * per)):
                _row_copy_in(tokn_ref, r, h_hbm, nxt, gsem).start()
                _row_copy_out(slotp_ref, r, prv, y_hbm, ssem).start()
            if j == 0:
                acc[...] = _swiglu_chunk(hb, w1_ref.at[0], w3_ref.at[0], w2_ref.at[0], j)
            else:
                acc[...] += _swiglu_chunk(hb, w1_ref.at[0], w3_ref.at[0], w2_ref.at[0], j)
        wait_scatter(1 - cur)

    @pl.when(jnp.logical_not(valid) & prev_valid & (i > 0))
    def _():
        wait_gather(cur)

        def issue(r, _):
            _row_copy_out(slotp_ref, r, ybuf.at[1 - cur], y_hbm, ssem).start()
            return 0

        lax.fori_loop(0, tb, issue, 0)
        wait_scatter(1 - cur)


def _moe_experts(h_rows, row_tok, row_slot, block_expert, block_valid, w1, w3, w2, n_out_rows, n_spare):
    d = w2.shape[-1]
    dff = w2.shape[1]
    nb = block_expert.shape[0]
    tb = MOE_ROWS
    smem = functools.partial(pl.BlockSpec, (1, 1, tb), memory_space=pltpu.SMEM)
    grid_spec = pltpu.PrefetchScalarGridSpec(
        num_scalar_prefetch=2,
        grid=(nb,),
        in_specs=[smem(lambda i, bv, be: (i, 0, 0)),
                  smem(lambda i, bv, be: (jnp.minimum(i + 1, nb - 1), 0, 0)),
                  smem(lambda i, bv, be: (i, 0, 0)),
                  pl.BlockSpec(memory_space=pl.ANY),
                  pl.BlockSpec((1, d, dff), lambda i, bv, be: (be[i], 0, 0)),
                  pl.BlockSpec((1, d, dff), lambda i, bv, be: (be[i], 0, 0)),
                  pl.BlockSpec((1, dff, d), lambda i, bv, be: (be[i], 0, 0))],
        out_specs=pl.BlockSpec(memory_space=pl.ANY),
        scratch_shapes=[pltpu.VMEM((2, tb, d // 2), jnp.uint32), pltpu.VMEM((2, tb, d), F32),
                        pltpu.SemaphoreType.DMA(()), pltpu.SemaphoreType.DMA(())],
    )
    return pl.pallas_call(
        functools.partial(_moe_body, n_asg=n_out_rows - n_spare),
        grid_spec=grid_spec,
        out_shape=jax.ShapeDtypeStruct((n_out_rows, d), F32),
        compiler_params=_cparams(("arbitrary",), VMEM_LIMIT),
        name="moe_experts",
    )(block_valid, block_expert, row_tok.reshape(nb, 1, tb), row_tok.reshape(nb, 1, tb), row_slot.reshape(nb + 1, 1, tb),
      h_rows, w1, w3, w2)


def _combine_body(y_ref, x1_ref, gate_ref, gm_ref, fg_ref, o_ref, *, final):
    d = x1_ref.shape[1]
    moe = gate_ref[:, 0:1] * y_ref[:, 0:d] + gate_ref[:, 1:2] * y_ref[:, d:]
    x2 = x1_ref[...] + gm_ref[0] * moe
    if final:
        ms = jnp.mean(x2 * x2, axis=-1, keepdims=True)
        x2 = x2 * lax.rsqrt(ms + EPS) * fg_ref[...]
    o_ref[...] = x2


def _moe_combine(y_rows, x1, gates, gate_mlp, final_g, final):
    b, l, d = x1.shape
    n_tok = b * l
    tm = 256
    per_b = l // tm
    y2 = y_rows.reshape(y_rows.shape[0] // 2, 2 * d)
    out = pl.pallas_call(
        functools.partial(_combine_body, final=final),
        grid=(n_tok // tm,),
        in_specs=[pl.BlockSpec((tm, 2 * d), lambda i: (i, 0)),
                  pl.BlockSpec((tm, d), lambda i: (i, 0)),
                  pl.BlockSpec((tm, 2), lambda i: (i, 0)),
                  pl.BlockSpec((1, 1, d), lambda i: (i // per_b, 0, 0)),
                  pl.BlockSpec((1, d), lambda i: (0, 0))],
        out_specs=pl.BlockSpec((tm, d), lambda i: (i, 0)),
        out_shape=jax.ShapeDtypeStruct((n_tok, d), F32),
        compiler_params=_cparams(("arbitrary",), VMEM_LIMIT),
        name="moe_combine_final_norm",
    )(y2, x1.reshape(n_tok, d), gates, gate_mlp, final_g.reshape(1, d))
    return out.reshape(b, l, d)


def _route(logits):
    n_tok = logits.shape[0]
    tb = MOE_ROWS
    n_asg = n_tok * TOP_K
    top_logit, top_e = lax.top_k(logits[:, :N_EXPERTS], TOP_K)
    gates = jax.nn.softmax(top_logit, axis=-1)
    e_flat = top_e.reshape(-1).astype(jnp.int32)
    asg = jnp.arange(n_asg, dtype=jnp.int32)
    by_expert = lax.sort(e_flat * n_asg + asg)
    counts = jnp.sum((e_flat[:, None] == jnp.arange(N_EXPERTS, dtype=jnp.int32)[None, :]).astype(jnp.int32), axis=0)
    starts = jnp.cumsum(counts) - counts
    padded = (counts + tb - 1) // tb * tb
    pad_ends = jnp.cumsum(padded)
    pad_starts = pad_ends - padded
    nb = (n_asg + tb - 1) // tb + N_EXPERTS + 1
    n_rows = nb * tb
    block_start = jnp.arange(nb, dtype=jnp.int32) * tb
    block_expert = jnp.minimum(jnp.searchsorted(pad_ends, block_start, side='right'), N_EXPERTS - 1).astype(jnp.int32)
    block_valid = (block_start < pad_ends[-1]).astype(jnp.int32)
    row = jnp.arange(n_rows, dtype=jnp.int32)
    row_e = jnp.repeat(block_expert, tb)
    off = row - pad_starts[row_e]
    is_pad = (off >= counts[row_e]) | (jnp.repeat(block_valid, tb) == 0)
    src = by_expert[jnp.clip(starts[row_e] + off, 0, n_asg - 1)] - row_e * n_asg
    row_asg = jnp.where(is_pad, -1, src)
    spare = tb + n_asg + jnp.cumsum(is_pad.astype(jnp.int32)) - 1
    row_slot = jnp.where(is_pad, spare, row_asg)
    row_tok = jnp.where(is_pad, 0, row_asg // TOP_K)
    row_slot = jnp.concatenate([n_asg + jnp.arange(tb, dtype=jnp.int32), row_slot])
    last_e = block_expert[jnp.maximum(pad_ends[-1] // tb - 1, 0)]
    block_expert = jnp.where(block_valid == 1, block_expert, last_e)
    n_spare = (tb + n_rows - n_asg + tb - 1) // tb * tb
    return row_tok, row_slot, gates, block_expert, block_valid, n_asg + n_spare, n_spare


def _final_norm_body(x_ref, g_ref, o_ref):
    x = x_ref[...]
    ms = jnp.mean(x * x, axis=-1, keepdims=True)
    o_ref[...] = x * lax.rsqrt(ms + EPS) * g_ref[...]


def _final_norm(x, g):
    b, l, d = x.shape
    n = b * l
    tm = min(512, n)
    out = pl.pallas_call(
        _final_norm_body,
        grid=(n // tm,),
        in_specs=[pl.BlockSpec((tm, d), lambda i: (i, 0)), pl.BlockSpec((1, d), lambda i: (0, 0))],
        out_specs=pl.BlockSpec((tm, d), lambda i: (i, 0)),
        out_shape=jax.ShapeDtypeStruct((n, d), F32),
        compiler_params=_cparams(("arbitrary",)),
        name="final_norm",
    )(x.reshape(n, d), g.reshape(1, d))
    return out.reshape(b, l, d)


def _rope_tables(n_tok):
    t = jnp.arange(n_tok, dtype=jnp.int32)
    row = (t // GRID_W).astype(F32)
    col = (t % GRID_W).astype(F32)
    inv_freq = ROPE_BASE ** (-jnp.arange(N_FREQ, dtype=F32) / N_FREQ)
    ang = jnp.concatenate([row[:, None] * inv_freq, col[:, None] * inv_freq], axis=-1)
    cos, sin = jnp.cos(ang), jnp.sin(ang)
    cosf = jnp.concatenate([cos, cos, cos, cos], axis=-1)
    sins = jnp.concatenate([-sin, sin, -sin, sin], axis=-1)
    return cosf, sins


def kernel(x, c, ctx, c_ctx, ada_w, ada_b, norm1_g, norm2_g, w_in, w_out, conv_w, ret_decay, gdn_a_log, gdn_dt_bias,
           gdn_norm_g, na_rpb, ffn_w1, ffn_w3, ffn_w2, moe_router, moe_w1, moe_w3, moe_w2, final_g):
    b, l, d = x.shape
    lc = ctx.shape[1]
    depth = ada_w.shape[0]
    cosf, sins = _rope_tables(l)
    ones_c = jnp.ones((lc, LANES), F32)
    zeros_c = jnp.zeros((lc, LANES), F32)

    rows = ((b + 1 + 7) // 8) * 8
    c_all = jnp.zeros((rows, d), F32).at[:b].set(c).at[b].set(c_ctx)
    mod = _ada_vectors(c_all, ada_w, ada_b).reshape(depth, rows, 6, d)

    y = ctx
    for layer in range(depth):
        need_ctx = layer < depth - 1
        mod_l = mod[layer, :b]
        mod_c = jnp.broadcast_to(mod[layer, b][None], (b, 6, d))
        w1p = _pack_w_in(w_in[layer])
        g1 = norm1_g[layer].reshape(1, d)
        ret_l, gdn_l, ab_l, na_l = _in_projection(x, mod_l, g1, cosf, sins, w1p, rope=True)
        ret_c, gdn_c, ab_c, na_c = _in_projection(y, mod_c, g1, ones_c, zeros_c, w1p, rope=False)

        log_gamma = jnp.log1p(-jnp.exp2(-ret_decay[layer].astype(F32)))
        r_l, r_c = _retention(ret_l, ret_c, log_gamma)
        g_l, g_c = _gdn(gdn_l, ab_l, gdn_c, ab_c, conv_w[layer], gdn_a_log[layer], gdn_dt_bias[layer], gdn_norm_g[layer])
        n_l, n_c = _neighbourhood_attention(na_l, na_c, na_rpb[layer])

        wo = w_out[layer].astype(BF16)
        g2 = norm2_g[layer].reshape(1, d)
        j = layer // 2
        if layer % 2 == 0:
            w1, w3, w2 = ffn_w1[j].astype(BF16), ffn_w3[j].astype(BF16), ffn_w2[j].astype(BF16)
            x = _dense_block(r_l, g_l, n_l, x, mod_l, g2, wo, w1, w3, w2)
            if need_ctx:
                y = _dense_block(r_c, g_c, n_c, y, mod_c, g2, wo, w1, w3, w2)
            if layer == depth - 1:
                x = _final_norm(x, final_g)
        else:
            wr = jnp.pad(moe_router[j].astype(F32), ((0, 0), (0, LANES - N_EXPERTS)))
            w1, w3, w2 = moe_w1[j].astype(BF16), moe_w3[j].astype(BF16), moe_w2[j].astype(BF16)

            def moe_ffn(r, g, n, xin, m, last):
                bb, ll, _ = xin.shape
                x1, h, logits = _router_block(r, g, n, xin, m, g2, wo, wr)
                row_tok, row_slot, gates, bexp, bval, n_out, n_spare = _route(logits.reshape(bb * ll, LANES))
                y_rows = _moe_experts(h.reshape(bb * ll, d // 2), row_tok, row_slot, bexp, bval, w1, w3, w2, n_out, n_spare)
                return _moe_combine(y_rows, x1, gates, m[:, 5:6, :], final_g, last)

            x = moe_ffn(r_l, g_l, n_l, x, mod_l, layer == depth - 1)
            if need_ctx:
                y = moe_ffn(r_c, g_c, n_c, y, mod_c, False)
    return x
```

```python
import functools

import numpy as np
import jax
import jax.numpy as jnp
from jax import lax
from jax.experimental import pallas as pl
from jax.experimental.pallas import tpu as pltpu

F32 = jnp.float32
BF16 = jnp.bfloat16
HIGHEST = lax.Precision.HIGHEST

LANES = 128
HEAD_DIM = 64
PAIR = 2 * HEAD_DIM
GRID_W = 64
H_RET, H_GDN, H_NA = 4, 4, 8
D_RET, D_GDN, D_NA = H_RET * HEAD_DIM, H_GDN * HEAD_DIM, H_NA * HEAD_DIM
RET_CHUNK = 128
GDN_CHUNK = 64
GDN_GROUP = 4
GDN_PREP_ROWS = 512
SHORT_CONV = 5
NA_KH, NA_KW = 8, 16
NA_QROWS = 4
NA_KROWS = NA_QROWS + NA_KH
N_FREQ = HEAD_DIM // 4
ROPE_BASE = 10000.0
N_EXPERTS = 8
TOP_K = 2
MOE_ROWS = 512
FF_CHUNK = 256
EPS = 1e-6
NEG_INF = -1e30
VMEM_LIMIT = 56 * 1024 * 1024

C_RET = 0
C_GDN = C_RET + 4 * D_RET
C_AB = C_GDN + 4 * D_GDN
C_NA = C_AB + 2 * LANES
C_END = C_NA + 3 * D_NA


def _cparams(sem, vmem=None):
    return pltpu.CompilerParams(dimension_semantics=sem, vmem_limit_bytes=vmem)


def _silu(x):
    return x * jax.nn.sigmoid(x)


def _dot(a, b):
    return jnp.dot(a, b, preferred_element_type=F32)


def _dot_nt(a, b):
    return lax.dot_general(a, b, (((1,), (1,)), ((), ())), preferred_element_type=F32)


def _dot_tn(a, b):
    return lax.dot_general(a, b, (((0,), (0,)), ((), ())), preferred_element_type=F32)


def _dot_hi(a, b):
    return jnp.dot(a, b, preferred_element_type=F32, precision=HIGHEST)


def _ada_body(c_ref, w_ref, b_ref, o_ref):
    s = _silu(c_ref[...])
    o_ref[0] = _dot_hi(s, w_ref[0]) + b_ref[0]


def _ada_vectors(c_all, ada_w, ada_b):
    depth, d, d6 = ada_w.shape
    rows = c_all.shape[0]
    tn = 1024
    return pl.pallas_call(
        _ada_body,
        grid=(depth, d6 // tn),
        in_specs=[pl.BlockSpec((rows, d), lambda l, j: (0, 0)),
                  pl.BlockSpec((1, d, tn), lambda l, j: (l, 0, j)),
                  pl.BlockSpec((1, 1, tn), lambda l, j: (l, 0, j))],
        out_specs=pl.BlockSpec((1, rows, tn), lambda l, j: (l, 0, j)),
        out_shape=jax.ShapeDtypeStruct((depth, rows, d6), F32),
        compiler_params=_cparams(("arbitrary", "arbitrary")),
        name="ada_vectors",
    )(c_all, ada_w, ada_b.reshape(depth, 1, d6))


def _norm_mod(x, g, shift, scale):
    ms = jnp.mean(x * x, axis=-1, keepdims=True)
    return (x * lax.rsqrt(ms + EPS) * g) * (1.0 + scale) + shift


def _rope_slab(t, cosf, sins):
    lane = lax.broadcasted_iota(jnp.int32, t.shape, 1)
    first = (lane % HEAD_DIM) < (HEAD_DIM // 2)
    partner = jnp.where(first, pltpu.roll(t, LANES - HEAD_DIM // 2, 1), pltpu.roll(t, HEAD_DIM // 2, 1))
    return t * cosf + partner * sins


def _inproj_body(x_ref, mod_ref, g_ref, cos_ref, sin_ref, w_ref, ret_ref, gdn_ref, ab_ref, na_ref, *, rope):
    h = _norm_mod(x_ref[0], g_ref[...], mod_ref[0, 0:1, :], mod_ref[0, 1:2, :]).astype(BF16)
    qk = _dot(h, w_ref[:, C_RET:C_RET + 2 * D_RET])
    slabs = []
    for s in range(2 * D_RET // LANES):
        t = qk[:, s * LANES:(s + 1) * LANES]
        if rope:
            t = _rope_slab(t, cos_ref[...], sin_ref[...])
        if s >= D_RET // LANES:
            t = t * HEAD_DIM ** -0.5
        slabs.append(t)
    ret_ref[0, :, 0:2 * D_RET] = jnp.concatenate(slabs, axis=1).astype(BF16)
    ret_ref[0, :, 2 * D_RET:] = _dot(h, w_ref[:, C_RET + 2 * D_RET:C_GDN]).astype(BF16)
    for j in range(2):
        gdn_ref[0, :, 512 * j:512 * (j + 1)] = _dot(h, w_ref[:, C_GDN + 512 * j:C_GDN + 512 * (j + 1)]).astype(BF16)
    ab_ref[0] = _dot(h, w_ref[:, C_AB:C_NA])
    for j in range(3):
        na_ref[0, :, 512 * j:512 * (j + 1)] = _dot(h, w_ref[:, C_NA + 512 * j:C_NA + 512 * (j + 1)]).astype(BF16)


def _in_projection(x, mod, g, cosf, sins, w, rope):
    b, l, d = x.shape
    tm = min(512, l)
    body = functools.partial(_inproj_body, rope=rope)
    return pl.pallas_call(
        body,
        grid=(b, l // tm),
        in_specs=[pl.BlockSpec((1, tm, d), lambda i, j: (i, j, 0)),
                  pl.BlockSpec((1, 6, d), lambda i, j: (i, 0, 0)),
                  pl.BlockSpec((1, d), lambda i, j: (0, 0)),
                  pl.BlockSpec((tm, LANES), lambda i, j: (j, 0)),
                  pl.BlockSpec((tm, LANES), lambda i, j: (j, 0)),
                  pl.BlockSpec((d, C_END), lambda i, j: (0, 0))],
        out_specs=[pl.BlockSpec((1, tm, 4 * D_RET), lambda i, j: (i, j, 0)),
                   pl.BlockSpec((1, tm, 4 * D_GDN), lambda i, j: (i, j, 0)),
                   pl.BlockSpec((1, tm, 2 * LANES), lambda i, j: (i, j, 0)),
                   pl.BlockSpec((1, tm, 3 * D_NA), lambda i, j: (i, j, 0))],
        out_shape=[jax.ShapeDtypeStruct((b, l, 4 * D_RET), BF16),
                   jax.ShapeDtypeStruct((b, l, 4 * D_GDN), BF16),
                   jax.ShapeDtypeStruct((b, l, 2 * LANES), F32),
                   jax.ShapeDtypeStruct((b, l, 3 * D_NA), BF16)],
        compiler_params=_cparams(("arbitrary", "arbitrary"), VMEM_LIMIT),
        name="in_projection",
    )(x, mod, g, cosf, sins, w)


def _pack_w_in(w_in, conv_cols=None):
    d = w_in.shape[0]
    c1 = 4 * D_RET
    c2 = c1 + 4 * D_GDN
    ab = w_in[:, c2:c2 + 4 * H_GDN]
    ab = ab.reshape(d, 2, 2, H_GDN // 2, 2)
    ab = jnp.transpose(ab, (0, 3, 1, 2, 4)).reshape(d, 2, 8)
    ab = jnp.pad(ab, ((0, 0), (0, 0), (0, LANES - 8))).reshape(d, 2 * LANES)
    return jnp.concatenate([w_in[:, :c2], ab, w_in[:, c2 + 4 * H_GDN:]], axis=1).astype(BF16)


def _head_masks(shape):
    lane = lax.broadcasted_iota(jnp.int32, shape, len(shape) - 1)
    return lane < HEAD_DIM, lane >= HEAD_DIM


def _per_head(lo, hi, shape):
    m0, _ = _head_masks(shape)
    return jnp.where(m0, lo, hi)


def _head_sumsq(o):
    m0, m1 = _head_masks(o.shape)
    sq = o * o
    s0 = jnp.sum(jnp.where(m0, sq, 0.0), axis=-1, keepdims=True)
    s1 = jnp.sum(jnp.where(m1, sq, 0.0), axis=-1, keepdims=True)
    return jnp.where(m0, s0, s1)


def _ret_body(lg_ref, ql, kl, vl, gl, qc, kc, vc, gc, r_ref, rc_ref, sb_ref):
    c = RET_CHUNK
    p = pl.program_id(1)
    lane_shape = (c, PAIR)
    pos = lax.broadcasted_iota(jnp.int32, lane_shape, 0).astype(F32)
    lgf = _per_head(lg_ref[0, 2 * p], lg_ref[0, 2 * p + 1], lane_shape)
    lgb = _per_head(lg_ref[1, 2 * p], lg_ref[1, 2 * p + 1], lane_shape)
    qdf = jnp.exp(lgf * (pos + 1.0))
    kdf = jnp.exp(lgf * (c - 1.0 - pos))
    qdb = jnp.exp(lgb * (c - pos))
    kdb = jnp.exp(lgb * pos)
    cdf = jnp.exp(lgf[0:1] * c)
    cdb = jnp.exp(lgb[0:1] * c)
    ii = lax.broadcasted_iota(jnp.int32, (c, c), 0)
    jj = lax.broadcasted_iota(jnp.int32, (c, c), 1)
    diff = (ii - jj).astype(F32)
    dmats = [jnp.where(diff > 0, jnp.exp(lg_ref[0, 2 * p + hh] * diff),
                       jnp.where(diff < 0, jnp.exp(-lg_ref[1, 2 * p + hh] * diff), 2.0)) for hh in range(2)]
    m0, m1 = _head_masks(lane_shape)
    masks = (m0, m1)
    bi = lax.broadcasted_iota(jnp.int32, (PAIR, PAIR), 0) // HEAD_DIM
    bj = lax.broadcasted_iota(jnp.int32, (PAIR, PAIR), 1) // HEAD_DIM
    bd = bi == bj

    def sweep(q_ref, k_ref, v_ref, g_ref, o_ref, n, sf0, sb0):
        def bstep(t, sb):
            ci = n - 1 - t
            sl = pl.ds(pl.multiple_of(ci * c, c), c)
            sb_ref[ci] = sb
            kd = (k_ref[0, sl, :].astype(F32) * kdb).astype(BF16)
            return sb * cdb + jnp.where(bd, _dot_tn(kd, v_ref[0, sl, :]), 0.0)

        sb_fin = lax.fori_loop(0, n, bstep, sb0)

        def fstep(ci, sf):
            sl = pl.ds(pl.multiple_of(ci * c, c), c)
            q = q_ref[0, sl, :]
            k = k_ref[0, sl, :]
            v = v_ref[0, sl, :]
            qf = q.astype(F32)
            o = _dot((qf * qdf).astype(BF16), sf.astype(BF16)) + _dot((qf * qdb).astype(BF16), sb_ref[ci].astype(BF16))
            for hh in range(2):
                qm = jnp.where(masks[hh], q, jnp.zeros_like(q))
                pm = (_dot_nt(qm, k) * dmats[hh]).astype(BF16)
                o = o + jnp.where(masks[hh], _dot(pm, v), 0.0)
            on = o * lax.rsqrt(_head_sumsq(o) * (1.0 / HEAD_DIM) + EPS)
            o_ref[0, sl, :] = (on * _silu(g_ref[0, sl, :].astype(F32))).astype(o_ref.dtype)
            kd = (k.astype(F32) * kdf).astype(BF16)
            return sf * cdf + jnp.where(bd, _dot_tn(kd, v), 0.0)

        sf_fin = lax.fori_loop(0, n, fstep, sf0)
        return sf_fin, sb_fin

    z = jnp.zeros((PAIR, PAIR), F32)
    sfc, sbc = sweep(qc, kc, vc, gc, rc_ref, qc.shape[1] // c, z, z)
    sweep(ql, kl, vl, gl, r_ref, ql.shape[1] // c, sfc, sbc)


def _retention(ret_l, ret_c, log_gamma):
    b, l, _ = ret_l.shape
    lc = ret_c.shape[1]
    npair = D_RET // PAIR

    def col(k):
        return lambda i, p: (i, 0, k * npair + p)

    lat = [pl.BlockSpec((1, l, PAIR), col(k)) for k in range(4)]
    ctx = [pl.BlockSpec((1, lc, PAIR), col(k)) for k in range(4)]
    return pl.pallas_call(
        _ret_body,
        grid=(b, npair),
        in_specs=[pl.BlockSpec(memory_space=pltpu.SMEM)] + lat + ctx,
        out_specs=[pl.BlockSpec((1, l, PAIR), lambda i, p: (i, 0, p)),
                   pl.BlockSpec((1, lc, PAIR), lambda i, p: (i, 0, p))],
        out_shape=[jax.ShapeDtypeStruct((b, l, D_RET), BF16), jax.ShapeDtypeStruct((b, lc, D_RET), BF16)],
        scratch_shapes=[pltpu.VMEM((max(l, lc) // RET_CHUNK, PAIR, PAIR), F32)],
        compiler_params=_cparams(("arbitrary", "arbitrary"), VMEM_LIMIT),
        name="retention",
    )(log_gamma, ret_l, ret_l, ret_l, ret_l, ret_c, ret_c, ret_c, ret_c)


def _stack_heads(t):
    m0, m1 = _head_masks(t.shape)
    z = jnp.zeros_like(t)
    return jnp.concatenate([jnp.where(m0, t, z), jnp.where(m1, t, z)], axis=0)


def _scan_sum(x, axis, n, reverse):
    size = x.shape[axis]
    pos = lax.broadcasted_iota(jnp.int32, x.shape, axis)
    s = 1
    while s < n:
        if reverse:
            x = x + jnp.where(pos + s < n, pltpu.roll(x, size - s, axis), 0.0)
        else:
            x = x + jnp.where(pos >= s, pltpu.roll(x, s, axis), 0.0)
        s *= 2
    return x


def _gdn_prep(src_ref, col, cw, dst_ref, r0, *, l2, scale):
    n = src_ref.shape[1]
    blk = min(GDN_PREP_ROWS, n)
    halo = 16
    lanes = slice(col % D_GDN, col % D_GDN + LANES)

    def block(i, _):
        b0 = pl.multiple_of(i * blk, blk)
        x = src_ref[0, pl.ds(b0, blk), col:col + LANES].astype(F32)
        lo = src_ref[0, pl.ds(pl.multiple_of(jnp.maximum(b0 - halo, 0), halo), halo), col:col + LANES].astype(F32)
        hi = src_ref[0, pl.ds(pl.multiple_of(jnp.minimum(b0 + blk, n - halo), halo), halo), col:col + LANES].astype(F32)
        lo = jnp.where(b0 > 0, lo, 0.0)
        hi = jnp.where(b0 + blk < n, hi, 0.0)
        ext = jnp.concatenate([lo, x, hi], axis=0)
        acc = x * cw[SHORT_CONV // 2:SHORT_CONV // 2 + 1, :]
        for j in range(SHORT_CONV):
            s = j - SHORT_CONV // 2
            if s != 0:
                sh = pltpu.roll(ext, (-s) % (blk + 2 * halo), 0)
                acc = acc + sh[halo:halo + blk] * cw[j:j + 1, :]
        y = _silu(acc)
        if l2:
            y = y * lax.rsqrt(_head_sumsq(y) + EPS)
        if scale != 1.0:
            y = y * scale
        dst_ref[pl.ds(pl.multiple_of(r0 + b0, halo), blk), lanes] = y.astype(dst_ref.dtype)
        return 0

    lax.fori_loop(0, n // blk, block, 0)


def _gdn_body(x_ref, xc_ref, ab_ref, abt_ref, cw_ref, cst_ref, cstt_ref, ng_ref,
              o_ref, oc_ref, qn, kn, vn, oacc, st_s, x_s, t_s, rhs_s, u_s, w_s, at_s, qg_s, kd_s, egl_s):
    c = GDN_CHUNK
    c2 = 2 * c
    grp = GDN_GROUP
    npair = D_GDN // PAIR
    lc = xc_ref.shape[1]
    l = x_ref.shape[1]
    nc = lc // c
    nt = (lc + l) // c
    ii = lax.broadcasted_iota(jnp.int32, (c2, c2), 0)
    jj = lax.broadcasted_iota(jnp.int32, (c2, c2), 1)
    same = (ii // c) == (jj // c)
    eye = (ii == jj).astype(F32)
    dirs = ((same & (ii >= jj), same & (ii > jj), c - 1),
            (same & (ii <= jj), same & (ii < jj), 0))

    def chains(g, slot):
        out = []
        for gi in range(grp):
            t = g * grp + gi
            cf = t
            cb = jnp.where(t < nc, nc - 1 - t, nt + nc - 1 - t)
            for p in range(npair):
                for d in range(2):
                    out.append((((slot * grp + gi) * 2 + d) * npair + p, p, d, cf if d == 0 else cb))
        return out

    def stage_inputs(idx, p, d, ci):
        incl, strict, last = dirs[d]
        lanes = slice(p * PAIR, (p + 1) * PAIR)
        nea, dtb = cst_ref[p, 0:1, :], cst_ref[p, 1:2, :]
        neat, dtbt = cstt_ref[p, 0], cstt_ref[p, 1]
        sl = pl.ds(pl.multiple_of(ci * c, c), c)
        abv = ab_ref[0, sl, p * LANES:(p + 1) * LANES]
        gall = nea * jax.nn.softplus(abv + dtb)
        ball = jax.nn.sigmoid(abv)
        shape = (c, PAIR)
        gl = _per_head(gall[:, 2 * d:2 * d + 1], gall[:, 2 * d + 1:2 * d + 2], shape)
        bl = _per_head(ball[:, 4 + 2 * d:5 + 2 * d], ball[:, 5 + 2 * d:6 + 2 * d], shape)
        gcum = _scan_sum(gl, 0, c, d == 1)
        gt = neat * jax.nn.softplus(abt_ref[0, p, ci] + dtbt)
        gtc = _scan_sum(gt, 1, c, d == 1)
        grow = jnp.concatenate([gtc[2 * d:2 * d + 1, 0:c], gtc[2 * d + 1:2 * d + 2, 0:c]], axis=1)
        gcol = jnp.concatenate([gcum[:, 0:1], gcum[:, HEAD_DIM:HEAD_DIM + 1]], axis=0)
        dec = jnp.where(incl, jnp.exp(jnp.where(incl, gcol - grow, 0.0)), 0.0)
        dec = jnp.where(ii == jj, 1.0, dec)
        q = qn[sl, lanes]
        k = kn[sl, lanes]
        v = vn[sl, lanes]
        kf = k.astype(F32)
        eg = jnp.exp(gcum)
        kb = kf * bl
        k_st = _stack_heads(k)
        a = jnp.where(strict, _dot_nt(_stack_heads(kb.astype(BF16)), k_st) * dec, 0.0)
        at_s[idx] = (_dot_nt(_stack_heads(q), k_st) * dec).astype(BF16)
        x_s[idx] = (-a).astype(BF16)
        t_s[idx] = eye - a
        rhs_s[idx] = jnp.concatenate([_stack_heads((v.astype(F32) * bl).astype(BF16)),
                                      _stack_heads((kb * eg).astype(BF16))], axis=1)
        qg_s[idx] = _stack_heads((q.astype(F32) * eg).astype(BF16))
        glast = gcum[last:last + 1, :]
        kd_s[idx] = _stack_heads(kf * jnp.exp(glast - gcum)).T.astype(BF16)
        egl_s[idx] = jnp.broadcast_to(jnp.exp(glast), (8, PAIR))

    def stage_group(g, slot):
        todo = chains(g, slot)
        for idx, p, d, ci in todo:
            stage_inputs(idx, p, d, ci)
        for _ in range(5):
            for idx, _, _, _ in todo:
                xb = x_s[idx]
                x_s[idx] = _dot(xb, xb).astype(BF16)
            for idx, _, _, _ in todo:
                t = t_s[idx]
                t_s[idx] = t + _dot(t.astype(BF16), x_s[idx])
        for idx, _, _, _ in todo:
            sol = _dot(t_s[idx].astype(BF16), rhs_s[idx])
            u_s[idx] = sol[:, :PAIR]
            w_s[idx] = sol[:, PAIR:].astype(BF16)

    def recur_group(g, slot):
        sts = {(p, d): st_s[d * npair + p] for p in range(npair) for d in range(2)}
        for idx, p, d, ci in chains(g, slot):
            sl = pl.ds(pl.multiple_of(ci * c, c), c)
            st = sts[(p, d)]
            stb = st.astype(BF16)
            v_new = (u_s[idx] - _dot(w_s[idx], stb)).astype(BF16)
            o_st = _dot(qg_s[idx], stb) + _dot(at_s[idx], v_new)
            oacc[sl, p * PAIR:(p + 1) * PAIR] += o_st[:c] + o_st[c:]
            sts[(p, d)] = st * egl_s[idx][0:1, :] + _dot(kd_s[idx], v_new)
        for (p, d), st in sts.items():
            st_s[d * npair + p] = st

    def prep(src_ref, r0):
        for s in range(D_GDN // LANES):
            for k, (dst, l2, scale) in enumerate(((qn, True, HEAD_DIM ** -0.5), (kn, True, 1.0), (vn, False, 1.0))):
                col = k * D_GDN + s * LANES
                _gdn_prep(src_ref, col, cw_ref[:, col:col + LANES], dst, r0, l2=l2, scale=scale)

    def finish(src_ref, out_ref, r0, n):
        for p in range(npair):
            lanes = slice(p * PAIR, (p + 1) * PAIR)
            o = oacc[r0:r0 + n, lanes]
            on = o * lax.rsqrt(_head_sumsq(o) * (1.0 / HEAD_DIM) + EPS)
            gate = src_ref[0, :, 3 * D_GDN + p * PAIR:3 * D_GDN + (p + 1) * PAIR].astype(F32)
            out_ref[0, :, lanes] = (on * ng_ref[...] * _silu(gate)).astype(out_ref.dtype)

    st_s[...] = jnp.zeros_like(st_s)
    oacc[...] = jnp.zeros_like(oacc)
    prep(xc_ref, 0)
    prep(x_ref, lc)

    ng = nt // grp
    stage_group(0, 0)

    def two_groups(kk, _):
        g = 2 * kk
        recur_group(g, 0)
        stage_group(g + 1, 1)
        recur_group(g + 1, 1)
        stage_group(g + 2, 0)
        return 0

    lax.fori_loop(0, (ng - 1) // 2, two_groups, 0)
    if ng % 2 == 0:
        recur_group(ng - 2, 0)
        stage_group(ng - 1, 1)
        recur_group(ng - 1, 1)
    else:
        recur_group(ng - 1, 0)
    finish(xc_ref, oc_ref, 0, lc)
    finish(x_ref, o_ref, lc, l)


def _gdn(gdn_l, ab_l, gdn_c, ab_c, conv_w, a_log, dt_bias, norm_g):
    b, l, _ = gdn_l.shape
    lc = gdn_c.shape[1]
    npair = D_GDN // PAIR
    lt = l + lc
    assert (lt // GDN_CHUNK) % GDN_GROUP == 0

    def tr(ab):
        x = jnp.stack([ab[:, :, 0:8], ab[:, :, LANES:LANES + 8]], axis=1)
        x = x.reshape(x.shape[0], npair, x.shape[2] // GDN_CHUNK, GDN_CHUNK, 8)
        return jnp.pad(jnp.transpose(x, (0, 1, 2, 4, 3)), ((0, 0),) * 4 + ((0, LANES - GDN_CHUNK),))

    nea = -jnp.exp(a_log.astype(F32)).reshape(2, npair, 2)
    dtb = dt_bias.astype(F32).reshape(2, npair, 2)
    rows = jnp.stack([jnp.transpose(nea, (1, 0, 2)).reshape(npair, 4), jnp.transpose(dtb, (1, 0, 2)).reshape(npair, 4)], axis=1)
    cst = jnp.pad(rows, ((0, 0), (0, 6), (0, LANES - 4)))
    cstt = jnp.broadcast_to(jnp.pad(rows, ((0, 0), (0, 0), (0, 4)))[..., None], (npair, 2, 8, LANES))
    cw = jnp.pad(conv_w.astype(F32), ((0, 8 - SHORT_CONV), (0, 0)))
    ng = jnp.tile(norm_g.astype(F32), 2).reshape(1, PAIR)
    nstage = 2 * GDN_GROUP * 2 * npair

    def whole(shape):
        nd = len(shape)
        return pl.BlockSpec(shape, lambda i: (0,) * nd)

    def per_batch(shape, single=False):
        nd = len(shape)
        return pl.BlockSpec((1,) + shape, lambda i: (i,) + (0,) * nd, pipeline_mode=pl.Buffered(1) if single else None)

    sq = (nstage, PAIR, PAIR)
    return pl.pallas_call(
        _gdn_body,
        grid=(b,),
        in_specs=[per_batch((l, 4 * D_GDN), True), per_batch((lc, 4 * D_GDN)), per_batch((lt, 2 * LANES), True),
                  per_batch((npair, lt // GDN_CHUNK, 8, LANES)),
                  whole((8, 3 * D_GDN)), whole((npair, 8, LANES)), whole((npair, 2, 8, LANES)), whole((1, PAIR))],
        out_specs=[per_batch((l, D_GDN)), per_batch((lc, D_GDN))],
        out_shape=[jax.ShapeDtypeStruct((b, l, D_GDN), BF16), jax.ShapeDtypeStruct((b, lc, D_GDN), BF16)],
        scratch_shapes=[pltpu.VMEM((lt, D_GDN), BF16)] * 3 + [pltpu.VMEM((lt, D_GDN), F32)]
        + [pltpu.VMEM((2 * npair, PAIR, PAIR), F32), pltpu.VMEM(sq, BF16), pltpu.VMEM(sq, F32),
           pltpu.VMEM((nstage, PAIR, 2 * PAIR), BF16), pltpu.VMEM(sq, F32)]
        + [pltpu.VMEM(sq, BF16)] * 4 + [pltpu.VMEM((nstage, 8, PAIR), F32)],
        compiler_params=_cparams(("arbitrary",), VMEM_LIMIT),
        name="gated_deltanet",
    )(gdn_l, gdn_c, jnp.concatenate([ab_c, ab_l], axis=1), jnp.concatenate([tr(ab_c), tr(ab_l)], axis=2), cw, cst, cstt, ng)


def _na_bias_tiles(rpb, rows):
    w = GRID_W
    ext = jnp.pad(rpb.astype(F32), ((0, 0), (0, 0), (w - NA_KW, w - NA_KW)))
    tb = jnp.stack([ext[:, :, w - 1 - cq:2 * w - 1 - cq] for cq in range(w)], axis=2)
    cq = np.arange(w)[:, None]
    ck = np.arange(w)[None, :]
    ws = np.clip(cq - NA_KW // 2, 0, w - NA_KW)
    col_ok = (ck >= ws) & (ck < ws + NA_KW)
    tb = jnp.where(jnp.asarray(col_ok)[None, None], tb, NEG_INF)
    tb = jnp.concatenate([tb, jnp.full((tb.shape[0], 1, w, w), NEG_INF, F32)], axis=1)
    nt = rows // NA_QROWS
    idx = np.zeros((3, NA_QROWS, NA_KROWS), np.int32)
    for cls, t in enumerate((0, 1, nt - 1)):
        ks = int(np.clip(NA_QROWS * t - NA_KH // 2, 0, rows - NA_KROWS))
        for rl in range(NA_QROWS):
            r = NA_QROWS * t + rl
            r0 = int(np.clip(r - NA_KH // 2, 0, rows - NA_KH))
            for j in range(NA_KROWS):
                kr = ks + j
                idx[cls, rl, j] = kr - r + NA_KH - 1 if r0 <= kr < r0 + NA_KH else 2 * NA_KH - 1
    tiles = tb[:, jnp.asarray(idx)]
    tiles = jnp.transpose(tiles, (1, 0, 2, 4, 3, 5))
    return tiles.reshape(3, tb.shape[0], NA_QROWS * w, NA_KROWS * w)


def _softmax_pv(s_parts, v_parts):
    m = functools.reduce(jnp.maximum, [jnp.max(s, axis=-1, keepdims=True) for s in s_parts])
    ps = [jnp.exp(s - m) for s in s_parts]
    den = functools.reduce(lambda x, y: x + y, [jnp.sum(p, axis=-1, keepdims=True) for p in ps])
    o = functools.reduce(lambda x, y: x + y, [_dot(p.astype(BF16), v) for p, v in zip(ps, v_parts)])
    return o / den


def _na_body(q_ref, k_ref, v_ref, qc_ref, kc_ref, vc_ref, bias_ref, o_ref, oc_ref, *, rows):
    w = GRID_W
    tq = NA_QROWS * w
    tk = NA_KROWS * w
    nt = rows // NA_QROWS
    scale = HEAD_DIM ** -0.5
    kc = kc_ref[0]
    vc = vc_ref[0]
    m0, m1 = _head_masks((tq, PAIR))
    masks = (m0, m1)

    def tile(t, _):
        cls = jnp.where(t > 0, 1, 0) + jnp.where(t == nt - 1, 1, 0)
        ks = jnp.clip(NA_QROWS * t - NA_KH // 2, 0, rows - NA_KROWS) * w
        ksl = pl.ds(pl.multiple_of(ks, w), tk)
        qsl = pl.ds(pl.multiple_of(t * tq, tq), tq)
        q = q_ref[0, qsl, :] * scale
        kb = k_ref[0, ksl, :]
        vb = v_ref[0, ksl, :]
        acc = jnp.zeros((tq, PAIR), F32)
        for hh in range(2):
            qm = jnp.where(masks[hh], q, jnp.zeros_like(q))
            s_loc = _dot_nt(qm, kb) + bias_ref[cls, hh]
            s_ctx = _dot_nt(qm, kc)
            acc = acc + jnp.where(masks[hh], _softmax_pv([s_loc, s_ctx], [vb, vc]), 0.0)
        o_ref[0, qsl, :] = acc.astype(o_ref.dtype)
        return 0

    lax.fori_loop(0, nt, tile, 0)
    qc = qc_ref[0] * scale
    mc0, mc1 = _head_masks(qc.shape)
    accc = jnp.zeros(qc.shape, F32)
    for hh, mk in enumerate((mc0, mc1)):
        qm = jnp.where(mk, qc, jnp.zeros_like(qc))
        accc = accc + jnp.where(mk, _softmax_pv([_dot_nt(qm, kc)], [vc]), 0.0)
    oc_ref[0] = accc.astype(oc_ref.dtype)


def _neighbourhood_attention(na_l, na_c, rpb):
    b, l, _ = na_l.shape
    lc = na_c.shape[1]
    rows = l // GRID_W
    npair = D_NA // PAIR
    bias = _na_bias_tiles(rpb, rows)

    def col(k):
        return lambda p, i: (i, 0, k * npair + p)

    lat = [pl.BlockSpec((1, l, PAIR), col(k)) for k in range(3)]
    ctx = [pl.BlockSpec((1, lc, PAIR), col(k)) for k in range(3)]
    return pl.pallas_call(
        functools.partial(_na_body, rows=rows),
        grid=(npair, b),
        in_specs=lat + ctx + [pl.BlockSpec((3, 2, NA_QROWS * GRID_W, NA_KROWS * GRID_W), lambda p, i: (0, p, 0, 0))],
        out_specs=[pl.BlockSpec((1, l, PAIR), lambda p, i: (i, 0, p)),
                   pl.BlockSpec((1, lc, PAIR), lambda p, i: (i, 0, p))],
        out_shape=[jax.ShapeDtypeStruct((b, l, D_NA), BF16), jax.ShapeDtypeStruct((b, lc, D_NA), BF16)],
        compiler_params=_cparams(("arbitrary", "arbitrary"), VMEM_LIMIT),
        name="neighbourhood_attention",
    )(na_l, na_l, na_l, na_c, na_c, na_c, bias)


def _outproj_residual(r_ref, g_ref, n_ref, x_ref, mod_ref, wo_ref):
    mix = jnp.concatenate([r_ref[0], g_ref[0], n_ref[0]], axis=1)
    return x_ref[0] + mod_ref[0, 2:3, :] * _dot(mix, wo_ref[...])


def _swiglu_chunk(hb, w1_ref, w3_ref, w2_ref, j, between=None):
    cols = slice(j * FF_CHUNK, (j + 1) * FF_CHUNK)
    between = between or (lambda k: None)
    between(0)
    a = _dot(hb, w1_ref[:, cols])
    between(1)
    t = (_silu(a) * _dot(hb, w3_ref[:, cols])).astype(BF16)
    between(2)
    return _dot(t, w2_ref[cols, :])


def _swiglu_chunks(hb, w1_ref, w3_ref, w2_ref, acc_ref):
    for j in range(w2_ref.shape[0] // FF_CHUNK):
        if j == 0:
            acc_ref[...] = _swiglu_chunk(hb, w1_ref, w3_ref, w2_ref, j)
        else:
            acc_ref[...] += _swiglu_chunk(hb, w1_ref, w3_ref, w2_ref, j)


def _dense_body(r_ref, g_ref, n_ref, x_ref, mod_ref, g2_ref, wo_ref, w1_ref, w3_ref, w2_ref, o_ref, acc_ref):
    x1 = _outproj_residual(r_ref, g_ref, n_ref, x_ref, mod_ref, wo_ref)
    hb = _norm_mod(x1, g2_ref[...], mod_ref[0, 3:4, :], mod_ref[0, 4:5, :]).astype(BF16)
    _swiglu_chunks(hb, w1_ref, w3_ref, w2_ref, acc_ref)
    o_ref[0] = x1 + mod_ref[0, 5:6, :] * acc_ref[...]


def _pack_bf16_pairs(h):
    m = h.shape[1] // 2
    bits = lax.bitcast_convert_type(h.astype(BF16).astype(F32), jnp.uint32)
    return (bits[:, :m] >> 16) | (bits[:, m:] & jnp.uint32(0xFFFF0000))


def _unpack_bf16_pairs(u):
    lo = lax.bitcast_convert_type(u << 16, F32)
    hi = lax.bitcast_convert_type(u & jnp.uint32(0xFFFF0000), F32)
    return jnp.concatenate([lo, hi], axis=1).astype(BF16)


def _router_body(r_ref, g_ref, n_ref, x_ref, mod_ref, g2_ref, wo_ref, wr_ref, x1_ref, h_ref, lg_ref):
    x1 = _outproj_residual(r_ref, g_ref, n_ref, x_ref, mod_ref, wo_ref)
    h = _norm_mod(x1, g2_ref[...], mod_ref[0, 3:4, :], mod_ref[0, 4:5, :])
    x1_ref[0] = x1
    h_ref[0] = _pack_bf16_pairs(h)
    lg_ref[0] = _dot_hi(h, wr_ref[...])


def _resident(shape):
    nd = len(shape)
    return pl.BlockSpec(shape, lambda i, j: (0,) * nd, pipeline_mode=pl.Buffered(1))


def _mixer_specs(tm, d):
    return [pl.BlockSpec((1, tm, D_RET), lambda i, j: (i, j, 0)),
            pl.BlockSpec((1, tm, D_GDN), lambda i, j: (i, j, 0)),
            pl.BlockSpec((1, tm, D_NA), lambda i, j: (i, j, 0)),
            pl.BlockSpec((1, tm, d), lambda i, j: (i, j, 0)),
            pl.BlockSpec((1, 6, d), lambda i, j: (i, 0, 0)),
            pl.BlockSpec((1, d), lambda i, j: (0, 0))]


def _dense_block(r, g, n, x, mod, g2, wo, w1, w3, w2):
    b, l, d = x.shape
    tm = min(512, l)
    dff = w1.shape[1]
    return pl.pallas_call(
        _dense_body,
        grid=(b, l // tm),
        in_specs=_mixer_specs(tm, d) + [_resident((d, d)), _resident((d, dff)), _resident((d, dff)), _resident((dff, d))],
        out_specs=pl.BlockSpec((1, tm, d), lambda i, j: (i, j, 0)),
        out_shape=jax.ShapeDtypeStruct((b, l, d), F32),
        scratch_shapes=[pltpu.VMEM((tm, d), F32)],
        compiler_params=_cparams(("arbitrary", "arbitrary"), VMEM_LIMIT),
        name="outproj_swiglu",
    )(r, g, n, x, mod, g2, wo, w1, w3, w2)


def _router_block(r, g, n, x, mod, g2, wo, wr):
    b, l, d = x.shape
    tm = min(512, l)
    blk = pl.BlockSpec((1, tm, d), lambda i, j: (i, j, 0))
    return pl.pallas_call(
        _router_body,
        grid=(b, l // tm),
        in_specs=_mixer_specs(tm, d) + [_resident((d, d)), _resident((d, LANES))],
        out_specs=[blk, pl.BlockSpec((1, tm, d // 2), lambda i, j: (i, j, 0)), pl.BlockSpec((1, tm, LANES), lambda i, j: (i, j, 0))],
        out_shape=[jax.ShapeDtypeStruct((b, l, d), F32), jax.ShapeDtypeStruct((b, l, d // 2), jnp.uint32),
                   jax.ShapeDtypeStruct((b, l, LANES), F32)],
        compiler_params=_cparams(("arbitrary", "arbitrary"), VMEM_LIMIT),
        name="outproj_router",
    )(r, g, n, x, mod, g2, wo, wr)


def _row_copy_in(idx_ref, r, src_hbm, dst_ref, sem):
    return pltpu.make_async_copy(src_hbm.at[pl.ds(idx_ref[0, 0, r], 1), :], dst_ref.at[pl.ds(r, 1), :], sem)


def _row_copy_out(idx_ref, r, src_ref, dst_hbm, sem):
    return pltpu.make_async_copy(src_ref.at[pl.ds(r, 1), :], dst_hbm.at[pl.ds(idx_ref[0, 0, r], 1), :], sem)


def _moe_body(bval_ref, bexp_ref, tok_ref, tokn_ref, slotp_ref, h_hbm, w1_ref, w3_ref, w2_ref, y_hbm,
              xbuf, ybuf, gsem, ssem, *, n_asg):
    i = pl.program_id(0)
    tb = xbuf.shape[1]
    nch = w2_ref.shape[1] // FF_CHUNK
    cur = i % 2
    per = -(-tb // (3 * nch))
    valid = bval_ref[i] == 1
    prev_valid = bval_ref[jnp.maximum(i - 1, 0)] == 1

    def wait_gather(slot):
        pltpu.make_async_copy(h_hbm.at[pl.ds(0, tb), :], xbuf.at[slot], gsem).wait()

    def wait_scatter(slot):
        pltpu.make_async_copy(ybuf.at[slot], y_hbm.at[pl.ds(0, tb), :], ssem).wait()

    @pl.when(i == 0)
    def _():
        ybuf[1] = jnp.zeros((tb, ybuf.shape[2]), F32)
        fills = [pltpu.make_async_copy(ybuf.at[1], y_hbm.at[pl.ds(r0, tb), :], ssem)
                 for r0 in range(n_asg, y_hbm.shape[0], tb)]
        for f in fills:
            f.start()
        for f in fills:
            f.wait()

        def issue(r, _):
            _row_copy_in(tok_ref, r, h_hbm, xbuf.at[0], gsem).start()
            return 0

        lax.fori_loop(0, tb, issue, 0)

    @pl.when(valid)
    def _():
        wait_gather(cur)
        hb = _unpack_bf16_pairs(xbuf[cur])
        acc = ybuf.at[cur]
        prv = ybuf.at[1 - cur]
        nxt = xbuf.at[1 - cur]
        for j in range(nch):
            def copies(k, j=j):
                g = 3 * j + k
                for r in range(g * per, min(tb, (g + 1) * per)):
                    _row_copy_in(tokn_ref, r, h_hbm, nxt, gsem).start()
                    _row_copy_out(slotp_ref, r, prv, y_hbm, ssem).start()

            part = _swiglu_chunk(hb, w1_ref.at[0], w3_ref.at[0], w2_ref.at[0], j, copies)
            if j == 0:
                acc[...] = part
            else:
                acc[...] += part
        wait_scatter(1 - cur)

    @pl.when(jnp.logical_not(valid) & prev_valid & (i > 0))
    def _():
        wait_gather(cur)

        def issue(r, _):
            _row_copy_out(slotp_ref, r, ybuf.at[1 - cur], y_hbm, ssem).start()
            return 0

        lax.fori_loop(0, tb, issue, 0)
        wait_scatter(1 - cur)


def _moe_experts(h_rows, row_tok, row_slot, block_expert, block_valid, w1, w3, w2, n_out_rows, n_spare):
    d = w2.shape[-1]
    dff = w2.shape[1]
    nb = block_expert.shape[0]
    tb = MOE_ROWS
    smem = functools.partial(pl.BlockSpec, (1, 1, tb), memory_space=pltpu.SMEM)
    grid_spec = pltpu.PrefetchScalarGridSpec(
        num_scalar_prefetch=2,
        grid=(nb,),
        in_specs=[smem(lambda i, bv, be: (i, 0, 0)),
                  smem(lambda i, bv, be: (jnp.minimum(i + 1, nb - 1), 0, 0)),
                  smem(lambda i, bv, be: (i, 0, 0)),
                  pl.BlockSpec(memory_space=pl.ANY),
                  pl.BlockSpec((1, d, dff), lambda i, bv, be: (be[i], 0, 0)),
                  pl.BlockSpec((1, d, dff), lambda i, bv, be: (be[i], 0, 0)),
                  pl.BlockSpec((1, dff, d), lambda i, bv, be: (be[i], 0, 0))],
        out_specs=pl.BlockSpec(memory_space=pl.ANY),
        scratch_shapes=[pltpu.VMEM((2, tb, d // 2), jnp.uint32), pltpu.VMEM((2, tb, d), F32),
                        pltpu.SemaphoreType.DMA(()), pltpu.SemaphoreType.DMA(())],
    )
    return pl.pallas_call(
        functools.partial(_moe_body, n_asg=n_out_rows - n_spare),
        grid_spec=grid_spec,
        out_shape=jax.ShapeDtypeStruct((n_out_rows, d), F32),
        compiler_params=_cparams(("arbitrary",), VMEM_LIMIT),
        name="moe_experts",
    )(block_valid, block_expert, row_tok.reshape(nb, 1, tb), row_tok.reshape(nb, 1, tb), row_slot.reshape(nb + 1, 1, tb),
      h_rows, w1, w3, w2)


def _combine_body(y0_ref, y1_ref, x1_ref, gate_ref, gm_ref, fg_ref, o_ref, *, final):
    moe = gate_ref[:, 0:1] * y0_ref[...] + gate_ref[:, 1:2] * y1_ref[...]
    x2 = x1_ref[...] + gm_ref[0] * moe
    if final:
        ms = jnp.mean(x2 * x2, axis=-1, keepdims=True)
        x2 = x2 * lax.rsqrt(ms + EPS) * fg_ref[...]
    o_ref[...] = x2


def _moe_combine(y_rows, x1, gates, gate_mlp, final_g, final):
    b, l, d = x1.shape
    n_tok = b * l
    tm = 256
    per_b = l // tm
    out = pl.pallas_call(
        functools.partial(_combine_body, final=final),
        grid=(n_tok // tm,),
        in_specs=[pl.BlockSpec((tm, d), lambda i: (i, 0)),
                  pl.BlockSpec((tm, d), lambda i: (n_tok // tm + i, 0)),
                  pl.BlockSpec((tm, d), lambda i: (i, 0)),
                  pl.BlockSpec((tm, 2), lambda i: (i, 0)),
                  pl.BlockSpec((1, 1, d), lambda i: (i // per_b, 0, 0)),
                  pl.BlockSpec((1, d), lambda i: (0, 0))],
        out_specs=pl.BlockSpec((tm, d), lambda i: (i, 0)),
        out_shape=jax.ShapeDtypeStruct((n_tok, d), F32),
        compiler_params=_cparams(("arbitrary",), VMEM_LIMIT),
        name="moe_combine_final_norm",
    )(y_rows, y_rows, x1.reshape(n_tok, d), gates, gate_mlp, final_g.reshape(1, d))
    return out.reshape(b, l, d)


def _route(logits):
    n_tok = logits.shape[0]
    tb = MOE_ROWS
    n_asg = n_tok * TOP_K
    top_logit, top_e = lax.top_k(logits[:, :N_EXPERTS], TOP_K)
    gates = jax.nn.softmax(top_logit, axis=-1)
    e_flat = top_e.reshape(-1).astype(jnp.int32)
    asg = jnp.arange(n_asg, dtype=jnp.int32)
    by_expert = lax.sort(e_flat * n_asg + asg)
    counts = jnp.sum((e_flat[:, None] == jnp.arange(N_EXPERTS, dtype=jnp.int32)[None, :]).astype(jnp.int32), axis=0)
    starts = jnp.cumsum(counts) - counts
    padded = (counts + tb - 1) // tb * tb
    pad_ends = jnp.cumsum(padded)
    pad_starts = pad_ends - padded
    nb = (n_asg + tb - 1) // tb + N_EXPERTS + 1
    n_rows = nb * tb
    block_start = jnp.arange(nb, dtype=jnp.int32) * tb
    block_expert = jnp.minimum(jnp.searchsorted(pad_ends, block_start, side='right'), N_EXPERTS - 1).astype(jnp.int32)
    block_valid = (block_start < pad_ends[-1]).astype(jnp.int32)
    row = jnp.arange(n_rows, dtype=jnp.int32)
    row_e = jnp.repeat(block_expert, tb)
    off = row - pad_starts[row_e]
    is_pad = (off >= counts[row_e]) | (jnp.repeat(block_valid, tb) == 0)
    src = by_expert[jnp.clip(starts[row_e] + off, 0, n_asg - 1)] - row_e * n_asg
    row_asg = jnp.where(is_pad, -1, src)
    spare = tb + n_asg + jnp.cumsum(is_pad.astype(jnp.int32)) - 1
    row_slot = jnp.where(is_pad, spare, (row_asg % TOP_K) * n_tok + row_asg // TOP_K)
    row_tok = jnp.where(is_pad, 0, row_asg // TOP_K)
    row_slot = jnp.concatenate([n_asg + jnp.arange(tb, dtype=jnp.int32), row_slot])
    last_e = block_expert[jnp.maximum(pad_ends[-1] // tb - 1, 0)]
    block_expert = jnp.where(block_valid == 1, block_expert, last_e)
    n_spare = (tb + n_rows - n_asg + tb - 1) // tb * tb
    return row_tok, row_slot, gates, block_expert, block_valid, n_asg + n_spare, n_spare


def _final_norm_body(x_ref, g_ref, o_ref):
    x = x_ref[...]
    ms = jnp.mean(x * x, axis=-1, keepdims=True)
    o_ref[...] = x * lax.rsqrt(ms + EPS) * g_ref[...]


def _final_norm(x, g):
    b, l, d = x.shape
    n = b * l
    tm = min(512, n)
    out = pl.pallas_call(
        _final_norm_body,
        grid=(n // tm,),
        in_specs=[pl.BlockSpec((tm, d), lambda i: (i, 0)), pl.BlockSpec((1, d), lambda i: (0, 0))],
        out_specs=pl.BlockSpec((tm, d), lambda i: (i, 0)),
        out_shape=jax.ShapeDtypeStruct((n, d), F32),
        compiler_params=_cparams(("arbitrary",)),
        name="final_norm",
    )(x.reshape(n, d), g.reshape(1, d))
    return out.reshape(b, l, d)


def _rope_tables(n_tok):
    t = jnp.arange(n_tok, dtype=jnp.int32)
    row = (t // GRID_W).astype(F32)
    col = (t % GRID_W).astype(F32)
    inv_freq = ROPE_BASE ** (-jnp.arange(N_FREQ, dtype=F32) / N_FREQ)
    ang = jnp.concatenate([row[:, None] * inv_freq, col[:, None] * inv_freq], axis=-1)
    cos, sin = jnp.cos(ang), jnp.sin(ang)
    cosf = jnp.concatenate([cos, cos, cos, cos], axis=-1)
    sins = jnp.concatenate([-sin, sin, -sin, sin], axis=-1)
    return cosf, sins


def kernel(x, c, ctx, c_ctx, ada_w, ada_b, norm1_g, norm2_g, w_in, w_out, conv_w, ret_decay, gdn_a_log, gdn_dt_bias,
           gdn_norm_g, na_rpb, ffn_w1, ffn_w3, ffn_w2, moe_router, moe_w1, moe_w3, moe_w2, final_g):
    b, l, d = x.shape
    lc = ctx.shape[1]
    depth = ada_w.shape[0]
    cosf, sins = _rope_tables(l)
    ones_c = jnp.ones((lc, LANES), F32)
    zeros_c = jnp.zeros((lc, LANES), F32)

    rows = ((b + 1 + 7) // 8) * 8
    c_all = jnp.zeros((rows, d), F32).at[:b].set(c).at[b].set(c_ctx)
    mod = _ada_vectors(c_all, ada_w, ada_b).reshape(depth, rows, 6, d)

    y = ctx
    for layer in range(depth):
        need_ctx = layer < depth - 1
        mod_l = mod[layer, :b]
        mod_c = jnp.broadcast_to(mod[layer, b][None], (b, 6, d))
        w1p = _pack_w_in(w_in[layer])
        g1 = norm1_g[layer].reshape(1, d)
        ret_l, gdn_l, ab_l, na_l = _in_projection(x, mod_l, g1, cosf, sins, w1p, rope=True)
        ret_c, gdn_c, ab_c, na_c = _in_projection(y, mod_c, g1, ones_c, zeros_c, w1p, rope=False)

        log_gamma = jnp.log1p(-jnp.exp2(-ret_decay[layer].astype(F32)))
        r_l, r_c = _retention(ret_l, ret_c, log_gamma)
        g_l, g_c = _gdn(gdn_l, ab_l, gdn_c, ab_c, conv_w[layer], gdn_a_log[layer], gdn_dt_bias[layer], gdn_norm_g[layer])
        n_l, n_c = _neighbourhood_attention(na_l, na_c, na_rpb[layer])

        wo = w_out[layer].astype(BF16)
        g2 = norm2_g[layer].reshape(1, d)
        j = layer // 2
        if layer % 2 == 0:
            w1, w3, w2 = ffn_w1[j].astype(BF16), ffn_w3[j].astype(BF16), ffn_w2[j].astype(BF16)
            x = _dense_block(r_l, g_l, n_l, x, mod_l, g2, wo, w1, w3, w2)
            if need_ctx:
                y = _dense_block(r_c, g_c, n_c, y, mod_c, g2, wo, w1, w3, w2)
            if layer == depth - 1:
                x = _final_norm(x, final_g)
        else:
            wr = jnp.pad(moe_router[j].astype(F32), ((0, 0), (0, LANES - N_EXPERTS)))
            w1, w3, w2 = moe_w1[j].astype(BF16), moe_w3[j].astype(BF16), moe_w2[j].astype(BF16)

            def moe_ffn(r, g, n, xin, m, last):
                bb, ll, _ = xin.shape
                x1, h, logits = _router_block(r, g, n, xin, m, g2, wo, wr)
                row_tok, row_slot, gates, bexp, bval, n_out, n_spare = _route(logits.reshape(bb * ll, LANES))
                y_rows = _moe_experts(h.reshape(bb * ll, d // 2), row_tok, row_slot, bexp, bval, w1, w3, w2, n_out, n_spare)
                return _moe_combine(y_rows, x1, gates, m[:, 5:6, :], final_g, last)

            x = moe_ffn(r_l, g_l, n_l, x, mod_l, layer == depth - 1)
            if need_ctx:
                y = moe_ffn(r_c, g_c, n_c, y, mod_c, False)
    return x
```

```python
import functools

import numpy as np
import jax
import jax.numpy as jnp
from jax import lax
from jax.experimental import pallas as pl
from jax.experimental.pallas import tpu as pltpu

F32 = jnp.float32
BF16 = jnp.bfloat16
HIGHEST = lax.Precision.HIGHEST

LANES = 128
HEAD_DIM = 64
PAIR = 2 * HEAD_DIM
GRID_W = 64
H_RET, H_GDN, H_NA = 4, 4, 8
D_RET, D_GDN, D_NA = H_RET * HEAD_DIM, H_GDN * HEAD_DIM, H_NA * HEAD_DIM
RET_CHUNK = 128
GDN_CHUNK = 64
GDN_GROUP = 4
GDN_PREP_ROWS = 512
SHORT_CONV = 5
NA_KH, NA_KW = 8, 16
NA_QROWS = 4
NA_KROWS = NA_QROWS + NA_KH
NA_PAIRS = 1
N_FREQ = HEAD_DIM // 4
ROPE_BASE = 10000.0
N_EXPERTS = 8
TOP_K = 2
MOE_ROWS = 512
FF_CHUNK = 256
EPS = 1e-6
NEG_INF = -1e30
VMEM_LIMIT = 56 * 1024 * 1024

C_RET = 0
C_GDN = C_RET + 4 * D_RET
C_AB = C_GDN + 4 * D_GDN
C_NA = C_AB + 2 * LANES
C_END = C_NA + 3 * D_NA


def _cparams(sem, vmem=None):
    return pltpu.CompilerParams(dimension_semantics=sem, vmem_limit_bytes=vmem)


def _silu(x):
    return x * jax.nn.sigmoid(x)


def _dot(a, b):
    return jnp.dot(a, b, preferred_element_type=F32)


def _dot_nt(a, b):
    return lax.dot_general(a, b, (((1,), (1,)), ((), ())), preferred_element_type=F32)


def _dot_tn(a, b):
    return lax.dot_general(a, b, (((0,), (0,)), ((), ())), preferred_element_type=F32)


def _dot_hi(a, b):
    return jnp.dot(a, b, preferred_element_type=F32, precision=HIGHEST)


def _ada_body(c_ref, w_ref, b_ref, o_ref):
    s = _silu(c_ref[...])
    o_ref[0] = _dot_hi(s, w_ref[0]) + b_ref[0]


def _ada_vectors(c_all, ada_w, ada_b):
    depth, d, d6 = ada_w.shape
    rows = c_all.shape[0]
    tn = 1024
    return pl.pallas_call(
        _ada_body,
        grid=(depth, d6 // tn),
        in_specs=[pl.BlockSpec((rows, d), lambda l, j: (0, 0)),
                  pl.BlockSpec((1, d, tn), lambda l, j: (l, 0, j)),
                  pl.BlockSpec((1, 1, tn), lambda l, j: (l, 0, j))],
        out_specs=pl.BlockSpec((1, rows, tn), lambda l, j: (l, 0, j)),
        out_shape=jax.ShapeDtypeStruct((depth, rows, d6), F32),
        compiler_params=_cparams(("arbitrary", "arbitrary")),
        name="ada_vectors",
    )(c_all, ada_w, ada_b.reshape(depth, 1, d6))


def _norm_mod(x, g, shift, scale):
    ms = jnp.mean(x * x, axis=-1, keepdims=True)
    return (x * lax.rsqrt(ms + EPS) * g) * (1.0 + scale) + shift


def _rope_slab(t, cosf, sins):
    lane = lax.broadcasted_iota(jnp.int32, t.shape, 1)
    first = (lane % HEAD_DIM) < (HEAD_DIM // 2)
    partner = jnp.where(first, pltpu.roll(t, LANES - HEAD_DIM // 2, 1), pltpu.roll(t, HEAD_DIM // 2, 1))
    return t * cosf + partner * sins


def _inproj_body(x_ref, mod_ref, g_ref, cos_ref, sin_ref, w_ref, ret_ref, gdn_ref, ab_ref, na_ref, *, rope):
    h = _norm_mod(x_ref[0], g_ref[...], mod_ref[0, 0:1, :], mod_ref[0, 1:2, :]).astype(BF16)
    qk = _dot(h, w_ref[:, C_RET:C_RET + 2 * D_RET])
    slabs = []
    for s in range(2 * D_RET // LANES):
        t = qk[:, s * LANES:(s + 1) * LANES]
        if rope:
            t = _rope_slab(t, cos_ref[...], sin_ref[...])
        if s >= D_RET // LANES:
            t = t * HEAD_DIM ** -0.5
        slabs.append(t)
    ret_ref[0, :, 0:2 * D_RET] = jnp.concatenate(slabs, axis=1).astype(BF16)
    ret_ref[0, :, 2 * D_RET:] = _dot(h, w_ref[:, C_RET + 2 * D_RET:C_GDN]).astype(BF16)
    for j in range(2):
        gdn_ref[0, :, 512 * j:512 * (j + 1)] = _dot(h, w_ref[:, C_GDN + 512 * j:C_GDN + 512 * (j + 1)]).astype(BF16)
    ab_ref[0] = _dot(h, w_ref[:, C_AB:C_NA])
    for j in range(3):
        na_ref[0, :, 512 * j:512 * (j + 1)] = _dot(h, w_ref[:, C_NA + 512 * j:C_NA + 512 * (j + 1)]).astype(BF16)


def _in_projection(x, mod, g, cosf, sins, w, rope):
    b, l, d = x.shape
    tm = min(512, l)
    body = functools.partial(_inproj_body, rope=rope)
    return pl.pallas_call(
        body,
        grid=(b, l // tm),
        in_specs=[pl.BlockSpec((1, tm, d), lambda i, j: (i, j, 0)),
                  pl.BlockSpec((1, 6, d), lambda i, j: (i, 0, 0)),
                  pl.BlockSpec((1, d), lambda i, j: (0, 0)),
                  pl.BlockSpec((tm, LANES), lambda i, j: (j, 0)),
                  pl.BlockSpec((tm, LANES), lambda i, j: (j, 0)),
                  pl.BlockSpec((d, C_END), lambda i, j: (0, 0))],
        out_specs=[pl.BlockSpec((1, tm, 4 * D_RET), lambda i, j: (i, j, 0)),
                   pl.BlockSpec((1, tm, 4 * D_GDN), lambda i, j: (i, j, 0)),
                   pl.BlockSpec((1, tm, 2 * LANES), lambda i, j: (i, j, 0)),
                   pl.BlockSpec((1, tm, 3 * D_NA), lambda i, j: (i, j, 0))],
        out_shape=[jax.ShapeDtypeStruct((b, l, 4 * D_RET), BF16),
                   jax.ShapeDtypeStruct((b, l, 4 * D_GDN), BF16),
                   jax.ShapeDtypeStruct((b, l, 2 * LANES), F32),
                   jax.ShapeDtypeStruct((b, l, 3 * D_NA), BF16)],
        compiler_params=_cparams(("arbitrary", "arbitrary"), VMEM_LIMIT),
        name="in_projection",
    )(x, mod, g, cosf, sins, w)


def _pack_w_in(w_in):
    w_in = w_in.astype(BF16)
    d = w_in.shape[0]
    c1 = 4 * D_RET
    c2 = c1 + 4 * D_GDN
    ab = w_in[:, c2:c2 + 4 * H_GDN]
    ab = ab.reshape(d, 2, 2, H_GDN // 2, 2)
    ab = jnp.transpose(ab, (0, 3, 1, 2, 4)).reshape(d, 2, 8)
    ab = jnp.pad(ab, ((0, 0), (0, 0), (0, LANES - 8))).reshape(d, 2 * LANES)
    return jnp.concatenate([w_in[:, :c2], ab, w_in[:, c2 + 4 * H_GDN:]], axis=1)


def _head_masks(shape):
    lane = lax.broadcasted_iota(jnp.int32, shape, len(shape) - 1)
    return lane < HEAD_DIM, lane >= HEAD_DIM


def _per_head(lo, hi, shape):
    m0, _ = _head_masks(shape)
    return jnp.where(m0, lo, hi)


def _head_sumsq(o):
    m0, m1 = _head_masks(o.shape)
    sq = o * o
    s0 = jnp.sum(jnp.where(m0, sq, 0.0), axis=-1, keepdims=True)
    s1 = jnp.sum(jnp.where(m1, sq, 0.0), axis=-1, keepdims=True)
    return jnp.where(m0, s0, s1)


def _ret_body(lg_ref, ql, kl, vl, gl, qc, kc, vc, gc, r_ref, rc_ref, sb_ref):
    c = RET_CHUNK
    npair = D_RET // PAIR
    lane_shape = (c, PAIR)
    pos = lax.broadcasted_iota(jnp.int32, lane_shape, 0).astype(F32)
    ii = lax.broadcasted_iota(jnp.int32, (c, c), 0)
    jj = lax.broadcasted_iota(jnp.int32, (c, c), 1)
    diff = (ii - jj).astype(F32)
    m0, m1 = _head_masks(lane_shape)
    masks = (m0, m1)
    bi = lax.broadcasted_iota(jnp.int32, (PAIR, PAIR), 0) // HEAD_DIM
    bj = lax.broadcasted_iota(jnp.int32, (PAIR, PAIR), 1) // HEAD_DIM
    bd = bi == bj
    cst = []
    for p in range(npair):
        lgf = _per_head(lg_ref[0, 2 * p], lg_ref[0, 2 * p + 1], lane_shape)
        lgb = _per_head(lg_ref[1, 2 * p], lg_ref[1, 2 * p + 1], lane_shape)
        dmats = [jnp.where(diff > 0, jnp.exp(lg_ref[0, 2 * p + hh] * diff),
                           jnp.where(diff < 0, jnp.exp(-lg_ref[1, 2 * p + hh] * diff), 2.0)) for hh in range(2)]
        cst.append(dict(qdf=jnp.exp(lgf * (pos + 1.0)), kdf=jnp.exp(lgf * (c - 1.0 - pos)),
                        qdb=jnp.exp(lgb * (c - pos)), kdb=jnp.exp(lgb * pos),
                        cdf=jnp.exp(lgf[0:1] * c), cdb=jnp.exp(lgb[0:1] * c), dmats=dmats))

    def sweep(q_ref, k_ref, v_ref, g_ref, o_ref, n, sf0, sb0):
        def bstep(t, sbs):
            ci = n - 1 - t
            sl = pl.ds(pl.multiple_of(ci * c, c), c)
            out = []
            for p in range(npair):
                lanes = slice(p * PAIR, (p + 1) * PAIR)
                sb_ref[p, ci] = sbs[p]
                kd = (k_ref[0, sl, lanes].astype(F32) * cst[p]["kdb"]).astype(BF16)
                out.append(sbs[p] * cst[p]["cdb"] + jnp.where(bd, _dot_tn(kd, v_ref[0, sl, lanes]), 0.0))
            return tuple(out)

        sb_fin = lax.fori_loop(0, n, bstep, sb0)

        def fstep(ci, sfs):
            sl = pl.ds(pl.multiple_of(ci * c, c), c)
            out = []
            for p in range(npair):
                lanes = slice(p * PAIR, (p + 1) * PAIR)
                k_ = cst[p]
                q = q_ref[0, sl, lanes]
                k = k_ref[0, sl, lanes]
                v = v_ref[0, sl, lanes]
                qf = q.astype(F32)
                o = (_dot((qf * k_["qdf"]).astype(BF16), sfs[p].astype(BF16))
                     + _dot((qf * k_["qdb"]).astype(BF16), sb_ref[p, ci].astype(BF16)))
                for hh in range(2):
                    qm = jnp.where(masks[hh], q, jnp.zeros_like(q))
                    pm = (_dot_nt(qm, k) * k_["dmats"][hh]).astype(BF16)
                    o = o + jnp.where(masks[hh], _dot(pm, v), 0.0)
                on = o * lax.rsqrt(_head_sumsq(o) * (1.0 / HEAD_DIM) + EPS)
                o_ref[0, sl, lanes] = (on * _silu(g_ref[0, sl, lanes].astype(F32))).astype(o_ref.dtype)
                kd = (k.astype(F32) * k_["kdf"]).astype(BF16)
                out.append(sfs[p] * k_["cdf"] + jnp.where(bd, _dot_tn(kd, v), 0.0))
            return tuple(out)

        sf_fin = lax.fori_loop(0, n, fstep, sf0)
        return sf_fin, sb_fin

    z = tuple(jnp.zeros((PAIR, PAIR), F32) for _ in range(npair))
    sfc, sbc = sweep(qc, kc, vc, gc, rc_ref, qc.shape[1] // c, z, z)
    sweep(ql, kl, vl, gl, r_ref, ql.shape[1] // c, sfc, sbc)


def _retention(ret_l, ret_c, log_gamma):
    b, l, _ = ret_l.shape
    lc = ret_c.shape[1]
    npair = D_RET // PAIR

    def col(k):
        return lambda i: (i, 0, k)

    lat = [pl.BlockSpec((1, l, D_RET), col(k)) for k in range(4)]
    ctx = [pl.BlockSpec((1, lc, D_RET), col(k)) for k in range(4)]
    return pl.pallas_call(
        _ret_body,
        grid=(b,),
        in_specs=[pl.BlockSpec(memory_space=pltpu.SMEM)] + lat + ctx,
        out_specs=[pl.BlockSpec((1, l, D_RET), lambda i: (i, 0, 0)),
                   pl.BlockSpec((1, lc, D_RET), lambda i: (i, 0, 0))],
        out_shape=[jax.ShapeDtypeStruct((b, l, D_RET), BF16), jax.ShapeDtypeStruct((b, lc, D_RET), BF16)],
        scratch_shapes=[pltpu.VMEM((npair, max(l, lc) // RET_CHUNK, PAIR, PAIR), F32)],
        compiler_params=_cparams(("arbitrary",), VMEM_LIMIT),
        name="retention",
    )(log_gamma, ret_l, ret_l, ret_l, ret_l, ret_c, ret_c, ret_c, ret_c)


def _stack_heads(t):
    m0, m1 = _head_masks(t.shape)
    z = jnp.zeros_like(t)
    return jnp.concatenate([jnp.where(m0, t, z), jnp.where(m1, t, z)], axis=0)


def _scan_sum(x, axis, n, reverse):
    size = x.shape[axis]
    pos = lax.broadcasted_iota(jnp.int32, x.shape, axis)
    s = 1
    while s < n:
        if reverse:
            x = x + jnp.where(pos + s < n, pltpu.roll(x, size - s, axis), 0.0)
        else:
            x = x + jnp.where(pos >= s, pltpu.roll(x, s, axis), 0.0)
        s *= 2
    return x


def _gdn_prep(src_ref, col, cw, dst_ref, r0, *, l2, scale):
    n = src_ref.shape[1]
    blk = min(GDN_PREP_ROWS, n)
    halo = 16
    lanes = slice(col % D_GDN, col % D_GDN + LANES)

    def block(i, _):
        b0 = pl.multiple_of(i * blk, blk)
        x = src_ref[0, pl.ds(b0, blk), col:col + LANES].astype(F32)
        lo = src_ref[0, pl.ds(pl.multiple_of(jnp.maximum(b0 - halo, 0), halo), halo), col:col + LANES].astype(F32)
        hi = src_ref[0, pl.ds(pl.multiple_of(jnp.minimum(b0 + blk, n - halo), halo), halo), col:col + LANES].astype(F32)
        lo = jnp.where(b0 > 0, lo, 0.0)
        hi = jnp.where(b0 + blk < n, hi, 0.0)
        ext = jnp.concatenate([lo, x, hi], axis=0)
        acc = x * cw[SHORT_CONV // 2:SHORT_CONV // 2 + 1, :]
        for j in range(SHORT_CONV):
            s = j - SHORT_CONV // 2
            if s != 0:
                sh = pltpu.roll(ext, (-s) % (blk + 2 * halo), 0)
                acc = acc + sh[halo:halo + blk] * cw[j:j + 1, :]
        y = _silu(acc)
        if l2:
            y = y * lax.rsqrt(_head_sumsq(y) + EPS)
        if scale != 1.0:
            y = y * scale
        dst_ref[pl.ds(pl.multiple_of(r0 + b0, halo), blk), lanes] = y.astype(dst_ref.dtype)
        return 0

    lax.fori_loop(0, n // blk, block, 0)


def _gdn_body(x_ref, xc_ref, ab_ref, abt_ref, cw_ref, cst_ref, cstt_ref, ng_ref,
              o_ref, oc_ref, qn, kn, vn, oacc, st_s, x_s, t_s, rhs_s, u_s, w_s, qa_s, kd_s, egl_s):
    c = GDN_CHUNK
    c2 = 2 * c
    grp = GDN_GROUP
    npair = D_GDN // PAIR
    lc = xc_ref.shape[1]
    l = x_ref.shape[1]
    nc = lc // c
    nt = (lc + l) // c
    ii = lax.broadcasted_iota(jnp.int32, (c2, c2), 0)
    jj = lax.broadcasted_iota(jnp.int32, (c2, c2), 1)
    same = (ii // c) == (jj // c)
    eye = (ii == jj).astype(F32)
    dirs = ((same & (ii >= jj), same & (ii > jj), c - 1),
            (same & (ii <= jj), same & (ii < jj), 0))

    def chains(g, slot):
        out = []
        for gi in range(grp):
            t = g * grp + gi
            cf = t
            cb = jnp.where(t < nc, nc - 1 - t, nt + nc - 1 - t)
            for p in range(npair):
                for d in range(2):
                    out.append((((slot * grp + gi) * 2 + d) * npair + p, p, d, cf if d == 0 else cb))
        return out

    def stage_inputs(idx, p, d, ci):
        incl, strict, last = dirs[d]
        lanes = slice(p * PAIR, (p + 1) * PAIR)
        nea, dtb = cst_ref[p, 0:1, :], cst_ref[p, 1:2, :]
        neat, dtbt = cstt_ref[p, 0], cstt_ref[p, 1]
        sl = pl.ds(pl.multiple_of(ci * c, c), c)
        abv = ab_ref[0, sl, p * LANES:(p + 1) * LANES]
        gall = nea * jax.nn.softplus(abv + dtb)
        ball = jax.nn.sigmoid(abv)
        shape = (c, PAIR)
        gl = _per_head(gall[:, 2 * d:2 * d + 1], gall[:, 2 * d + 1:2 * d + 2], shape)
        bl = _per_head(ball[:, 4 + 2 * d:5 + 2 * d], ball[:, 5 + 2 * d:6 + 2 * d], shape)
        gcum = _scan_sum(gl, 0, c, d == 1)
        gt = neat * jax.nn.softplus(abt_ref[0, p, ci] + dtbt)
        gtc = _scan_sum(gt, 1, c, d == 1)
        grow = jnp.concatenate([gtc[2 * d:2 * d + 1, 0:c], gtc[2 * d + 1:2 * d + 2, 0:c]], axis=1)
        gcol = jnp.concatenate([gcum[:, 0:1], gcum[:, HEAD_DIM:HEAD_DIM + 1]], axis=0)
        dec = jnp.where(incl, jnp.exp(jnp.where(incl, gcol - grow, 0.0)), 0.0)
        dec = jnp.where(ii == jj, 1.0, dec)
        q = qn[sl, lanes]
        k = kn[sl, lanes]
        v = vn[sl, lanes]
        kf = k.astype(F32)
        eg = jnp.exp(gcum)
        kb = kf * bl
        k_st = _stack_heads(k)
        a = jnp.where(strict, _dot_nt(_stack_heads(kb.astype(BF16)), k_st) * dec, 0.0)
        attn = (_dot_nt(_stack_heads(q), k_st) * dec).astype(BF16)
        x_s[idx] = (-a).astype(BF16)
        t_s[idx] = eye - a
        rhs_s[idx] = jnp.concatenate([_stack_heads((v.astype(F32) * bl).astype(BF16)),
                                      _stack_heads((kb * eg).astype(BF16))], axis=1)
        qa_s[idx] = jnp.concatenate([_stack_heads((q.astype(F32) * eg).astype(BF16)), attn], axis=1)
        glast = gcum[last:last + 1, :]
        kd_s[idx] = _stack_heads(kf * jnp.exp(glast - gcum)).T.astype(BF16)
        egl_s[idx] = jnp.broadcast_to(jnp.exp(glast), (8, PAIR))

    def stage_group(g, slot):
        todo = chains(g, slot)
        for idx, p, d, ci in todo:
            stage_inputs(idx, p, d, ci)
        for _ in range(5):
            for idx, _, _, _ in todo:
                xb = x_s[idx]
                x_s[idx] = _dot(xb, xb).astype(BF16)
            for idx, _, _, _ in todo:
                t = t_s[idx]
                t_s[idx] = t + _dot(t.astype(BF16), x_s[idx])
        for idx, _, _, _ in todo:
            sol = _dot(t_s[idx].astype(BF16), rhs_s[idx])
            u_s[idx] = sol[:, :PAIR]
            w_s[idx] = sol[:, PAIR:].astype(BF16)

    def recur_group(g, slot):
        sts = {(p, d): st_s[d * npair + p] for p in range(npair) for d in range(2)}
        for idx, p, d, ci in chains(g, slot):
            sl = pl.ds(pl.multiple_of(ci * c, c), c)
            st = sts[(p, d)]
            stb = st.astype(BF16)
            v_new = (u_s[idx] - _dot(w_s[idx], stb)).astype(BF16)
            o_st = _dot(qa_s[idx], jnp.concatenate([stb, v_new], axis=0))
            oacc[sl, p * PAIR:(p + 1) * PAIR] += o_st[:c] + o_st[c:]
            sts[(p, d)] = st * egl_s[idx][0:1, :] + _dot(kd_s[idx], v_new)
        for (p, d), st in sts.items():
            st_s[d * npair + p] = st

    def prep(src_ref, r0):
        for s in range(D_GDN // LANES):
            for k, (dst, l2, scale) in enumerate(((qn, True, HEAD_DIM ** -0.5), (kn, True, 1.0), (vn, False, 1.0))):
                col = k * D_GDN + s * LANES
                _gdn_prep(src_ref, col, cw_ref[:, col:col + LANES], dst, r0, l2=l2, scale=scale)

    def finish(src_ref, out_ref, r0, n):
        for p in range(npair):
            lanes = slice(p * PAIR, (p + 1) * PAIR)
            o = oacc[r0:r0 + n, lanes]
            on = o * lax.rsqrt(_head_sumsq(o) * (1.0 / HEAD_DIM) + EPS)
            gate = src_ref[0, :, 3 * D_GDN + p * PAIR:3 * D_GDN + (p + 1) * PAIR].astype(F32)
            out_ref[0, :, lanes] = (on * ng_ref[...] * _silu(gate)).astype(out_ref.dtype)

    st_s[...] = jnp.zeros_like(st_s)
    oacc[...] = jnp.zeros_like(oacc)
    prep(xc_ref, 0)
    prep(x_ref, lc)

    ng = nt // grp
    stage_group(0, 0)

    def two_groups(kk, _):
        g = 2 * kk
        recur_group(g, 0)
        stage_group(g + 1, 1)
        recur_group(g + 1, 1)
        stage_group(g + 2, 0)
        return 0

    lax.fori_loop(0, (ng - 1) // 2, two_groups, 0)
    if ng % 2 == 0:
        recur_group(ng - 2, 0)
        stage_group(ng - 1, 1)
        recur_group(ng - 1, 1)
    else:
        recur_group(ng - 1, 0)
    finish(xc_ref, oc_ref, 0, lc)
    finish(x_ref, o_ref, lc, l)


def _gdn_tables(conv_w, a_log, dt_bias, norm_g):
    npair = D_GDN // PAIR
    nea = -jnp.exp(a_log.astype(F32)).reshape(2, npair, 2)
    dtb = dt_bias.astype(F32).reshape(2, npair, 2)
    rows = jnp.stack([jnp.transpose(nea, (1, 0, 2)).reshape(npair, 4), jnp.transpose(dtb, (1, 0, 2)).reshape(npair, 4)], axis=1)
    cst = jnp.pad(rows, ((0, 0), (0, 6), (0, LANES - 4)))
    cstt = jnp.broadcast_to(jnp.pad(rows, ((0, 0), (0, 0), (0, 4)))[..., None], (npair, 2, 8, LANES))
    cw = jnp.pad(conv_w.astype(F32), ((0, 8 - SHORT_CONV), (0, 0)))
    ng = jnp.tile(norm_g.astype(F32), 2).reshape(1, PAIR)
    return cw, cst, cstt, ng


def _gdn(gdn_l, ab_l, gdn_c, ab_c, tables):
    b, l, _ = gdn_l.shape
    lc = gdn_c.shape[1]
    npair = D_GDN // PAIR
    lt = l + lc
    assert (lt // GDN_CHUNK) % GDN_GROUP == 0
    cw, cst, cstt, ng = tables

    def tr(ab):
        x = jnp.stack([ab[:, :, 0:8], ab[:, :, LANES:LANES + 8]], axis=1)
        x = x.reshape(x.shape[0], npair, x.shape[2] // GDN_CHUNK, GDN_CHUNK, 8)
        return jnp.pad(jnp.transpose(x, (0, 1, 2, 4, 3)), ((0, 0),) * 4 + ((0, LANES - GDN_CHUNK),))

    nstage = 2 * GDN_GROUP * 2 * npair

    def whole(shape):
        nd = len(shape)
        return pl.BlockSpec(shape, lambda i: (0,) * nd)

    def per_batch(shape, single=False):
        nd = len(shape)
        return pl.BlockSpec((1,) + shape, lambda i: (i,) + (0,) * nd, pipeline_mode=pl.Buffered(1) if single else None)

    sq = (nstage, PAIR, PAIR)
    return pl.pallas_call(
        _gdn_body,
        grid=(b,),
        in_specs=[per_batch((l, 4 * D_GDN), True), per_batch((lc, 4 * D_GDN)), per_batch((lt, 2 * LANES), True),
                  per_batch((npair, lt // GDN_CHUNK, 8, LANES)),
                  whole((8, 3 * D_GDN)), whole((npair, 8, LANES)), whole((npair, 2, 8, LANES)), whole((1, PAIR))],
        out_specs=[per_batch((l, D_GDN)), per_batch((lc, D_GDN))],
        out_shape=[jax.ShapeDtypeStruct((b, l, D_GDN), BF16), jax.ShapeDtypeStruct((b, lc, D_GDN), BF16)],
        scratch_shapes=[pltpu.VMEM((lt, D_GDN), BF16)] * 3 + [pltpu.VMEM((lt, D_GDN), F32)]
        + [pltpu.VMEM((2 * npair, PAIR, PAIR), F32), pltpu.VMEM(sq, BF16), pltpu.VMEM(sq, F32),
           pltpu.VMEM((nstage, PAIR, 2 * PAIR), BF16), pltpu.VMEM(sq, F32), pltpu.VMEM(sq, BF16),
           pltpu.VMEM((nstage, PAIR, 2 * PAIR), BF16), pltpu.VMEM(sq, BF16), pltpu.VMEM((nstage, 8, PAIR), F32)],
        compiler_params=_cparams(("arbitrary",), VMEM_LIMIT),
        name="gated_deltanet",
    )(gdn_l, gdn_c, jnp.concatenate([ab_c, ab_l], axis=1), jnp.concatenate([tr(ab_c), tr(ab_l)], axis=2), cw, cst, cstt, ng)


def _na_bias_tiles(rpb, rows):
    w = GRID_W
    ext = jnp.pad(rpb.astype(F32), ((0, 0), (0, 0), (w - NA_KW, w - NA_KW)))
    cq = np.arange(w)[:, None]
    ck = np.arange(w)[None, :]
    tb = ext[:, :, jnp.asarray(ck - cq + w - 1)]
    ws = np.clip(cq - NA_KW // 2, 0, w - NA_KW)
    col_ok = (ck >= ws) & (ck < ws + NA_KW)
    tb = jnp.where(jnp.asarray(col_ok)[None, None], tb, NEG_INF)
    tb = jnp.concatenate([tb, jnp.full((tb.shape[0], 1, w, w), NEG_INF, F32)], axis=1)
    nt = rows // NA_QROWS
    idx = np.zeros((3, NA_QROWS, NA_KROWS), np.int32)
    for cls, t in enumerate((0, 1, nt - 1)):
        ks = int(np.clip(NA_QROWS * t - NA_KH // 2, 0, rows - NA_KROWS))
        for rl in range(NA_QROWS):
            r = NA_QROWS * t + rl
            r0 = int(np.clip(r - NA_KH // 2, 0, rows - NA_KH))
            for j in range(NA_KROWS):
                kr = ks + j
                idx[cls, rl, j] = kr - r + NA_KH - 1 if r0 <= kr < r0 + NA_KH else 2 * NA_KH - 1
    tiles = tb[:, jnp.asarray(idx)]
    tiles = jnp.transpose(tiles, (1, 0, 2, 4, 3, 5))
    return tiles.reshape(3, tb.shape[0], NA_QROWS * w, NA_KROWS * w)


def _softmax_pv(s_parts, v_parts):
    m = functools.reduce(jnp.maximum, [jnp.max(s, axis=-1, keepdims=True) for s in s_parts])
    ps = [jnp.exp(s - m) for s in s_parts]
    den = functools.reduce(lambda x, y: x + y, [jnp.sum(p, axis=-1, keepdims=True) for p in ps])
    o = functools.reduce(lambda x, y: x + y, [_dot(p.astype(BF16), v) for p, v in zip(ps, v_parts)])
    return o / den


def _na_body(q_ref, k_ref, v_ref, qc_ref, kc_ref, vc_ref, bias_ref, o_ref, oc_ref, *, rows):
    w = GRID_W
    tq = NA_QROWS * w
    tk = NA_KROWS * w
    nt = rows // NA_QROWS
    scale = HEAD_DIM ** -0.5
    m0, m1 = _head_masks((tq, PAIR))
    masks = (m0, m1)

    def tile(t, _):
        cls = jnp.where(t > 0, 1, 0) + jnp.where(t == nt - 1, 1, 0)
        ks = jnp.clip(NA_QROWS * t - NA_KH // 2, 0, rows - NA_KROWS) * w
        ksl = pl.ds(pl.multiple_of(ks, w), tk)
        qsl = pl.ds(pl.multiple_of(t * tq, tq), tq)
        for p in range(NA_PAIRS):
            lanes = slice(p * PAIR, (p + 1) * PAIR)
            q = q_ref[0, qsl, lanes] * scale
            kb = k_ref[0, ksl, lanes]
            vb = v_ref[0, ksl, lanes]
            kc = kc_ref[0, :, lanes]
            vc = vc_ref[0, :, lanes]
            acc = jnp.zeros((tq, PAIR), F32)
            for hh in range(2):
                qm = jnp.where(masks[hh], q, jnp.zeros_like(q))
                s_loc = _dot_nt(qm, kb) + bias_ref[cls, 2 * p + hh]
                s_ctx = _dot_nt(qm, kc)
                acc = acc + jnp.where(masks[hh], _softmax_pv([s_loc, s_ctx], [vb, vc]), 0.0)
            o_ref[0, qsl, lanes] = acc.astype(o_ref.dtype)
        return 0

    lax.fori_loop(0, nt, tile, 0)
    for p in range(NA_PAIRS):
        lanes = slice(p * PAIR, (p + 1) * PAIR)
        qc = qc_ref[0, :, lanes] * scale
        kc = kc_ref[0, :, lanes]
        vc = vc_ref[0, :, lanes]
        mc0, mc1 = _head_masks(qc.shape)
        accc = jnp.zeros(qc.shape, F32)
        for mk in (mc0, mc1):
            qm = jnp.where(mk, qc, jnp.zeros_like(qc))
            accc = accc + jnp.where(mk, _softmax_pv([_dot_nt(qm, kc)], [vc]), 0.0)
        oc_ref[0, :, lanes] = accc.astype(oc_ref.dtype)


def _neighbourhood_attention(na_l, na_c, bias):
    b, l, _ = na_l.shape
    lc = na_c.shape[1]
    rows = l // GRID_W
    wide = NA_PAIRS * PAIR
    ngrp = D_NA // wide

    def col(k):
        return lambda p, i: (i, 0, k * ngrp + p)

    lat = [pl.BlockSpec((1, l, wide), col(k)) for k in range(3)]
    ctx = [pl.BlockSpec((1, lc, wide), col(k)) for k in range(3)]
    return pl.pallas_call(
        functools.partial(_na_body, rows=rows),
        grid=(ngrp, b),
        in_specs=lat + ctx + [pl.BlockSpec((3, 2 * NA_PAIRS, NA_QROWS * GRID_W, NA_KROWS * GRID_W), lambda p, i: (0, p, 0, 0))],
        out_specs=[pl.BlockSpec((1, l, wide), lambda p, i: (i, 0, p)),
                   pl.BlockSpec((1, lc, wide), lambda p, i: (i, 0, p))],
        out_shape=[jax.ShapeDtypeStruct((b, l, D_NA), BF16), jax.ShapeDtypeStruct((b, lc, D_NA), BF16)],
        compiler_params=_cparams(("arbitrary", "arbitrary"), VMEM_LIMIT),
        name="neighbourhood_attention",
    )(na_l, na_l, na_l, na_c, na_c, na_c, bias)


def _outproj_residual(r_ref, g_ref, n_ref, x_ref, mod_ref, wo_ref):
    mix = jnp.concatenate([r_ref[0], g_ref[0], n_ref[0]], axis=1)
    return x_ref[0] + mod_ref[0, 2:3, :] * _dot(mix, wo_ref[...])


def _swiglu_chunk(hb, w1_ref, w3_ref, w2_ref, j, between=None):
    cols = slice(j * FF_CHUNK, (j + 1) * FF_CHUNK)
    between = between or (lambda k: None)
    between(0)
    a = _dot(hb, w1_ref[:, cols])
    between(1)
    t = (_silu(a) * _dot(hb, w3_ref[:, cols])).astype(BF16)
    between(2)
    return _dot(t, w2_ref[cols, :])


def _swiglu_chunks(hb, w1_ref, w3_ref, w2_ref, acc_ref):
    for j in range(w2_ref.shape[0] // FF_CHUNK):
        if j == 0:
            acc_ref[...] = _swiglu_chunk(hb, w1_ref, w3_ref, w2_ref, j)
        else:
            acc_ref[...] += _swiglu_chunk(hb, w1_ref, w3_ref, w2_ref, j)


def _dense_body(r_ref, g_ref, n_ref, x_ref, mod_ref, g2_ref, wo_ref, w1_ref, w3_ref, w2_ref, o_ref, acc_ref):
    x1 = _outproj_residual(r_ref, g_ref, n_ref, x_ref, mod_ref, wo_ref)
    hb = _norm_mod(x1, g2_ref[...], mod_ref[0, 3:4, :], mod_ref[0, 4:5, :]).astype(BF16)
    _swiglu_chunks(hb, w1_ref, w3_ref, w2_ref, acc_ref)
    o_ref[0] = x1 + mod_ref[0, 5:6, :] * acc_ref[...]


def _pack_bf16_pairs(h):
    m = h.shape[1] // 2
    bits = lax.bitcast_convert_type(h.astype(BF16).astype(F32), jnp.uint32)
    return (bits[:, :m] >> 16) | (bits[:, m:] & jnp.uint32(0xFFFF0000))


def _unpack_bf16_pairs(u):
    lo = lax.bitcast_convert_type(u << 16, F32)
    hi = lax.bitcast_convert_type(u & jnp.uint32(0xFFFF0000), F32)
    return jnp.concatenate([lo, hi], axis=1).astype(BF16)


def _router_body(r_ref, g_ref, n_ref, x_ref, mod_ref, g2_ref, wo_ref, wr_ref, x1_ref, h_ref, lg_ref):
    x1 = _outproj_residual(r_ref, g_ref, n_ref, x_ref, mod_ref, wo_ref)
    h = _norm_mod(x1, g2_ref[...], mod_ref[0, 3:4, :], mod_ref[0, 4:5, :])
    x1_ref[0] = x1
    h_ref[0] = _pack_bf16_pairs(h)
    lg_ref[0] = _dot_hi(h, wr_ref[...])


def _resident(shape):
    nd = len(shape)
    return pl.BlockSpec(shape, lambda i, j: (0,) * nd, pipeline_mode=pl.Buffered(1))


def _mixer_specs(tm, d):
    return [pl.BlockSpec((1, tm, D_RET), lambda i, j: (i, j, 0)),
            pl.BlockSpec((1, tm, D_GDN), lambda i, j: (i, j, 0)),
            pl.BlockSpec((1, tm, D_NA), lambda i, j: (i, j, 0)),
            pl.BlockSpec((1, tm, d), lambda i, j: (i, j, 0)),
            pl.BlockSpec((1, 6, d), lambda i, j: (i, 0, 0)),
            pl.BlockSpec((1, d), lambda i, j: (0, 0))]


def _dense_block(r, g, n, x, mod, g2, wo, w1, w3, w2):
    b, l, d = x.shape
    tm = min(512, l)
    dff = w1.shape[1]
    return pl.pallas_call(
        _dense_body,
        grid=(b, l // tm),
        in_specs=_mixer_specs(tm, d) + [_resident((d, d)), _resident((d, dff)), _resident((d, dff)), _resident((dff, d))],
        out_specs=pl.BlockSpec((1, tm, d), lambda i, j: (i, j, 0)),
        out_shape=jax.ShapeDtypeStruct((b, l, d), F32),
        scratch_shapes=[pltpu.VMEM((tm, d), F32)],
        compiler_params=_cparams(("arbitrary", "arbitrary"), VMEM_LIMIT),
        name="outproj_swiglu",
    )(r, g, n, x, mod, g2, wo, w1, w3, w2)


def _router_block(r, g, n, x, mod, g2, wo, wr):
    b, l, d = x.shape
    tm = min(512, l)
    blk = pl.BlockSpec((1, tm, d), lambda i, j: (i, j, 0))
    return pl.pallas_call(
        _router_body,
        grid=(b, l // tm),
        in_specs=_mixer_specs(tm, d) + [_resident((d, d)), _resident((d, LANES))],
        out_specs=[blk, pl.BlockSpec((1, tm, d // 2), lambda i, j: (i, j, 0)), pl.BlockSpec((1, tm, LANES), lambda i, j: (i, j, 0))],
        out_shape=[jax.ShapeDtypeStruct((b, l, d), F32), jax.ShapeDtypeStruct((b, l, d // 2), jnp.uint32),
                   jax.ShapeDtypeStruct((b, l, LANES), F32)],
        compiler_params=_cparams(("arbitrary", "arbitrary"), VMEM_LIMIT),
        name="outproj_router",
    )(r, g, n, x, mod, g2, wo, wr)


def _row_copy_in(idx_ref, r, src_hbm, dst_ref, sem):
    return pltpu.make_async_copy(src_hbm.at[pl.ds(idx_ref[0, 0, r], 1), :], dst_ref.at[pl.ds(r, 1), :], sem)


def _row_copy_out(idx_ref, r, src_ref, dst_hbm, sem):
    return pltpu.make_async_copy(src_ref.at[pl.ds(r, 1), :], dst_hbm.at[pl.ds(idx_ref[0, 0, r], 1), :], sem)


def _moe_body(bval_ref, bexp_ref, tok_ref, tokn_ref, slotp_ref, h_hbm, w1_ref, w3_ref, w2_ref, y_hbm,
              xbuf, ybuf, gsem, ssem, *, n_asg):
    i = pl.program_id(0)
    tb = xbuf.shape[1]
    nch = w2_ref.shape[1] // FF_CHUNK
    cur = i % 2
    per = -(-tb // (3 * nch))
    valid = bval_ref[i] == 1
    prev_valid = bval_ref[jnp.maximum(i - 1, 0)] == 1

    def wait_gather(slot):
        pltpu.make_async_copy(h_hbm.at[pl.ds(0, tb), :], xbuf.at[slot], gsem).wait()

    def wait_scatter(slot):
        pltpu.make_async_copy(ybuf.at[slot], y_hbm.at[pl.ds(0, tb), :], ssem).wait()

    @pl.when(i == 0)
    def _():
        ybuf[1] = jnp.zeros((tb, ybuf.shape[2]), F32)
        fills = [pltpu.make_async_copy(ybuf.at[1], y_hbm.at[pl.ds(r0, tb), :], ssem)
                 for r0 in range(n_asg, y_hbm.shape[0], tb)]
        for f in fills:
            f.start()
        for f in fills:
            f.wait()

        def issue(r, _):
            _row_copy_in(tok_ref, r, h_hbm, xbuf.at[0], gsem).start()
            return 0

        lax.fori_loop(0, tb, issue, 0)

    @pl.when(valid)
    def _():
        wait_gather(cur)
        hb = _unpack_bf16_pairs(xbuf[cur])
        acc = ybuf.at[cur]
        prv = ybuf.at[1 - cur]
        nxt = xbuf.at[1 - cur]
        for j in range(nch):
            def copies(k, j=j):
                g = 3 * j + k
                for r in range(g * per, min(tb, (g + 1) * per)):
                    _row_copy_in(tokn_ref, r, h_hbm, nxt, gsem).start()
                    _row_copy_out(slotp_ref, r, prv, y_hbm, ssem).start(priority=1)

            part = _swiglu_chunk(hb, w1_ref.at[0], w3_ref.at[0], w2_ref.at[0], j, copies)
            if j == 0:
                acc[...] = part
            else:
                acc[...] += part
        wait_scatter(1 - cur)

    @pl.when(jnp.logical_not(valid) & prev_valid & (i > 0))
    def _():
        wait_gather(cur)

        def issue(r, _):
            _row_copy_out(slotp_ref, r, ybuf.at[1 - cur], y_hbm, ssem).start()
            return 0

        lax.fori_loop(0, tb, issue, 0)
        wait_scatter(1 - cur)


def _moe_experts(h_rows, row_tok, row_slot, block_expert, block_valid, w1, w3, w2, n_out_rows, n_spare):
    d = w2.shape[-1]
    dff = w2.shape[1]
    nb = block_expert.shape[0]
    tb = MOE_ROWS
    smem = functools.partial(pl.BlockSpec, (1, 1, tb), memory_space=pltpu.SMEM)
    grid_spec = pltpu.PrefetchScalarGridSpec(
        num_scalar_prefetch=2,
        grid=(nb,),
        in_specs=[smem(lambda i, bv, be: (i, 0, 0)),
                  smem(lambda i, bv, be: (jnp.minimum(i + 1, nb - 1), 0, 0)),
                  smem(lambda i, bv, be: (i, 0, 0)),
                  pl.BlockSpec(memory_space=pl.ANY),
                  pl.BlockSpec((1, d, dff), lambda i, bv, be: (be[i], 0, 0)),
                  pl.BlockSpec((1, d, dff), lambda i, bv, be: (be[i], 0, 0)),
                  pl.BlockSpec((1, dff, d), lambda i, bv, be: (be[i], 0, 0))],
        out_specs=pl.BlockSpec(memory_space=pl.ANY),
        scratch_shapes=[pltpu.VMEM((2, tb, d // 2), jnp.uint32), pltpu.VMEM((2, tb, d), F32),
                        pltpu.SemaphoreType.DMA(()), pltpu.SemaphoreType.DMA(())],
    )
    return pl.pallas_call(
        functools.partial(_moe_body, n_asg=n_out_rows - n_spare),
        grid_spec=grid_spec,
        out_shape=jax.ShapeDtypeStruct((n_out_rows, d), F32),
        compiler_params=_cparams(("arbitrary",), VMEM_LIMIT),
        name="moe_experts",
    )(block_valid, block_expert, row_tok.reshape(nb, 1, tb), row_tok.reshape(nb, 1, tb), row_slot.reshape(nb + 1, 1, tb),
      h_rows, w1, w3, w2)


def _combine_body(y0_ref, y1_ref, x1_ref, gate_ref, gm_ref, fg_ref, o_ref, *, final):
    moe = gate_ref[:, 0:1] * y0_ref[...] + gate_ref[:, 1:2] * y1_ref[...]
    x2 = x1_ref[...] + gm_ref[0] * moe
    if final:
        ms = jnp.mean(x2 * x2, axis=-1, keepdims=True)
        x2 = x2 * lax.rsqrt(ms + EPS) * fg_ref[...]
    o_ref[...] = x2


def _moe_combine(y_rows, x1, gates, gate_mlp, final_g, final):
    b, l, d = x1.shape
    n_tok = b * l
    tm = 256
    per_b = l // tm
    out = pl.pallas_call(
        functools.partial(_combine_body, final=final),
        grid=(n_tok // tm,),
        in_specs=[pl.BlockSpec((tm, d), lambda i: (i, 0)),
                  pl.BlockSpec((tm, d), lambda i: (n_tok // tm + i, 0)),
                  pl.BlockSpec((tm, d), lambda i: (i, 0)),
                  pl.BlockSpec((tm, 2), lambda i: (i, 0)),
                  pl.BlockSpec((1, 1, d), lambda i: (i // per_b, 0, 0)),
                  pl.BlockSpec((1, d), lambda i: (0, 0))],
        out_specs=pl.BlockSpec((tm, d), lambda i: (i, 0)),
        out_shape=jax.ShapeDtypeStruct((n_tok, d), F32),
        compiler_params=_cparams(("arbitrary",), VMEM_LIMIT),
        name="moe_combine_final_norm",
    )(y_rows, y_rows, x1.reshape(n_tok, d), gates, gate_mlp, final_g.reshape(1, d))
    return out.reshape(b, l, d)


def _route(logits):
    n_tok = logits.shape[0]
    tb = MOE_ROWS
    n_asg = n_tok * TOP_K
    top_logit, top_e = lax.top_k(logits[:, :N_EXPERTS], TOP_K)
    gates = jax.nn.softmax(top_logit, axis=-1)
    e_flat = top_e.reshape(-1).astype(jnp.int32)
    asg = jnp.arange(n_asg, dtype=jnp.int32)
    by_expert = lax.sort(e_flat * n_asg + asg)
    counts = jnp.sum((e_flat[:, None] == jnp.arange(N_EXPERTS, dtype=jnp.int32)[None, :]).astype(jnp.int32), axis=0)
    starts = jnp.cumsum(counts) - counts
    padded = (counts + tb - 1) // tb * tb
    pad_ends = jnp.cumsum(padded)
    pad_starts = pad_ends - padded
    nb = (n_asg + tb - 1) // tb + N_EXPERTS + 1
    n_rows = nb * tb
    block_start = jnp.arange(nb, dtype=jnp.int32) * tb
    block_expert = jnp.minimum(jnp.searchsorted(pad_ends, block_start, side='right'), N_EXPERTS - 1).astype(jnp.int32)
    block_valid = (block_start < pad_ends[-1]).astype(jnp.int32)
    row = jnp.arange(n_rows, dtype=jnp.int32)
    row_e = jnp.repeat(block_expert, tb)
    off = row - pad_starts[row_e]
    is_pad = (off >= counts[row_e]) | (jnp.repeat(block_valid, tb) == 0)
    src = by_expert[jnp.clip(starts[row_e] + off, 0, n_asg - 1)] - row_e * n_asg
    row_asg = jnp.where(is_pad, -1, src)
    spare = tb + n_asg + jnp.cumsum(is_pad.astype(jnp.int32)) - 1
    row_slot = jnp.where(is_pad, spare, (row_asg % TOP_K) * n_tok + row_asg // TOP_K)
    row_tok = jnp.where(is_pad, 0, row_asg // TOP_K)
    row_slot = jnp.concatenate([n_asg + jnp.arange(tb, dtype=jnp.int32), row_slot])
    last_e = block_expert[jnp.maximum(pad_ends[-1] // tb - 1, 0)]
    block_expert = jnp.where(block_valid == 1, block_expert, last_e)
    n_spare = (tb + n_rows - n_asg + tb - 1) // tb * tb
    return row_tok, row_slot, gates, block_expert, block_valid, n_asg + n_spare, n_spare


def _final_norm_body(x_ref, g_ref, o_ref):
    x = x_ref[...]
    ms = jnp.mean(x * x, axis=-1, keepdims=True)
    o_ref[...] = x * lax.rsqrt(ms + EPS) * g_ref[...]


def _final_norm(x, g):
    b, l, d = x.shape
    n = b * l
    tm = min(512, n)
    out = pl.pallas_call(
        _final_norm_body,
        grid=(n // tm,),
        in_specs=[pl.BlockSpec((tm, d), lambda i: (i, 0)), pl.BlockSpec((1, d), lambda i: (0, 0))],
        out_specs=pl.BlockSpec((tm, d), lambda i: (i, 0)),
        out_shape=jax.ShapeDtypeStruct((n, d), F32),
        compiler_params=_cparams(("arbitrary",)),
        name="final_norm",
    )(x.reshape(n, d), g.reshape(1, d))
    return out.reshape(b, l, d)


def _rope_tables(n_tok):
    t = jnp.arange(n_tok, dtype=jnp.int32)
    row = (t // GRID_W).astype(F32)
    col = (t % GRID_W).astype(F32)
    inv_freq = ROPE_BASE ** (-jnp.arange(N_FREQ, dtype=F32) / N_FREQ)
    ang = jnp.concatenate([row[:, None] * inv_freq, col[:, None] * inv_freq], axis=-1)
    cos, sin = jnp.cos(ang), jnp.sin(ang)
    cosf = jnp.concatenate([cos, cos, cos, cos], axis=-1)
    sins = jnp.concatenate([-sin, sin, -sin, sin], axis=-1)
    return cosf, sins


def kernel(x, c, ctx, c_ctx, ada_w, ada_b, norm1_g, norm2_g, w_in, w_out, conv_w, ret_decay, gdn_a_log, gdn_dt_bias,
           gdn_norm_g, na_rpb, ffn_w1, ffn_w3, ffn_w2, moe_router, moe_w1, moe_w3, moe_w2, final_g):
    b, l, d = x.shape
    lc = ctx.shape[1]
    depth = ada_w.shape[0]
    cosf, sins = _rope_tables(l)
    ones_c = jnp.ones((lc, LANES), F32)
    zeros_c = jnp.zeros((lc, LANES), F32)

    rows = ((b + 1 + 7) // 8) * 8
    c_all = jnp.zeros((rows, d), F32).at[:b].set(c).at[b].set(c_ctx)
    mod = _ada_vectors(c_all, ada_w, ada_b).reshape(depth, rows, 6, d)

    w1p_all = jax.vmap(_pack_w_in)(w_in)
    wo_all = w_out.astype(BF16)
    log_gamma_all = jnp.log1p(-jnp.exp2(-ret_decay.astype(F32)))
    gdn_tables = jax.vmap(_gdn_tables)(conv_w, gdn_a_log, gdn_dt_bias, gdn_norm_g)
    na_bias = jax.vmap(lambda r: _na_bias_tiles(r, l // GRID_W))(na_rpb)

    y = ctx
    for layer in range(depth):
        need_ctx = layer < depth - 1
        mod_l = mod[layer, :b]
        mod_c = jnp.broadcast_to(mod[layer, b][None], (b, 6, d))
        w1p = w1p_all[layer]
        g1 = norm1_g[layer].reshape(1, d)
        ret_l, gdn_l, ab_l, na_l = _in_projection(x, mod_l, g1, cosf, sins, w1p, rope=True)
        ret_c, gdn_c, ab_c, na_c = _in_projection(y, mod_c, g1, ones_c, zeros_c, w1p, rope=False)

        r_l, r_c = _retention(ret_l, ret_c, log_gamma_all[layer])
        g_l, g_c = _gdn(gdn_l, ab_l, gdn_c, ab_c, [t[layer] for t in gdn_tables])
        n_l, n_c = _neighbourhood_attention(na_l, na_c, na_bias[layer])

        wo = wo_all[layer]
        g2 = norm2_g[layer].reshape(1, d)
        j = layer // 2
        if layer % 2 == 0:
            w1, w3, w2 = ffn_w1[j].astype(BF16), ffn_w3[j].astype(BF16), ffn_w2[j].astype(BF16)
            x = _dense_block(r_l, g_l, n_l, x, mod_l, g2, wo, w1, w3, w2)
            if need_ctx:
                y = _dense_block(r_c, g_c, n_c, y, mod_c, g2, wo, w1, w3, w2)
            if layer == depth - 1:
                x = _final_norm(x, final_g)
        else:
            wr = jnp.pad(moe_router[j].astype(F32), ((0, 0), (0, LANES - N_EXPERTS)))
            w1, w3, w2 = moe_w1[j].astype(BF16), moe_w3[j].astype(BF16), moe_w2[j].astype(BF16)

            def moe_ffn(r, g, n, xin, m, last):
                bb, ll, _ = xin.shape
                x1, h, logits = _router_block(r, g, n, xin, m, g2, wo, wr)
                row_tok, row_slot, gates, bexp, bval, n_out, n_spare = _route(logits.reshape(bb * ll, LANES))
                y_rows = _moe_experts(h.reshape(bb * ll, d // 2), row_tok, row_slot, bexp, bval, w1, w3, w2, n_out, n_spare)
                return _moe_combine(y_rows, x1, gates, m[:, 5:6, :], final_g, last)

            x = moe_ffn(r_l, g_l, n_l, x, mod_l, layer == depth - 1)
            if need_ctx:
                y = moe_ffn(r_c, g_c, n_c, y, mod_c, False)
    return x
```

```python
import functools

import numpy as np
import jax
import jax.numpy as jnp
from jax import lax
from jax.experimental import pallas as pl
from jax.experimental.pallas import tpu as pltpu

F32 = jnp.float32
BF16 = jnp.bfloat16
HIGHEST = lax.Precision.HIGHEST

LANES = 128
HEAD_DIM = 64
PAIR = 2 * HEAD_DIM
GRID_W = 64
H_RET, H_GDN, H_NA = 4, 4, 8
D_RET, D_GDN, D_NA = H_RET * HEAD_DIM, H_GDN * HEAD_DIM, H_NA * HEAD_DIM
RET_CHUNK = 128
GDN_CHUNK = 64
GDN_GROUP = 4
GDN_PREP_ROWS = 512
SHORT_CONV = 5
NA_KH, NA_KW = 8, 16
NA_QROWS = 4
NA_KROWS = NA_QROWS + NA_KH
NA_PAIRS = 1
N_FREQ = HEAD_DIM // 4
ROPE_BASE = 10000.0
N_EXPERTS = 8
TOP_K = 2
MOE_ROWS = 512
FF_CHUNK = 256
EPS = 1e-6
NEG_INF = -1e30
VMEM_LIMIT = 56 * 1024 * 1024

C_RET = 0
C_GDN = C_RET + 4 * D_RET
C_AB = C_GDN + 4 * D_GDN
C_NA = C_AB + LANES
C_END = C_NA + 3 * D_NA


def _cparams(sem, vmem=None):
    return pltpu.CompilerParams(dimension_semantics=sem, vmem_limit_bytes=vmem)


def _silu(x):
    return x * jax.nn.sigmoid(x)


def _dot(a, b):
    return jnp.dot(a, b, preferred_element_type=F32)


def _dot_nt(a, b):
    return lax.dot_general(a, b, (((1,), (1,)), ((), ())), preferred_element_type=F32)


def _dot_tn(a, b):
    return lax.dot_general(a, b, (((0,), (0,)), ((), ())), preferred_element_type=F32)


def _split_bf16(x):
    hi = x.astype(BF16)
    return hi, (x - hi.astype(F32)).astype(BF16)


def _dot_3pass(a, b):
    ah, al = _split_bf16(a)
    bh, bl = _split_bf16(b)
    return _dot(ah, bh) + (_dot(ah, bl) + _dot(al, bh))


def _dot_hi(a, b):
    return jnp.dot(a, b, preferred_element_type=F32, precision=HIGHEST)


def _ada_body(c_ref, w_ref, b_ref, o_ref):
    s = _silu(c_ref[...])
    o_ref[0] = _dot_hi(s, w_ref[0]) + b_ref[0]


def _ada_vectors(c_all, ada_w, ada_b):
    depth, d, d6 = ada_w.shape
    rows = c_all.shape[0]
    tn = 1024
    return pl.pallas_call(
        _ada_body,
        grid=(depth, d6 // tn),
        in_specs=[pl.BlockSpec((rows, d), lambda l, j: (0, 0)),
                  pl.BlockSpec((1, d, tn), lambda l, j: (l, 0, j)),
                  pl.BlockSpec((1, 1, tn), lambda l, j: (l, 0, j))],
        out_specs=pl.BlockSpec((1, rows, tn), lambda l, j: (l, 0, j)),
        out_shape=jax.ShapeDtypeStruct((depth, rows, d6), F32),
        compiler_params=_cparams(("arbitrary", "arbitrary")),
        name="ada_vectors",
    )(c_all, ada_w, ada_b.reshape(depth, 1, d6))


def _norm_mod(x, g, shift, scale):
    ms = jnp.mean(x * x, axis=-1, keepdims=True)
    return (x * lax.rsqrt(ms + EPS) * g) * (1.0 + scale) + shift


def _rope_slab(t, cosf, sins):
    lane = lax.broadcasted_iota(jnp.int32, t.shape, 1)
    first = (lane % HEAD_DIM) < (HEAD_DIM // 2)
    partner = jnp.where(first, pltpu.roll(t, LANES - HEAD_DIM // 2, 1), pltpu.roll(t, HEAD_DIM // 2, 1))
    return t * cosf + partner * sins


def _inproj_body(x_ref, mod_ref, g_ref, cos_ref, sin_ref, w_ref, ret_ref, gdn_ref, ab_ref, abt_ref, na_ref, *, rope):
    h = _norm_mod(x_ref[0], g_ref[...], mod_ref[0, 0:1, :], mod_ref[0, 1:2, :]).astype(BF16)
    qk = _dot(h, w_ref[:, C_RET:C_RET + 2 * D_RET])
    slabs = []
    for s in range(2 * D_RET // LANES):
        t = qk[:, s * LANES:(s + 1) * LANES]
        if rope:
            t = _rope_slab(t, cos_ref[...], sin_ref[...])
        if s >= D_RET // LANES:
            t = t * HEAD_DIM ** -0.5
        slabs.append(t)
    ret_ref[0, :, 0:2 * D_RET] = jnp.concatenate(slabs, axis=1).astype(BF16)
    ret_ref[0, :, 2 * D_RET:] = _dot(h, w_ref[:, C_RET + 2 * D_RET:C_GDN]).astype(BF16)
    for j in range(2):
        gdn_ref[0, :, 512 * j:512 * (j + 1)] = _dot(h, w_ref[:, C_GDN + 512 * j:C_GDN + 512 * (j + 1)]).astype(BF16)
    ab = _dot(h, w_ref[:, C_AB:C_NA])
    ab_ref[0] = ab
    for grp in range(ab.shape[0] // LANES):
        t = ab[grp * LANES:(grp + 1) * LANES, :].T
        for p in range(D_GDN // PAIR):
            abt_ref[0, p, grp] = t[8 * p:8 * p + 8, :]
    for j in range(3):
        na_ref[0, :, 512 * j:512 * (j + 1)] = _dot(h, w_ref[:, C_NA + 512 * j:C_NA + 512 * (j + 1)]).astype(BF16)


def _in_projection(x, mod, g, cosf, sins, w, rope):
    b, l, d = x.shape
    tm = min(512, l)
    body = functools.partial(_inproj_body, rope=rope)
    return pl.pallas_call(
        body,
        grid=(b, l // tm),
        in_specs=[pl.BlockSpec((1, tm, d), lambda i, j: (i, j, 0)),
                  pl.BlockSpec((1, 6, d), lambda i, j: (i, 0, 0)),
                  pl.BlockSpec((1, d), lambda i, j: (0, 0)),
                  pl.BlockSpec((tm, LANES), lambda i, j: (j, 0)),
                  pl.BlockSpec((tm, LANES), lambda i, j: (j, 0)),
                  pl.BlockSpec((d, C_END), lambda i, j: (0, 0))],
        out_specs=[pl.BlockSpec((1, tm, 4 * D_RET), lambda i, j: (i, j, 0)),
                   pl.BlockSpec((1, tm, 4 * D_GDN), lambda i, j: (i, j, 0)),
                   pl.BlockSpec((1, tm, LANES), lambda i, j: (i, j, 0)),
                   pl.BlockSpec((1, D_GDN // PAIR, tm // LANES, 8, LANES), lambda i, j: (i, 0, j, 0, 0)),
                   pl.BlockSpec((1, tm, 3 * D_NA), lambda i, j: (i, j, 0))],
        out_shape=[jax.ShapeDtypeStruct((b, l, 4 * D_RET), BF16),
                   jax.ShapeDtypeStruct((b, l, 4 * D_GDN), BF16),
                   jax.ShapeDtypeStruct((b, l, LANES), F32),
                   jax.ShapeDtypeStruct((b, D_GDN // PAIR, l // LANES, 8, LANES), F32),
                   jax.ShapeDtypeStruct((b, l, 3 * D_NA), BF16)],
        compiler_params=_cparams(("arbitrary", "arbitrary"), VMEM_LIMIT),
        name="in_projection",
    )(x, mod, g, cosf, sins, w)


def _pack_w_in(w_in):
    w_in = w_in.astype(BF16)
    d = w_in.shape[0]
    c1 = 4 * D_RET
    c2 = c1 + 4 * D_GDN
    ab = w_in[:, c2:c2 + 4 * H_GDN]
    ab = ab.reshape(d, 2, 2, H_GDN // 2, 2)
    ab = jnp.transpose(ab, (0, 3, 1, 2, 4)).reshape(d, 4 * H_GDN)
    ab = jnp.pad(ab, ((0, 0), (0, LANES - 4 * H_GDN)))
    return jnp.concatenate([w_in[:, :c2], ab, w_in[:, c2 + 4 * H_GDN:]], axis=1)


def _head_masks(shape):
    lane = lax.broadcasted_iota(jnp.int32, shape, len(shape) - 1)
    return lane < HEAD_DIM, lane >= HEAD_DIM


def _per_head(lo, hi, shape):
    m0, _ = _head_masks(shape)
    return jnp.where(m0, lo, hi)


def _head_sumsq(o):
    m0, m1 = _head_masks(o.shape)
    sq = o * o
    s0 = jnp.sum(jnp.where(m0, sq, 0.0), axis=-1, keepdims=True)
    s1 = jnp.sum(jnp.where(m1, sq, 0.0), axis=-1, keepdims=True)
    return jnp.where(m0, s0, s1)


def _ret_body(lg_ref, ql, kl, vl, gl, qc, kc, vc, gc, r_ref, rc_ref, sb_ref):
    c = RET_CHUNK
    npair = D_RET // PAIR
    lane_shape = (c, PAIR)
    pos = lax.broadcasted_iota(jnp.int32, lane_shape, 0).astype(F32)
    ii = lax.broadcasted_iota(jnp.int32, (c, c), 0)
    jj = lax.broadcasted_iota(jnp.int32, (c, c), 1)
    diff = (ii - jj).astype(F32)
    m0, m1 = _head_masks(lane_shape)
    masks = (m0, m1)
    bi = lax.broadcasted_iota(jnp.int32, (PAIR, PAIR), 0) // HEAD_DIM
    bj = lax.broadcasted_iota(jnp.int32, (PAIR, PAIR), 1) // HEAD_DIM
    bd = bi == bj
    cst = []
    for p in range(npair):
        lgf = _per_head(lg_ref[0, 2 * p], lg_ref[0, 2 * p + 1], lane_shape)
        lgb = _per_head(lg_ref[1, 2 * p], lg_ref[1, 2 * p + 1], lane_shape)
        dmats = [jnp.where(diff > 0, jnp.exp(lg_ref[0, 2 * p + hh] * diff),
                           jnp.where(diff < 0, jnp.exp(-lg_ref[1, 2 * p + hh] * diff), 2.0)) for hh in range(2)]
        cst.append(dict(qdf=jnp.exp(lgf * (pos + 1.0)), kdf=jnp.exp(lgf * (c - 1.0 - pos)),
                        qdb=jnp.exp(lgb * (c - pos)), kdb=jnp.exp(lgb * pos),
                        cdf=jnp.exp(lgf[0:1] * c), cdb=jnp.exp(lgb[0:1] * c), dmats=dmats))

    def sweep(q_ref, k_ref, v_ref, g_ref, o_ref, n, sf0, sb0):
        def bstep(t, sbs):
            ci = n - 1 - t
            sl = pl.ds(pl.multiple_of(ci * c, c), c)
            out = []
            for p in range(npair):
                lanes = slice(p * PAIR, (p + 1) * PAIR)
                sb_ref[p, ci] = sbs[p]
                kd = (k_ref[0, sl, lanes].astype(F32) * cst[p]["kdb"]).astype(BF16)
                out.append(sbs[p] * cst[p]["cdb"] + jnp.where(bd, _dot_tn(kd, v_ref[0, sl, lanes]), 0.0))
            return tuple(out)

        sb_fin = lax.fori_loop(0, n, bstep, sb0)

        def fstep(ci, sfs):
            sl = pl.ds(pl.multiple_of(ci * c, c), c)
            out = []
            for p in range(npair):
                lanes = slice(p * PAIR, (p + 1) * PAIR)
                k_ = cst[p]
                q = q_ref[0, sl, lanes]
                k = k_ref[0, sl, lanes]
                v = v_ref[0, sl, lanes]
                qf = q.astype(F32)
                o = (_dot((qf * k_["qdf"]).astype(BF16), sfs[p].astype(BF16))
                     + _dot((qf * k_["qdb"]).astype(BF16), sb_ref[p, ci].astype(BF16)))
                for hh in range(2):
                    qm = jnp.where(masks[hh], q, jnp.zeros_like(q))
                    pm = (_dot_nt(qm, k) * k_["dmats"][hh]).astype(BF16)
                    o = o + jnp.where(masks[hh], _dot(pm, v), 0.0)
                on = o * lax.rsqrt(_head_sumsq(o) * (1.0 / HEAD_DIM) + EPS)
                o_ref[0, sl, lanes] = (on * _silu(g_ref[0, sl, lanes].astype(F32))).astype(o_ref.dtype)
                kd = (k.astype(F32) * k_["kdf"]).astype(BF16)
                out.append(sfs[p] * k_["cdf"] + jnp.where(bd, _dot_tn(kd, v), 0.0))
            return tuple(out)

        sf_fin = lax.fori_loop(0, n, fstep, sf0)
        return sf_fin, sb_fin

    z = tuple(jnp.zeros((PAIR, PAIR), F32) for _ in range(npair))
    sfc, sbc = sweep(qc, kc, vc, gc, rc_ref, qc.shape[1] // c, z, z)
    sweep(ql, kl, vl, gl, r_ref, ql.shape[1] // c, sfc, sbc)


def _retention(ret_l, ret_c, log_gamma):
    b, l, _ = ret_l.shape
    lc = ret_c.shape[1]
    npair = D_RET // PAIR

    def col(k):
        return lambda i: (i, 0, k)

    lat = [pl.BlockSpec((1, l, D_RET), col(k)) for k in range(4)]
    ctx = [pl.BlockSpec((1, lc, D_RET), col(k)) for k in range(4)]
    return pl.pallas_call(
        _ret_body,
        grid=(b,),
        in_specs=[pl.BlockSpec(memory_space=pltpu.SMEM)] + lat + ctx,
        out_specs=[pl.BlockSpec((1, l, D_RET), lambda i: (i, 0, 0)),
                   pl.BlockSpec((1, lc, D_RET), lambda i: (i, 0, 0))],
        out_shape=[jax.ShapeDtypeStruct((b, l, D_RET), BF16), jax.ShapeDtypeStruct((b, lc, D_RET), BF16)],
        scratch_shapes=[pltpu.VMEM((npair, max(l, lc) // RET_CHUNK, PAIR, PAIR), F32)],
        compiler_params=_cparams(("arbitrary",), VMEM_LIMIT),
        name="retention",
    )(log_gamma, ret_l, ret_l, ret_l, ret_l, ret_c, ret_c, ret_c, ret_c)


def _stack_heads(t):
    m0, m1 = _head_masks(t.shape)
    z = jnp.zeros_like(t)
    return jnp.concatenate([jnp.where(m0, t, z), jnp.where(m1, t, z)], axis=0)


def _scan_sum(x, axis, n, reverse):
    size = x.shape[axis]
    pos = lax.broadcasted_iota(jnp.int32, x.shape, axis)
    s = 1
    while s < n:
        if reverse:
            x = x + jnp.where(pos + s < n, pltpu.roll(x, size - s, axis), 0.0)
        else:
            x = x + jnp.where(pos >= s, pltpu.roll(x, s, axis), 0.0)
        s *= 2
    return x


def _gdn_prep(src_ref, col, cw, dst_ref, r0, *, l2, scale):
    n = src_ref.shape[1]
    blk = min(GDN_PREP_ROWS, n)
    halo = 16
    lanes = slice(col % D_GDN, col % D_GDN + LANES)

    def block(i, _):
        b0 = pl.multiple_of(i * blk, blk)
        x = src_ref[0, pl.ds(b0, blk), col:col + LANES].astype(F32)
        lo = src_ref[0, pl.ds(pl.multiple_of(jnp.maximum(b0 - halo, 0), halo), halo), col:col + LANES].astype(F32)
        hi = src_ref[0, pl.ds(pl.multiple_of(jnp.minimum(b0 + blk, n - halo), halo), halo), col:col + LANES].astype(F32)
        lo = jnp.where(b0 > 0, lo, 0.0)
        hi = jnp.where(b0 + blk < n, hi, 0.0)
        ext = jnp.concatenate([lo, x, hi], axis=0)
        acc = x * cw[SHORT_CONV // 2:SHORT_CONV // 2 + 1, :]
        for j in range(SHORT_CONV):
            s = j - SHORT_CONV // 2
            if s != 0:
                sh = pltpu.roll(ext, (-s) % (blk + 2 * halo), 0)
                acc = acc + sh[halo:halo + blk] * cw[j:j + 1, :]
        y = _silu(acc)
        if l2:
            y = y * lax.rsqrt(_head_sumsq(y) + EPS)
        if scale != 1.0:
            y = y * scale
        dst_ref[pl.ds(pl.multiple_of(r0 + b0, halo), blk), lanes] = y.astype(dst_ref.dtype)
        return 0

    lax.fori_loop(0, n // blk, block, 0)


def _gdn_body(x_ref, xc_ref, ab_ref, abt_ref, cw_ref, cst_ref, cstt_ref, ng_ref,
              o_ref, oc_ref, qn, kn, vn, oacc, st_s, x_s, t_s, rhs_s, u_s, w_s, qa_s, kd_s, egl_s):
    c = GDN_CHUNK
    c2 = 2 * c
    grp = GDN_GROUP
    npair = D_GDN // PAIR
    lc = xc_ref.shape[1]
    l = x_ref.shape[1]
    nc = lc // c
    nt = (lc + l) // c
    ii = lax.broadcasted_iota(jnp.int32, (c2, c2), 0)
    jj = lax.broadcasted_iota(jnp.int32, (c2, c2), 1)
    same = (ii // c) == (jj // c)
    eye = (ii == jj).astype(F32)
    dirs = ((same & (ii >= jj), same & (ii > jj), c - 1),
            (same & (ii <= jj), same & (ii < jj), 0))

    def chains(g, slot):
        out = []
        for gi in range(grp):
            t = g * grp + gi
            cf = t
            cb = jnp.where(t < nc, nc - 1 - t, nt + nc - 1 - t)
            for p in range(npair):
                for d in range(2):
                    out.append((((slot * grp + gi) * 2 + d) * npair + p, p, d, cf if d == 0 else cb))
        return out

    def stage_inputs(idx, p, d, ci):
        incl, strict, last = dirs[d]
        lanes = slice(p * PAIR, (p + 1) * PAIR)
        nea, dtb = cst_ref[0:1, :], cst_ref[1:2, :]
        neat, dtbt = cstt_ref[p, 0], cstt_ref[p, 1]
        sl = pl.ds(pl.multiple_of(ci * c, c), c)
        abv = ab_ref[0, sl, :]
        gall = nea * jax.nn.softplus(abv + dtb)
        ball = jax.nn.sigmoid(abv)
        shape = (c, PAIR)
        c0 = 8 * p + 2 * d
        gl = _per_head(gall[:, c0:c0 + 1], gall[:, c0 + 1:c0 + 2], shape)
        bl = _per_head(ball[:, c0 + 4:c0 + 5], ball[:, c0 + 5:c0 + 6], shape)
        gcum = _scan_sum(gl, 0, c, d == 1)
        abt = abt_ref[0, p, ci // 2]
        abt = jnp.where(ci % 2 == 1, pltpu.roll(abt, c, 1), abt)
        gt = neat * jax.nn.softplus(abt + dtbt)
        gtc = _scan_sum(gt, 1, c, d == 1)
        grow = jnp.concatenate([gtc[2 * d:2 * d + 1, 0:c], gtc[2 * d + 1:2 * d + 2, 0:c]], axis=1)
        gcol = jnp.concatenate([gcum[:, 0:1], gcum[:, HEAD_DIM:HEAD_DIM + 1]], axis=0)
        dec = jnp.where(incl, jnp.exp(jnp.where(incl, gcol - grow, 0.0)), 0.0)
        dec = jnp.where(ii == jj, 1.0, dec)
        q = qn[sl, lanes]
        k = kn[sl, lanes]
        v = vn[sl, lanes]
        kf = k.astype(F32)
        eg = jnp.exp(gcum)
        kb = kf * bl
        k_st = _stack_heads(k)
        a = jnp.where(strict, _dot_nt(_stack_heads(kb.astype(BF16)), k_st) * dec, 0.0)
        attn = (_dot_nt(_stack_heads(q), k_st) * dec).astype(BF16)
        x_s[idx] = (-a).astype(BF16)
        t_s[idx] = eye - a
        rhs_s[idx] = jnp.concatenate([_stack_heads((v.astype(F32) * bl).astype(BF16)),
                                      _stack_heads((kb * eg).astype(BF16))], axis=1)
        qa_s[idx] = jnp.concatenate([_stack_heads((q.astype(F32) * eg).astype(BF16)), attn], axis=1)
        glast = gcum[last:last + 1, :]
        kd_s[idx] = _stack_heads(kf * jnp.exp(glast - gcum)).T.astype(BF16)
        egl_s[idx] = jnp.broadcast_to(jnp.exp(glast), (8, PAIR))

    def stage_group(g, slot):
        todo = chains(g, slot)
        for idx, p, d, ci in todo:
            stage_inputs(idx, p, d, ci)
        for _ in range(5):
            for idx, _, _, _ in todo:
                xb = x_s[idx]
                x_s[idx] = _dot(xb, xb).astype(BF16)
            for idx, _, _, _ in todo:
                t = t_s[idx]
                t_s[idx] = t + _dot(t.astype(BF16), x_s[idx])
        for idx, _, _, _ in todo:
            sol = _dot(t_s[idx].astype(BF16), rhs_s[idx])
            u_s[idx] = sol[:, :PAIR]
            w_s[idx] = sol[:, PAIR:].astype(BF16)

    def recur_group(g, slot):
        sts = {(p, d): st_s[d * npair + p] for p in range(npair) for d in range(2)}
        for idx, p, d, ci in chains(g, slot):
            sl = pl.ds(pl.multiple_of(ci * c, c), c)
            st = sts[(p, d)]
            stb = st.astype(BF16)
            v_new = (u_s[idx] - _dot(w_s[idx], stb)).astype(BF16)
            o_st = _dot(qa_s[idx], jnp.concatenate([stb, v_new], axis=0))
            oacc[sl, p * PAIR:(p + 1) * PAIR] += o_st[:c] + o_st[c:]
            sts[(p, d)] = st * egl_s[idx][0:1, :] + _dot(kd_s[idx], v_new)
        for (p, d), st in sts.items():
            st_s[d * npair + p] = st

    def prep(src_ref, r0):
        for s in range(D_GDN // LANES):
            for k, (dst, l2, scale) in enumerate(((qn, True, HEAD_DIM ** -0.5), (kn, True, 1.0), (vn, False, 1.0))):
                col = k * D_GDN + s * LANES
                _gdn_prep(src_ref, col, cw_ref[:, col:col + LANES], dst, r0, l2=l2, scale=scale)

    def finish(src_ref, out_ref, r0, n):
        for p in range(npair):
            lanes = slice(p * PAIR, (p + 1) * PAIR)
            o = oacc[r0:r0 + n, lanes]
            on = o * lax.rsqrt(_head_sumsq(o) * (1.0 / HEAD_DIM) + EPS)
            gate = src_ref[0, :, 3 * D_GDN + p * PAIR:3 * D_GDN + (p + 1) * PAIR].astype(F32)
            out_ref[0, :, lanes] = (on * ng_ref[...] * _silu(gate)).astype(out_ref.dtype)

    st_s[...] = jnp.zeros_like(st_s)
    oacc[...] = jnp.zeros_like(oacc)
    prep(xc_ref, 0)
    prep(x_ref, lc)

    ng = nt // grp
    stage_group(0, 0)

    def two_groups(kk, _):
        g = 2 * kk
        recur_group(g, 0)
        stage_group(g + 1, 1)
        recur_group(g + 1, 1)
        stage_group(g + 2, 0)
        return 0

    lax.fori_loop(0, (ng - 1) // 2, two_groups, 0)
    if ng % 2 == 0:
        recur_group(ng - 2, 0)
        stage_group(ng - 1, 1)
        recur_group(ng - 1, 1)
    else:
        recur_group(ng - 1, 0)
    finish(xc_ref, oc_ref, 0, lc)
    finish(x_ref, o_ref, lc, l)


def _gdn_tables(conv_w, a_log, dt_bias, norm_g):
    npair = D_GDN // PAIR
    nea = -jnp.exp(a_log.astype(F32)).reshape(2, npair, 2)
    dtb = dt_bias.astype(F32).reshape(2, npair, 2)
    rows = jnp.stack([jnp.transpose(nea, (1, 0, 2)).reshape(npair, 4), jnp.transpose(dtb, (1, 0, 2)).reshape(npair, 4)], axis=1)
    cst = jnp.pad(jnp.transpose(jnp.pad(rows, ((0, 0), (0, 0), (0, 4))), (1, 0, 2)).reshape(2, 8 * npair),
                  ((0, 6), (0, LANES - 8 * npair)))
    cstt = jnp.broadcast_to(jnp.pad(rows, ((0, 0), (0, 0), (0, 4)))[..., None], (npair, 2, 8, LANES))
    cw = jnp.pad(conv_w.astype(F32), ((0, 8 - SHORT_CONV), (0, 0)))
    ng = jnp.tile(norm_g.astype(F32), 2).reshape(1, PAIR)
    return cw, cst, cstt, ng


def _gdn(gdn_l, ab_l, abt_l, gdn_c, ab_c, abt_c, tables):
    b, l, _ = gdn_l.shape
    lc = gdn_c.shape[1]
    npair = D_GDN // PAIR
    lt = l + lc
    assert (lt // GDN_CHUNK) % GDN_GROUP == 0 and lc % LANES == 0
    cw, cst, cstt, ng = tables

    nstage = 2 * GDN_GROUP * 2 * npair

    def whole(shape):
        nd = len(shape)
        return pl.BlockSpec(shape, lambda i: (0,) * nd)

    def per_batch(shape, single=False):
        nd = len(shape)
        return pl.BlockSpec((1,) + shape, lambda i: (i,) + (0,) * nd, pipeline_mode=pl.Buffered(1) if single else None)

    sq = (nstage, PAIR, PAIR)
    return pl.pallas_call(
        _gdn_body,
        grid=(b,),
        in_specs=[per_batch((l, 4 * D_GDN), True), per_batch((lc, 4 * D_GDN)), per_batch((lt, LANES), True),
                  per_batch((npair, lt // LANES, 8, LANES)),
                  whole((8, 3 * D_GDN)), whole((8, LANES)), whole((npair, 2, 8, LANES)), whole((1, PAIR))],
        out_specs=[per_batch((l, D_GDN)), per_batch((lc, D_GDN))],
        out_shape=[jax.ShapeDtypeStruct((b, l, D_GDN), BF16), jax.ShapeDtypeStruct((b, lc, D_GDN), BF16)],
        scratch_shapes=[pltpu.VMEM((lt, D_GDN), BF16)] * 3 + [pltpu.VMEM((lt, D_GDN), F32)]
        + [pltpu.VMEM((2 * npair, PAIR, PAIR), F32), pltpu.VMEM(sq, BF16), pltpu.VMEM(sq, F32),
           pltpu.VMEM((nstage, PAIR, 2 * PAIR), BF16), pltpu.VMEM(sq, F32), pltpu.VMEM(sq, BF16),
           pltpu.VMEM((nstage, PAIR, 2 * PAIR), BF16), pltpu.VMEM(sq, BF16), pltpu.VMEM((nstage, 8, PAIR), F32)],
        compiler_params=_cparams(("arbitrary",), VMEM_LIMIT),
        name="gated_deltanet",
    )(gdn_l, gdn_c, jnp.concatenate([ab_c, ab_l], axis=1), jnp.concatenate([abt_c, abt_l], axis=2), cw, cst, cstt, ng)


def _na_bias_tiles(rpb, rows):
    w = GRID_W
    ext = jnp.pad(rpb.astype(F32), ((0, 0), (0, 0), (w - NA_KW, w - NA_KW)))
    cq = np.arange(w)[:, None]
    ck = np.arange(w)[None, :]
    tb = ext[:, :, jnp.asarray(ck - cq + w - 1)]
    ws = np.clip(cq - NA_KW // 2, 0, w - NA_KW)
    col_ok = (ck >= ws) & (ck < ws + NA_KW)
    tb = jnp.where(jnp.asarray(col_ok)[None, None], tb, NEG_INF)
    tb = jnp.concatenate([tb, jnp.full((tb.shape[0], 1, w, w), NEG_INF, F32)], axis=1)
    nt = rows // NA_QROWS
    idx = np.zeros((3, NA_QROWS, NA_KROWS), np.int32)
    for cls, t in enumerate((0, 1, nt - 1)):
        ks = int(np.clip(NA_QROWS * t - NA_KH // 2, 0, rows - NA_KROWS))
        for rl in range(NA_QROWS):
            r = NA_QROWS * t + rl
            r0 = int(np.clip(r - NA_KH // 2, 0, rows - NA_KH))
            for j in range(NA_KROWS):
                kr = ks + j
                idx[cls, rl, j] = kr - r + NA_KH - 1 if r0 <= kr < r0 + NA_KH else 2 * NA_KH - 1
    tiles = tb[:, jnp.asarray(idx)]
    tiles = jnp.transpose(tiles, (1, 0, 2, 4, 3, 5))
    return tiles.reshape(3, tb.shape[0], NA_QROWS * w, NA_KROWS * w)


def _softmax_pv(s_parts, v_parts):
    m = functools.reduce(jnp.maximum, [jnp.max(s, axis=-1, keepdims=True) for s in s_parts])
    ps = [jnp.exp(s - m) for s in s_parts]
    den = functools.reduce(lambda x, y: x + y, [jnp.sum(p, axis=-1, keepdims=True) for p in ps])
    o = functools.reduce(lambda x, y: x + y, [_dot(p.astype(BF16), v) for p, v in zip(ps, v_parts)])
    return o / den


def _na_body(q_ref, k_ref, v_ref, qc_ref, kc_ref, vc_ref, bias_ref, o_ref, oc_ref, *, rows):
    w = GRID_W
    tq = NA_QROWS * w
    tk = NA_KROWS * w
    nt = rows // NA_QROWS
    scale = HEAD_DIM ** -0.5
    m0, m1 = _head_masks((tq, PAIR))
    masks = (m0, m1)

    def tile(t, _):
        cls = jnp.where(t > 0, 1, 0) + jnp.where(t == nt - 1, 1, 0)
        ks = jnp.clip(NA_QROWS * t - NA_KH // 2, 0, rows - NA_KROWS) * w
        ksl = pl.ds(pl.multiple_of(ks, w), tk)
        qsl = pl.ds(pl.multiple_of(t * tq, tq), tq)
        for p in range(NA_PAIRS):
            lanes = slice(p * PAIR, (p + 1) * PAIR)
            q = q_ref[0, qsl, lanes] * scale
            kb = k_ref[0, ksl, lanes]
            vb = v_ref[0, ksl, lanes]
            kc = kc_ref[0, :, lanes]
            vc = vc_ref[0, :, lanes]
            acc = jnp.zeros((tq, PAIR), F32)
            for hh in range(2):
                qm = jnp.where(masks[hh], q, jnp.zeros_like(q))
                s_loc = _dot_nt(qm, kb) + bias_ref[cls, 2 * p + hh]
                s_ctx = _dot_nt(qm, kc)
                acc = acc + jnp.where(masks[hh], _softmax_pv([s_loc, s_ctx], [vb, vc]), 0.0)
            o_ref[0, qsl, lanes] = acc.astype(o_ref.dtype)
        return 0

    lax.fori_loop(0, nt, tile, 0)
    for p in range(NA_PAIRS):
        lanes = slice(p * PAIR, (p + 1) * PAIR)
        qc = qc_ref[0, :, lanes] * scale
        kc = kc_ref[0, :, lanes]
        vc = vc_ref[0, :, lanes]
        mc0, mc1 = _head_masks(qc.shape)
        accc = jnp.zeros(qc.shape, F32)
        for mk in (mc0, mc1):
            qm = jnp.where(mk, qc, jnp.zeros_like(qc))
            accc = accc + jnp.where(mk, _softmax_pv([_dot_nt(qm, kc)], [vc]), 0.0)
        oc_ref[0, :, lanes] = accc.astype(oc_ref.dtype)


def _neighbourhood_attention(na_l, na_c, bias):
    b, l, _ = na_l.shape
    lc = na_c.shape[1]
    rows = l // GRID_W
    wide = NA_PAIRS * PAIR
    ngrp = D_NA // wide

    def col(k):
        return lambda p, i: (i, 0, k * ngrp + p)

    lat = [pl.BlockSpec((1, l, wide), col(k)) for k in range(3)]
    ctx = [pl.BlockSpec((1, lc, wide), col(k)) for k in range(3)]
    return pl.pallas_call(
        functools.partial(_na_body, rows=rows),
        grid=(ngrp, b),
        in_specs=lat + ctx + [pl.BlockSpec((3, 2 * NA_PAIRS, NA_QROWS * GRID_W, NA_KROWS * GRID_W), lambda p, i: (0, p, 0, 0))],
        out_specs=[pl.BlockSpec((1, l, wide), lambda p, i: (i, 0, p)),
                   pl.BlockSpec((1, lc, wide), lambda p, i: (i, 0, p))],
        out_shape=[jax.ShapeDtypeStruct((b, l, D_NA), BF16), jax.ShapeDtypeStruct((b, lc, D_NA), BF16)],
        compiler_params=_cparams(("arbitrary", "arbitrary"), VMEM_LIMIT),
        name="neighbourhood_attention",
    )(na_l, na_l, na_l, na_c, na_c, na_c, bias)


def _outproj_residual(r_ref, g_ref, n_ref, x_ref, mod_ref, wo_ref):
    mix = jnp.concatenate([r_ref[0], g_ref[0], n_ref[0]], axis=1)
    return x_ref[0] + mod_ref[0, 2:3, :] * _dot(mix, wo_ref[...])


def _swiglu_chunk(hb, w1_ref, w3_ref, w2_ref, j, between=None):
    cols = slice(j * FF_CHUNK, (j + 1) * FF_CHUNK)
    between = between or (lambda k: None)
    between(0)
    a = _dot(hb, w1_ref[:, cols])
    between(1)
    t = (_silu(a) * _dot(hb, w3_ref[:, cols])).astype(BF16)
    between(2)
    return _dot(t, w2_ref[cols, :])


def _swiglu_chunks(hb, w1_ref, w3_ref, w2_ref, acc_ref):
    for j in range(w2_ref.shape[0] // FF_CHUNK):
        if j == 0:
            acc_ref[...] = _swiglu_chunk(hb, w1_ref, w3_ref, w2_ref, j)
        else:
            acc_ref[...] += _swiglu_chunk(hb, w1_ref, w3_ref, w2_ref, j)


def _dense_body(r_ref, g_ref, n_ref, x_ref, mod_ref, g2_ref, wo_ref, w1_ref, w3_ref, w2_ref, o_ref, acc_ref):
    x1 = _outproj_residual(r_ref, g_ref, n_ref, x_ref, mod_ref, wo_ref)
    hb = _norm_mod(x1, g2_ref[...], mod_ref[0, 3:4, :], mod_ref[0, 4:5, :]).astype(BF16)
    _swiglu_chunks(hb, w1_ref, w3_ref, w2_ref, acc_ref)
    o_ref[0] = x1 + mod_ref[0, 5:6, :] * acc_ref[...]


def _pack_bf16_pairs(h):
    m = h.shape[1] // 2
    bits = lax.bitcast_convert_type(h.astype(BF16).astype(F32), jnp.uint32)
    return (bits[:, :m] >> 16) | (bits[:, m:] & jnp.uint32(0xFFFF0000))


def _unpack_bf16_pairs(u):
    lo = lax.bitcast_convert_type(u << 16, F32)
    hi = lax.bitcast_convert_type(u & jnp.uint32(0xFFFF0000), F32)
    return jnp.concatenate([lo, hi], axis=1).astype(BF16)


def _router_body(r_ref, g_ref, n_ref, x_ref, mod_ref, g2_ref, wo_ref, wr_ref, x1_ref, h_ref, lg_ref):
    x1 = _outproj_residual(r_ref, g_ref, n_ref, x_ref, mod_ref, wo_ref)
    h = _norm_mod(x1, g2_ref[...], mod_ref[0, 3:4, :], mod_ref[0, 4:5, :])
    x1_ref[0] = x1
    h_ref[0] = _pack_bf16_pairs(h)
    lg_ref[0] = _dot_3pass(h, wr_ref[...])


def _resident(shape):
    nd = len(shape)
    return pl.BlockSpec(shape, lambda i, j: (0,) * nd, pipeline_mode=pl.Buffered(1))


def _mixer_specs(tm, d):
    return [pl.BlockSpec((1, tm, D_RET), lambda i, j: (i, j, 0)),
            pl.BlockSpec((1, tm, D_GDN), lambda i, j: (i, j, 0)),
            pl.BlockSpec((1, tm, D_NA), lambda i, j: (i, j, 0)),
            pl.BlockSpec((1, tm, d), lambda i, j: (i, j, 0)),
            pl.BlockSpec((1, 6, d), lambda i, j: (i, 0, 0)),
            pl.BlockSpec((1, d), lambda i, j: (0, 0))]


def _dense_block(r, g, n, x, mod, g2, wo, w1, w3, w2):
    b, l, d = x.shape
    tm = min(512, l)
    dff = w1.shape[1]
    return pl.pallas_call(
        _dense_body,
        grid=(b, l // tm),
        in_specs=_mixer_specs(tm, d) + [_resident((d, d)), _resident((d, dff)), _resident((d, dff)), _resident((dff, d))],
        out_specs=pl.BlockSpec((1, tm, d), lambda i, j: (i, j, 0)),
        out_shape=jax.ShapeDtypeStruct((b, l, d), F32),
        scratch_shapes=[pltpu.VMEM((tm, d), F32)],
        compiler_params=_cparams(("arbitrary", "arbitrary"), VMEM_LIMIT),
        name="outproj_swiglu",
    )(r, g, n, x, mod, g2, wo, w1, w3, w2)


def _router_block(r, g, n, x, mod, g2, wo, wr):
    b, l, d = x.shape
    tm = min(512, l)
    blk = pl.BlockSpec((1, tm, d), lambda i, j: (i, j, 0))
    return pl.pallas_call(
        _router_body,
        grid=(b, l // tm),
        in_specs=_mixer_specs(tm, d) + [_resident((d, d)), _resident((d, LANES))],
        out_specs=[blk, pl.BlockSpec((1, tm, d // 2), lambda i, j: (i, j, 0)), pl.BlockSpec((1, tm, LANES), lambda i, j: (i, j, 0))],
        out_shape=[jax.ShapeDtypeStruct((b, l, d), F32), jax.ShapeDtypeStruct((b, l, d // 2), jnp.uint32),
                   jax.ShapeDtypeStruct((b, l, LANES), F32)],
        compiler_params=_cparams(("arbitrary", "arbitrary"), VMEM_LIMIT),
        name="outproj_router",
    )(r, g, n, x, mod, g2, wo, wr)


def _row_copy_in(idx_ref, r, src_hbm, dst_ref, sem):
    return pltpu.make_async_copy(src_hbm.at[pl.ds(idx_ref[0, 0, r], 1), :], dst_ref.at[pl.ds(r, 1), :], sem)


def _row_copy_out(idx_ref, r, src_ref, dst_hbm, sem):
    return pltpu.make_async_copy(src_ref.at[pl.ds(r, 1), :], dst_hbm.at[pl.ds(idx_ref[0, 0, r], 1), :], sem)


def _moe_body(bval_ref, bexp_ref, tok_ref, tokn_ref, slotp_ref, h_hbm, w1_ref, w3_ref, w2_ref, y_hbm,
              xbuf, ybuf, gsem, ssem, *, n_asg):
    i = pl.program_id(0)
    tb = xbuf.shape[1]
    nch = w2_ref.shape[1] // FF_CHUNK
    cur = i % 2
    per = -(-tb // (3 * nch))
    valid = bval_ref[i] == 1
    prev_valid = bval_ref[jnp.maximum(i - 1, 0)] == 1

    def wait_gather(slot):
        pltpu.make_async_copy(h_hbm.at[pl.ds(0, tb), :], xbuf.at[slot], gsem).wait()

    def wait_scatter(slot):
        pltpu.make_async_copy(ybuf.at[slot], y_hbm.at[pl.ds(0, tb), :], ssem).wait()

    @pl.when(i == 0)
    def _():
        ybuf[1] = jnp.zeros((tb, ybuf.shape[2]), F32)
        fills = [pltpu.make_async_copy(ybuf.at[1], y_hbm.at[pl.ds(r0, tb), :], ssem)
                 for r0 in range(n_asg, y_hbm.shape[0], tb)]
        for f in fills:
            f.start()
        for f in fills:
            f.wait()

        def issue(r, _):
            _row_copy_in(tok_ref, r, h_hbm, xbuf.at[0], gsem).start()
            return 0

        lax.fori_loop(0, tb, issue, 0)

    @pl.when(valid)
    def _():
        wait_gather(cur)
        hb = _unpack_bf16_pairs(xbuf[cur])
        acc = ybuf.at[cur]
        prv = ybuf.at[1 - cur]
        nxt = xbuf.at[1 - cur]
        for j in range(nch):
            def copies(k, j=j):
                g = 3 * j + k
                for r in range(g * per, min(tb, (g + 1) * per)):
                    _row_copy_in(tokn_ref, r, h_hbm, nxt, gsem).start()
                    _row_copy_out(slotp_ref, r, prv, y_hbm, ssem).start(priority=1)

            part = _swiglu_chunk(hb, w1_ref.at[0], w3_ref.at[0], w2_ref.at[0], j, copies)
            if j == 0:
                acc[...] = part
            else:
                acc[...] += part
        wait_scatter(1 - cur)

    @pl.when(jnp.logical_not(valid) & prev_valid & (i > 0))
    def _():
        wait_gather(cur)

        def issue(r, _):
            _row_copy_out(slotp_ref, r, ybuf.at[1 - cur], y_hbm, ssem).start()
            return 0

        lax.fori_loop(0, tb, issue, 0)
        wait_scatter(1 - cur)


def _moe_experts(h_rows, row_tok, row_slot, block_expert, block_valid, w1, w3, w2, n_out_rows, n_spare):
    d = w2.shape[-1]
    dff = w2.shape[1]
    nb = block_expert.shape[0]
    tb = MOE_ROWS
    smem = functools.partial(pl.BlockSpec, (1, 1, tb), memory_space=pltpu.SMEM)
    grid_spec = pltpu.PrefetchScalarGridSpec(
        num_scalar_prefetch=2,
        grid=(nb,),
        in_specs=[smem(lambda i, bv, be: (i, 0, 0)),
                  smem(lambda i, bv, be: (jnp.minimum(i + 1, nb - 1), 0, 0)),
                  smem(lambda i, bv, be: (i, 0, 0)),
                  pl.BlockSpec(memory_space=pl.ANY),
                  pl.BlockSpec((1, d, dff), lambda i, bv, be: (be[i], 0, 0)),
                  pl.BlockSpec((1, d, dff), lambda i, bv, be: (be[i], 0, 0)),
                  pl.BlockSpec((1, dff, d), lambda i, bv, be: (be[i], 0, 0))],
        out_specs=pl.BlockSpec(memory_space=pl.ANY),
        scratch_shapes=[pltpu.VMEM((2, tb, d // 2), jnp.uint32), pltpu.VMEM((2, tb, d), F32),
                        pltpu.SemaphoreType.DMA(()), pltpu.SemaphoreType.DMA(())],
    )
    return pl.pallas_call(
        functools.partial(_moe_body, n_asg=n_out_rows - n_spare),
        grid_spec=grid_spec,
        out_shape=jax.ShapeDtypeStruct((n_out_rows, d), F32),
        compiler_params=_cparams(("arbitrary",), VMEM_LIMIT),
        name="moe_experts",
    )(block_valid, block_expert, row_tok.reshape(nb, 1, tb), row_tok.reshape(nb, 1, tb), row_slot.reshape(nb + 1, 1, tb),
      h_rows, w1, w3, w2)


def _combine_body(y0_ref, y1_ref, x1_ref, gate_ref, gm_ref, fg_ref, o_ref, *, final):
    moe = gate_ref[:, 0:1] * y0_ref[...] + gate_ref[:, 1:2] * y1_ref[...]
    x2 = x1_ref[...] + gm_ref[0] * moe
    if final:
        ms = jnp.mean(x2 * x2, axis=-1, keepdims=True)
        x2 = x2 * lax.rsqrt(ms + EPS) * fg_ref[...]
    o_ref[...] = x2


def _moe_combine(y_rows, x1, gates, gate_mlp, final_g, final):
    b, l, d = x1.shape
    n_tok = b * l
    tm = 256
    per_b = l // tm
    out = pl.pallas_call(
        functools.partial(_combine_body, final=final),
        grid=(n_tok // tm,),
        in_specs=[pl.BlockSpec((tm, d), lambda i: (i, 0)),
                  pl.BlockSpec((tm, d), lambda i: (n_tok // tm + i, 0)),
                  pl.BlockSpec((tm, d), lambda i: (i, 0)),
                  pl.BlockSpec((tm, 2), lambda i: (i, 0)),
                  pl.BlockSpec((1, 1, d), lambda i: (i // per_b, 0, 0)),
                  pl.BlockSpec((1, d), lambda i: (0, 0))],
        out_specs=pl.BlockSpec((tm, d), lambda i: (i, 0)),
        out_shape=jax.ShapeDtypeStruct((n_tok, d), F32),
        compiler_params=_cparams(("arbitrary",), VMEM_LIMIT),
        name="moe_combine_final_norm",
    )(y_rows, y_rows, x1.reshape(n_tok, d), gates, gate_mlp, final_g.reshape(1, d))
    return out.reshape(b, l, d)


def _route(logits):
    n_tok = logits.shape[0]
    tb = MOE_ROWS
    n_asg = n_tok * TOP_K
    top_logit, top_e = lax.top_k(logits[:, :N_EXPERTS], TOP_K)
    gates = jax.nn.softmax(top_logit, axis=-1)
    e_flat = top_e.reshape(-1).astype(jnp.int32)
    asg = jnp.arange(n_asg, dtype=jnp.int32)
    by_expert = lax.sort(e_flat * n_asg + asg)
    counts = jnp.sum((e_flat[:, None] == jnp.arange(N_EXPERTS, dtype=jnp.int32)[None, :]).astype(jnp.int32), axis=0)
    starts = jnp.cumsum(counts) - counts
    padded = (counts + tb - 1) // tb * tb
    pad_ends = jnp.cumsum(padded)
    pad_starts = pad_ends - padded
    nb = (n_asg + tb - 1) // tb + N_EXPERTS + 1
    n_rows = nb * tb
    block_start = jnp.arange(nb, dtype=jnp.int32) * tb
    block_expert = jnp.minimum(jnp.sum((block_start[:, None] >= pad_ends[None, :]).astype(jnp.int32), axis=1), N_EXPERTS - 1)
    block_valid = (block_start < pad_ends[-1]).astype(jnp.int32)
    row = jnp.arange(n_rows, dtype=jnp.int32)
    row_e = jnp.repeat(block_expert, tb)
    off = row - pad_starts[row_e]
    is_pad = (off >= counts[row_e]) | (jnp.repeat(block_valid, tb) == 0)
    src = by_expert[jnp.clip(starts[row_e] + off, 0, n_asg - 1)] - row_e * n_asg
    row_asg = jnp.where(is_pad, -1, src)
    spare = tb + n_asg + jnp.cumsum(is_pad.astype(jnp.int32)) - 1
    row_slot = jnp.where(is_pad, spare, (row_asg % TOP_K) * n_tok + row_asg // TOP_K)
    row_tok = jnp.where(is_pad, 0, row_asg // TOP_K)
    row_slot = jnp.concatenate([n_asg + jnp.arange(tb, dtype=jnp.int32), row_slot])
    last_e = block_expert[jnp.maximum(pad_ends[-1] // tb - 1, 0)]
    block_expert = jnp.where(block_valid == 1, block_expert, last_e)
    n_spare = (tb + n_rows - n_asg + tb - 1) // tb * tb
    return row_tok, row_slot, gates, block_expert, block_valid, n_asg + n_spare, n_spare


def _final_norm_body(x_ref, g_ref, o_ref):
    x = x_ref[...]
    ms = jnp.mean(x * x, axis=-1, keepdims=True)
    o_ref[...] = x * lax.rsqrt(ms + EPS) * g_ref[...]


def _final_norm(x, g):
    b, l, d = x.shape
    n = b * l
    tm = min(512, n)
    out = pl.pallas_call(
        _final_norm_body,
        grid=(n // tm,),
        in_specs=[pl.BlockSpec((tm, d), lambda i: (i, 0)), pl.BlockSpec((1, d), lambda i: (0, 0))],
        out_specs=pl.BlockSpec((tm, d), lambda i: (i, 0)),
        out_shape=jax.ShapeDtypeStruct((n, d), F32),
        compiler_params=_cparams(("arbitrary",)),
        name="final_norm",
    )(x.reshape(n, d), g.reshape(1, d))
    return out.reshape(b, l, d)


def _rope_tables(n_tok):
    t = jnp.arange(n_tok, dtype=jnp.int32)
    row = (t // GRID_W).astype(F32)
    col = (t % GRID_W).astype(F32)
    inv_freq = ROPE_BASE ** (-jnp.arange(N_FREQ, dtype=F32) / N_FREQ)
    ang = jnp.concatenate([row[:, None] * inv_freq, col[:, None] * inv_freq], axis=-1)
    cos, sin = jnp.cos(ang), jnp.sin(ang)
    cosf = jnp.concatenate([cos, cos, cos, cos], axis=-1)
    sins = jnp.concatenate([-sin, sin, -sin, sin], axis=-1)
    return cosf, sins


def kernel(x, c, ctx, c_ctx, ada_w, ada_b, norm1_g, norm2_g, w_in, w_out, conv_w, ret_decay, gdn_a_log, gdn_dt_bias,
           gdn_norm_g, na_rpb, ffn_w1, ffn_w3, ffn_w2, moe_router, moe_w1, moe_w3, moe_w2, final_g):
    b, l, d = x.shape
    lc = ctx.shape[1]
    depth = ada_w.shape[0]
    cosf, sins = _rope_tables(l)
    ones_c = jnp.ones((lc, LANES), F32)
    zeros_c = jnp.zeros((lc, LANES), F32)

    rows = ((b + 1 + 7) // 8) * 8
    c_all = jnp.zeros((rows, d), F32).at[:b].set(c).at[b].set(c_ctx)
    mod = _ada_vectors(c_all, ada_w, ada_b).reshape(depth, rows, 6, d)

    w1p_all = jax.vmap(_pack_w_in)(w_in)
    wo_all = w_out.astype(BF16)
    log_gamma_all = jnp.log1p(-jnp.exp2(-ret_decay.astype(F32)))
    gdn_tables = jax.vmap(_gdn_tables)(conv_w, gdn_a_log, gdn_dt_bias, gdn_norm_g)
    na_bias = jax.vmap(lambda r: _na_bias_tiles(r, l // GRID_W))(na_rpb)

    y = ctx
    for layer in range(depth):
        need_ctx = layer < depth - 1
        mod_l = mod[layer, :b]
        mod_c = jnp.broadcast_to(mod[layer, b][None], (b, 6, d))
        w1p = w1p_all[layer]
        g1 = norm1_g[layer].reshape(1, d)
        ret_l, gdn_l, ab_l, abt_l, na_l = _in_projection(x, mod_l, g1, cosf, sins, w1p, rope=True)
        ret_c, gdn_c, ab_c, abt_c, na_c = _in_projection(y, mod_c, g1, ones_c, zeros_c, w1p, rope=False)

        r_l, r_c = _retention(ret_l, ret_c, log_gamma_all[layer])
        g_l, g_c = _gdn(gdn_l, ab_l, abt_l, gdn_c, ab_c, abt_c, [t[layer] for t in gdn_tables])
        n_l, n_c = _neighbourhood_attention(na_l, na_c, na_bias[layer])

        wo = wo_all[layer]
        g2 = norm2_g[layer].reshape(1, d)
        j = layer // 2
        if layer % 2 == 0:
            w1, w3, w2 = ffn_w1[j].astype(BF16), ffn_w3[j].astype(BF16), ffn_w2[j].astype(BF16)
            x = _dense_block(r_l, g_l, n_l, x, mod_l, g2, wo, w1, w3, w2)
            if need_ctx:
                y = _dense_block(r_c, g_c, n_c, y, mod_c, g2, wo, w1, w3, w2)
            if layer == depth - 1:
                x = _final_norm(x, final_g)
        else:
            wr = jnp.pad(moe_router[j].astype(F32), ((0, 0), (0, LANES - N_EXPERTS)))
            w1, w3, w2 = moe_w1[j].astype(BF16), moe_w3[j].astype(BF16), moe_w2[j].astype(BF16)

            def moe_ffn(r, g, n, xin, m, last):
                bb, ll, _ = xin.shape
                x1, h, logits = _router_block(r, g, n, xin, m, g2, wo, wr)
                row_tok, row_slot, gates, bexp, bval, n_out, n_spare = _route(logits.reshape(bb * ll, LANES))
                y_rows = _moe_experts(h.reshape(bb * ll, d // 2), row_tok, row_slot, bexp, bval, w1, w3, w2, n_out, n_spare)
                return _moe_combine(y_rows, x1, gates, m[:, 5:6, :], final_g, last)

            x = moe_ffn(r_l, g_l, n_l, x, mod_l, layer == depth - 1)
            if need_ctx:
                y = moe_ffn(r_c, g_c, n_c, y, mod_c, False)
    return x
```

```python
import functools

import numpy as np
import jax
import jax.numpy as jnp
from jax import lax
from jax.experimental import pallas as pl
from jax.experimental.pallas import tpu as pltpu

F32 = jnp.float32
BF16 = jnp.bfloat16
HIGHEST = lax.Precision.HIGHEST

LANES = 128
HEAD_DIM = 64
PAIR = 2 * HEAD_DIM
GRID_W = 64
H_RET, H_GDN, H_NA = 4, 4, 8
D_RET, D_GDN, D_NA = H_RET * HEAD_DIM, H_GDN * HEAD_DIM, H_NA * HEAD_DIM
RET_CHUNK = 128
GDN_CHUNK = 64
GDN_GROUP = 4
GDN_PREP_ROWS = 512
SHORT_CONV = 5
NA_KH, NA_KW = 8, 16
NA_QROWS = 4
NA_KROWS = NA_QROWS + NA_KH
NA_PAIRS = 1
NA_QSCALE = float(np.log2(np.e)) * HEAD_DIM ** -0.5
N_FREQ = HEAD_DIM // 4
ROPE_BASE = 10000.0
N_EXPERTS = 8
TOP_K = 2
MOE_ROWS = 512
FF_CHUNK = 256
EPS = 1e-6
NEG_INF = -1e30
VMEM_LIMIT = 56 * 1024 * 1024

C_RET = 0
C_GDN = C_RET + 4 * D_RET
C_AB = C_GDN + 4 * D_GDN
C_NA = C_AB + LANES
C_END = C_NA + 3 * D_NA


def _cparams(sem, vmem=None):
    return pltpu.CompilerParams(dimension_semantics=sem, vmem_limit_bytes=vmem)


def _silu(x):
    return x * jax.nn.sigmoid(x)


def _dot(a, b):
    return jnp.dot(a, b, preferred_element_type=F32)


def _dot_nt(a, b):
    return lax.dot_general(a, b, (((1,), (1,)), ((), ())), preferred_element_type=F32)


def _dot_tn(a, b):
    return lax.dot_general(a, b, (((0,), (0,)), ((), ())), preferred_element_type=F32)


def _split_bf16(x):
    hi = x.astype(BF16)
    return hi, (x - hi.astype(F32)).astype(BF16)


def _dot_3pass(a, b):
    ah, al = _split_bf16(a)
    bh, bl = _split_bf16(b)
    return _dot(ah, bh) + (_dot(ah, bl) + _dot(al, bh))


def _dot_hi(a, b):
    return jnp.dot(a, b, preferred_element_type=F32, precision=HIGHEST)


def _ada_body(c_ref, w_ref, b_ref, o_ref):
    s = _silu(c_ref[...])
    o_ref[0] = _dot_hi(s, w_ref[0]) + b_ref[0]


def _ada_vectors(c_all, ada_w, ada_b):
    depth, d, d6 = ada_w.shape
    rows = c_all.shape[0]
    tn = 1024
    return pl.pallas_call(
        _ada_body,
        grid=(depth, d6 // tn),
        in_specs=[pl.BlockSpec((rows, d), lambda l, j: (0, 0)),
                  pl.BlockSpec((1, d, tn), lambda l, j: (l, 0, j)),
                  pl.BlockSpec((1, 1, tn), lambda l, j: (l, 0, j))],
        out_specs=pl.BlockSpec((1, rows, tn), lambda l, j: (l, 0, j)),
        out_shape=jax.ShapeDtypeStruct((depth, rows, d6), F32),
        compiler_params=_cparams(("arbitrary", "arbitrary")),
        name="ada_vectors",
    )(c_all, ada_w, ada_b.reshape(depth, 1, d6))


def _norm_mod(x, g, shift, scale):
    ms = jnp.mean(x * x, axis=-1, keepdims=True)
    return (x * lax.rsqrt(ms + EPS) * g) * (1.0 + scale) + shift


def _rope_slab(t, cosf, sins):
    lane = lax.broadcasted_iota(jnp.int32, t.shape, 1)
    first = (lane % HEAD_DIM) < (HEAD_DIM // 2)
    partner = jnp.where(first, pltpu.roll(t, LANES - HEAD_DIM // 2, 1), pltpu.roll(t, HEAD_DIM // 2, 1))
    return t * cosf + partner * sins


def _inproj_body(x_ref, mod_ref, g_ref, cos_ref, sin_ref, w_ref, ret_ref, gdn_ref, ab_ref, abt_ref, na_ref, *, rope):
    h = _norm_mod(x_ref[0], g_ref[...], mod_ref[0, 0:1, :], mod_ref[0, 1:2, :]).astype(BF16)
    qk = _dot(h, w_ref[:, C_RET:C_RET + 2 * D_RET])
    slabs = []
    for s in range(2 * D_RET // LANES):
        t = qk[:, s * LANES:(s + 1) * LANES]
        if rope:
            t = _rope_slab(t, cos_ref[...], sin_ref[...])
        if s >= D_RET // LANES:
            t = t * HEAD_DIM ** -0.5
        slabs.append(t)
    ret_ref[0, :, 0:2 * D_RET] = jnp.concatenate(slabs, axis=1).astype(BF16)
    ret_ref[0, :, 2 * D_RET:] = _dot(h, w_ref[:, C_RET + 2 * D_RET:C_GDN]).astype(BF16)
    for j in range(2):
        gdn_ref[0, :, 512 * j:512 * (j + 1)] = _dot(h, w_ref[:, C_GDN + 512 * j:C_GDN + 512 * (j + 1)]).astype(BF16)
    ab = _dot(h, w_ref[:, C_AB:C_NA])
    ab_ref[0] = ab
    for grp in range(ab.shape[0] // LANES):
        t = ab[grp * LANES:(grp + 1) * LANES, :].T
        for p in range(D_GDN // PAIR):
            abt_ref[0, p, grp] = t[8 * p:8 * p + 8, :]
    for j in range(3):
        na_ref[0, :, 512 * j:512 * (j + 1)] = _dot(h, w_ref[:, C_NA + 512 * j:C_NA + 512 * (j + 1)]).astype(BF16)


def _in_projection(x, mod, g, cosf, sins, w, rope):
    b, l, d = x.shape
    tm = min(512, l)
    body = functools.partial(_inproj_body, rope=rope)
    return pl.pallas_call(
        body,
        grid=(b, l // tm),
        in_specs=[pl.BlockSpec((1, tm, d), lambda i, j: (i, j, 0)),
                  pl.BlockSpec((1, 6, d), lambda i, j: (i, 0, 0)),
                  pl.BlockSpec((1, d), lambda i, j: (0, 0)),
                  pl.BlockSpec((tm, LANES), lambda i, j: (j, 0)),
                  pl.BlockSpec((tm, LANES), lambda i, j: (j, 0)),
                  pl.BlockSpec((d, C_END), lambda i, j: (0, 0))],
        out_specs=[pl.BlockSpec((1, tm, 4 * D_RET), lambda i, j: (i, j, 0)),
                   pl.BlockSpec((1, tm, 4 * D_GDN), lambda i, j: (i, j, 0)),
                   pl.BlockSpec((1, tm, LANES), lambda i, j: (i, j, 0)),
                   pl.BlockSpec((1, D_GDN // PAIR, tm // LANES, 8, LANES), lambda i, j: (i, 0, j, 0, 0)),
                   pl.BlockSpec((1, tm, 3 * D_NA), lambda i, j: (i, j, 0))],
        out_shape=[jax.ShapeDtypeStruct((b, l, 4 * D_RET), BF16),
                   jax.ShapeDtypeStruct((b, l, 4 * D_GDN), BF16),
                   jax.ShapeDtypeStruct((b, l, LANES), F32),
                   jax.ShapeDtypeStruct((b, D_GDN // PAIR, l // LANES, 8, LANES), F32),
                   jax.ShapeDtypeStruct((b, l, 3 * D_NA), BF16)],
        compiler_params=_cparams(("arbitrary", "arbitrary"), VMEM_LIMIT),
        name="in_projection",
    )(x, mod, g, cosf, sins, w)


def _pack_w_in(w_in):
    d = w_in.shape[0]
    c1 = 4 * D_RET
    c2 = c1 + 4 * D_GDN
    q0 = c2 + 4 * H_GDN
    col = jnp.arange(w_in.shape[1])
    w_in = jnp.where((col >= q0) & (col < q0 + D_NA), w_in * NA_QSCALE, w_in).astype(BF16)
    ab = w_in[:, c2:c2 + 4 * H_GDN]
    ab = ab.reshape(d, 2, 2, H_GDN // 2, 2)
    ab = jnp.transpose(ab, (0, 3, 1, 2, 4)).reshape(d, 4 * H_GDN)
    ab = jnp.pad(ab, ((0, 0), (0, LANES - 4 * H_GDN)))
    return jnp.concatenate([w_in[:, :c2], ab, w_in[:, c2 + 4 * H_GDN:]], axis=1)


def _head_masks(shape):
    lane = lax.broadcasted_iota(jnp.int32, shape, len(shape) - 1)
    return lane < HEAD_DIM, lane >= HEAD_DIM


def _per_head(lo, hi, shape):
    m0, _ = _head_masks(shape)
    return jnp.where(m0, lo, hi)


def _head_sumsq(o):
    m0, m1 = _head_masks(o.shape)
    sq = o * o
    s0 = jnp.sum(jnp.where(m0, sq, 0.0), axis=-1, keepdims=True)
    s1 = jnp.sum(jnp.where(m1, sq, 0.0), axis=-1, keepdims=True)
    return jnp.where(m0, s0, s1)


def _ret_body(lg_ref, ql, kl, vl, gl, qc, kc, vc, gc, r_ref, rc_ref, sb_ref):
    c = RET_CHUNK
    npair = D_RET // PAIR
    lane_shape = (c, PAIR)
    pos = lax.broadcasted_iota(jnp.int32, lane_shape, 0).astype(F32)
    ii = lax.broadcasted_iota(jnp.int32, (c, c), 0)
    jj = lax.broadcasted_iota(jnp.int32, (c, c), 1)
    diff = (ii - jj).astype(F32)
    m0, m1 = _head_masks(lane_shape)
    masks = (m0, m1)
    bi = lax.broadcasted_iota(jnp.int32, (PAIR, PAIR), 0) // HEAD_DIM
    bj = lax.broadcasted_iota(jnp.int32, (PAIR, PAIR), 1) // HEAD_DIM
    bd = bi == bj
    cst = []
    for p in range(npair):
        lgf = _per_head(lg_ref[0, 2 * p], lg_ref[0, 2 * p + 1], lane_shape)
        lgb = _per_head(lg_ref[1, 2 * p], lg_ref[1, 2 * p + 1], lane_shape)
        dmats = [jnp.where(diff > 0, jnp.exp(lg_ref[0, 2 * p + hh] * diff),
                           jnp.where(diff < 0, jnp.exp(-lg_ref[1, 2 * p + hh] * diff), 2.0)) for hh in range(2)]
        cst.append(dict(qdf=jnp.exp(lgf * (pos + 1.0)), kdf=jnp.exp(lgf * (c - 1.0 - pos)),
                        qdb=jnp.exp(lgb * (c - pos)), kdb=jnp.exp(lgb * pos),
                        cdf=jnp.exp(lgf[0:1] * c), cdb=jnp.exp(lgb[0:1] * c), dmats=dmats))

    def sweep(q_ref, k_ref, v_ref, g_ref, o_ref, n, sf0, sb0):
        def bstep(t, sbs):
            ci = n - 1 - t
            sl = pl.ds(pl.multiple_of(ci * c, c), c)
            out = []
            for p in range(npair):
                lanes = slice(p * PAIR, (p + 1) * PAIR)
                sb_ref[p, ci] = sbs[p]
                kd = (k_ref[0, sl, lanes].astype(F32) * cst[p]["kdb"]).astype(BF16)
                out.append(sbs[p] * cst[p]["cdb"] + jnp.where(bd, _dot_tn(kd, v_ref[0, sl, lanes]), 0.0))
            return tuple(out)

        sb_fin = lax.fori_loop(0, n, bstep, sb0)

        def fstep(ci, sfs):
            sl = pl.ds(pl.multiple_of(ci * c, c), c)
            out = []
            for p in range(npair):
                lanes = slice(p * PAIR, (p + 1) * PAIR)
                k_ = cst[p]
                q = q_ref[0, sl, lanes]
                k = k_ref[0, sl, lanes]
                v = v_ref[0, sl, lanes]
                qf = q.astype(F32)
                o = (_dot((qf * k_["qdf"]).astype(BF16), sfs[p].astype(BF16))
                     + _dot((qf * k_["qdb"]).astype(BF16), sb_ref[p, ci].astype(BF16)))
                for hh in range(2):
                    qm = jnp.where(masks[hh], q, jnp.zeros_like(q))
                    pm = (_dot_nt(qm, k) * k_["dmats"][hh]).astype(BF16)
                    o = o + jnp.where(masks[hh], _dot(pm, v), 0.0)
                on = o * lax.rsqrt(_head_sumsq(o) * (1.0 / HEAD_DIM) + EPS)
                o_ref[0, sl, lanes] = (on * _silu(g_ref[0, sl, lanes].astype(F32))).astype(o_ref.dtype)
                kd = (k.astype(F32) * k_["kdf"]).astype(BF16)
                out.append(sfs[p] * k_["cdf"] + jnp.where(bd, _dot_tn(kd, v), 0.0))
            return tuple(out)

        sf_fin = lax.fori_loop(0, n, fstep, sf0)
        return sf_fin, sb_fin

    z = tuple(jnp.zeros((PAIR, PAIR), F32) for _ in range(npair))
    sfc, sbc = sweep(qc, kc, vc, gc, rc_ref, qc.shape[1] // c, z, z)
    sweep(ql, kl, vl, gl, r_ref, ql.shape[1] // c, sfc, sbc)


def _retention(ret_l, ret_c, log_gamma):
    b, l, _ = ret_l.shape
    lc = ret_c.shape[1]
    npair = D_RET // PAIR

    def col(k):
        return lambda i: (i, 0, k)

    lat = [pl.BlockSpec((1, l, D_RET), col(k)) for k in range(4)]
    ctx = [pl.BlockSpec((1, lc, D_RET), col(k)) for k in range(4)]
    return pl.pallas_call(
        _ret_body,
        grid=(b,),
        in_specs=[pl.BlockSpec(memory_space=pltpu.SMEM)] + lat + ctx,
        out_specs=[pl.BlockSpec((1, l, D_RET), lambda i: (i, 0, 0)),
                   pl.BlockSpec((1, lc, D_RET), lambda i: (i, 0, 0))],
        out_shape=[jax.ShapeDtypeStruct((b, l, D_RET), BF16), jax.ShapeDtypeStruct((b, lc, D_RET), BF16)],
        scratch_shapes=[pltpu.VMEM((npair, max(l, lc) // RET_CHUNK, PAIR, PAIR), F32)],
        compiler_params=_cparams(("arbitrary",), VMEM_LIMIT),
        name="retention",
    )(log_gamma, ret_l, ret_l, ret_l, ret_l, ret_c, ret_c, ret_c, ret_c)


def _stack_heads(t):
    m0, m1 = _head_masks(t.shape)
    z = jnp.zeros_like(t)
    return jnp.concatenate([jnp.where(m0, t, z), jnp.where(m1, t, z)], axis=0)


def _scan_sum(x, axis, n, reverse):
    size = x.shape[axis]
    pos = lax.broadcasted_iota(jnp.int32, x.shape, axis)
    s = 1
    while s < n:
        if reverse:
            x = x + jnp.where(pos + s < n, pltpu.roll(x, size - s, axis), 0.0)
        else:
            x = x + jnp.where(pos >= s, pltpu.roll(x, s, axis), 0.0)
        s *= 2
    return x


def _gdn_prep(src_ref, col, cw, dst_ref, r0, *, l2, scale):
    n = src_ref.shape[1]
    blk = min(GDN_PREP_ROWS, n)
    halo = 16
    lanes = slice(col % D_GDN, col % D_GDN + LANES)

    def block(i, _):
        b0 = pl.multiple_of(i * blk, blk)
        x = src_ref[0, pl.ds(b0, blk), col:col + LANES].astype(F32)
        lo = src_ref[0, pl.ds(pl.multiple_of(jnp.maximum(b0 - halo, 0), halo), halo), col:col + LANES].astype(F32)
        hi = src_ref[0, pl.ds(pl.multiple_of(jnp.minimum(b0 + blk, n - halo), halo), halo), col:col + LANES].astype(F32)
        lo = jnp.where(b0 > 0, lo, 0.0)
        hi = jnp.where(b0 + blk < n, hi, 0.0)
        ext = jnp.concatenate([lo, x, hi], axis=0)
        acc = x * cw[SHORT_CONV // 2:SHORT_CONV // 2 + 1, :]
        for j in range(SHORT_CONV):
            s = j - SHORT_CONV // 2
            if s != 0:
                sh = pltpu.roll(ext, (-s) % (blk + 2 * halo), 0)
                acc = acc + sh[halo:halo + blk] * cw[j:j + 1, :]
        y = _silu(acc)
        if l2:
            y = y * lax.rsqrt(_head_sumsq(y) + EPS)
        if scale != 1.0:
            y = y * scale
        dst_ref[pl.ds(pl.multiple_of(r0 + b0, halo), blk), lanes] = y.astype(dst_ref.dtype)
        return 0

    lax.fori_loop(0, n // blk, block, 0)


def _gdn_body(x_ref, xc_ref, ab_ref, abt_ref, cw_ref, cst_ref, cstt_ref, ng_ref,
              o_ref, oc_ref, qn, kn, vn, oacc, st_s, x_s, t_s, rhs_s, u_s, w_s, qa_s, kd_s, egl_s):
    c = GDN_CHUNK
    c2 = 2 * c
    grp = GDN_GROUP
    npair = D_GDN // PAIR
    lc = xc_ref.shape[1]
    l = x_ref.shape[1]
    nc = lc // c
    nt = (lc + l) // c
    ii = lax.broadcasted_iota(jnp.int32, (c2, c2), 0)
    jj = lax.broadcasted_iota(jnp.int32, (c2, c2), 1)
    same = (ii // c) == (jj // c)
    eye = (ii == jj).astype(F32)
    dirs = ((same & (ii >= jj), same & (ii > jj), c - 1),
            (same & (ii <= jj), same & (ii < jj), 0))

    def chains(g, slot):
        out = []
        for gi in range(grp):
            t = g * grp + gi
            cf = t
            cb = jnp.where(t < nc, nc - 1 - t, nt + nc - 1 - t)
            for p in range(npair):
                for d in range(2):
                    out.append((((slot * grp + gi) * 2 + d) * npair + p, p, d, cf if d == 0 else cb))
        return out

    def stage_inputs(idx, p, d, ci):
        incl, strict, last = dirs[d]
        lanes = slice(p * PAIR, (p + 1) * PAIR)
        nea, dtb = cst_ref[0:1, :], cst_ref[1:2, :]
        neat, dtbt = cstt_ref[p, 0], cstt_ref[p, 1]
        sl = pl.ds(pl.multiple_of(ci * c, c), c)
        abv = ab_ref[0, sl, :]
        gall = nea * jax.nn.softplus(abv + dtb)
        ball = jax.nn.sigmoid(abv)
        shape = (c, PAIR)
        c0 = 8 * p + 2 * d
        gl = _per_head(gall[:, c0:c0 + 1], gall[:, c0 + 1:c0 + 2], shape)
        bl = _per_head(ball[:, c0 + 4:c0 + 5], ball[:, c0 + 5:c0 + 6], shape)
        gcum = _scan_sum(gl, 0, c, d == 1)
        abt = abt_ref[0, p, ci // 2]
        abt = jnp.where(ci % 2 == 1, pltpu.roll(abt, c, 1), abt)
        gt = neat * jax.nn.softplus(abt + dtbt)
        gtc = _scan_sum(gt, 1, c, d == 1)
        grow = jnp.concatenate([gtc[2 * d:2 * d + 1, 0:c], gtc[2 * d + 1:2 * d + 2, 0:c]], axis=1)
        gcol = jnp.concatenate([gcum[:, 0:1], gcum[:, HEAD_DIM:HEAD_DIM + 1]], axis=0)
        dec = jnp.where(incl, jnp.exp(jnp.where(incl, gcol - grow, 0.0)), 0.0)
        dec = jnp.where(ii == jj, 1.0, dec)
        q = qn[sl, lanes]
        k = kn[sl, lanes]
        v = vn[sl, lanes]
        kf = k.astype(F32)
        eg = jnp.exp(gcum)
        kb = kf * bl
        k_st = _stack_heads(k)
        a = jnp.where(strict, _dot_nt(_stack_heads(kb.astype(BF16)), k_st) * dec, 0.0)
        attn = (_dot_nt(_stack_heads(q), k_st) * dec).astype(BF16)
        x_s[idx] = (-a).astype(BF16)
        t_s[idx] = eye - a
        rhs_s[idx] = jnp.concatenate([_stack_heads((v.astype(F32) * bl).astype(BF16)),
                                      _stack_heads((kb * eg).astype(BF16))], axis=1)
        qa_s[idx] = jnp.concatenate([_stack_heads((q.astype(F32) * eg).astype(BF16)), attn], axis=1)
        glast = gcum[last:last + 1, :]
        kd_s[idx] = _stack_heads(kf * jnp.exp(glast - gcum)).T.astype(BF16)
        egl_s[idx] = jnp.broadcast_to(jnp.exp(glast), (8, PAIR))

    def stage_group(g, slot):
        todo = chains(g, slot)
        for idx, p, d, ci in todo:
            stage_inputs(idx, p, d, ci)
        for _ in range(5):
            for idx, _, _, _ in todo:
                xb = x_s[idx]
                x_s[idx] = _dot(xb, xb).astype(BF16)
            for idx, _, _, _ in todo:
                t = t_s[idx]
                t_s[idx] = t + _dot(t.astype(BF16), x_s[idx])
        for idx, _, _, _ in todo:
            sol = _dot(t_s[idx].astype(BF16), rhs_s[idx])
            u_s[idx] = sol[:, :PAIR]
            w_s[idx] = sol[:, PAIR:].astype(BF16)

    def recur_group(g, slot):
        sts = {(p, d): st_s[d * npair + p] for p in range(npair) for d in range(2)}
        for idx, p, d, ci in chains(g, slot):
            sl = pl.ds(pl.multiple_of(ci * c, c), c)
            st = sts[(p, d)]
            stb = st.astype(BF16)
            v_new = (u_s[idx] - _dot(w_s[idx], stb)).astype(BF16)
            o_st = _dot(qa_s[idx], jnp.concatenate([stb, v_new], axis=0))
            oacc[sl, p * PAIR:(p + 1) * PAIR] += o_st[:c] + o_st[c:]
            sts[(p, d)] = st * egl_s[idx][0:1, :] + _dot(kd_s[idx], v_new)
        for (p, d), st in sts.items():
            st_s[d * npair + p] = st

    def prep(src_ref, r0):
        for s in range(D_GDN // LANES):
            for k, (dst, l2, scale) in enumerate(((qn, True, HEAD_DIM ** -0.5), (kn, True, 1.0), (vn, False, 1.0))):
                col = k * D_GDN + s * LANES
                _gdn_prep(src_ref, col, cw_ref[:, col:col + LANES], dst, r0, l2=l2, scale=scale)

    def finish(src_ref, out_ref, r0, n):
        for p in range(npair):
            lanes = slice(p * PAIR, (p + 1) * PAIR)
            o = oacc[r0:r0 + n, lanes]
            on = o * lax.rsqrt(_head_sumsq(o) * (1.0 / HEAD_DIM) + EPS)
            gate = src_ref[0, :, 3 * D_GDN + p * PAIR:3 * D_GDN + (p + 1) * PAIR].astype(F32)
            out_ref[0, :, lanes] = (on * ng_ref[...] * _silu(gate)).astype(out_ref.dtype)

    st_s[...] = jnp.zeros_like(st_s)
    oacc[...] = jnp.zeros_like(oacc)
    prep(xc_ref, 0)
    prep(x_ref, lc)

    ng = nt // grp
    stage_group(0, 0)

    def two_groups(kk, _):
        g = 2 * kk
        recur_group(g, 0)
        stage_group(g + 1, 1)
        recur_group(g + 1, 1)
        stage_group(g + 2, 0)
        return 0

    lax.fori_loop(0, (ng - 1) // 2, two_groups, 0)
    if ng % 2 == 0:
        recur_group(ng - 2, 0)
        stage_group(ng - 1, 1)
        recur_group(ng - 1, 1)
    else:
        recur_group(ng - 1, 0)
    finish(xc_ref, oc_ref, 0, lc)
    finish(x_ref, o_ref, lc, l)


def _gdn_tables(conv_w, a_log, dt_bias, norm_g):
    npair = D_GDN // PAIR
    nea = -jnp.exp(a_log.astype(F32)).reshape(2, npair, 2)
    dtb = dt_bias.astype(F32).reshape(2, npair, 2)
    rows = jnp.stack([jnp.transpose(nea, (1, 0, 2)).reshape(npair, 4), jnp.transpose(dtb, (1, 0, 2)).reshape(npair, 4)], axis=1)
    cst = jnp.pad(jnp.transpose(jnp.pad(rows, ((0, 0), (0, 0), (0, 4))), (1, 0, 2)).reshape(2, 8 * npair),
                  ((0, 6), (0, LANES - 8 * npair)))
    cstt = jnp.broadcast_to(jnp.pad(rows, ((0, 0), (0, 0), (0, 4)))[..., None], (npair, 2, 8, LANES))
    cw = jnp.pad(conv_w.astype(F32), ((0, 8 - SHORT_CONV), (0, 0)))
    ng = jnp.tile(norm_g.astype(F32), 2).reshape(1, PAIR)
    return cw, cst, cstt, ng


def _gdn(gdn_l, ab_l, abt_l, gdn_c, ab_c, abt_c, tables):
    b, l, _ = gdn_l.shape
    lc = gdn_c.shape[1]
    npair = D_GDN // PAIR
    lt = l + lc
    assert (lt // GDN_CHUNK) % GDN_GROUP == 0 and lc % LANES == 0
    cw, cst, cstt, ng = tables

    nstage = 2 * GDN_GROUP * 2 * npair

    def whole(shape):
        nd = len(shape)
        return pl.BlockSpec(shape, lambda i: (0,) * nd)

    def per_batch(shape, single=False):
        nd = len(shape)
        return pl.BlockSpec((1,) + shape, lambda i: (i,) + (0,) * nd, pipeline_mode=pl.Buffered(1) if single else None)

    sq = (nstage, PAIR, PAIR)
    return pl.pallas_call(
        _gdn_body,
        grid=(b,),
        in_specs=[per_batch((l, 4 * D_GDN), True), per_batch((lc, 4 * D_GDN)), per_batch((lt, LANES), True),
                  per_batch((npair, lt // LANES, 8, LANES)),
                  whole((8, 3 * D_GDN)), whole((8, LANES)), whole((npair, 2, 8, LANES)), whole((1, PAIR))],
        out_specs=[per_batch((l, D_GDN)), per_batch((lc, D_GDN))],
        out_shape=[jax.ShapeDtypeStruct((b, l, D_GDN), BF16), jax.ShapeDtypeStruct((b, lc, D_GDN), BF16)],
        scratch_shapes=[pltpu.VMEM((lt, D_GDN), BF16)] * 3 + [pltpu.VMEM((lt, D_GDN), F32)]
        + [pltpu.VMEM((2 * npair, PAIR, PAIR), F32), pltpu.VMEM(sq, BF16), pltpu.VMEM(sq, F32),
           pltpu.VMEM((nstage, PAIR, 2 * PAIR), BF16), pltpu.VMEM(sq, F32), pltpu.VMEM(sq, BF16),
           pltpu.VMEM((nstage, PAIR, 2 * PAIR), BF16), pltpu.VMEM(sq, BF16), pltpu.VMEM((nstage, 8, PAIR), F32)],
        compiler_params=_cparams(("arbitrary",), VMEM_LIMIT),
        name="gated_deltanet",
    )(gdn_l, gdn_c, jnp.concatenate([ab_c, ab_l], axis=1), jnp.concatenate([abt_c, abt_l], axis=2), cw, cst, cstt, ng)


def _na_bias_tiles(rpb, rows):
    w = GRID_W
    ext = jnp.pad(rpb.astype(F32) * float(np.log2(np.e)), ((0, 0), (0, 0), (w - NA_KW, w - NA_KW)))
    cq = np.arange(w)[:, None]
    ck = np.arange(w)[None, :]
    tb = ext[:, :, jnp.asarray(ck - cq + w - 1)]
    ws = np.clip(cq - NA_KW // 2, 0, w - NA_KW)
    col_ok = (ck >= ws) & (ck < ws + NA_KW)
    tb = jnp.where(jnp.asarray(col_ok)[None, None], tb, NEG_INF)
    none = 2 * NA_KH - 1
    tb = jnp.concatenate([tb, jnp.full((tb.shape[0], 1, w, w), NEG_INF, F32)], axis=1)
    nt = rows // NA_QROWS
    idx = np.zeros((3, NA_QROWS, NA_KROWS), np.int32)
    for cls, t in enumerate((0, 1, nt - 1)):
        ks = int(np.clip(NA_QROWS * t - NA_KH // 2, 0, rows - NA_KROWS))
        for rl in range(NA_QROWS):
            r = NA_QROWS * t + rl
            r0 = int(np.clip(r - NA_KH // 2, 0, rows - NA_KH))
            for j in range(NA_KROWS):
                kr = ks + j
                idx[cls, rl, j] = kr - r + NA_KH - 1 if r0 <= kr < r0 + NA_KH else none
    pairs = idx.reshape(3, NA_QROWS, NA_KROWS // 2, 2)
    uniq, inv = np.unique(pairs.reshape(-1, 2), axis=0, return_inverse=True)
    wide = jnp.concatenate([tb[:, jnp.asarray(uniq[:, 0])], tb[:, jnp.asarray(uniq[:, 1])]], axis=-1)
    tiles = wide[:, jnp.asarray(inv.reshape(3, NA_QROWS, NA_KROWS // 2))]
    tiles = jnp.transpose(tiles, (1, 0, 3, 2, 4, 5))
    return tiles.reshape(3, tb.shape[0], NA_KROWS // 2, NA_QROWS * w, 2 * w)


def _softmax_pv(s_parts, v_parts):
    m = functools.reduce(jnp.maximum, [jnp.max(s, axis=-1, keepdims=True) for s in s_parts])
    ps = [jnp.exp2(s - m) for s in s_parts]
    den = functools.reduce(lambda x, y: x + y, [jnp.sum(p, axis=-1, keepdims=True) for p in ps])
    o = functools.reduce(lambda x, y: x + y, [_dot(p.astype(BF16), v) for p, v in zip(ps, v_parts)])
    return o / den


def _na_body(q_ref, k_ref, v_ref, qc_ref, kc_ref, vc_ref, bias_ref, o_ref, oc_ref, *, rows):
    w = GRID_W
    tq = NA_QROWS * w
    tk = NA_KROWS * w
    nt = rows // NA_QROWS
    m0, m1 = _head_masks((tq, PAIR))
    masks = (m0, m1)

    def tile(t, _):
        cls = jnp.where(t > 0, 1, 0) + jnp.where(t == nt - 1, 1, 0)
        ks = jnp.clip(NA_QROWS * t - NA_KH // 2, 0, rows - NA_KROWS) * w
        ksl = pl.ds(pl.multiple_of(ks, w), tk)
        qsl = pl.ds(pl.multiple_of(t * tq, tq), tq)
        for p in range(NA_PAIRS):
            lanes = slice(p * PAIR, (p + 1) * PAIR)
            q = q_ref[0, qsl, lanes]
            kb = k_ref[0, ksl, lanes]
            vb = v_ref[0, ksl, lanes]
            kc = kc_ref[0, :, lanes]
            vc = vc_ref[0, :, lanes]
            acc = jnp.zeros((tq, PAIR), F32)
            for hh in range(2):
                qm = jnp.where(masks[hh], q, jnp.zeros_like(q))
                bias = jnp.concatenate([bias_ref[0, cls, 2 * p + hh, s] for s in range(NA_KROWS // 2)], axis=1)
                s_loc = _dot_nt(qm, kb) + bias
                s_ctx = _dot_nt(qm, kc)
                acc = acc + jnp.where(masks[hh], _softmax_pv([s_loc, s_ctx], [vb, vc]), 0.0)
            o_ref[0, qsl, lanes] = acc.astype(o_ref.dtype)
        return 0

    lax.fori_loop(0, nt, tile, 0)
    for p in range(NA_PAIRS):
        lanes = slice(p * PAIR, (p + 1) * PAIR)
        qc = qc_ref[0, :, lanes]
        kc = kc_ref[0, :, lanes]
        vc = vc_ref[0, :, lanes]
        mc0, mc1 = _head_masks(qc.shape)
        accc = jnp.zeros(qc.shape, F32)
        for mk in (mc0, mc1):
            qm = jnp.where(mk, qc, jnp.zeros_like(qc))
            accc = accc + jnp.where(mk, _softmax_pv([_dot_nt(qm, kc)], [vc]), 0.0)
        oc_ref[0, :, lanes] = accc.astype(oc_ref.dtype)


def _neighbourhood_attention(na_l, na_c, bias, layer):
    b, l, _ = na_l.shape
    lc = na_c.shape[1]
    rows = l // GRID_W
    wide = NA_PAIRS * PAIR
    ngrp = D_NA // wide

    def col(k):
        return lambda p, i: (i, 0, k * ngrp + p)

    lat = [pl.BlockSpec((1, l, wide), col(k)) for k in range(3)]
    ctx = [pl.BlockSpec((1, lc, wide), col(k)) for k in range(3)]
    return pl.pallas_call(
        functools.partial(_na_body, rows=rows),
        grid=(ngrp, b),
        in_specs=lat + ctx + [pl.BlockSpec((1, 3, 2 * NA_PAIRS, NA_KROWS // 2, NA_QROWS * GRID_W, 2 * GRID_W),
                                           lambda p, i: (layer, 0, p, 0, 0, 0))],
        out_specs=[pl.BlockSpec((1, l, wide), lambda p, i: (i, 0, p)),
                   pl.BlockSpec((1, lc, wide), lambda p, i: (i, 0, p))],
        out_shape=[jax.ShapeDtypeStruct((b, l, D_NA), BF16), jax.ShapeDtypeStruct((b, lc, D_NA), BF16)],
        compiler_params=_cparams(("arbitrary", "arbitrary"), VMEM_LIMIT),
        name="neighbourhood_attention",
    )(na_l, na_l, na_l, na_c, na_c, na_c, bias)


def _outproj_residual(r_ref, g_ref, n_ref, x_ref, mod_ref, wo_ref):
    mix = jnp.concatenate([r_ref[0], g_ref[0], n_ref[0]], axis=1)
    return x_ref[0] + mod_ref[0, 2:3, :] * _dot(mix, wo_ref[...])


def _swiglu_chunk(hb, w1_ref, w3_ref, w2_ref, j):
    cols = slice(j * FF_CHUNK, (j + 1) * FF_CHUNK)
    t = (_silu(_dot(hb, w1_ref[:, cols])) * _dot(hb, w3_ref[:, cols])).astype(BF16)
    return _dot(t, w2_ref[cols, :])


def _swiglu_chunks(hb, w1_ref, w3_ref, w2_ref, acc_ref):
    for j in range(w2_ref.shape[0] // FF_CHUNK):
        if j == 0:
            acc_ref[...] = _swiglu_chunk(hb, w1_ref, w3_ref, w2_ref, j)
        else:
            acc_ref[...] += _swiglu_chunk(hb, w1_ref, w3_ref, w2_ref, j)


def _dense_body(r_ref, g_ref, n_ref, x_ref, mod_ref, g2_ref, wo_ref, w1_ref, w3_ref, w2_ref, o_ref, acc_ref):
    x1 = _outproj_residual(r_ref, g_ref, n_ref, x_ref, mod_ref, wo_ref)
    hb = _norm_mod(x1, g2_ref[...], mod_ref[0, 3:4, :], mod_ref[0, 4:5, :]).astype(BF16)
    _swiglu_chunks(hb, w1_ref, w3_ref, w2_ref, acc_ref)
    o_ref[0] = x1 + mod_ref[0, 5:6, :] * acc_ref[...]


def _pack_bf16_pairs(h):
    m = h.shape[1] // 2
    bits = lax.bitcast_convert_type(h.astype(BF16).astype(F32), jnp.uint32)
    return (bits[:, :m] >> 16) | (bits[:, m:] & jnp.uint32(0xFFFF0000))


def _unpack_bf16_pairs(u):
    lo = lax.bitcast_convert_type(u << 16, F32)
    hi = lax.bitcast_convert_type(u & jnp.uint32(0xFFFF0000), F32)
    return jnp.concatenate([lo, hi], axis=1).astype(BF16)


def _router_body(r_ref, g_ref, n_ref, x_ref, mod_ref, g2_ref, wo_ref, wr_ref, x1_ref, h_ref, lg_ref):
    x1 = _outproj_residual(r_ref, g_ref, n_ref, x_ref, mod_ref, wo_ref)
    h = _norm_mod(x1, g2_ref[...], mod_ref[0, 3:4, :], mod_ref[0, 4:5, :])
    x1_ref[0] = x1
    h_ref[0] = _pack_bf16_pairs(h)
    lg_ref[0] = _dot_3pass(h, wr_ref[...])


def _resident(shape):
    nd = len(shape)
    return pl.BlockSpec(shape, lambda i, j: (0,) * nd, pipeline_mode=pl.Buffered(1))


def _mixer_specs(tm, d):
    return [pl.BlockSpec((1, tm, D_RET), lambda i, j: (i, j, 0)),
            pl.BlockSpec((1, tm, D_GDN), lambda i, j: (i, j, 0)),
            pl.BlockSpec((1, tm, D_NA), lambda i, j: (i, j, 0)),
            pl.BlockSpec((1, tm, d), lambda i, j: (i, j, 0)),
            pl.BlockSpec((1, 6, d), lambda i, j: (i, 0, 0)),
            pl.BlockSpec((1, d), lambda i, j: (0, 0))]


def _dense_block(r, g, n, x, mod, g2, wo, w1, w3, w2):
    b, l, d = x.shape
    tm = min(512, l)
    dff = w1.shape[1]
    return pl.pallas_call(
        _dense_body,
        grid=(b, l // tm),
        in_specs=_mixer_specs(tm, d) + [_resident((d, d)), _resident((d, dff)), _resident((d, dff)), _resident((dff, d))],
        out_specs=pl.BlockSpec((1, tm, d), lambda i, j: (i, j, 0)),
        out_shape=jax.ShapeDtypeStruct((b, l, d), F32),
        scratch_shapes=[pltpu.VMEM((tm, d), F32)],
        compiler_params=_cparams(("arbitrary", "arbitrary"), VMEM_LIMIT),
        name="outproj_swiglu",
    )(r, g, n, x, mod, g2, wo, w1, w3, w2)


def _router_block(r, g, n, x, mod, g2, wo, wr):
    b, l, d = x.shape
    tm = min(512, l)
    blk = pl.BlockSpec((1, tm, d), lambda i, j: (i, j, 0))
    return pl.pallas_call(
        _router_body,
        grid=(b, l // tm),
        in_specs=_mixer_specs(tm, d) + [_resident((d, d)), _resident((d, LANES))],
        out_specs=[blk, pl.BlockSpec((1, tm, d // 2), lambda i, j: (i, j, 0)), pl.BlockSpec((1, tm, LANES), lambda i, j: (i, j, 0))],
        out_shape=[jax.ShapeDtypeStruct((b, l, d), F32), jax.ShapeDtypeStruct((b, l, d // 2), jnp.uint32),
                   jax.ShapeDtypeStruct((b, l, LANES), F32)],
        compiler_params=_cparams(("arbitrary", "arbitrary"), VMEM_LIMIT),
        name="outproj_router",
    )(r, g, n, x, mod, g2, wo, wr)


def _row_copy_in(idx_ref, r, src_hbm, dst_ref, sem):
    return pltpu.make_async_copy(src_hbm.at[pl.ds(idx_ref[0, 0, r], 1), :], dst_ref.at[pl.ds(r, 1), :], sem)


def _row_copy_out(idx_ref, r, src_ref, dst_hbm, sem):
    return pltpu.make_async_copy(src_ref.at[pl.ds(r, 1), :], dst_hbm.at[pl.ds(idx_ref[0, 0, r], 1), :], sem)


def _moe_body(bval_ref, bexp_ref, tok_ref, tokn_ref, slotp_ref, h_hbm, w1_ref, w3_ref, w2_ref, y_hbm,
              xbuf, ybuf, gsem, ssem, *, n_asg):
    i = pl.program_id(0)
    tb = xbuf.shape[1]
    nch = w2_ref.shape[1] // FF_CHUNK
    cur = i % 2
    per = -(-tb // nch)
    valid = bval_ref[i] == 1
    prev_valid = bval_ref[jnp.maximum(i - 1, 0)] == 1

    def wait_gather(slot):
        pltpu.make_async_copy(h_hbm.at[pl.ds(0, tb), :], xbuf.at[slot], gsem).wait()

    def wait_scatter(slot):
        pltpu.make_async_copy(ybuf.at[slot], y_hbm.at[pl.ds(0, tb), :], ssem).wait()

    @pl.when(i == 0)
    def _():
        ybuf[1] = jnp.zeros((tb, ybuf.shape[2]), F32)
        fills = [pltpu.make_async_copy(ybuf.at[1], y_hbm.at[pl.ds(r0, tb), :], ssem)
                 for r0 in range(n_asg, y_hbm.shape[0], tb)]
        for f in fills:
            f.start()
        for f in fills:
            f.wait()

        def issue(r, _):
            _row_copy_in(tok_ref, r, h_hbm, xbuf.at[0], gsem).start()
            return 0

        lax.fori_loop(0, tb, issue, 0)

    @pl.when(valid)
    def _():
        wait_gather(cur)
        hb = _unpack_bf16_pairs(xbuf[cur])
        acc = ybuf.at[cur]
        prv = ybuf.at[1 - cur]
        nxt = xbuf.at[1 - cur]
        for j in range(nch):
            for r in range(j * per, min(tb, (j + 1) * per)):
                _row_copy_in(tokn_ref, r, h_hbm, nxt, gsem).start()
                _row_copy_out(slotp_ref, r, prv, y_hbm, ssem).start()
            part = _swiglu_chunk(hb, w1_ref.at[0], w3_ref.at[0], w2_ref.at[0], j)
            if j == 0:
                acc[...] = part
            else:
                acc[...] += part
        wait_scatter(1 - cur)

    @pl.when(jnp.logical_not(valid) & prev_valid & (i > 0))
    def _():
        wait_gather(cur)

        def issue(r, _):
            _row_copy_out(slotp_ref, r, ybuf.at[1 - cur], y_hbm, ssem).start()
            return 0

        lax.fori_loop(0, tb, issue, 0)
        wait_scatter(1 - cur)


def _moe_experts(h_rows, row_tok, row_slot, block_expert, block_valid, w1, w3, w2, n_out_rows, n_spare):
    d = w2.shape[-1]
    dff = w2.shape[1]
    nb = block_expert.shape[0]
    tb = MOE_ROWS
    smem = functools.partial(pl.BlockSpec, (1, 1, tb), memory_space=pltpu.SMEM)
    grid_spec = pltpu.PrefetchScalarGridSpec(
        num_scalar_prefetch=2,
        grid=(nb,),
        in_specs=[smem(lambda i, bv, be: (i, 0, 0)),
                  smem(lambda i, bv, be: (jnp.minimum(i + 1, nb - 1), 0, 0)),
                  smem(lambda i, bv, be: (i, 0, 0)),
                  pl.BlockSpec(memory_space=pl.ANY),
                  pl.BlockSpec((1, d, dff), lambda i, bv, be: (be[i], 0, 0)),
                  pl.BlockSpec((1, d, dff), lambda i, bv, be: (be[i], 0, 0)),
                  pl.BlockSpec((1, dff, d), lambda i, bv, be: (be[i], 0, 0))],
        out_specs=pl.BlockSpec(memory_space=pl.ANY),
        scratch_shapes=[pltpu.VMEM((2, tb, d // 2), jnp.uint32), pltpu.VMEM((2, tb, d), F32),
                        pltpu.SemaphoreType.DMA(()), pltpu.SemaphoreType.DMA(())],
    )
    return pl.pallas_call(
        functools.partial(_moe_body, n_asg=n_out_rows - n_spare),
        grid_spec=grid_spec,
        out_shape=jax.ShapeDtypeStruct((n_out_rows, d), F32),
        compiler_params=_cparams(("arbitrary",), VMEM_LIMIT),
        name="moe_experts",
    )(block_valid, block_expert, row_tok.reshape(nb, 1, tb), row_tok.reshape(nb, 1, tb), row_slot.reshape(nb + 1, 1, tb),
      h_rows, w1, w3, w2)


def _combine_body(y0_ref, y1_ref, x1_ref, gate_ref, gm_ref, fg_ref, o_ref, *, final):
    moe = gate_ref[:, 0:1] * y0_ref[...] + gate_ref[:, 1:2] * y1_ref[...]
    x2 = x1_ref[...] + gm_ref[0] * moe
    if final:
        ms = jnp.mean(x2 * x2, axis=-1, keepdims=True)
        x2 = x2 * lax.rsqrt(ms + EPS) * fg_ref[...]
    o_ref[...] = x2


def _moe_combine(y_rows, x1, gates, gate_mlp, final_g, final):
    b, l, d = x1.shape
    n_tok = b * l
    tm = 256
    per_b = l // tm
    out = pl.pallas_call(
        functools.partial(_combine_body, final=final),
        grid=(n_tok // tm,),
        in_specs=[pl.BlockSpec((tm, d), lambda i: (i, 0)),
                  pl.BlockSpec((tm, d), lambda i: (n_tok // tm + i, 0)),
                  pl.BlockSpec((tm, d), lambda i: (i, 0)),
                  pl.BlockSpec((tm, 2), lambda i: (i, 0)),
                  pl.BlockSpec((1, 1, d), lambda i: (i // per_b, 0, 0)),
                  pl.BlockSpec((1, d), lambda i: (0, 0))],
        out_specs=pl.BlockSpec((tm, d), lambda i: (i, 0)),
        out_shape=jax.ShapeDtypeStruct((n_tok, d), F32),
        compiler_params=_cparams(("arbitrary",), VMEM_LIMIT),
        name="moe_combine_final_norm",
    )(y_rows, y_rows, x1.reshape(n_tok, d), gates, gate_mlp, final_g.reshape(1, d))
    return out.reshape(b, l, d)


def _route(logits):
    n_tok = logits.shape[0]
    tb = MOE_ROWS
    n_asg = n_tok * TOP_K
    top_logit, top_e = lax.top_k(logits[:, :N_EXPERTS], TOP_K)
    gates = jax.nn.softmax(top_logit, axis=-1)
    e_flat = top_e.reshape(-1).astype(jnp.int32)
    asg = jnp.arange(n_asg, dtype=jnp.int32)
    by_expert = lax.sort(e_flat * n_asg + asg)
    counts = jnp.sum((e_flat[:, None] == jnp.arange(N_EXPERTS, dtype=jnp.int32)[None, :]).astype(jnp.int32), axis=0)
    starts = jnp.cumsum(counts) - counts
    padded = (counts + tb - 1) // tb * tb
    pad_ends = jnp.cumsum(padded)
    pad_starts = pad_ends - padded
    nb = (n_asg + tb - 1) // tb + N_EXPERTS + 1
    n_rows = nb * tb
    block_start = jnp.arange(nb, dtype=jnp.int32) * tb
    block_expert = jnp.minimum(jnp.sum((block_start[:, None] >= pad_ends[None, :]).astype(jnp.int32), axis=1), N_EXPERTS - 1)
    block_valid = (block_start < pad_ends[-1]).astype(jnp.int32)
    row = jnp.arange(n_rows, dtype=jnp.int32)
    row_e = jnp.repeat(block_expert, tb)
    off = row - pad_starts[row_e]
    is_pad = (off >= counts[row_e]) | (jnp.repeat(block_valid, tb) == 0)
    src = by_expert[jnp.clip(starts[row_e] + off, 0, n_asg - 1)] - row_e * n_asg
    row_asg = jnp.where(is_pad, -1, src)
    spare = tb + n_asg + jnp.cumsum(is_pad.astype(jnp.int32)) - 1
    row_slot = jnp.where(is_pad, spare, (row_asg % TOP_K) * n_tok + row_asg // TOP_K)
    row_tok = jnp.where(is_pad, 0, row_asg // TOP_K)
    row_slot = jnp.concatenate([n_asg + jnp.arange(tb, dtype=jnp.int32), row_slot])
    last_e = block_expert[jnp.maximum(pad_ends[-1] // tb - 1, 0)]
    block_expert = jnp.where(block_valid == 1, block_expert, last_e)
    n_spare = (tb + n_rows - n_asg + tb - 1) // tb * tb
    return row_tok, row_slot, gates, block_expert, block_valid, n_asg + n_spare, n_spare


def _final_norm_body(x_ref, g_ref, o_ref):
    x = x_ref[...]
    ms = jnp.mean(x * x, axis=-1, keepdims=True)
    o_ref[...] = x * lax.rsqrt(ms + EPS) * g_ref[...]


def _final_norm(x, g):
    b, l, d = x.shape
    n = b * l
    tm = min(512, n)
    out = pl.pallas_call(
        _final_norm_body,
        grid=(n // tm,),
        in_specs=[pl.BlockSpec((tm, d), lambda i: (i, 0)), pl.BlockSpec((1, d), lambda i: (0, 0))],
        out_specs=pl.BlockSpec((tm, d), lambda i: (i, 0)),
        out_shape=jax.ShapeDtypeStruct((n, d), F32),
        compiler_params=_cparams(("arbitrary",)),
        name="final_norm",
    )(x.reshape(n, d), g.reshape(1, d))
    return out.reshape(b, l, d)


def _rope_tables(n_tok):
    t = jnp.arange(n_tok, dtype=jnp.int32)
    row = (t // GRID_W).astype(F32)
    col = (t % GRID_W).astype(F32)
    inv_freq = ROPE_BASE ** (-jnp.arange(N_FREQ, dtype=F32) / N_FREQ)
    ang = jnp.concatenate([row[:, None] * inv_freq, col[:, None] * inv_freq], axis=-1)
    cos, sin = jnp.cos(ang), jnp.sin(ang)
    cosf = jnp.concatenate([cos, cos, cos, cos], axis=-1)
    sins = jnp.concatenate([-sin, sin, -sin, sin], axis=-1)
    return cosf, sins


def kernel(x, c, ctx, c_ctx, ada_w, ada_b, norm1_g, norm2_g, w_in, w_out, conv_w, ret_decay, gdn_a_log, gdn_dt_bias,
           gdn_norm_g, na_rpb, ffn_w1, ffn_w3, ffn_w2, moe_router, moe_w1, moe_w3, moe_w2, final_g):
    b, l, d = x.shape
    lc = ctx.shape[1]
    depth = ada_w.shape[0]
    cosf, sins = _rope_tables(l)
    ones_c = jnp.ones((lc, LANES), F32)
    zeros_c = jnp.zeros((lc, LANES), F32)

    rows = ((b + 1 + 7) // 8) * 8
    c_all = jnp.zeros((rows, d), F32).at[:b].set(c).at[b].set(c_ctx)
    mod = _ada_vectors(c_all, ada_w, ada_b).reshape(depth, rows, 6, d)

    w1p_all = jax.vmap(_pack_w_in)(w_in)
    wo_all = w_out.astype(BF16)
    log_gamma_all = jnp.log1p(-jnp.exp2(-ret_decay.astype(F32)))
    gdn_tables = jax.vmap(_gdn_tables)(conv_w, gdn_a_log, gdn_dt_bias, gdn_norm_g)
    na_bias = jax.vmap(lambda r: _na_bias_tiles(r, l // GRID_W))(na_rpb)

    y = ctx
    for layer in range(depth):
        need_ctx = layer < depth - 1
        mod_l = mod[layer, :b]
        mod_c = jnp.broadcast_to(mod[layer, b][None], (b, 6, d))
        w1p = w1p_all[layer]
        g1 = norm1_g[layer].reshape(1, d)
        ret_l, gdn_l, ab_l, abt_l, na_l = _in_projection(x, mod_l, g1, cosf, sins, w1p, rope=True)
        ret_c, gdn_c, ab_c, abt_c, na_c = _in_projection(y, mod_c, g1, ones_c, zeros_c, w1p, rope=False)

        r_l, r_c = _retention(ret_l, ret_c, log_gamma_all[layer])
        g_l, g_c = _gdn(gdn_l, ab_l, abt_l, gdn_c, ab_c, abt_c, [t[layer] for t in gdn_tables])
        n_l, n_c = _neighbourhood_attention(na_l, na_c, na_bias, layer)

        wo = wo_all[layer]
        g2 = norm2_g[layer].reshape(1, d)
        j = layer // 2
        if layer % 2 == 0:
            w1, w3, w2 = ffn_w1[j].astype(BF16), ffn_w3[j].astype(BF16), ffn_w2[j].astype(BF16)
            x = _dense_block(r_l, g_l, n_l, x, mod_l, g2, wo, w1, w3, w2)
            if need_ctx:
                y = _dense_block(r_c, g_c, n_c, y, mod_c, g2, wo, w1, w3, w2)
            if layer == depth - 1:
                x = _final_norm(x, final_g)
        else:
            wr = jnp.pad(moe_router[j].astype(F32), ((0, 0), (0, LANES - N_EXPERTS)))
            w1, w3, w2 = moe_w1[j].astype(BF16), moe_w3[j].astype(BF16), moe_w2[j].astype(BF16)

            def moe_ffn(r, g, n, xin, m, last):
                bb, ll, _ = xin.shape
                x1, h, logits = _router_block(r, g, n, xin, m, g2, wo, wr)
                row_tok, row_slot, gates, bexp, bval, n_out, n_spare = _route(logits.reshape(bb * ll, LANES))
                y_rows = _moe_experts(h.reshape(bb * ll, d // 2), row_tok, row_slot, bexp, bval, w1, w3, w2, n_out, n_spare)
                return _moe_combine(y_rows, x1, gates, m[:, 5:6, :], final_g, last)

            x = moe_ffn(r_l, g_l, n_l, x, mod_l, layer == depth - 1)
            if need_ctx:
                y = moe_ffn(r_c, g_c, n_c, y, mod_c, False)
    return x
```

```python
import functools

import numpy as np
import jax
import jax.numpy as jnp
from jax import lax
from jax.experimental import pallas as pl
from jax.experimental.pallas import tpu as pltpu

F32 = jnp.float32
BF16 = jnp.bfloat16
HIGHEST = lax.Precision.HIGHEST

LANES = 128
HEAD_DIM = 64
PAIR = 2 * HEAD_DIM
GRID_W = 64
H_RET, H_GDN, H_NA = 4, 4, 8
D_RET, D_GDN, D_NA = H_RET * HEAD_DIM, H_GDN * HEAD_DIM, H_NA * HEAD_DIM
RET_CHUNK = 128
GDN_CHUNK = 64
GDN_GROUP = 4
GDN_PREP_ROWS = 512
SHORT_CONV = 5
NA_KH, NA_KW = 8, 16
NA_QROWS = 4
NA_KROWS = NA_QROWS + NA_KH
NA_PAIRS = 1
NA_QSCALE = float(np.log2(np.e)) * HEAD_DIM ** -0.5
N_FREQ = HEAD_DIM // 4
ROPE_BASE = 10000.0
N_EXPERTS = 8
TOP_K = 2
MOE_ROWS = 512
FF_CHUNK = 256
EPS = 1e-6
NEG_INF = -1e30
VMEM_LIMIT = 56 * 1024 * 1024

C_RET = 0
C_GDN = C_RET + 4 * D_RET
C_AB = C_GDN + 4 * D_GDN
C_NA = C_AB + LANES
C_END = C_NA + 3 * D_NA


def _cparams(sem, vmem=None):
    return pltpu.CompilerParams(dimension_semantics=sem, vmem_limit_bytes=vmem)


def _silu(x):
    return x * jax.nn.sigmoid(x)


def _dot(a, b):
    return jnp.dot(a, b, preferred_element_type=F32)


def _dot_nt(a, b):
    return lax.dot_general(a, b, (((1,), (1,)), ((), ())), preferred_element_type=F32)


def _dot_tn(a, b):
    return lax.dot_general(a, b, (((0,), (0,)), ((), ())), preferred_element_type=F32)


def _split_bf16(x):
    hi = x.astype(BF16)
    return hi, (x - hi.astype(F32)).astype(BF16)


def _dot_3pass(a, b):
    ah, al = _split_bf16(a)
    bh, bl = _split_bf16(b)
    return _dot(ah, bh) + (_dot(ah, bl) + _dot(al, bh))


def _dot_hi(a, b):
    return jnp.dot(a, b, preferred_element_type=F32, precision=HIGHEST)


def _ada_body(c_ref, w_ref, b_ref, o_ref):
    s = _silu(c_ref[...])
    o_ref[0] = _dot_hi(s, w_ref[0]) + b_ref[0]


def _ada_vectors(c_all, ada_w, ada_b):
    depth, d, d6 = ada_w.shape
    rows = c_all.shape[0]
    tn = 1024
    return pl.pallas_call(
        _ada_body,
        grid=(depth, d6 // tn),
        in_specs=[pl.BlockSpec((rows, d), lambda l, j: (0, 0)),
                  pl.BlockSpec((1, d, tn), lambda l, j: (l, 0, j)),
                  pl.BlockSpec((1, 1, tn), lambda l, j: (l, 0, j))],
        out_specs=pl.BlockSpec((1, rows, tn), lambda l, j: (l, 0, j)),
        out_shape=jax.ShapeDtypeStruct((depth, rows, d6), F32),
        compiler_params=_cparams(("arbitrary", "arbitrary")),
        name="ada_vectors",
    )(c_all, ada_w, ada_b.reshape(depth, 1, d6))


def _norm_mod(x, g, shift, scale):
    ms = jnp.mean(x * x, axis=-1, keepdims=True)
    return (x * lax.rsqrt(ms + EPS) * g) * (1.0 + scale) + shift


def _rope_slab(t, cosf, sins):
    lane = lax.broadcasted_iota(jnp.int32, t.shape, 1)
    first = (lane % HEAD_DIM) < (HEAD_DIM // 2)
    partner = jnp.where(first, pltpu.roll(t, LANES - HEAD_DIM // 2, 1), pltpu.roll(t, HEAD_DIM // 2, 1))
    return t * cosf + partner * sins


def _inproj_body(x_ref, mod_ref, g_ref, cos_ref, sin_ref, w_ref, ret_ref, gdn_ref, ab_ref, abt_ref, na_ref, *, rope):
    h = _norm_mod(x_ref[0], g_ref[...], mod_ref[0, 0:1, :], mod_ref[0, 1:2, :]).astype(BF16)
    qk = _dot(h, w_ref[:, C_RET:C_RET + 2 * D_RET])
    slabs = []
    for s in range(2 * D_RET // LANES):
        t = qk[:, s * LANES:(s + 1) * LANES]
        if rope:
            t = _rope_slab(t, cos_ref[...], sin_ref[...])
        if s >= D_RET // LANES:
            t = t * HEAD_DIM ** -0.5
        slabs.append(t)
    ret_ref[0, :, 0:2 * D_RET] = jnp.concatenate(slabs, axis=1).astype(BF16)
    ret_ref[0, :, 2 * D_RET:] = _dot(h, w_ref[:, C_RET + 2 * D_RET:C_GDN]).astype(BF16)
    for j in range(2):
        gdn_ref[0, :, 512 * j:512 * (j + 1)] = _dot(h, w_ref[:, C_GDN + 512 * j:C_GDN + 512 * (j + 1)]).astype(BF16)
    ab = _dot(h, w_ref[:, C_AB:C_NA])
    ab_ref[0] = ab
    for grp in range(ab.shape[0] // LANES):
        t = ab[grp * LANES:(grp + 1) * LANES, :].T
        for p in range(D_GDN // PAIR):
            abt_ref[0, p, grp] = t[8 * p:8 * p + 8, :]
    for j in range(3):
        na_ref[0, :, 512 * j:512 * (j + 1)] = _dot(h, w_ref[:, C_NA + 512 * j:C_NA + 512 * (j + 1)]).astype(BF16)


def _in_projection(x, mod, g, cosf, sins, w, rope):
    b, l, d = x.shape
    tm = min(512, l)
    body = functools.partial(_inproj_body, rope=rope)
    return pl.pallas_call(
        body,
        grid=(b, l // tm),
        in_specs=[pl.BlockSpec((1, tm, d), lambda i, j: (i, j, 0)),
                  pl.BlockSpec((1, 6, d), lambda i, j: (i, 0, 0)),
                  pl.BlockSpec((1, d), lambda i, j: (0, 0)),
                  pl.BlockSpec((tm, LANES), lambda i, j: (j, 0)),
                  pl.BlockSpec((tm, LANES), lambda i, j: (j, 0)),
                  pl.BlockSpec((d, C_END), lambda i, j: (0, 0))],
        out_specs=[pl.BlockSpec((1, tm, 4 * D_RET), lambda i, j: (i, j, 0)),
                   pl.BlockSpec((1, tm, 4 * D_GDN), lambda i, j: (i, j, 0)),
                   pl.BlockSpec((1, tm, LANES), lambda i, j: (i, j, 0)),
                   pl.BlockSpec((1, D_GDN // PAIR, tm // LANES, 8, LANES), lambda i, j: (i, 0, j, 0, 0)),
                   pl.BlockSpec((1, tm, 3 * D_NA), lambda i, j: (i, j, 0))],
        out_shape=[jax.ShapeDtypeStruct((b, l, 4 * D_RET), BF16),
                   jax.ShapeDtypeStruct((b, l, 4 * D_GDN), BF16),
                   jax.ShapeDtypeStruct((b, l, LANES), F32),
                   jax.ShapeDtypeStruct((b, D_GDN // PAIR, l // LANES, 8, LANES), F32),
                   jax.ShapeDtypeStruct((b, l, 3 * D_NA), BF16)],
        compiler_params=_cparams(("arbitrary", "arbitrary"), VMEM_LIMIT),
        name="in_projection",
    )(x, mod, g, cosf, sins, w)


def _pack_w_in(w_in):
    d = w_in.shape[0]
    c1 = 4 * D_RET
    c2 = c1 + 4 * D_GDN
    q0 = c2 + 4 * H_GDN
    col = jnp.arange(w_in.shape[1])
    w_in = jnp.where((col >= q0) & (col < q0 + D_NA), w_in * NA_QSCALE, w_in).astype(BF16)
    ab = w_in[:, c2:c2 + 4 * H_GDN]
    ab = ab.reshape(d, 2, 2, H_GDN // 2, 2)
    ab = jnp.transpose(ab, (0, 3, 1, 2, 4)).reshape(d, 4 * H_GDN)
    ab = jnp.pad(ab, ((0, 0), (0, LANES - 4 * H_GDN)))
    return jnp.concatenate([w_in[:, :c2], ab, w_in[:, c2 + 4 * H_GDN:]], axis=1)


def _head_masks(shape):
    lane = lax.broadcasted_iota(jnp.int32, shape, len(shape) - 1)
    return lane < HEAD_DIM, lane >= HEAD_DIM


def _per_head(lo, hi, shape):
    m0, _ = _head_masks(shape)
    return jnp.where(m0, lo, hi)


def _head_sumsq(o):
    m0, m1 = _head_masks(o.shape)
    sq = o * o
    s0 = jnp.sum(jnp.where(m0, sq, 0.0), axis=-1, keepdims=True)
    s1 = jnp.sum(jnp.where(m1, sq, 0.0), axis=-1, keepdims=True)
    return jnp.where(m0, s0, s1)


def _ret_body(lg_ref, ql, kl, vl, gl, qc, kc, vc, gc, r_ref, rc_ref, sb_ref):
    c = RET_CHUNK
    npair = D_RET // PAIR
    lane_shape = (c, PAIR)
    pos = lax.broadcasted_iota(jnp.int32, lane_shape, 0).astype(F32)
    ii = lax.broadcasted_iota(jnp.int32, (c, c), 0)
    jj = lax.broadcasted_iota(jnp.int32, (c, c), 1)
    diff = (ii - jj).astype(F32)
    m0, m1 = _head_masks(lane_shape)
    masks = (m0, m1)
    bi = lax.broadcasted_iota(jnp.int32, (PAIR, PAIR), 0) // HEAD_DIM
    bj = lax.broadcasted_iota(jnp.int32, (PAIR, PAIR), 1) // HEAD_DIM
    bd = bi == bj
    cst = []
    for p in range(npair):
        lgf = _per_head(lg_ref[0, 2 * p], lg_ref[0, 2 * p + 1], lane_shape)
        lgb = _per_head(lg_ref[1, 2 * p], lg_ref[1, 2 * p + 1], lane_shape)
        dmats = [jnp.where(diff > 0, jnp.exp(lg_ref[0, 2 * p + hh] * diff),
                           jnp.where(diff < 0, jnp.exp(-lg_ref[1, 2 * p + hh] * diff), 2.0)) for hh in range(2)]
        cst.append(dict(qdf=jnp.exp(lgf * (pos + 1.0)), kdf=jnp.exp(lgf * (c - 1.0 - pos)),
                        qdb=jnp.exp(lgb * (c - pos)), kdb=jnp.exp(lgb * pos),
                        cdf=jnp.exp(lgf[0:1] * c), cdb=jnp.exp(lgb[0:1] * c), dmats=dmats))

    def sweep(q_ref, k_ref, v_ref, g_ref, o_ref, n, sf0, sb0):
        def bstep(t, sbs):
            ci = n - 1 - t
            sl = pl.ds(pl.multiple_of(ci * c, c), c)
            out = []
            for p in range(npair):
                lanes = slice(p * PAIR, (p + 1) * PAIR)
                sb_ref[p, ci] = sbs[p]
                kd = (k_ref[0, sl, lanes].astype(F32) * cst[p]["kdb"]).astype(BF16)
                out.append(sbs[p] * cst[p]["cdb"] + jnp.where(bd, _dot_tn(kd, v_ref[0, sl, lanes]), 0.0))
            return tuple(out)

        sb_fin = lax.fori_loop(0, n, bstep, sb0)

        def fstep(ci, sfs):
            sl = pl.ds(pl.multiple_of(ci * c, c), c)
            out = []
            for p in range(npair):
                lanes = slice(p * PAIR, (p + 1) * PAIR)
                k_ = cst[p]
                q = q_ref[0, sl, lanes]
                k = k_ref[0, sl, lanes]
                v = v_ref[0, sl, lanes]
                qf = q.astype(F32)
                o = (_dot((qf * k_["qdf"]).astype(BF16), sfs[p].astype(BF16))
                     + _dot((qf * k_["qdb"]).astype(BF16), sb_ref[p, ci].astype(BF16)))
                for hh in range(2):
                    qm = jnp.where(masks[hh], q, jnp.zeros_like(q))
                    pm = (_dot_nt(qm, k) * k_["dmats"][hh]).astype(BF16)
                    o = o + jnp.where(masks[hh], _dot(pm, v), 0.0)
                on = o * lax.rsqrt(_head_sumsq(o) * (1.0 / HEAD_DIM) + EPS)
                o_ref[0, sl, lanes] = (on * _silu(g_ref[0, sl, lanes].astype(F32))).astype(o_ref.dtype)
                kd = (k.astype(F32) * k_["kdf"]).astype(BF16)
                out.append(sfs[p] * k_["cdf"] + jnp.where(bd, _dot_tn(kd, v), 0.0))
            return tuple(out)

        sf_fin = lax.fori_loop(0, n, fstep, sf0)
        return sf_fin, sb_fin

    z = tuple(jnp.zeros((PAIR, PAIR), F32) for _ in range(npair))
    sfc, sbc = sweep(qc, kc, vc, gc, rc_ref, qc.shape[1] // c, z, z)
    sweep(ql, kl, vl, gl, r_ref, ql.shape[1] // c, sfc, sbc)


def _retention(ret_l, ret_c, log_gamma):
    b, l, _ = ret_l.shape
    lc = ret_c.shape[1]
    npair = D_RET // PAIR

    def col(k):
        return lambda i: (i, 0, k)

    lat = [pl.BlockSpec((1, l, D_RET), col(k)) for k in range(4)]
    ctx = [pl.BlockSpec((1, lc, D_RET), col(k)) for k in range(4)]
    return pl.pallas_call(
        _ret_body,
        grid=(b,),
        in_specs=[pl.BlockSpec(memory_space=pltpu.SMEM)] + lat + ctx,
        out_specs=[pl.BlockSpec((1, l, D_RET), lambda i: (i, 0, 0)),
                   pl.BlockSpec((1, lc, D_RET), lambda i: (i, 0, 0))],
        out_shape=[jax.ShapeDtypeStruct((b, l, D_RET), BF16), jax.ShapeDtypeStruct((b, lc, D_RET), BF16)],
        scratch_shapes=[pltpu.VMEM((npair, max(l, lc) // RET_CHUNK, PAIR, PAIR), F32)],
        compiler_params=_cparams(("arbitrary",), VMEM_LIMIT),
        name="retention",
    )(log_gamma, ret_l, ret_l, ret_l, ret_l, ret_c, ret_c, ret_c, ret_c)


def _stack_heads(t):
    m0, m1 = _head_masks(t.shape)
    z = jnp.zeros_like(t)
    return jnp.concatenate([jnp.where(m0, t, z), jnp.where(m1, t, z)], axis=0)


def _scan_sum(x, axis, n, reverse):
    size = x.shape[axis]
    pos = lax.broadcasted_iota(jnp.int32, x.shape, axis)
    s = 1
    while s < n:
        if reverse:
            x = x + jnp.where(pos + s < n, pltpu.roll(x, size - s, axis), 0.0)
        else:
            x = x + jnp.where(pos >= s, pltpu.roll(x, s, axis), 0.0)
        s *= 2
    return x


def _gdn_prep(src_ref, col, cw, dst_ref, r0, *, l2, scale):
    n = src_ref.shape[1]
    blk = min(GDN_PREP_ROWS, n)
    halo = 16
    lanes = slice(col % D_GDN, col % D_GDN + LANES)

    def block(i, _):
        b0 = pl.multiple_of(i * blk, blk)
        x = src_ref[0, pl.ds(b0, blk), col:col + LANES].astype(F32)
        lo = src_ref[0, pl.ds(pl.multiple_of(jnp.maximum(b0 - halo, 0), halo), halo), col:col + LANES].astype(F32)
        hi = src_ref[0, pl.ds(pl.multiple_of(jnp.minimum(b0 + blk, n - halo), halo), halo), col:col + LANES].astype(F32)
        lo = jnp.where(b0 > 0, lo, 0.0)
        hi = jnp.where(b0 + blk < n, hi, 0.0)
        ext = jnp.concatenate([lo, x, hi], axis=0)
        acc = x * cw[SHORT_CONV // 2:SHORT_CONV // 2 + 1, :]
        for j in range(SHORT_CONV):
            s = j - SHORT_CONV // 2
            if s != 0:
                sh = pltpu.roll(ext, (-s) % (blk + 2 * halo), 0)
                acc = acc + sh[halo:halo + blk] * cw[j:j + 1, :]
        y = _silu(acc)
        if l2:
            y = y * lax.rsqrt(_head_sumsq(y) + EPS)
        if scale != 1.0:
            y = y * scale
        dst_ref[pl.ds(pl.multiple_of(r0 + b0, halo), blk), lanes] = y.astype(dst_ref.dtype)
        return 0

    lax.fori_loop(0, n // blk, block, 0)


def _gdn_body(x_ref, xc_ref, ab_ref, abt_ref, cw_ref, cst_ref, cstt_ref, ng_ref,
              o_ref, oc_ref, qn, kn, vn, oacc, st_s, x_s, t_s, rhs_s, u_s, wq_s, ak_s, egl_s):
    c = GDN_CHUNK
    c2 = 2 * c
    grp = GDN_GROUP
    npair = D_GDN // PAIR
    lc = xc_ref.shape[1]
    l = x_ref.shape[1]
    nc = lc // c
    nt = (lc + l) // c
    ii = lax.broadcasted_iota(jnp.int32, (c2, c2), 0)
    jj = lax.broadcasted_iota(jnp.int32, (c2, c2), 1)
    same = (ii // c) == (jj // c)
    eye = (ii == jj).astype(F32)
    dirs = ((same & (ii >= jj), same & (ii > jj), c - 1),
            (same & (ii <= jj), same & (ii < jj), 0))

    def chains(g, slot):
        out = []
        for gi in range(grp):
            t = g * grp + gi
            cf = t
            cb = jnp.where(t < nc, nc - 1 - t, nt + nc - 1 - t)
            for p in range(npair):
                for d in range(2):
                    out.append((((slot * grp + gi) * 2 + d) * npair + p, p, d, cf if d == 0 else cb))
        return out

    def stage_inputs(idx, p, d, ci):
        incl, strict, last = dirs[d]
        lanes = slice(p * PAIR, (p + 1) * PAIR)
        nea, dtb = cst_ref[0:1, :], cst_ref[1:2, :]
        neat, dtbt = cstt_ref[p, 0], cstt_ref[p, 1]
        sl = pl.ds(pl.multiple_of(ci * c, c), c)
        abv = ab_ref[0, sl, :]
        gall = nea * jax.nn.softplus(abv + dtb)
        ball = jax.nn.sigmoid(abv)
        shape = (c, PAIR)
        c0 = 8 * p + 2 * d
        gl = _per_head(gall[:, c0:c0 + 1], gall[:, c0 + 1:c0 + 2], shape)
        bl = _per_head(ball[:, c0 + 4:c0 + 5], ball[:, c0 + 5:c0 + 6], shape)
        gcum = _scan_sum(gl, 0, c, d == 1)
        abt = abt_ref[0, p, ci // 2]
        abt = jnp.where(ci % 2 == 1, pltpu.roll(abt, c, 1), abt)
        gt = neat * jax.nn.softplus(abt + dtbt)
        gtc = _scan_sum(gt, 1, c, d == 1)
        grow = jnp.concatenate([gtc[2 * d:2 * d + 1, 0:c], gtc[2 * d + 1:2 * d + 2, 0:c]], axis=1)
        gcol = jnp.concatenate([gcum[:, 0:1], gcum[:, HEAD_DIM:HEAD_DIM + 1]], axis=0)
        dec = jnp.where(incl, jnp.exp(jnp.where(incl, gcol - grow, 0.0)), 0.0)
        dec = jnp.where(ii == jj, 1.0, dec)
        q = qn[sl, lanes]
        k = kn[sl, lanes]
        v = vn[sl, lanes]
        kf = k.astype(F32)
        eg = jnp.exp(gcum)
        kb = kf * bl
        k_st = _stack_heads(k)
        kq = _dot_nt(jnp.concatenate([_stack_heads(kb.astype(BF16)), _stack_heads(q)], axis=0), k_st)
        a = jnp.where(strict, kq[:c2] * dec, 0.0)
        attn = (kq[c2:] * dec).astype(BF16)
        x_s[idx] = (-a).astype(BF16)
        t_s[idx] = eye - a
        rhs_s[idx] = jnp.concatenate([_stack_heads((v.astype(F32) * bl).astype(BF16)),
                                      _stack_heads((kb * eg).astype(BF16))], axis=1)
        glast = gcum[last:last + 1, :]
        wq_s[idx, c2:, :] = _stack_heads((q.astype(F32) * eg).astype(BF16))
        ak_s[idx] = jnp.concatenate([attn, _stack_heads(kf * jnp.exp(glast - gcum)).T.astype(BF16)], axis=0)
        egl_s[idx] = jnp.broadcast_to(jnp.exp(glast), (8, PAIR))

    def stage_group(g, slot):
        todo = chains(g, slot)
        for idx, p, d, ci in todo:
            stage_inputs(idx, p, d, ci)
        for idx, _, _, _ in todo:
            xb = x_s[idx]
            x_s[idx] = _dot(xb, xb).astype(BF16)
        for _ in range(4):
            for idx, _, _, _ in todo:
                xb = x_s[idx]
                t = t_s[idx]
                prod = _dot(jnp.concatenate([t.astype(BF16), xb], axis=0), xb)
                t_s[idx] = t + prod[:c2]
                x_s[idx] = prod[c2:].astype(BF16)
        for idx, _, _, _ in todo:
            t = t_s[idx]
            t = t + _dot(t.astype(BF16), x_s[idx])
            sol = _dot(t.astype(BF16), rhs_s[idx])
            u_s[idx] = sol[:, :PAIR]
            wq_s[idx, :c2, :] = sol[:, PAIR:].astype(BF16)

    def recur_group(g, slot):
        sts = {(p, d): st_s[d * npair + p] for p in range(npair) for d in range(2)}
        for idx, p, d, ci in chains(g, slot):
            sl = pl.ds(pl.multiple_of(ci * c, c), c)
            st = sts[(p, d)]
            ws = _dot(wq_s[idx], st.astype(BF16))
            v_new = (u_s[idx] - ws[:c2]).astype(BF16)
            av = _dot(ak_s[idx], v_new)
            o_st = ws[c2:] + av[:c2]
            oacc[sl, p * PAIR:(p + 1) * PAIR] += o_st[:c] + o_st[c:]
            sts[(p, d)] = st * egl_s[idx][0:1, :] + av[c2:]
        for (p, d), st in sts.items():
            st_s[d * npair + p] = st

    def prep(src_ref, r0):
        for s in range(D_GDN // LANES):
            for k, (dst, l2, scale) in enumerate(((qn, True, HEAD_DIM ** -0.5), (kn, True, 1.0), (vn, False, 1.0))):
                col = k * D_GDN + s * LANES
                _gdn_prep(src_ref, col, cw_ref[:, col:col + LANES], dst, r0, l2=l2, scale=scale)

    def finish(src_ref, out_ref, r0, n):
        for p in range(npair):
            lanes = slice(p * PAIR, (p + 1) * PAIR)
            o = oacc[r0:r0 + n, lanes]
            on = o * lax.rsqrt(_head_sumsq(o) * (1.0 / HEAD_DIM) + EPS)
            gate = src_ref[0, :, 3 * D_GDN + p * PAIR:3 * D_GDN + (p + 1) * PAIR].astype(F32)
            out_ref[0, :, lanes] = (on * ng_ref[...] * _silu(gate)).astype(out_ref.dtype)

    st_s[...] = jnp.zeros_like(st_s)
    oacc[...] = jnp.zeros_like(oacc)
    prep(xc_ref, 0)
    prep(x_ref, lc)

    ng = nt // grp
    stage_group(0, 0)

    def two_groups(kk, _):
        g = 2 * kk
        recur_group(g, 0)
        stage_group(g + 1, 1)
        recur_group(g + 1, 1)
        stage_group(g + 2, 0)
        return 0

    lax.fori_loop(0, (ng - 1) // 2, two_groups, 0)
    if ng % 2 == 0:
        recur_group(ng - 2, 0)
        stage_group(ng - 1, 1)
        recur_group(ng - 1, 1)
    else:
        recur_group(ng - 1, 0)
    finish(xc_ref, oc_ref, 0, lc)
    finish(x_ref, o_ref, lc, l)


def _gdn_tables(conv_w, a_log, dt_bias, norm_g):
    npair = D_GDN // PAIR
    nea = -jnp.exp(a_log.astype(F32)).reshape(2, npair, 2)
    dtb = dt_bias.astype(F32).reshape(2, npair, 2)
    rows = jnp.stack([jnp.transpose(nea, (1, 0, 2)).reshape(npair, 4), jnp.transpose(dtb, (1, 0, 2)).reshape(npair, 4)], axis=1)
    cst = jnp.pad(jnp.transpose(jnp.pad(rows, ((0, 0), (0, 0), (0, 4))), (1, 0, 2)).reshape(2, 8 * npair),
                  ((0, 6), (0, LANES - 8 * npair)))
    cstt = jnp.broadcast_to(jnp.pad(rows, ((0, 0), (0, 0), (0, 4)))[..., None], (npair, 2, 8, LANES))
    cw = jnp.pad(conv_w.astype(F32), ((0, 8 - SHORT_CONV), (0, 0)))
    ng = jnp.tile(norm_g.astype(F32), 2).reshape(1, PAIR)
    return cw, cst, cstt, ng


def _gdn(gdn_l, ab_l, abt_l, gdn_c, ab_c, abt_c, tables):
    b, l, _ = gdn_l.shape
    lc = gdn_c.shape[1]
    npair = D_GDN // PAIR
    lt = l + lc
    assert (lt // GDN_CHUNK) % GDN_GROUP == 0 and lc % LANES == 0
    cw, cst, cstt, ng = tables

    nstage = 2 * GDN_GROUP * 2 * npair

    def whole(shape):
        nd = len(shape)
        return pl.BlockSpec(shape, lambda i: (0,) * nd)

    def per_batch(shape, single=False):
        nd = len(shape)
        return pl.BlockSpec((1,) + shape, lambda i: (i,) + (0,) * nd, pipeline_mode=pl.Buffered(1) if single else None)

    sq = (nstage, PAIR, PAIR)
    return pl.pallas_call(
        _gdn_body,
        grid=(b,),
        in_specs=[per_batch((l, 4 * D_GDN), True), per_batch((lc, 4 * D_GDN)), per_batch((lt, LANES), True),
                  per_batch((npair, lt // LANES, 8, LANES)),
                  whole((8, 3 * D_GDN)), whole((8, LANES)), whole((npair, 2, 8, LANES)), whole((1, PAIR))],
        out_specs=[per_batch((l, D_GDN)), per_batch((lc, D_GDN))],
        out_shape=[jax.ShapeDtypeStruct((b, l, D_GDN), BF16), jax.ShapeDtypeStruct((b, lc, D_GDN), BF16)],
        scratch_shapes=[pltpu.VMEM((lt, D_GDN), BF16)] * 3 + [pltpu.VMEM((lt, D_GDN), F32)]
        + [pltpu.VMEM((2 * npair, PAIR, PAIR), F32), pltpu.VMEM(sq, BF16), pltpu.VMEM(sq, F32),
           pltpu.VMEM((nstage, PAIR, 2 * PAIR), BF16), pltpu.VMEM(sq, F32),
           pltpu.VMEM((nstage, 2 * PAIR, PAIR), BF16), pltpu.VMEM((nstage, 2 * PAIR, PAIR), BF16),
           pltpu.VMEM((nstage, 8, PAIR), F32)],
        compiler_params=_cparams(("arbitrary",), VMEM_LIMIT),
        name="gated_deltanet",
    )(gdn_l, gdn_c, jnp.concatenate([ab_c, ab_l], axis=1), jnp.concatenate([abt_c, abt_l], axis=2), cw, cst, cstt, ng)


def _na_bias_tiles(rpb, rows):
    w = GRID_W
    ext = jnp.pad(rpb.astype(F32) * float(np.log2(np.e)), ((0, 0), (0, 0), (w - NA_KW, w - NA_KW)))
    cq = np.arange(w)[:, None]
    ck = np.arange(w)[None, :]
    tb = ext[:, :, jnp.asarray(ck - cq + w - 1)]
    ws = np.clip(cq - NA_KW // 2, 0, w - NA_KW)
    col_ok = (ck >= ws) & (ck < ws + NA_KW)
    tb = jnp.where(jnp.asarray(col_ok)[None, None], tb, NEG_INF)
    none = 2 * NA_KH - 1
    tb = jnp.concatenate([tb, jnp.full((tb.shape[0], 1, w, w), NEG_INF, F32)], axis=1)
    nt = rows // NA_QROWS
    idx = np.zeros((3, NA_QROWS, NA_KROWS), np.int32)
    for cls, t in enumerate((0, 1, nt - 1)):
        ks = int(np.clip(NA_QROWS * t - NA_KH // 2, 0, rows - NA_KROWS))
        for rl in range(NA_QROWS):
            r = NA_QROWS * t + rl
            r0 = int(np.clip(r - NA_KH // 2, 0, rows - NA_KH))
            for j in range(NA_KROWS):
                kr = ks + j
                idx[cls, rl, j] = kr - r + NA_KH - 1 if r0 <= kr < r0 + NA_KH else none
    pairs = idx.reshape(3, NA_QROWS, NA_KROWS // 2, 2)
    uniq, inv = np.unique(pairs.reshape(-1, 2), axis=0, return_inverse=True)
    wide = jnp.concatenate([tb[:, jnp.asarray(uniq[:, 0])], tb[:, jnp.asarray(uniq[:, 1])]], axis=-1)
    tiles = wide[:, jnp.asarray(inv.reshape(3, NA_QROWS, NA_KROWS // 2))]
    tiles = jnp.transpose(tiles, (1, 0, 3, 2, 4, 5))
    return tiles.reshape(3, tb.shape[0], NA_KROWS // 2, NA_QROWS * w, 2 * w)


def _softmax_pv(s_parts, v_parts):
    m = functools.reduce(jnp.maximum, [jnp.max(s, axis=-1, keepdims=True) for s in s_parts])
    ps = [jnp.exp2(s - m) for s in s_parts]
    den = functools.reduce(lambda x, y: x + y, [jnp.sum(p, axis=-1, keepdims=True) for p in ps])
    o = _dot(jnp.concatenate([p.astype(BF16) for p in ps], axis=1), jnp.concatenate(v_parts, axis=0))
    return o / den


def _na_body(q_ref, k_ref, v_ref, qc_ref, kc_ref, vc_ref, bias_ref, o_ref, oc_ref, *, rows):
    w = GRID_W
    tq = NA_QROWS * w
    tk = NA_KROWS * w
    nt = rows // NA_QROWS
    m0, m1 = _head_masks((tq, PAIR))
    masks = (m0, m1)

    def tile(t, _):
        cls = jnp.where(t > 0, 1, 0) + jnp.where(t == nt - 1, 1, 0)
        ks = jnp.clip(NA_QROWS * t - NA_KH // 2, 0, rows - NA_KROWS) * w
        ksl = pl.ds(pl.multiple_of(ks, w), tk)
        qsl = pl.ds(pl.multiple_of(t * tq, tq), tq)
        for p in range(NA_PAIRS):
            lanes = slice(p * PAIR, (p + 1) * PAIR)
            q = q_ref[0, qsl, lanes]
            kb = k_ref[0, ksl, lanes]
            vb = v_ref[0, ksl, lanes]
            kc = kc_ref[0, :, lanes]
            vc = vc_ref[0, :, lanes]
            s_all = _dot_nt(_stack_heads(q), jnp.concatenate([kb, kc], axis=0))
            v_all = jnp.concatenate([vb, vc], axis=0)
            acc = jnp.zeros((tq, PAIR), F32)
            for hh in range(2):
                bias = jnp.concatenate([bias_ref[0, cls, 2 * p + hh, s] for s in range(NA_KROWS // 2)]
                                       + [jnp.zeros((tq, kc.shape[0]), F32)], axis=1)
                o_h = _softmax_pv([s_all[hh * tq:(hh + 1) * tq] + bias], [v_all])
                acc = acc + jnp.where(masks[hh], o_h, 0.0)
            o_ref[0, qsl, lanes] = acc.astype(o_ref.dtype)
        return 0

    lax.fori_loop(0, nt, tile, 0)
    for p in range(NA_PAIRS):
        lanes = slice(p * PAIR, (p + 1) * PAIR)
        qc = qc_ref[0, :, lanes]
        kc = kc_ref[0, :, lanes]
        vc = vc_ref[0, :, lanes]
        mc0, mc1 = _head_masks(qc.shape)
        accc = jnp.zeros(qc.shape, F32)
        for mk in (mc0, mc1):
            qm = jnp.where(mk, qc, jnp.zeros_like(qc))
            accc = accc + jnp.where(mk, _softmax_pv([_dot_nt(qm, kc)], [vc]), 0.0)
        oc_ref[0, :, lanes] = accc.astype(oc_ref.dtype)


def _neighbourhood_attention(na_l, na_c, bias, layer):
    b, l, _ = na_l.shape
    lc = na_c.shape[1]
    rows = l // GRID_W
    wide = NA_PAIRS * PAIR
    ngrp = D_NA // wide

    def col(k):
        return lambda p, i: (i, 0, k * ngrp + p)

    lat = [pl.BlockSpec((1, l, wide), col(k)) for k in range(3)]
    ctx = [pl.BlockSpec((1, lc, wide), col(k)) for k in range(3)]
    return pl.pallas_call(
        functools.partial(_na_body, rows=rows),
        grid=(ngrp, b),
        in_specs=lat + ctx + [pl.BlockSpec((1, 3, 2 * NA_PAIRS, NA_KROWS // 2, NA_QROWS * GRID_W, 2 * GRID_W),
                                           lambda p, i: (layer, 0, p, 0, 0, 0))],
        out_specs=[pl.BlockSpec((1, l, wide), lambda p, i: (i, 0, p)),
                   pl.BlockSpec((1, lc, wide), lambda p, i: (i, 0, p))],
        out_shape=[jax.ShapeDtypeStruct((b, l, D_NA), BF16), jax.ShapeDtypeStruct((b, lc, D_NA), BF16)],
        compiler_params=_cparams(("arbitrary", "arbitrary"), VMEM_LIMIT),
        name="neighbourhood_attention",
    )(na_l, na_l, na_l, na_c, na_c, na_c, bias)


def _outproj_residual(r_ref, g_ref, n_ref, x_ref, mod_ref, wo_ref):
    mix = jnp.concatenate([r_ref[0], g_ref[0], n_ref[0]], axis=1)
    return x_ref[0] + mod_ref[0, 2:3, :] * _dot(mix, wo_ref[...])


def _swiglu_chunk(hb, w1_ref, w3_ref, w2_ref, j):
    cols = slice(j * FF_CHUNK, (j + 1) * FF_CHUNK)
    t = (_silu(_dot(hb, w1_ref[:, cols])) * _dot(hb, w3_ref[:, cols])).astype(BF16)
    return _dot(t, w2_ref[cols, :])


def _swiglu_chunks(hb, w1_ref, w3_ref, w2_ref, acc_ref):
    for j in range(w2_ref.shape[0] // FF_CHUNK):
        if j == 0:
            acc_ref[...] = _swiglu_chunk(hb, w1_ref, w3_ref, w2_ref, j)
        else:
            acc_ref[...] += _swiglu_chunk(hb, w1_ref, w3_ref, w2_ref, j)


def _dense_body(r_ref, g_ref, n_ref, x_ref, mod_ref, g2_ref, wo_ref, w1_ref, w3_ref, w2_ref, o_ref, acc_ref):
    x1 = _outproj_residual(r_ref, g_ref, n_ref, x_ref, mod_ref, wo_ref)
    hb = _norm_mod(x1, g2_ref[...], mod_ref[0, 3:4, :], mod_ref[0, 4:5, :]).astype(BF16)
    _swiglu_chunks(hb, w1_ref, w3_ref, w2_ref, acc_ref)
    o_ref[0] = x1 + mod_ref[0, 5:6, :] * acc_ref[...]


def _pack_bf16_pairs(h):
    m = h.shape[1] // 2
    bits = lax.bitcast_convert_type(h.astype(BF16).astype(F32), jnp.uint32)
    return (bits[:, :m] >> 16) | (bits[:, m:] & jnp.uint32(0xFFFF0000))


def _unpack_bf16_pairs(u):
    lo = lax.bitcast_convert_type(u << 16, F32)
    hi = lax.bitcast_convert_type(u & jnp.uint32(0xFFFF0000), F32)
    return jnp.concatenate([lo, hi], axis=1).astype(BF16)


def _router_body(r_ref, g_ref, n_ref, x_ref, mod_ref, g2_ref, wo_ref, wr_ref, x1_ref, h_ref, lg_ref):
    x1 = _outproj_residual(r_ref, g_ref, n_ref, x_ref, mod_ref, wo_ref)
    h = _norm_mod(x1, g2_ref[...], mod_ref[0, 3:4, :], mod_ref[0, 4:5, :])
    x1_ref[0] = x1
    h_ref[0] = _pack_bf16_pairs(h)
    lg_ref[0] = _dot_3pass(h, wr_ref[...])


def _resident(shape):
    nd = len(shape)
    return pl.BlockSpec(shape, lambda i, j: (0,) * nd, pipeline_mode=pl.Buffered(1))


def _mixer_specs(tm, d):
    return [pl.BlockSpec((1, tm, D_RET), lambda i, j: (i, j, 0)),
            pl.BlockSpec((1, tm, D_GDN), lambda i, j: (i, j, 0)),
            pl.BlockSpec((1, tm, D_NA), lambda i, j: (i, j, 0)),
            pl.BlockSpec((1, tm, d), lambda i, j: (i, j, 0)),
            pl.BlockSpec((1, 6, d), lambda i, j: (i, 0, 0)),
            pl.BlockSpec((1, d), lambda i, j: (0, 0))]


def _dense_block(r, g, n, x, mod, g2, wo, w1, w3, w2):
    b, l, d = x.shape
    tm = min(512, l)
    dff = w1.shape[1]
    return pl.pallas_call(
        _dense_body,
        grid=(b, l // tm),
        in_specs=_mixer_specs(tm, d) + [_resident((d, d)), _resident((d, dff)), _resident((d, dff)), _resident((dff, d))],
        out_specs=pl.BlockSpec((1, tm, d), lambda i, j: (i, j, 0)),
        out_shape=jax.ShapeDtypeStruct((b, l, d), F32),
        scratch_shapes=[pltpu.VMEM((tm, d), F32)],
        compiler_params=_cparams(("arbitrary", "arbitrary"), VMEM_LIMIT),
        name="outproj_swiglu",
    )(r, g, n, x, mod, g2, wo, w1, w3, w2)


def _router_block(r, g, n, x, mod, g2, wo, wr):
    b, l, d = x.shape
    tm = min(512, l)
    blk = pl.BlockSpec((1, tm, d), lambda i, j: (i, j, 0))
    return pl.pallas_call(
        _router_body,
        grid=(b, l // tm),
        in_specs=_mixer_specs(tm, d) + [_resident((d, d)), _resident((d, LANES))],
        out_specs=[blk, pl.BlockSpec((1, tm, d // 2), lambda i, j: (i, j, 0)), pl.BlockSpec((1, tm, LANES), lambda i, j: (i, j, 0))],
        out_shape=[jax.ShapeDtypeStruct((b, l, d), F32), jax.ShapeDtypeStruct((b, l, d // 2), jnp.uint32),
                   jax.ShapeDtypeStruct((b, l, LANES), F32)],
        compiler_params=_cparams(("arbitrary", "arbitrary"), VMEM_LIMIT),
        name="outproj_router",
    )(r, g, n, x, mod, g2, wo, wr)


def _row_copy_in(idx_ref, r, src_hbm, dst_ref, sem):
    return pltpu.make_async_copy(src_hbm.at[pl.ds(idx_ref[0, 0, r], 1), :], dst_ref.at[pl.ds(r, 1), :], sem)


def _row_copy_out(idx_ref, r, src_ref, dst_hbm, sem):
    return pltpu.make_async_copy(src_ref.at[pl.ds(r, 1), :], dst_hbm.at[pl.ds(idx_ref[0, 0, r], 1), :], sem)


def _moe_body(bval_ref, bexp_ref, tok_ref, tokn_ref, slotp_ref, h_hbm, w1_ref, w3_ref, w2_ref, y_hbm,
              xbuf, ybuf, gsem, ssem, *, n_asg):
    i = pl.program_id(0)
    tb = xbuf.shape[1]
    nch = w2_ref.shape[1] // FF_CHUNK
    cur = i % 2
    per = -(-tb // nch)
    valid = bval_ref[i] == 1
    prev_valid = bval_ref[jnp.maximum(i - 1, 0)] == 1

    def wait_gather(slot):
        pltpu.make_async_copy(h_hbm.at[pl.ds(0, tb), :], xbuf.at[slot], gsem).wait()

    def wait_scatter(slot):
        pltpu.make_async_copy(ybuf.at[slot], y_hbm.at[pl.ds(0, tb), :], ssem).wait()

    @pl.when(i == 0)
    def _():
        ybuf[1] = jnp.zeros((tb, ybuf.shape[2]), F32)
        fills = [pltpu.make_async_copy(ybuf.at[1], y_hbm.at[pl.ds(r0, tb), :], ssem)
                 for r0 in range(n_asg, y_hbm.shape[0], tb)]
        for f in fills:
            f.start()
        for f in fills:
            f.wait()

        def issue(r, _):
            _row_copy_in(tok_ref, r, h_hbm, xbuf.at[0], gsem).start()
            return 0

        lax.fori_loop(0, tb, issue, 0)

    @pl.when(valid)
    def _():
        wait_gather(cur)
        hb = _unpack_bf16_pairs(xbuf[cur])
        acc = ybuf.at[cur]
        prv = ybuf.at[1 - cur]
        nxt = xbuf.at[1 - cur]
        for j in range(nch):
            for r in range(j * per, min(tb, (j + 1) * per)):
                _row_copy_in(tokn_ref, r, h_hbm, nxt, gsem).start()
                _row_copy_out(slotp_ref, r, prv, y_hbm, ssem).start()
            part = _swiglu_chunk(hb, w1_ref.at[0], w3_ref.at[0], w2_ref.at[0], j)
            if j == 0:
                acc[...] = part
            else:
                acc[...] += part
        wait_scatter(1 - cur)

    @pl.when(jnp.logical_not(valid) & prev_valid & (i > 0))
    def _():
        wait_gather(cur)

        def issue(r, _):
            _row_copy_out(slotp_ref, r, ybuf.at[1 - cur], y_hbm, ssem).start()
            return 0

        lax.fori_loop(0, tb, issue, 0)
        wait_scatter(1 - cur)


def _moe_experts(h_rows, row_tok, row_slot, block_expert, block_valid, w1, w3, w2, n_out_rows, n_spare):
    d = w2.shape[-1]
    dff = w2.shape[1]
    nb = block_expert.shape[0]
    tb = MOE_ROWS
    smem = functools.partial(pl.BlockSpec, (1, 1, tb), memory_space=pltpu.SMEM)
    grid_spec = pltpu.PrefetchScalarGridSpec(
        num_scalar_prefetch=2,
        grid=(nb,),
        in_specs=[smem(lambda i, bv, be: (i, 0, 0)),
                  smem(lambda i, bv, be: (jnp.minimum(i + 1, nb - 1), 0, 0)),
                  smem(lambda i, bv, be: (i, 0, 0)),
                  pl.BlockSpec(memory_space=pl.ANY),
                  pl.BlockSpec((1, d, dff), lambda i, bv, be: (be[i], 0, 0)),
                  pl.BlockSpec((1, d, dff), lambda i, bv, be: (be[i], 0, 0)),
                  pl.BlockSpec((1, dff, d), lambda i, bv, be: (be[i], 0, 0))],
        out_specs=pl.BlockSpec(memory_space=pl.ANY),
        scratch_shapes=[pltpu.VMEM((2, tb, d // 2), jnp.uint32), pltpu.VMEM((2, tb, d), F32),
                        pltpu.SemaphoreType.DMA(()), pltpu.SemaphoreType.DMA(())],
    )
    return pl.pallas_call(
        functools.partial(_moe_body, n_asg=n_out_rows - n_spare),
        grid_spec=grid_spec,
        out_shape=jax.ShapeDtypeStruct((n_out_rows, d), F32),
        compiler_params=_cparams(("arbitrary",), VMEM_LIMIT),
        name="moe_experts",
    )(block_valid, block_expert, row_tok.reshape(nb, 1, tb), row_tok.reshape(nb, 1, tb), row_slot.reshape(nb + 1, 1, tb),
      h_rows, w1, w3, w2)


def _combine_body(y0_ref, y1_ref, x1_ref, gate_ref, gm_ref, fg_ref, o_ref, *, final):
    moe = gate_ref[:, 0:1] * y0_ref[...] + gate_ref[:, 1:2] * y1_ref[...]
    x2 = x1_ref[...] + gm_ref[0] * moe
    if final:
        ms = jnp.mean(x2 * x2, axis=-1, keepdims=True)
        x2 = x2 * lax.rsqrt(ms + EPS) * fg_ref[...]
    o_ref[...] = x2


def _moe_combine(y_rows, x1, gates, gate_mlp, final_g, final):
    b, l, d = x1.shape
    n_tok = b * l
    tm = 256
    per_b = l // tm
    out = pl.pallas_call(
        functools.partial(_combine_body, final=final),
        grid=(n_tok // tm,),
        in_specs=[pl.BlockSpec((tm, d), lambda i: (i, 0)),
                  pl.BlockSpec((tm, d), lambda i: (n_tok // tm + i, 0)),
                  pl.BlockSpec((tm, d), lambda i: (i, 0)),
                  pl.BlockSpec((tm, 2), lambda i: (i, 0)),
                  pl.BlockSpec((1, 1, d), lambda i: (i // per_b, 0, 0)),
                  pl.BlockSpec((1, d), lambda i: (0, 0))],
        out_specs=pl.BlockSpec((tm, d), lambda i: (i, 0)),
        out_shape=jax.ShapeDtypeStruct((n_tok, d), F32),
        compiler_params=_cparams(("arbitrary",), VMEM_LIMIT),
        name="moe_combine_final_norm",
    )(y_rows, y_rows, x1.reshape(n_tok, d), gates, gate_mlp, final_g.reshape(1, d))
    return out.reshape(b, l, d)


def _route(logits):
    n_tok = logits.shape[0]
    tb = MOE_ROWS
    n_asg = n_tok * TOP_K
    top_logit, top_e = lax.top_k(logits[:, :N_EXPERTS], TOP_K)
    gates = jax.nn.softmax(top_logit, axis=-1)
    e_flat = top_e.reshape(-1).astype(jnp.int32)
    asg = jnp.arange(n_asg, dtype=jnp.int32)
    by_expert = lax.sort(e_flat * n_asg + asg)
    counts = jnp.sum((e_flat[:, None] == jnp.arange(N_EXPERTS, dtype=jnp.int32)[None, :]).astype(jnp.int32), axis=0)
    starts = jnp.cumsum(counts) - counts
    padded = (counts + tb - 1) // tb * tb
    pad_ends = jnp.cumsum(padded)
    pad_starts = pad_ends - padded
    nb = (n_asg + tb - 1) // tb + N_EXPERTS + 1
    n_rows = nb * tb
    block_start = jnp.arange(nb, dtype=jnp.int32) * tb
    block_expert = jnp.minimum(jnp.sum((block_start[:, None] >= pad_ends[None, :]).astype(jnp.int32), axis=1), N_EXPERTS - 1)
    block_valid = (block_start < pad_ends[-1]).astype(jnp.int32)
    row = jnp.arange(n_rows, dtype=jnp.int32)
    row_e = jnp.repeat(block_expert, tb)
    off = row - pad_starts[row_e]
    is_pad = (off >= counts[row_e]) | (jnp.repeat(block_valid, tb) == 0)
    src = by_expert[jnp.clip(starts[row_e] + off, 0, n_asg - 1)] - row_e * n_asg
    row_asg = jnp.where(is_pad, -1, src)
    spare = tb + n_asg + jnp.cumsum(is_pad.astype(jnp.int32)) - 1
    row_slot = jnp.where(is_pad, spare, (row_asg % TOP_K) * n_tok + row_asg // TOP_K)
    row_tok = jnp.where(is_pad, 0, row_asg // TOP_K)
    row_slot = jnp.concatenate([n_asg + jnp.arange(tb, dtype=jnp.int32), row_slot])
    last_e = block_expert[jnp.maximum(pad_ends[-1] // tb - 1, 0)]
    block_expert = jnp.where(block_valid == 1, block_expert, last_e)
    n_spare = (tb + n_rows - n_asg + tb - 1) // tb * tb
    return row_tok, row_slot, gates, block_expert, block_valid, n_asg + n_spare, n_spare


def _final_norm_body(x_ref, g_ref, o_ref):
    x = x_ref[...]
    ms = jnp.mean(x * x, axis=-1, keepdims=True)
    o_ref[...] = x * lax.rsqrt(ms + EPS) * g_ref[...]


def _final_norm(x, g):
    b, l, d = x.shape
    n = b * l
    tm = min(512, n)
    out = pl.pallas_call(
        _final_norm_body,
        grid=(n // tm,),
        in_specs=[pl.BlockSpec((tm, d), lambda i: (i, 0)), pl.BlockSpec((1, d), lambda i: (0, 0))],
        out_specs=pl.BlockSpec((tm, d), lambda i: (i, 0)),
        out_shape=jax.ShapeDtypeStruct((n, d), F32),
        compiler_params=_cparams(("arbitrary",)),
        name="final_norm",
    )(x.reshape(n, d), g.reshape(1, d))
    return out.reshape(b, l, d)


def _rope_tables(n_tok):
    t = jnp.arange(n_tok, dtype=jnp.int32)
    row = (t // GRID_W).astype(F32)
    col = (t % GRID_W).astype(F32)
    inv_freq = ROPE_BASE ** (-jnp.arange(N_FREQ, dtype=F32) / N_FREQ)
    ang = jnp.concatenate([row[:, None] * inv_freq, col[:, None] * inv_freq], axis=-1)
    cos, sin = jnp.cos(ang), jnp.sin(ang)
    cosf = jnp.concatenate([cos, cos, cos, cos], axis=-1)
    sins = jnp.concatenate([-sin, sin, -sin, sin], axis=-1)
    return cosf, sins


def kernel(x, c, ctx, c_ctx, ada_w, ada_b, norm1_g, norm2_g, w_in, w_out, conv_w, ret_decay, gdn_a_log, gdn_dt_bias,
           gdn_norm_g, na_rpb, ffn_w1, ffn_w3, ffn_w2, moe_router, moe_w1, moe_w3, moe_w2, final_g):
    b, l, d = x.shape
    lc = ctx.shape[1]
    depth = ada_w.shape[0]
    cosf, sins = _rope_tables(l)
    ones_c = jnp.ones((lc, LANES), F32)
    zeros_c = jnp.zeros((lc, LANES), F32)

    rows = ((b + 1 + 7) // 8) * 8
    c_all = jnp.zeros((rows, d), F32).at[:b].set(c).at[b].set(c_ctx)
    mod = _ada_vectors(c_all, ada_w, ada_b).reshape(depth, rows, 6, d)

    w1p_all = jax.vmap(_pack_w_in)(w_in)
    wo_all = w_out.astype(BF16)
    log_gamma_all = jnp.log1p(-jnp.exp2(-ret_decay.astype(F32)))
    gdn_tables = jax.vmap(_gdn_tables)(conv_w, gdn_a_log, gdn_dt_bias, gdn_norm_g)
    na_bias = jax.vmap(lambda r: _na_bias_tiles(r, l // GRID_W))(na_rpb)

    y = ctx
    for layer in range(depth):
        need_ctx = layer < depth - 1
        mod_l = mod[layer, :b]
        mod_c = jnp.broadcast_to(mod[layer, b][None], (b, 6, d))
        w1p = w1p_all[layer]
        g1 = norm1_g[layer].reshape(1, d)
        ret_l, gdn_l, ab_l, abt_l, na_l = _in_projection(x, mod_l, g1, cosf, sins, w1p, rope=True)
        ret_c, gdn_c, ab_c, abt_c, na_c = _in_projection(y, mod_c, g1, ones_c, zeros_c, w1p, rope=False)

        r_l, r_c = _retention(ret_l, ret_c, log_gamma_all[layer])
        g_l, g_c = _gdn(gdn_l, ab_l, abt_l, gdn_c, ab_c, abt_c, [t[layer] for t in gdn_tables])
        n_l, n_c = _neighbourhood_attention(na_l, na_c, na_bias, layer)

        wo = wo_all[layer]
        g2 = norm2_g[layer].reshape(1, d)
        j = layer // 2
        if layer % 2 == 0:
            w1, w3, w2 = ffn_w1[j].astype(BF16), ffn_w3[j].astype(BF16), ffn_w2[j].astype(BF16)
            x = _dense_block(r_l, g_l, n_l, x, mod_l, g2, wo, w1, w3, w2)
            if need_ctx:
                y = _dense_block(r_c, g_c, n_c, y, mod_c, g2, wo, w1, w3, w2)
            if layer == depth - 1:
                x = _final_norm(x, final_g)
        else:
            wr = jnp.pad(moe_router[j].astype(F32), ((0, 0), (0, LANES - N_EXPERTS)))
            w1, w3, w2 = moe_w1[j].astype(BF16), moe_w3[j].astype(BF16), moe_w2[j].astype(BF16)

            def moe_ffn(r, g, n, xin, m, last):
                bb, ll, _ = xin.shape
                x1, h, logits = _router_block(r, g, n, xin, m, g2, wo, wr)
                row_tok, row_slot, gates, bexp, bval, n_out, n_spare = _route(logits.reshape(bb * ll, LANES))
                y_rows = _moe_experts(h.reshape(bb * ll, d // 2), row_tok, row_slot, bexp, bval, w1, w3, w2, n_out, n_spare)
                return _moe_combine(y_rows, x1, gates, m[:, 5:6, :], final_g, last)

            x = moe_ffn(r_l, g_l, n_l, x, mod_l, layer == depth - 1)
            if need_ctx:
                y = moe_ffn(r_c, g_c, n_c, y, mod_c, False)
    return x
```

```python
import functools

import numpy as np
import jax
import jax.numpy as jnp
from jax import lax
from jax.experimental import pallas as pl
from jax.experimental.pallas import tpu as pltpu

F32 = jnp.float32
BF16 = jnp.bfloat16
HIGHEST = lax.Precision.HIGHEST

LANES = 128
HEAD_DIM = 64
PAIR = 2 * HEAD_DIM
GRID_W = 64
H_RET, H_GDN, H_NA = 4, 4, 8
D_RET, D_GDN, D_NA = H_RET * HEAD_DIM, H_GDN * HEAD_DIM, H_NA * HEAD_DIM
RET_CHUNK = 128
RET_UNROLL = 4
GDN_CHUNK = 64
GDN_GROUP = 4
GDN_PREP_ROWS = 512
SHORT_CONV = 5
NA_KH, NA_KW = 8, 16
NA_QROWS = 4
NA_KROWS = NA_QROWS + NA_KH
NA_PAIRS = 1
NA_QSCALE = float(np.log2(np.e)) * HEAD_DIM ** -0.5
N_FREQ = HEAD_DIM // 4
ROPE_BASE = 10000.0
N_EXPERTS = 8
TOP_K = 2
MOE_ROWS = 512
FF_CHUNK = 256
EPS = 1e-6
NEG_INF = -1e30
VMEM_LIMIT = 56 * 1024 * 1024

C_RET = 0
C_GDN = C_RET + 4 * D_RET
C_AB = C_GDN + 4 * D_GDN
C_NA = C_AB + LANES
C_END = C_NA + 3 * D_NA


def _cparams(sem, vmem=None):
    return pltpu.CompilerParams(dimension_semantics=sem, vmem_limit_bytes=vmem)


def _silu(x):
    return x * jax.nn.sigmoid(x)


def _dot(a, b):
    return jnp.dot(a, b, preferred_element_type=F32)


def _dot_nt(a, b):
    return lax.dot_general(a, b, (((1,), (1,)), ((), ())), preferred_element_type=F32)


def _dot_tn(a, b):
    return lax.dot_general(a, b, (((0,), (0,)), ((), ())), preferred_element_type=F32)


def _split_bf16(x):
    hi = x.astype(BF16)
    return hi, (x - hi.astype(F32)).astype(BF16)


def _dot_3pass(a, b):
    ah, al = _split_bf16(a)
    bh, bl = _split_bf16(b)
    return _dot(ah, bh) + (_dot(ah, bl) + _dot(al, bh))


def _dot_hi(a, b):
    return jnp.dot(a, b, preferred_element_type=F32, precision=HIGHEST)


def _ada_body(c_ref, w_ref, b_ref, o_ref):
    s = _silu(c_ref[...])
    o_ref[0] = _dot_hi(s, w_ref[0]) + b_ref[0]


def _ada_vectors(c_all, ada_w, ada_b):
    depth, d, d6 = ada_w.shape
    rows = c_all.shape[0]
    tn = 1024
    return pl.pallas_call(
        _ada_body,
        grid=(depth, d6 // tn),
        in_specs=[pl.BlockSpec((rows, d), lambda l, j: (0, 0)),
                  pl.BlockSpec((1, d, tn), lambda l, j: (l, 0, j)),
                  pl.BlockSpec((1, 1, tn), lambda l, j: (l, 0, j))],
        out_specs=pl.BlockSpec((1, rows, tn), lambda l, j: (l, 0, j)),
        out_shape=jax.ShapeDtypeStruct((depth, rows, d6), F32),
        compiler_params=_cparams(("arbitrary", "arbitrary")),
        name="ada_vectors",
    )(c_all, ada_w, ada_b.reshape(depth, 1, d6))


def _norm_mod(x, g, shift, scale):
    ms = jnp.mean(x * x, axis=-1, keepdims=True)
    return (x * lax.rsqrt(ms + EPS) * g) * (1.0 + scale) + shift


def _rope_slab(t, cosf, sins):
    lane = lax.broadcasted_iota(jnp.int32, t.shape, 1)
    first = (lane % HEAD_DIM) < (HEAD_DIM // 2)
    partner = jnp.where(first, pltpu.roll(t, LANES - HEAD_DIM // 2, 1), pltpu.roll(t, HEAD_DIM // 2, 1))
    return t * cosf + partner * sins


def _inproj_body(x_ref, mod_ref, g_ref, cos_ref, sin_ref, w_ref, ret_ref, gdn_ref, ab_ref, abt_ref, na_ref, *, rope):
    h = _norm_mod(x_ref[0], g_ref[...], mod_ref[0, 0:1, :], mod_ref[0, 1:2, :]).astype(BF16)
    qk = _dot(h, w_ref[:, C_RET:C_RET + 2 * D_RET])
    slabs = []
    for s in range(2 * D_RET // LANES):
        t = qk[:, s * LANES:(s + 1) * LANES]
        if rope:
            t = _rope_slab(t, cos_ref[...], sin_ref[...])
        if s >= D_RET // LANES:
            t = t * HEAD_DIM ** -0.5
        slabs.append(t)
    ret_ref[0, :, 0:2 * D_RET] = jnp.concatenate(slabs, axis=1).astype(BF16)
    ret_ref[0, :, 2 * D_RET:] = _dot(h, w_ref[:, C_RET + 2 * D_RET:C_GDN]).astype(BF16)
    for j in range(2):
        gdn_ref[0, :, 512 * j:512 * (j + 1)] = _dot(h, w_ref[:, C_GDN + 512 * j:C_GDN + 512 * (j + 1)]).astype(BF16)
    ab = _dot(h, w_ref[:, C_AB:C_NA])
    ab_ref[0] = ab
    for grp in range(ab.shape[0] // LANES):
        t = ab[grp * LANES:(grp + 1) * LANES, :].T
        for p in range(D_GDN // PAIR):
            abt_ref[0, p, grp] = t[8 * p:8 * p + 8, :]
    for j in range(3):
        na_ref[0, :, 512 * j:512 * (j + 1)] = _dot(h, w_ref[:, C_NA + 512 * j:C_NA + 512 * (j + 1)]).astype(BF16)


def _in_projection(x, mod, g, cosf, sins, w, rope):
    b, l, d = x.shape
    tm = min(512, l)
    body = functools.partial(_inproj_body, rope=rope)
    return pl.pallas_call(
        body,
        grid=(b, l // tm),
        in_specs=[pl.BlockSpec((1, tm, d), lambda i, j: (i, j, 0)),
                  pl.BlockSpec((1, 6, d), lambda i, j: (i, 0, 0)),
                  pl.BlockSpec((1, d), lambda i, j: (0, 0)),
                  pl.BlockSpec((tm, LANES), lambda i, j: (j, 0)),
                  pl.BlockSpec((tm, LANES), lambda i, j: (j, 0)),
                  pl.BlockSpec((d, C_END), lambda i, j: (0, 0))],
        out_specs=[pl.BlockSpec((1, tm, 4 * D_RET), lambda i, j: (i, j, 0)),
                   pl.BlockSpec((1, tm, 4 * D_GDN), lambda i, j: (i, j, 0)),
                   pl.BlockSpec((1, tm, LANES), lambda i, j: (i, j, 0)),
                   pl.BlockSpec((1, D_GDN // PAIR, tm // LANES, 8, LANES), lambda i, j: (i, 0, j, 0, 0)),
                   pl.BlockSpec((1, tm, 3 * D_NA), lambda i, j: (i, j, 0))],
        out_shape=[jax.ShapeDtypeStruct((b, l, 4 * D_RET), BF16),
                   jax.ShapeDtypeStruct((b, l, 4 * D_GDN), BF16),
                   jax.ShapeDtypeStruct((b, l, LANES), F32),
                   jax.ShapeDtypeStruct((b, D_GDN // PAIR, l // LANES, 8, LANES), F32),
                   jax.ShapeDtypeStruct((b, l, 3 * D_NA), BF16)],
        compiler_params=_cparams(("arbitrary", "arbitrary"), VMEM_LIMIT),
        name="in_projection",
    )(x, mod, g, cosf, sins, w)


def _pack_w_in(w_in):
    d = w_in.shape[0]
    c1 = 4 * D_RET
    c2 = c1 + 4 * D_GDN
    q0 = c2 + 4 * H_GDN
    col = jnp.arange(w_in.shape[1])
    w_in = jnp.where((col >= q0) & (col < q0 + D_NA), w_in * NA_QSCALE, w_in).astype(BF16)
    ab = w_in[:, c2:c2 + 4 * H_GDN]
    ab = ab.reshape(d, 2, 2, H_GDN // 2, 2)
    ab = jnp.transpose(ab, (0, 3, 1, 2, 4)).reshape(d, 4 * H_GDN)
    ab = jnp.pad(ab, ((0, 0), (0, LANES - 4 * H_GDN)))
    return jnp.concatenate([w_in[:, :c2], ab, w_in[:, c2 + 4 * H_GDN:]], axis=1)


def _head_masks(shape):
    lane = lax.broadcasted_iota(jnp.int32, shape, len(shape) - 1)
    return lane < HEAD_DIM, lane >= HEAD_DIM


def _per_head(lo, hi, shape):
    m0, _ = _head_masks(shape)
    return jnp.where(m0, lo, hi)


def _head_sumsq(o):
    m0, m1 = _head_masks(o.shape)
    sq = o * o
    s0 = jnp.sum(jnp.where(m0, sq, 0.0), axis=-1, keepdims=True)
    s1 = jnp.sum(jnp.where(m1, sq, 0.0), axis=-1, keepdims=True)
    return jnp.where(m0, s0, s1)


def _ret_body(lg_ref, ql, kl, vl, gl, qc, kc, vc, gc, r_ref, rc_ref, sb_ref):
    c = RET_CHUNK
    npair = D_RET // PAIR
    lane_shape = (c, PAIR)
    pos = lax.broadcasted_iota(jnp.int32, lane_shape, 0).astype(F32)
    ii = lax.broadcasted_iota(jnp.int32, (c, c), 0)
    jj = lax.broadcasted_iota(jnp.int32, (c, c), 1)
    diff = (ii - jj).astype(F32)
    m0, m1 = _head_masks(lane_shape)
    masks = (m0, m1)
    bi = lax.broadcasted_iota(jnp.int32, (PAIR, PAIR), 0) // HEAD_DIM
    bj = lax.broadcasted_iota(jnp.int32, (PAIR, PAIR), 1) // HEAD_DIM
    bd = bi == bj
    cst = []
    for p in range(npair):
        lgf = _per_head(lg_ref[0, 2 * p], lg_ref[0, 2 * p + 1], lane_shape)
        lgb = _per_head(lg_ref[1, 2 * p], lg_ref[1, 2 * p + 1], lane_shape)
        dmats = [jnp.where(diff > 0, jnp.exp(lg_ref[0, 2 * p + hh] * diff),
                           jnp.where(diff < 0, jnp.exp(-lg_ref[1, 2 * p + hh] * diff), 2.0)) for hh in range(2)]
        cst.append(dict(qdf=jnp.exp(lgf * (pos + 1.0)), kdf=jnp.exp(lgf * (c - 1.0 - pos)),
                        qdb=jnp.exp(lgb * (c - pos)), kdb=jnp.exp(lgb * pos),
                        cdf=jnp.exp(lgf[0:1] * c), cdb=jnp.exp(lgb[0:1] * c), dmats=dmats))

    def sweep(q_ref, k_ref, v_ref, g_ref, o_ref, n, sf0, sb0):
        unroll = min(RET_UNROLL, n)

        def bchunk(ci, sbs):
            sl = pl.ds(pl.multiple_of(ci * c, c), c)
            out = []
            for p in range(npair):
                lanes = slice(p * PAIR, (p + 1) * PAIR)
                sb_ref[p, ci] = sbs[p]
                kd = (k_ref[0, sl, lanes].astype(F32) * cst[p]["kdb"]).astype(BF16)
                out.append(sbs[p] * cst[p]["cdb"] + jnp.where(bd, _dot_tn(kd, v_ref[0, sl, lanes]), 0.0))
            return tuple(out)

        def bstep(t, sbs):
            for u in range(unroll):
                sbs = bchunk(n - 1 - (t * unroll + u), sbs)
            return sbs

        sb_fin = lax.fori_loop(0, n // unroll, bstep, sb0)

        def fchunk(ci, sfs):
            sl = pl.ds(pl.multiple_of(ci * c, c), c)
            out = []
            for p in range(npair):
                lanes = slice(p * PAIR, (p + 1) * PAIR)
                k_ = cst[p]
                q = q_ref[0, sl, lanes]
                k = k_ref[0, sl, lanes]
                v = v_ref[0, sl, lanes]
                qf = q.astype(F32)
                qd = jnp.concatenate([(qf * k_["qdf"]).astype(BF16), (qf * k_["qdb"]).astype(BF16)], axis=1)
                o = _dot(qd, jnp.concatenate([sfs[p].astype(BF16), sb_ref[p, ci].astype(BF16)], axis=0))
                s2 = _dot_nt(_stack_heads(q), k)
                pm = jnp.concatenate([s2[:c] * k_["dmats"][0], s2[c:] * k_["dmats"][1]], axis=0).astype(BF16)
                pv = _dot(pm, v)
                o = o + jnp.where(masks[0], pv[:c], pv[c:])
                on = o * lax.rsqrt(_head_sumsq(o) * (1.0 / HEAD_DIM) + EPS)
                o_ref[0, sl, lanes] = (on * _silu(g_ref[0, sl, lanes].astype(F32))).astype(o_ref.dtype)
                kd = (k.astype(F32) * k_["kdf"]).astype(BF16)
                out.append(sfs[p] * k_["cdf"] + jnp.where(bd, _dot_tn(kd, v), 0.0))
            return tuple(out)

        def fstep(t, sfs):
            for u in range(unroll):
                sfs = fchunk(t * unroll + u, sfs)
            return sfs

        sf_fin = lax.fori_loop(0, n // unroll, fstep, sf0)
        return sf_fin, sb_fin

    z = tuple(jnp.zeros((PAIR, PAIR), F32) for _ in range(npair))
    sfc, sbc = sweep(qc, kc, vc, gc, rc_ref, qc.shape[1] // c, z, z)
    sweep(ql, kl, vl, gl, r_ref, ql.shape[1] // c, sfc, sbc)


def _retention(ret_l, ret_c, log_gamma):
    b, l, _ = ret_l.shape
    lc = ret_c.shape[1]
    npair = D_RET // PAIR

    def col(k):
        return lambda i: (i, 0, k)

    lat = [pl.BlockSpec((1, l, D_RET), col(k)) for k in range(4)]
    ctx = [pl.BlockSpec((1, lc, D_RET), col(k)) for k in range(4)]
    return pl.pallas_call(
        _ret_body,
        grid=(b,),
        in_specs=[pl.BlockSpec(memory_space=pltpu.SMEM)] + lat + ctx,
        out_specs=[pl.BlockSpec((1, l, D_RET), lambda i: (i, 0, 0)),
                   pl.BlockSpec((1, lc, D_RET), lambda i: (i, 0, 0))],
        out_shape=[jax.ShapeDtypeStruct((b, l, D_RET), BF16), jax.ShapeDtypeStruct((b, lc, D_RET), BF16)],
        scratch_shapes=[pltpu.VMEM((npair, max(l, lc) // RET_CHUNK, PAIR, PAIR), F32)],
        compiler_params=_cparams(("arbitrary",), VMEM_LIMIT),
        name="retention",
    )(log_gamma, ret_l, ret_l, ret_l, ret_l, ret_c, ret_c, ret_c, ret_c)


def _stack_heads(t):
    m0, m1 = _head_masks(t.shape)
    z = jnp.zeros_like(t)
    return jnp.concatenate([jnp.where(m0, t, z), jnp.where(m1, t, z)], axis=0)


def _scan_sum(x, axis, n, reverse):
    size = x.shape[axis]
    pos = lax.broadcasted_iota(jnp.int32, x.shape, axis)
    s = 1
    while s < n:
        if reverse:
            x = x + jnp.where(pos + s < n, pltpu.roll(x, size - s, axis), 0.0)
        else:
            x = x + jnp.where(pos >= s, pltpu.roll(x, s, axis), 0.0)
        s *= 2
    return x


def _gdn_prep(src_ref, col, cw, dst_ref, r0, *, l2, scale):
    n = src_ref.shape[1]
    blk = min(GDN_PREP_ROWS, n)
    halo = 16
    lanes = slice(col % D_GDN, col % D_GDN + LANES)

    def block(i, _):
        b0 = pl.multiple_of(i * blk, blk)
        x = src_ref[0, pl.ds(b0, blk), col:col + LANES].astype(F32)
        lo = src_ref[0, pl.ds(pl.multiple_of(jnp.maximum(b0 - halo, 0), halo), halo), col:col + LANES].astype(F32)
        hi = src_ref[0, pl.ds(pl.multiple_of(jnp.minimum(b0 + blk, n - halo), halo), halo), col:col + LANES].astype(F32)
        lo = jnp.where(b0 > 0, lo, 0.0)
        hi = jnp.where(b0 + blk < n, hi, 0.0)
        ext = jnp.concatenate([lo, x, hi], axis=0)
        acc = x * cw[SHORT_CONV // 2:SHORT_CONV // 2 + 1, :]
        for j in range(SHORT_CONV):
            s = j - SHORT_CONV // 2
            if s != 0:
                sh = pltpu.roll(ext, (-s) % (blk + 2 * halo), 0)
                acc = acc + sh[halo:halo + blk] * cw[j:j + 1, :]
        y = _silu(acc)
        if l2:
            y = y * lax.rsqrt(_head_sumsq(y) + EPS)
        if scale != 1.0:
            y = y * scale
        dst_ref[pl.ds(pl.multiple_of(r0 + b0, halo), blk), lanes] = y.astype(dst_ref.dtype)
        return 0

    lax.fori_loop(0, n // blk, block, 0)


def _gdn_body(x_ref, xc_ref, ab_ref, abt_ref, cw_ref, cst_ref, cstt_ref, ng_ref,
              o_ref, oc_ref, qn, kn, vn, oacc, st_s, x_s, t_s, rhs_s, u_s, wq_s, ak_s, egl_s):
    c = GDN_CHUNK
    c2 = 2 * c
    grp = GDN_GROUP
    npair = D_GDN // PAIR
    lc = xc_ref.shape[1]
    l = x_ref.shape[1]
    nc = lc // c
    nt = (lc + l) // c
    ii = lax.broadcasted_iota(jnp.int32, (c2, c2), 0)
    jj = lax.broadcasted_iota(jnp.int32, (c2, c2), 1)
    same = (ii // c) == (jj // c)
    eye = (ii == jj).astype(F32)
    dirs = ((same & (ii >= jj), same & (ii > jj), c - 1),
            (same & (ii <= jj), same & (ii < jj), 0))

    def chains(g, slot):
        out = []
        for gi in range(grp):
            t = g * grp + gi
            cf = t
            cb = jnp.where(t < nc, nc - 1 - t, nt + nc - 1 - t)
            for p in range(npair):
                for d in range(2):
                    out.append((((slot * grp + gi) * 2 + d) * npair + p, p, d, cf if d == 0 else cb))
        return out

    def stage_inputs(idx, p, d, ci):
        incl, strict, last = dirs[d]
        lanes = slice(p * PAIR, (p + 1) * PAIR)
        nea, dtb = cst_ref[0:1, :], cst_ref[1:2, :]
        neat, dtbt = cstt_ref[p, 0], cstt_ref[p, 1]
        sl = pl.ds(pl.multiple_of(ci * c, c), c)
        abv = ab_ref[0, sl, :]
        gall = nea * jax.nn.softplus(abv + dtb)
        ball = jax.nn.sigmoid(abv)
        shape = (c, PAIR)
        c0 = 8 * p + 2 * d
        gl = _per_head(gall[:, c0:c0 + 1], gall[:, c0 + 1:c0 + 2], shape)
        bl = _per_head(ball[:, c0 + 4:c0 + 5], ball[:, c0 + 5:c0 + 6], shape)
        gcum = _scan_sum(gl, 0, c, d == 1)
        abt = abt_ref[0, p, ci // 2]
        abt = jnp.where(ci % 2 == 1, pltpu.roll(abt, c, 1), abt)
        gt = neat * jax.nn.softplus(abt + dtbt)
        gtc = _scan_sum(gt, 1, c, d == 1)
        grow = jnp.concatenate([gtc[2 * d:2 * d + 1, 0:c], gtc[2 * d + 1:2 * d + 2, 0:c]], axis=1)
        gcol = jnp.concatenate([gcum[:, 0:1], gcum[:, HEAD_DIM:HEAD_DIM + 1]], axis=0)
        dec = jnp.where(incl, jnp.exp(jnp.where(incl, gcol - grow, 0.0)), 0.0)
        dec = jnp.where(ii == jj, 1.0, dec)
        q = qn[sl, lanes]
        k = kn[sl, lanes]
        v = vn[sl, lanes]
        kf = k.astype(F32)
        eg = jnp.exp(gcum)
        kb = kf * bl
        k_st = _stack_heads(k)
        kq = _dot_nt(jnp.concatenate([_stack_heads(kb.astype(BF16)), _stack_heads(q)], axis=0), k_st)
        a = jnp.where(strict, kq[:c2] * dec, 0.0)
        attn = (kq[c2:] * dec).astype(BF16)
        x_s[idx] = (-a).astype(BF16)
        t_s[idx] = eye - a
        rhs_s[idx] = jnp.concatenate([_stack_heads((v.astype(F32) * bl).astype(BF16)),
                                      _stack_heads((kb * eg).astype(BF16))], axis=1)
        glast = gcum[last:last + 1, :]
        wq_s[idx, c2:, :] = _stack_heads((q.astype(F32) * eg).astype(BF16))
        ak_s[idx] = jnp.concatenate([attn, _stack_heads(kf * jnp.exp(glast - gcum)).T.astype(BF16)], axis=0)
        egl_s[idx] = jnp.broadcast_to(jnp.exp(glast), (8, PAIR))

    def stage_group(g, slot):
        todo = chains(g, slot)
        for idx, p, d, ci in todo:
            stage_inputs(idx, p, d, ci)
        for idx, _, _, _ in todo:
            xb = x_s[idx]
            x_s[idx] = _dot(xb, xb).astype(BF16)
        for _ in range(4):
            for idx, _, _, _ in todo:
                xb = x_s[idx]
                t = t_s[idx]
                prod = _dot(jnp.concatenate([t.astype(BF16), xb], axis=0), xb)
                t_s[idx] = t + prod[:c2]
                x_s[idx] = prod[c2:].astype(BF16)
        for idx, _, _, _ in todo:
            t = t_s[idx]
            t = t + _dot(t.astype(BF16), x_s[idx])
            sol = _dot(t.astype(BF16), rhs_s[idx])
            u_s[idx] = sol[:, :PAIR]
            wq_s[idx, :c2, :] = sol[:, PAIR:].astype(BF16)

    def recur_group(g, slot):
        sts = {(p, d): st_s[d * npair + p] for p in range(npair) for d in range(2)}
        for idx, p, d, ci in chains(g, slot):
            sl = pl.ds(pl.multiple_of(ci * c, c), c)
            st = sts[(p, d)]
            ws = _dot(wq_s[idx], st.astype(BF16))
            v_new = (u_s[idx] - ws[:c2]).astype(BF16)
            av = _dot(ak_s[idx], v_new)
            o_st = ws[c2:] + av[:c2]
            oacc[sl, p * PAIR:(p + 1) * PAIR] += o_st[:c] + o_st[c:]
            sts[(p, d)] = st * egl_s[idx][0:1, :] + av[c2:]
        for (p, d), st in sts.items():
            st_s[d * npair + p] = st

    def prep(src_ref, r0):
        for s in range(D_GDN // LANES):
            for k, (dst, l2, scale) in enumerate(((qn, True, HEAD_DIM ** -0.5), (kn, True, 1.0), (vn, False, 1.0))):
                col = k * D_GDN + s * LANES
                _gdn_prep(src_ref, col, cw_ref[:, col:col + LANES], dst, r0, l2=l2, scale=scale)

    def finish(src_ref, out_ref, r0, n):
        for p in range(npair):
            lanes = slice(p * PAIR, (p + 1) * PAIR)
            o = oacc[r0:r0 + n, lanes]
            on = o * lax.rsqrt(_head_sumsq(o) * (1.0 / HEAD_DIM) + EPS)
            gate = src_ref[0, :, 3 * D_GDN + p * PAIR:3 * D_GDN + (p + 1) * PAIR].astype(F32)
            out_ref[0, :, lanes] = (on * ng_ref[...] * _silu(gate)).astype(out_ref.dtype)

    st_s[...] = jnp.zeros_like(st_s)
    oacc[...] = jnp.zeros_like(oacc)
    prep(xc_ref, 0)
    prep(x_ref, lc)

    ng = nt // grp
    stage_group(0, 0)

    def two_groups(kk, _):
        g = 2 * kk
        recur_group(g, 0)
        stage_group(g + 1, 1)
        recur_group(g + 1, 1)
        stage_group(g + 2, 0)
        return 0

    lax.fori_loop(0, (ng - 1) // 2, two_groups, 0)
    if ng % 2 == 0:
        recur_group(ng - 2, 0)
        stage_group(ng - 1, 1)
        recur_group(ng - 1, 1)
    else:
        recur_group(ng - 1, 0)
    finish(xc_ref, oc_ref, 0, lc)
    finish(x_ref, o_ref, lc, l)


def _gdn_tables(conv_w, a_log, dt_bias, norm_g):
    npair = D_GDN // PAIR
    nea = -jnp.exp(a_log.astype(F32)).reshape(2, npair, 2)
    dtb = dt_bias.astype(F32).reshape(2, npair, 2)
    rows = jnp.stack([jnp.transpose(nea, (1, 0, 2)).reshape(npair, 4), jnp.transpose(dtb, (1, 0, 2)).reshape(npair, 4)], axis=1)
    cst = jnp.pad(jnp.transpose(jnp.pad(rows, ((0, 0), (0, 0), (0, 4))), (1, 0, 2)).reshape(2, 8 * npair),
                  ((0, 6), (0, LANES - 8 * npair)))
    cstt = jnp.broadcast_to(jnp.pad(rows, ((0, 0), (0, 0), (0, 4)))[..., None], (npair, 2, 8, LANES))
    cw = jnp.pad(conv_w.astype(F32), ((0, 8 - SHORT_CONV), (0, 0)))
    ng = jnp.tile(norm_g.astype(F32), 2).reshape(1, PAIR)
    return cw, cst, cstt, ng


def _gdn(gdn_l, ab_l, abt_l, gdn_c, ab_c, abt_c, tables):
    b, l, _ = gdn_l.shape
    lc = gdn_c.shape[1]
    npair = D_GDN // PAIR
    lt = l + lc
    assert (lt // GDN_CHUNK) % GDN_GROUP == 0 and lc % LANES == 0
    cw, cst, cstt, ng = tables

    nstage = 2 * GDN_GROUP * 2 * npair

    def whole(shape):
        nd = len(shape)
        return pl.BlockSpec(shape, lambda i: (0,) * nd)

    def per_batch(shape, single=False):
        nd = len(shape)
        return pl.BlockSpec((1,) + shape, lambda i: (i,) + (0,) * nd, pipeline_mode=pl.Buffered(1) if single else None)

    sq = (nstage, PAIR, PAIR)
    return pl.pallas_call(
        _gdn_body,
        grid=(b,),
        in_specs=[per_batch((l, 4 * D_GDN), True), per_batch((lc, 4 * D_GDN)), per_batch((lt, LANES), True),
                  per_batch((npair, lt // LANES, 8, LANES)),
                  whole((8, 3 * D_GDN)), whole((8, LANES)), whole((npair, 2, 8, LANES)), whole((1, PAIR))],
        out_specs=[per_batch((l, D_GDN)), per_batch((lc, D_GDN))],
        out_shape=[jax.ShapeDtypeStruct((b, l, D_GDN), BF16), jax.ShapeDtypeStruct((b, lc, D_GDN), BF16)],
        scratch_shapes=[pltpu.VMEM((lt, D_GDN), BF16)] * 3 + [pltpu.VMEM((lt, D_GDN), F32)]
        + [pltpu.VMEM((2 * npair, PAIR, PAIR), F32), pltpu.VMEM(sq, BF16), pltpu.VMEM(sq, F32),
           pltpu.VMEM((nstage, PAIR, 2 * PAIR), BF16), pltpu.VMEM(sq, F32),
           pltpu.VMEM((nstage, 2 * PAIR, PAIR), BF16), pltpu.VMEM((nstage, 2 * PAIR, PAIR), BF16),
           pltpu.VMEM((nstage, 8, PAIR), F32)],
        compiler_params=_cparams(("arbitrary",), VMEM_LIMIT),
        name="gated_deltanet",
    )(gdn_l, gdn_c, jnp.concatenate([ab_c, ab_l], axis=1), jnp.concatenate([abt_c, abt_l], axis=2), cw, cst, cstt, ng)


def _na_bias_tiles(rpb, rows):
    w = GRID_W
    ext = jnp.pad(rpb.astype(F32) * float(np.log2(np.e)), ((0, 0), (0, 0), (w - NA_KW, w - NA_KW)))
    cq = np.arange(w)[:, None]
    ck = np.arange(w)[None, :]
    tb = ext[:, :, jnp.asarray(ck - cq + w - 1)]
    ws = np.clip(cq - NA_KW // 2, 0, w - NA_KW)
    col_ok = (ck >= ws) & (ck < ws + NA_KW)
    tb = jnp.where(jnp.asarray(col_ok)[None, None], tb, NEG_INF)
    none = 2 * NA_KH - 1
    tb = jnp.concatenate([tb, jnp.full((tb.shape[0], 1, w, w), NEG_INF, F32)], axis=1)
    nt = rows // NA_QROWS
    idx = np.zeros((3, NA_QROWS, NA_KROWS), np.int32)
    for cls, t in enumerate((0, 1, nt - 1)):
        ks = int(np.clip(NA_QROWS * t - NA_KH // 2, 0, rows - NA_KROWS))
        for rl in range(NA_QROWS):
            r = NA_QROWS * t + rl
            r0 = int(np.clip(r - NA_KH // 2, 0, rows - NA_KH))
            for j in range(NA_KROWS):
                kr = ks + j
                idx[cls, rl, j] = kr - r + NA_KH - 1 if r0 <= kr < r0 + NA_KH else none
    pairs = idx.reshape(3, NA_QROWS, NA_KROWS // 2, 2)
    uniq, inv = np.unique(pairs.reshape(-1, 2), axis=0, return_inverse=True)
    wide = jnp.concatenate([tb[:, jnp.asarray(uniq[:, 0])], tb[:, jnp.asarray(uniq[:, 1])]], axis=-1)
    tiles = wide[:, jnp.asarray(inv.reshape(3, NA_QROWS, NA_KROWS // 2))]
    tiles = jnp.transpose(tiles, (1, 0, 3, 2, 4, 5))
    return tiles.reshape(3, tb.shape[0], NA_KROWS // 2, NA_QROWS * w, 2 * w)


def _softmax_pv(s_parts, v_parts):
    m = functools.reduce(jnp.maximum, [jnp.max(s, axis=-1, keepdims=True) for s in s_parts])
    ps = [jnp.exp2(s - m) for s in s_parts]
    den = functools.reduce(lambda x, y: x + y, [jnp.sum(p, axis=-1, keepdims=True) for p in ps])
    o = _dot(jnp.concatenate([p.astype(BF16) for p in ps], axis=1), jnp.concatenate(v_parts, axis=0))
    return o / den


def _na_body(q_ref, k_ref, v_ref, qc_ref, kc_ref, vc_ref, bias_ref, o_ref, oc_ref, *, rows):
    w = GRID_W
    tq = NA_QROWS * w
    tk = NA_KROWS * w
    nt = rows // NA_QROWS
    m0, m1 = _head_masks((tq, PAIR))
    masks = (m0, m1)

    def tile(t, _):
        cls = jnp.where(t > 0, 1, 0) + jnp.where(t == nt - 1, 1, 0)
        ks = jnp.clip(NA_QROWS * t - NA_KH // 2, 0, rows - NA_KROWS) * w
        ksl = pl.ds(pl.multiple_of(ks, w), tk)
        qsl = pl.ds(pl.multiple_of(t * tq, tq), tq)
        for p in range(NA_PAIRS):
            lanes = slice(p * PAIR, (p + 1) * PAIR)
            q = q_ref[0, qsl, lanes]
            kb = k_ref[0, ksl, lanes]
            vb = v_ref[0, ksl, lanes]
            kc = kc_ref[0, :, lanes]
            vc = vc_ref[0, :, lanes]
            s_all = _dot_nt(_stack_heads(q), jnp.concatenate([kb, kc], axis=0))
            v_all = jnp.concatenate([vb, vc], axis=0)
            acc = jnp.zeros((tq, PAIR), F32)
            for hh in range(2):
                bias = jnp.concatenate([bias_ref[0, cls, 2 * p + hh, s] for s in range(NA_KROWS // 2)]
                                       + [jnp.zeros((tq, kc.shape[0]), F32)], axis=1)
                o_h = _softmax_pv([s_all[hh * tq:(hh + 1) * tq] + bias], [v_all])
                acc = acc + jnp.where(masks[hh], o_h, 0.0)
            o_ref[0, qsl, lanes] = acc.astype(o_ref.dtype)
        return 0

    lax.fori_loop(0, nt, tile, 0)
    for p in range(NA_PAIRS):
        lanes = slice(p * PAIR, (p + 1) * PAIR)
        qc = qc_ref[0, :, lanes]
        kc = kc_ref[0, :, lanes]
        vc = vc_ref[0, :, lanes]
        mc0, mc1 = _head_masks(qc.shape)
        accc = jnp.zeros(qc.shape, F32)
        for mk in (mc0, mc1):
            qm = jnp.where(mk, qc, jnp.zeros_like(qc))
            accc = accc + jnp.where(mk, _softmax_pv([_dot_nt(qm, kc)], [vc]), 0.0)
        oc_ref[0, :, lanes] = accc.astype(oc_ref.dtype)


def _neighbourhood_attention(na_l, na_c, bias, layer):
    b, l, _ = na_l.shape
    lc = na_c.shape[1]
    rows = l // GRID_W
    wide = NA_PAIRS * PAIR
    ngrp = D_NA // wide

    def col(k):
        return lambda p, i: (i, 0, k * ngrp + p)

    lat = [pl.BlockSpec((1, l, wide), col(k)) for k in range(3)]
    ctx = [pl.BlockSpec((1, lc, wide), col(k)) for k in range(3)]
    return pl.pallas_call(
        functools.partial(_na_body, rows=rows),
        grid=(ngrp, b),
        in_specs=lat + ctx + [pl.BlockSpec((1, 3, 2 * NA_PAIRS, NA_KROWS // 2, NA_QROWS * GRID_W, 2 * GRID_W),
                                           lambda p, i: (layer, 0, p, 0, 0, 0))],
        out_specs=[pl.BlockSpec((1, l, wide), lambda p, i: (i, 0, p)),
                   pl.BlockSpec((1, lc, wide), lambda p, i: (i, 0, p))],
        out_shape=[jax.ShapeDtypeStruct((b, l, D_NA), BF16), jax.ShapeDtypeStruct((b, lc, D_NA), BF16)],
        compiler_params=_cparams(("arbitrary", "arbitrary"), VMEM_LIMIT),
        name="neighbourhood_attention",
    )(na_l, na_l, na_l, na_c, na_c, na_c, bias)


def _outproj_residual(r_ref, g_ref, n_ref, x_ref, mod_ref, wo_ref):
    mix = jnp.concatenate([r_ref[0], g_ref[0], n_ref[0]], axis=1)
    return x_ref[0] + mod_ref[0, 2:3, :] * _dot(mix, wo_ref[...])


def _swiglu_chunk(hb, w1_ref, w3_ref, w2_ref, j):
    cols = slice(j * FF_CHUNK, (j + 1) * FF_CHUNK)
    t = (_silu(_dot(hb, w1_ref[:, cols])) * _dot(hb, w3_ref[:, cols])).astype(BF16)
    return _dot(t, w2_ref[cols, :])


def _swiglu_chunks(hb, w1_ref, w3_ref, w2_ref, acc_ref):
    for j in range(w2_ref.shape[0] // FF_CHUNK):
        if j == 0:
            acc_ref[...] = _swiglu_chunk(hb, w1_ref, w3_ref, w2_ref, j)
        else:
            acc_ref[...] += _swiglu_chunk(hb, w1_ref, w3_ref, w2_ref, j)


def _dense_body(r_ref, g_ref, n_ref, x_ref, mod_ref, g2_ref, wo_ref, w1_ref, w3_ref, w2_ref, o_ref, acc_ref):
    x1 = _outproj_residual(r_ref, g_ref, n_ref, x_ref, mod_ref, wo_ref)
    hb = _norm_mod(x1, g2_ref[...], mod_ref[0, 3:4, :], mod_ref[0, 4:5, :]).astype(BF16)
    _swiglu_chunks(hb, w1_ref, w3_ref, w2_ref, acc_ref)
    o_ref[0] = x1 + mod_ref[0, 5:6, :] * acc_ref[...]


def _pack_bf16_pairs(h):
    m = h.shape[1] // 2
    bits = lax.bitcast_convert_type(h.astype(BF16).astype(F32), jnp.uint32)
    return (bits[:, :m] >> 16) | (bits[:, m:] & jnp.uint32(0xFFFF0000))


def _unpack_bf16_pairs(u):
    lo = lax.bitcast_convert_type(u << 16, F32)
    hi = lax.bitcast_convert_type(u & jnp.uint32(0xFFFF0000), F32)
    return jnp.concatenate([lo, hi], axis=1).astype(BF16)


def _router_body(r_ref, g_ref, n_ref, x_ref, mod_ref, g2_ref, wo_ref, wr_ref, x1_ref, h_ref, lg_ref):
    x1 = _outproj_residual(r_ref, g_ref, n_ref, x_ref, mod_ref, wo_ref)
    h = _norm_mod(x1, g2_ref[...], mod_ref[0, 3:4, :], mod_ref[0, 4:5, :])
    x1_ref[0] = x1
    h_ref[0] = _pack_bf16_pairs(h)
    lg_ref[0] = _dot_3pass(h, wr_ref[...])


def _resident(shape):
    nd = len(shape)
    return pl.BlockSpec(shape, lambda i, j: (0,) * nd, pipeline_mode=pl.Buffered(1))


def _mixer_specs(tm, d):
    return [pl.BlockSpec((1, tm, D_RET), lambda i, j: (i, j, 0)),
            pl.BlockSpec((1, tm, D_GDN), lambda i, j: (i, j, 0)),
            pl.BlockSpec((1, tm, D_NA), lambda i, j: (i, j, 0)),
            pl.BlockSpec((1, tm, d), lambda i, j: (i, j, 0)),
            pl.BlockSpec((1, 6, d), lambda i, j: (i, 0, 0)),
            pl.BlockSpec((1, d), lambda i, j: (0, 0))]


def _dense_block(r, g, n, x, mod, g2, wo, w1, w3, w2):
    b, l, d = x.shape
    tm = min(512, l)
    dff = w1.shape[1]
    return pl.pallas_call(
        _dense_body,
        grid=(b, l // tm),
        in_specs=_mixer_specs(tm, d) + [_resident((d, d)), _resident((d, dff)), _resident((d, dff)), _resident((dff, d))],
        out_specs=pl.BlockSpec((1, tm, d), lambda i, j: (i, j, 0)),
        out_shape=jax.ShapeDtypeStruct((b, l, d), F32),
        scratch_shapes=[pltpu.VMEM((tm, d), F32)],
        compiler_params=_cparams(("arbitrary", "arbitrary"), VMEM_LIMIT),
        name="outproj_swiglu",
    )(r, g, n, x, mod, g2, wo, w1, w3, w2)


def _router_block(r, g, n, x, mod, g2, wo, wr):
    b, l, d = x.shape
    tm = min(512, l)
    blk = pl.BlockSpec((1, tm, d), lambda i, j: (i, j, 0))
    return pl.pallas_call(
        _router_body,
        grid=(b, l // tm),
        in_specs=_mixer_specs(tm, d) + [_resident((d, d)), _resident((d, LANES))],
        out_specs=[blk, pl.BlockSpec((1, tm, d // 2), lambda i, j: (i, j, 0)), pl.BlockSpec((1, tm, LANES), lambda i, j: (i, j, 0))],
        out_shape=[jax.ShapeDtypeStruct((b, l, d), F32), jax.ShapeDtypeStruct((b, l, d // 2), jnp.uint32),
                   jax.ShapeDtypeStruct((b, l, LANES), F32)],
        compiler_params=_cparams(("arbitrary", "arbitrary"), VMEM_LIMIT),
        name="outproj_router",
    )(r, g, n, x, mod, g2, wo, wr)


def _row_copy_in(idx_ref, r, src_hbm, dst_ref, sem):
    return pltpu.make_async_copy(src_hbm.at[pl.ds(idx_ref[0, 0, r], 1), :], dst_ref.at[pl.ds(r, 1), :], sem)


def _row_copy_out(idx_ref, r, src_ref, dst_hbm, sem):
    return pltpu.make_async_copy(src_ref.at[pl.ds(r, 1), :], dst_hbm.at[pl.ds(idx_ref[0, 0, r], 1), :], sem)


def _moe_body(bval_ref, bexp_ref, tok_ref, tokn_ref, slotp_ref, h_hbm, w1_ref, w3_ref, w2_ref, y_hbm,
              xbuf, ybuf, gsem, ssem, *, n_asg):
    i = pl.program_id(0)
    tb = xbuf.shape[1]
    nch = w2_ref.shape[1] // FF_CHUNK
    cur = i % 2
    per = -(-tb // nch)
    valid = bval_ref[i] == 1
    prev_valid = bval_ref[jnp.maximum(i - 1, 0)] == 1

    def wait_gather(slot):
        pltpu.make_async_copy(h_hbm.at[pl.ds(0, tb), :], xbuf.at[slot], gsem).wait()

    def wait_scatter(slot):
        pltpu.make_async_copy(ybuf.at[slot], y_hbm.at[pl.ds(0, tb), :], ssem).wait()

    @pl.when(i == 0)
    def _():
        ybuf[1] = jnp.zeros((tb, ybuf.shape[2]), F32)
        fills = [pltpu.make_async_copy(ybuf.at[1], y_hbm.at[pl.ds(r0, tb), :], ssem)
                 for r0 in range(n_asg, y_hbm.shape[0], tb)]
        for f in fills:
            f.start()
        for f in fills:
            f.wait()

        def issue(r, _):
            _row_copy_in(tok_ref, r, h_hbm, xbuf.at[0], gsem).start()
            return 0

        lax.fori_loop(0, tb, issue, 0)

    @pl.when(valid)
    def _():
        wait_gather(cur)
        hb = _unpack_bf16_pairs(xbuf[cur])
        acc = ybuf.at[cur]
        prv = ybuf.at[1 - cur]
        nxt = xbuf.at[1 - cur]
        for j in range(nch):
            for r in range(j * per, min(tb, (j + 1) * per)):
                _row_copy_in(tokn_ref, r, h_hbm, nxt, gsem).start()
                _row_copy_out(slotp_ref, r, prv, y_hbm, ssem).start()
            part = _swiglu_chunk(hb, w1_ref.at[0], w3_ref.at[0], w2_ref.at[0], j)
            if j == 0:
                acc[...] = part
            else:
                acc[...] += part
        wait_scatter(1 - cur)

    @pl.when(jnp.logical_not(valid) & prev_valid & (i > 0))
    def _():
        wait_gather(cur)

        def issue(r, _):
            _row_copy_out(slotp_ref, r, ybuf.at[1 - cur], y_hbm, ssem).start()
            return 0

        lax.fori_loop(0, tb, issue, 0)
        wait_scatter(1 - cur)


def _moe_experts(h_rows, row_tok, row_slot, block_expert, block_valid, w1, w3, w2, n_out_rows, n_spare):
    d = w2.shape[-1]
    dff = w2.shape[1]
    nb = block_expert.shape[0]
    tb = MOE_ROWS
    smem = functools.partial(pl.BlockSpec, (1, 1, tb), memory_space=pltpu.SMEM)
    grid_spec = pltpu.PrefetchScalarGridSpec(
        num_scalar_prefetch=2,
        grid=(nb,),
        in_specs=[smem(lambda i, bv, be: (i, 0, 0)),
                  smem(lambda i, bv, be: (jnp.minimum(i + 1, nb - 1), 0, 0)),
                  smem(lambda i, bv, be: (i, 0, 0)),
                  pl.BlockSpec(memory_space=pl.ANY),
                  pl.BlockSpec((1, d, dff), lambda i, bv, be: (be[i], 0, 0)),
                  pl.BlockSpec((1, d, dff), lambda i, bv, be: (be[i], 0, 0)),
                  pl.BlockSpec((1, dff, d), lambda i, bv, be: (be[i], 0, 0))],
        out_specs=pl.BlockSpec(memory_space=pl.ANY),
        scratch_shapes=[pltpu.VMEM((2, tb, d // 2), jnp.uint32), pltpu.VMEM((2, tb, d), F32),
                        pltpu.SemaphoreType.DMA(()), pltpu.SemaphoreType.DMA(())],
    )
    return pl.pallas_call(
        functools.partial(_moe_body, n_asg=n_out_rows - n_spare),
        grid_spec=grid_spec,
        out_shape=jax.ShapeDtypeStruct((n_out_rows, d), F32),
        compiler_params=_cparams(("arbitrary",), VMEM_LIMIT),
        name="moe_experts",
    )(block_valid, block_expert, row_tok.reshape(nb, 1, tb), row_tok.reshape(nb, 1, tb), row_slot.reshape(nb + 1, 1, tb),
      h_rows, w1, w3, w2)


def _combine_body(y0_ref, y1_ref, x1_ref, gate_ref, gm_ref, fg_ref, o_ref, *, final):
    moe = gate_ref[:, 0:1] * y0_ref[...] + gate_ref[:, 1:2] * y1_ref[...]
    x2 = x1_ref[...] + gm_ref[0] * moe
    if final:
        ms = jnp.mean(x2 * x2, axis=-1, keepdims=True)
        x2 = x2 * lax.rsqrt(ms + EPS) * fg_ref[...]
    o_ref[...] = x2


def _moe_combine(y_rows, x1, gates, gate_mlp, final_g, final):
    b, l, d = x1.shape
    n_tok = b * l
    tm = 256
    per_b = l // tm
    out = pl.pallas_call(
        functools.partial(_combine_body, final=final),
        grid=(n_tok // tm,),
        in_specs=[pl.BlockSpec((tm, d), lambda i: (i, 0)),
                  pl.BlockSpec((tm, d), lambda i: (n_tok // tm + i, 0)),
                  pl.BlockSpec((tm, d), lambda i: (i, 0)),
                  pl.BlockSpec((tm, 2), lambda i: (i, 0)),
                  pl.BlockSpec((1, 1, d), lambda i: (i // per_b, 0, 0)),
                  pl.BlockSpec((1, d), lambda i: (0, 0))],
        out_specs=pl.BlockSpec((tm, d), lambda i: (i, 0)),
        out_shape=jax.ShapeDtypeStruct((n_tok, d), F32),
        compiler_params=_cparams(("arbitrary",), VMEM_LIMIT),
        name="moe_combine_final_norm",
    )(y_rows, y_rows, x1.reshape(n_tok, d), gates, gate_mlp, final_g.reshape(1, d))
    return out.reshape(b, l, d)


def _route(logits):
    n_tok = logits.shape[0]
    tb = MOE_ROWS
    n_asg = n_tok * TOP_K
    top_logit, top_e = lax.top_k(logits[:, :N_EXPERTS], TOP_K)
    gates = jax.nn.softmax(top_logit, axis=-1)
    e_flat = top_e.reshape(-1).astype(jnp.int32)
    asg = jnp.arange(n_asg, dtype=jnp.int32)
    by_expert = lax.sort(e_flat * n_asg + asg)
    counts = jnp.sum((e_flat[:, None] == jnp.arange(N_EXPERTS, dtype=jnp.int32)[None, :]).astype(jnp.int32), axis=0)
    starts = jnp.cumsum(counts) - counts
    padded = (counts + tb - 1) // tb * tb
    pad_ends = jnp.cumsum(padded)
    pad_starts = pad_ends - padded
    nb = (n_asg + tb - 1) // tb + N_EXPERTS + 1
    n_rows = nb * tb
    block_start = jnp.arange(nb, dtype=jnp.int32) * tb
    block_expert = jnp.minimum(jnp.sum((block_start[:, None] >= pad_ends[None, :]).astype(jnp.int32), axis=1), N_EXPERTS - 1)
    block_valid = (block_start < pad_ends[-1]).astype(jnp.int32)
    row = jnp.arange(n_rows, dtype=jnp.int32)
    row_e = jnp.repeat(block_expert, tb)
    off = row - pad_starts[row_e]
    is_pad = (off >= counts[row_e]) | (jnp.repeat(block_valid, tb) == 0)
    src = by_expert[jnp.clip(starts[row_e] + off, 0, n_asg - 1)] - row_e * n_asg
    row_asg = jnp.where(is_pad, -1, src)
    spare = tb + n_asg + jnp.cumsum(is_pad.astype(jnp.int32)) - 1
    row_slot = jnp.where(is_pad, spare, (row_asg % TOP_K) * n_tok + row_asg // TOP_K)
    row_tok = jnp.where(is_pad, 0, row_asg // TOP_K)
    row_slot = jnp.concatenate([n_asg + jnp.arange(tb, dtype=jnp.int32), row_slot])
    last_e = block_expert[jnp.maximum(pad_ends[-1] // tb - 1, 0)]
    block_expert = jnp.where(block_valid == 1, block_expert, last_e)
    n_spare = (tb + n_rows - n_asg + tb - 1) // tb * tb
    return row_tok, row_slot, gates, block_expert, block_valid, n_asg + n_spare, n_spare


def _final_norm_body(x_ref, g_ref, o_ref):
    x = x_ref[...]
    ms = jnp.mean(x * x, axis=-1, keepdims=True)
    o_ref[...] = x * lax.rsqrt(ms + EPS) * g_ref[...]


def _final_norm(x, g):
    b, l, d = x.shape
    n = b * l
    tm = min(512, n)
    out = pl.pallas_call(
        _final_norm_body,
        grid=(n // tm,),
        in_specs=[pl.BlockSpec((tm, d), lambda i: (i, 0)), pl.BlockSpec((1, d), lambda i: (0, 0))],
        out_specs=pl.BlockSpec((tm, d), lambda i: (i, 0)),
        out_shape=jax.ShapeDtypeStruct((n, d), F32),
        compiler_params=_cparams(("arbitrary",)),
        name="final_norm",
    )(x.reshape(n, d), g.reshape(1, d))
    return out.reshape(b, l, d)


def _rope_tables(n_tok):
    t = jnp.arange(n_tok, dtype=jnp.int32)
    row = (t // GRID_W).astype(F32)
    col = (t % GRID_W).astype(F32)
    inv_freq = ROPE_BASE ** (-jnp.arange(N_FREQ, dtype=F32) / N_FREQ)
    ang = jnp.concatenate([row[:, None] * inv_freq, col[:, None] * inv_freq], axis=-1)
    cos, sin = jnp.cos(ang), jnp.sin(ang)
    cosf = jnp.concatenate([cos, cos, cos, cos], axis=-1)
    sins = jnp.concatenate([-sin, sin, -sin, sin], axis=-1)
    return cosf, sins


def kernel(x, c, ctx, c_ctx, ada_w, ada_b, norm1_g, norm2_g, w_in, w_out, conv_w, ret_decay, gdn_a_log, gdn_dt_bias,
           gdn_norm_g, na_rpb, ffn_w1, ffn_w3, ffn_w2, moe_router, moe_w1, moe_w3, moe_w2, final_g):
    b, l, d = x.shape
    lc = ctx.shape[1]
    depth = ada_w.shape[0]
    cosf, sins = _rope_tables(l)
    ones_c = jnp.ones((lc, LANES), F32)
    zeros_c = jnp.zeros((lc, LANES), F32)

    rows = ((b + 1 + 7) // 8) * 8
    c_all = jnp.zeros((rows, d), F32).at[:b].set(c).at[b].set(c_ctx)
    mod = _ada_vectors(c_all, ada_w, ada_b).reshape(depth, rows, 6, d)

    w1p_all = jax.vmap(_pack_w_in)(w_in)
    wo_all = w_out.astype(BF16)
    log_gamma_all = jnp.log1p(-jnp.exp2(-ret_decay.astype(F32)))
    gdn_tables = jax.vmap(_gdn_tables)(conv_w, gdn_a_log, gdn_dt_bias, gdn_norm_g)
    na_bias = jax.vmap(lambda r: _na_bias_tiles(r, l // GRID_W))(na_rpb)

    y = ctx
    for layer in range(depth):
        need_ctx = layer < depth - 1
        mod_l = mod[layer, :b]
        mod_c = jnp.broadcast_to(mod[layer, b][None], (b, 6, d))
        w1p = w1p_all[layer]
        g1 = norm1_g[layer].reshape(1, d)
        ret_l, gdn_l, ab_l, abt_l, na_l = _in_projection(x, mod_l, g1, cosf, sins, w1p, rope=True)
        ret_c, gdn_c, ab_c, abt_c, na_c = _in_projection(y, mod_c, g1, ones_c, zeros_c, w1p, rope=False)

        r_l, r_c = _retention(ret_l, ret_c, log_gamma_all[layer])
        g_l, g_c = _gdn(gdn_l, ab_l, abt_l, gdn_c, ab_c, abt_c, [t[layer] for t in gdn_tables])
        n_l, n_c = _neighbourhood_attention(na_l, na_c, na_bias, layer)

        wo = wo_all[layer]
        g2 = norm2_g[layer].reshape(1, d)
        j = layer // 2
        if layer % 2 == 0:
            w1, w3, w2 = ffn_w1[j].astype(BF16), ffn_w3[j].astype(BF16), ffn_w2[j].astype(BF16)
            x = _dense_block(r_l, g_l, n_l, x, mod_l, g2, wo, w1, w3, w2)
            if need_ctx:
                y = _dense_block(r_c, g_c, n_c, y, mod_c, g2, wo, w1, w3, w2)
            if layer == depth - 1:
                x = _final_norm(x, final_g)
        else:
            wr = jnp.pad(moe_router[j].astype(F32), ((0, 0), (0, LANES - N_EXPERTS)))
            w1, w3, w2 = moe_w1[j].astype(BF16), moe_w3[j].astype(BF16), moe_w2[j].astype(BF16)

            def moe_ffn(r, g, n, xin, m, last):
                bb, ll, _ = xin.shape
                x1, h, logits = _router_block(r, g, n, xin, m, g2, wo, wr)
                row_tok, row_slot, gates, bexp, bval, n_out, n_spare = _route(logits.reshape(bb * ll, LANES))
                y_rows = _moe_experts(h.reshape(bb * ll, d // 2), row_tok, row_slot, bexp, bval, w1, w3, w2, n_out, n_spare)
                return _moe_combine(y_rows, x1, gates, m[:, 5:6, :], final_g, last)

            x = moe_ffn(r_l, g_l, n_l, x, mod_l, layer == depth - 1)
            if need_ctx:
                y = moe_ffn(r_c, g_c, n_c, y, mod_c, False)
    return x
```

```python
import functools

import numpy as np
import jax
import jax.numpy as jnp
from jax import lax
from jax.experimental import pallas as pl
from jax.experimental.pallas import tpu as pltpu

F32 = jnp.float32
BF16 = jnp.bfloat16
HIGHEST = lax.Precision.HIGHEST

LANES = 128
BF16_ROWS = 16
ROW_TILE = 512
PROJ_COLS = 512
HEAD_DIM = 64
PAIR = 2 * HEAD_DIM
GRID_W = 64
H_RET, H_GDN, H_NA = 4, 4, 8
D_RET, D_GDN, D_NA = H_RET * HEAD_DIM, H_GDN * HEAD_DIM, H_NA * HEAD_DIM
RET_CHUNK = 128
RET_UNROLL = 4
GDN_CHUNK = 64
GDN_GROUP = 4
GDN_PREP_ROWS = 512
SHORT_CONV = 5
NA_KH, NA_KW = 8, 16
NA_QROWS = 4
NA_KROWS = NA_QROWS + NA_KH
NA_PAIRS = 1
NA_QSCALE = float(np.log2(np.e)) * HEAD_DIM ** -0.5
N_FREQ = HEAD_DIM // 4
ROPE_BASE = 10000.0
N_EXPERTS = 8
TOP_K = 2
MOE_ROWS = 512
FF_CHUNK = 256
EPS = 1e-6
NEG_INF = -1e30
VMEM_LIMIT = 56 * 1024 * 1024

C_RET = 0
C_GDN = C_RET + 4 * D_RET
C_AB = C_GDN + 4 * D_GDN
C_NA = C_AB + LANES
C_END = C_NA + 3 * D_NA


def _cparams(sem, vmem=None):
    return pltpu.CompilerParams(dimension_semantics=sem, vmem_limit_bytes=vmem)


def _silu(x):
    return x * jax.nn.sigmoid(x)


def _dot(a, b):
    return jnp.dot(a, b, preferred_element_type=F32)


def _dot_nt(a, b):
    return lax.dot_general(a, b, (((1,), (1,)), ((), ())), preferred_element_type=F32)


def _dot_tn(a, b):
    return lax.dot_general(a, b, (((0,), (0,)), ((), ())), preferred_element_type=F32)


def _split_bf16(x):
    hi = x.astype(BF16)
    return hi, (x - hi.astype(F32)).astype(BF16)


def _dot_3pass(a, b):
    ah, al = _split_bf16(a)
    bh, bl = _split_bf16(b)
    return _dot(ah, bh) + (_dot(ah, bl) + _dot(al, bh))


def _dot_hi(a, b):
    return jnp.dot(a, b, preferred_element_type=F32, precision=HIGHEST)


def _ada_body(c_ref, w_ref, b_ref, o_ref):
    s = _silu(c_ref[...])
    o_ref[0] = _dot_hi(s, w_ref[0]) + b_ref[0]


def _ada_vectors(c_all, ada_w, ada_b):
    depth, d, d6 = ada_w.shape
    rows = c_all.shape[0]
    tn = d
    return pl.pallas_call(
        _ada_body,
        grid=(depth, d6 // tn),
        in_specs=[pl.BlockSpec((rows, d), lambda l, j: (0, 0)),
                  pl.BlockSpec((1, d, tn), lambda l, j: (l, 0, j)),
                  pl.BlockSpec((1, 1, tn), lambda l, j: (l, 0, j))],
        out_specs=pl.BlockSpec((1, rows, tn), lambda l, j: (l, 0, j)),
        out_shape=jax.ShapeDtypeStruct((depth, rows, d6), F32),
        compiler_params=_cparams(("arbitrary", "arbitrary")),
        name="ada_vectors",
    )(c_all, ada_w, ada_b.reshape(depth, 1, d6))


def _norm_mod(x, g, shift, scale):
    ms = jnp.mean(x * x, axis=-1, keepdims=True)
    return (x * lax.rsqrt(ms + EPS) * g) * (1.0 + scale) + shift


def _rope_slab(t, cosf, sins):
    lane = lax.broadcasted_iota(jnp.int32, t.shape, 1)
    first = (lane % HEAD_DIM) < (HEAD_DIM // 2)
    partner = jnp.where(first, pltpu.roll(t, LANES - HEAD_DIM // 2, 1), pltpu.roll(t, HEAD_DIM // 2, 1))
    return t * cosf + partner * sins


def _inproj_body(x_ref, mod_ref, g_ref, cos_ref, sin_ref, w_ref, ret_ref, gdn_ref, ab_ref, abt_ref, na_ref, *, rope):
    h = _norm_mod(x_ref[0], g_ref[...], mod_ref[0, 0:1, :], mod_ref[0, 1:2, :]).astype(BF16)
    qk = _dot(h, w_ref[:, C_RET:C_RET + 2 * D_RET])
    slabs = []
    for s in range(2 * D_RET // LANES):
        t = qk[:, s * LANES:(s + 1) * LANES]
        if rope:
            t = _rope_slab(t, cos_ref[...], sin_ref[...])
        if s >= D_RET // LANES:
            t = t * HEAD_DIM ** -0.5
        slabs.append(t)
    ret_ref[0, :, 0:2 * D_RET] = jnp.concatenate(slabs, axis=1).astype(BF16)
    ret_ref[0, :, 2 * D_RET:] = _dot(h, w_ref[:, C_RET + 2 * D_RET:C_GDN]).astype(BF16)
    for j in range(4 * D_GDN // PROJ_COLS):
        cols = slice(PROJ_COLS * j, PROJ_COLS * (j + 1))
        gdn_ref[0, :, cols] = _dot(h, w_ref[:, C_GDN + cols.start:C_GDN + cols.stop]).astype(BF16)
    ab = _dot(h, w_ref[:, C_AB:C_NA])
    ab_ref[0] = ab
    for grp in range(ab.shape[0] // LANES):
        t = ab[grp * LANES:(grp + 1) * LANES, :].T
        for p in range(D_GDN // PAIR):
            abt_ref[0, p, grp] = t[8 * p:8 * p + 8, :]
    for j in range(3 * D_NA // PROJ_COLS):
        cols = slice(PROJ_COLS * j, PROJ_COLS * (j + 1))
        na_ref[0, :, cols] = _dot(h, w_ref[:, C_NA + cols.start:C_NA + cols.stop]).astype(BF16)


def _in_projection(x, mod, g, cosf, sins, w, rope):
    b, l, d = x.shape
    tm = min(ROW_TILE, l)
    body = functools.partial(_inproj_body, rope=rope)
    return pl.pallas_call(
        body,
        grid=(b, l // tm),
        in_specs=[pl.BlockSpec((1, tm, d), lambda i, j: (i, j, 0)),
                  pl.BlockSpec((1, 6, d), lambda i, j: (i, 0, 0)),
                  pl.BlockSpec((1, d), lambda i, j: (0, 0)),
                  pl.BlockSpec((tm, LANES), lambda i, j: (j, 0)),
                  pl.BlockSpec((tm, LANES), lambda i, j: (j, 0)),
                  pl.BlockSpec((d, C_END), lambda i, j: (0, 0))],
        out_specs=[pl.BlockSpec((1, tm, 4 * D_RET), lambda i, j: (i, j, 0)),
                   pl.BlockSpec((1, tm, 4 * D_GDN), lambda i, j: (i, j, 0)),
                   pl.BlockSpec((1, tm, LANES), lambda i, j: (i, j, 0)),
                   pl.BlockSpec((1, D_GDN // PAIR, tm // LANES, 8, LANES), lambda i, j: (i, 0, j, 0, 0)),
                   pl.BlockSpec((1, tm, 3 * D_NA), lambda i, j: (i, j, 0))],
        out_shape=[jax.ShapeDtypeStruct((b, l, 4 * D_RET), BF16),
                   jax.ShapeDtypeStruct((b, l, 4 * D_GDN), BF16),
                   jax.ShapeDtypeStruct((b, l, LANES), F32),
                   jax.ShapeDtypeStruct((b, D_GDN // PAIR, l // LANES, 8, LANES), F32),
                   jax.ShapeDtypeStruct((b, l, 3 * D_NA), BF16)],
        compiler_params=_cparams(("arbitrary", "arbitrary"), VMEM_LIMIT),
        name="in_projection",
    )(x, mod, g, cosf, sins, w)


def _pack_w_in(w_in):
    d = w_in.shape[0]
    c1 = 4 * D_RET
    c2 = c1 + 4 * D_GDN
    q0 = c2 + 4 * H_GDN
    col = jnp.arange(w_in.shape[1])
    w_in = jnp.where((col >= q0) & (col < q0 + D_NA), w_in * NA_QSCALE, w_in).astype(BF16)
    ab = w_in[:, c2:c2 + 4 * H_GDN]
    ab = ab.reshape(d, 2, 2, H_GDN // 2, 2)
    ab = jnp.transpose(ab, (0, 3, 1, 2, 4)).reshape(d, 4 * H_GDN)
    ab = jnp.pad(ab, ((0, 0), (0, LANES - 4 * H_GDN)))
    return jnp.concatenate([w_in[:, :c2], ab, w_in[:, c2 + 4 * H_GDN:]], axis=1)


def _head_masks(shape):
    lane = lax.broadcasted_iota(jnp.int32, shape, len(shape) - 1)
    return lane < HEAD_DIM, lane >= HEAD_DIM


def _per_head(lo, hi, shape):
    m0, _ = _head_masks(shape)
    return jnp.where(m0, lo, hi)


def _head_sumsq(o):
    m0, m1 = _head_masks(o.shape)
    sq = o * o
    s0 = jnp.sum(jnp.where(m0, sq, 0.0), axis=-1, keepdims=True)
    s1 = jnp.sum(jnp.where(m1, sq, 0.0), axis=-1, keepdims=True)
    return jnp.where(m0, s0, s1)


def _ret_body(lg_ref, ql, kl, vl, gl, qc, kc, vc, gc, r_ref, rc_ref, sb_ref):
    c = RET_CHUNK
    npair = D_RET // PAIR
    lane_shape = (c, PAIR)
    pos = lax.broadcasted_iota(jnp.int32, lane_shape, 0).astype(F32)
    ii = lax.broadcasted_iota(jnp.int32, (c, c), 0)
    jj = lax.broadcasted_iota(jnp.int32, (c, c), 1)
    diff = (ii - jj).astype(F32)
    m0, m1 = _head_masks(lane_shape)
    masks = (m0, m1)
    bi = lax.broadcasted_iota(jnp.int32, (PAIR, PAIR), 0) // HEAD_DIM
    bj = lax.broadcasted_iota(jnp.int32, (PAIR, PAIR), 1) // HEAD_DIM
    bd = bi == bj
    cst = []
    for p in range(npair):
        lgf = _per_head(lg_ref[0, 2 * p], lg_ref[0, 2 * p + 1], lane_shape)
        lgb = _per_head(lg_ref[1, 2 * p], lg_ref[1, 2 * p + 1], lane_shape)
        dmats = [jnp.where(diff > 0, jnp.exp(lg_ref[0, 2 * p + hh] * diff),
                           jnp.where(diff < 0, jnp.exp(-lg_ref[1, 2 * p + hh] * diff), 2.0)) for hh in range(2)]
        cst.append(dict(qdf=jnp.exp(lgf * (pos + 1.0)), kdf=jnp.exp(lgf * (c - 1.0 - pos)),
                        qdb=jnp.exp(lgb * (c - pos)), kdb=jnp.exp(lgb * pos),
                        cdf=jnp.exp(lgf[0:1] * c), cdb=jnp.exp(lgb[0:1] * c), dmats=dmats))

    def sweep(q_ref, k_ref, v_ref, g_ref, o_ref, n, sf0, sb0):
        unroll = min(RET_UNROLL, n)

        def bchunk(ci, sbs):
            sl = pl.ds(pl.multiple_of(ci * c, c), c)
            out = []
            for p in range(npair):
                lanes = slice(p * PAIR, (p + 1) * PAIR)
                sb_ref[p, ci] = sbs[p]
                kd = (k_ref[0, sl, lanes].astype(F32) * cst[p]["kdb"]).astype(BF16)
                out.append(sbs[p] * cst[p]["cdb"] + jnp.where(bd, _dot_tn(kd, v_ref[0, sl, lanes]), 0.0))
            return tuple(out)

        def bstep(t, sbs):
            for u in range(unroll):
                sbs = bchunk(n - 1 - (t * unroll + u), sbs)
            return sbs

        sb_fin = lax.fori_loop(0, n // unroll, bstep, sb0)

        def fchunk(ci, sfs):
            sl = pl.ds(pl.multiple_of(ci * c, c), c)
            out = []
            for p in range(npair):
                lanes = slice(p * PAIR, (p + 1) * PAIR)
                k_ = cst[p]
                q = q_ref[0, sl, lanes]
                k = k_ref[0, sl, lanes]
                v = v_ref[0, sl, lanes]
                qf = q.astype(F32)
                qd = jnp.concatenate([(qf * k_["qdf"]).astype(BF16), (qf * k_["qdb"]).astype(BF16)], axis=1)
                o = _dot(qd, jnp.concatenate([sfs[p].astype(BF16), sb_ref[p, ci].astype(BF16)], axis=0))
                s2 = _dot_nt(_stack_heads(q), k)
                pm = jnp.concatenate([s2[:c] * k_["dmats"][0], s2[c:] * k_["dmats"][1]], axis=0).astype(BF16)
                pv = _dot(pm, v)
                o = o + jnp.where(masks[0], pv[:c], pv[c:])
                on = o * lax.rsqrt(_head_sumsq(o) * (1.0 / HEAD_DIM) + EPS)
                o_ref[0, sl, lanes] = (on * _silu(g_ref[0, sl, lanes].astype(F32))).astype(o_ref.dtype)
                kd = (k.astype(F32) * k_["kdf"]).astype(BF16)
                out.append(sfs[p] * k_["cdf"] + jnp.where(bd, _dot_tn(kd, v), 0.0))
            return tuple(out)

        def fstep(t, sfs):
            for u in range(unroll):
                sfs = fchunk(t * unroll + u, sfs)
            return sfs

        sf_fin = lax.fori_loop(0, n // unroll, fstep, sf0)
        return sf_fin, sb_fin

    z = tuple(jnp.zeros((PAIR, PAIR), F32) for _ in range(npair))
    sfc, sbc = sweep(qc, kc, vc, gc, rc_ref, qc.shape[1] // c, z, z)
    sweep(ql, kl, vl, gl, r_ref, ql.shape[1] // c, sfc, sbc)


def _retention(ret_l, ret_c, log_gamma):
    b, l, _ = ret_l.shape
    lc = ret_c.shape[1]
    npair = D_RET // PAIR

    def col(k):
        return lambda i: (i, 0, k)

    lat = [pl.BlockSpec((1, l, D_RET), col(k)) for k in range(4)]
    ctx = [pl.BlockSpec((1, lc, D_RET), col(k)) for k in range(4)]
    return pl.pallas_call(
        _ret_body,
        grid=(b,),
        in_specs=[pl.BlockSpec(memory_space=pltpu.SMEM)] + lat + ctx,
        out_specs=[pl.BlockSpec((1, l, D_RET), lambda i: (i, 0, 0)),
                   pl.BlockSpec((1, lc, D_RET), lambda i: (i, 0, 0))],
        out_shape=[jax.ShapeDtypeStruct((b, l, D_RET), BF16), jax.ShapeDtypeStruct((b, lc, D_RET), BF16)],
        scratch_shapes=[pltpu.VMEM((npair, max(l, lc) // RET_CHUNK, PAIR, PAIR), F32)],
        compiler_params=_cparams(("arbitrary",), VMEM_LIMIT),
        name="retention",
    )(log_gamma, ret_l, ret_l, ret_l, ret_l, ret_c, ret_c, ret_c, ret_c)


def _stack_heads(t):
    m0, m1 = _head_masks(t.shape)
    z = jnp.zeros_like(t)
    return jnp.concatenate([jnp.where(m0, t, z), jnp.where(m1, t, z)], axis=0)


def _scan_sum(x, axis, n, reverse):
    size = x.shape[axis]
    pos = lax.broadcasted_iota(jnp.int32, x.shape, axis)
    s = 1
    while s < n:
        if reverse:
            x = x + jnp.where(pos + s < n, pltpu.roll(x, size - s, axis), 0.0)
        else:
            x = x + jnp.where(pos >= s, pltpu.roll(x, s, axis), 0.0)
        s *= 2
    return x


def _gdn_prep(src_ref, slabs, cw_ref, r0):
    n = src_ref.shape[1]
    blk = min(GDN_PREP_ROWS, n)
    halo = BF16_ROWS

    def block(i, _):
        b0 = pl.multiple_of(i * blk, blk)
        lo_rows = pl.ds(pl.multiple_of(jnp.maximum(b0 - halo, 0), halo), halo)
        hi_rows = pl.ds(pl.multiple_of(jnp.minimum(b0 + blk, n - halo), halo), halo)
        for col, dst_ref, l2, scale in slabs:
            cols = slice(col, col + LANES)
            cw = cw_ref[:, cols]
            x = src_ref[0, pl.ds(b0, blk), cols].astype(F32)
            lo = jnp.where(b0 > 0, src_ref[0, lo_rows, cols].astype(F32), 0.0)
            hi = jnp.where(b0 + blk < n, src_ref[0, hi_rows, cols].astype(F32), 0.0)
            ext = jnp.concatenate([lo, x, hi], axis=0)
            acc = x * cw[SHORT_CONV // 2:SHORT_CONV // 2 + 1, :]
            for j in range(SHORT_CONV):
                s = j - SHORT_CONV // 2
                if s != 0:
                    sh = pltpu.roll(ext, (-s) % (blk + 2 * halo), 0)
                    acc = acc + sh[halo:halo + blk] * cw[j:j + 1, :]
            y = _silu(acc)
            if l2:
                y = y * lax.rsqrt(_head_sumsq(y) + EPS)
            if scale != 1.0:
                y = y * scale
            dst_ref[pl.ds(pl.multiple_of(r0 + b0, halo), blk), col % D_GDN:col % D_GDN + LANES] = y.astype(dst_ref.dtype)
        return 0

    lax.fori_loop(0, n // blk, block, 0)


def _gdn_body(x_ref, xc_ref, ab_ref, abt_ref, cw_ref, cst_ref, cstt_ref, ng_ref,
              o_ref, oc_ref, qn, kn, vn, oacc, st_s, x_s, t_s, rhs_s, u_s, wq_s, ak_s, egl_s):
    c = GDN_CHUNK
    c2 = 2 * c
    grp = GDN_GROUP
    npair = D_GDN // PAIR
    lc = xc_ref.shape[1]
    l = x_ref.shape[1]
    nc = lc // c
    nt = (lc + l) // c
    ii = lax.broadcasted_iota(jnp.int32, (c2, c2), 0)
    jj = lax.broadcasted_iota(jnp.int32, (c2, c2), 1)
    same = (ii // c) == (jj // c)
    eye = (ii == jj).astype(F32)
    dirs = ((same & (ii >= jj), same & (ii > jj), c - 1),
            (same & (ii <= jj), same & (ii < jj), 0))

    def chains(g, slot):
        out = []
        for gi in range(grp):
            t = g * grp + gi
            cf = t
            cb = jnp.where(t < nc, nc - 1 - t, nt + nc - 1 - t)
            for p in range(npair):
                for d in range(2):
                    out.append((((slot * grp + gi) * 2 + d) * npair + p, p, d, cf if d == 0 else cb))
        return out

    def stage_inputs(idx, p, d, ci):
        incl, strict, last = dirs[d]
        lanes = slice(p * PAIR, (p + 1) * PAIR)
        nea, dtb = cst_ref[0:1, :], cst_ref[1:2, :]
        neat, dtbt = cstt_ref[p, 0], cstt_ref[p, 1]
        sl = pl.ds(pl.multiple_of(ci * c, c), c)
        abv = ab_ref[0, sl, :]
        gall = nea * jax.nn.softplus(abv + dtb)
        ball = jax.nn.sigmoid(abv)
        shape = (c, PAIR)
        c0 = 8 * p + 2 * d
        gl = _per_head(gall[:, c0:c0 + 1], gall[:, c0 + 1:c0 + 2], shape)
        bl = _per_head(ball[:, c0 + 4:c0 + 5], ball[:, c0 + 5:c0 + 6], shape)
        gcum = _scan_sum(gl, 0, c, d == 1)
        abt = abt_ref[0, p, ci // 2]
        abt = jnp.where(ci % 2 == 1, pltpu.roll(abt, c, 1), abt)
        gt = neat * jax.nn.softplus(abt + dtbt)
        gtc = _scan_sum(gt, 1, c, d == 1)
        grow = jnp.concatenate([gtc[2 * d:2 * d + 1, 0:c], gtc[2 * d + 1:2 * d + 2, 0:c]], axis=1)
        gcol = jnp.concatenate([gcum[:, 0:1], gcum[:, HEAD_DIM:HEAD_DIM + 1]], axis=0)
        dec = jnp.where(incl, jnp.exp(jnp.where(incl, gcol - grow, 0.0)), 0.0)
        dec = jnp.where(ii == jj, 1.0, dec)
        q = qn[sl, lanes]
        k = kn[sl, lanes]
        v = vn[sl, lanes]
        kf = k.astype(F32)
        eg = jnp.exp(gcum)
        kb = kf * bl
        k_st = _stack_heads(k)
        kq = _dot_nt(jnp.concatenate([_stack_heads(kb.astype(BF16)), _stack_heads(q)], axis=0), k_st)
        a = jnp.where(strict, kq[:c2] * dec, 0.0)
        attn = (kq[c2:] * dec).astype(BF16)
        x_s[idx] = (-a).astype(BF16)
        t_s[idx] = eye - a
        rhs_s[idx] = jnp.concatenate([_stack_heads((v.astype(F32) * bl).astype(BF16)),
                                      _stack_heads((kb * eg).astype(BF16))], axis=1)
        glast = gcum[last:last + 1, :]
        wq_s[idx, c2:, :] = _stack_heads((q.astype(F32) * eg).astype(BF16))
        ak_s[idx] = jnp.concatenate([attn, _stack_heads(kf * jnp.exp(glast - gcum)).T.astype(BF16)], axis=0)
        egl_s[idx] = jnp.broadcast_to(jnp.exp(glast), (8, PAIR))

    def stage_group(g, slot):
        todo = chains(g, slot)
        for idx, p, d, ci in todo:
            stage_inputs(idx, p, d, ci)
        for idx, _, _, _ in todo:
            xb = x_s[idx]
            x_s[idx] = _dot(xb, xb).astype(BF16)
        for _ in range(4):
            for idx, _, _, _ in todo:
                xb = x_s[idx]
                t = t_s[idx]
                prod = _dot(jnp.concatenate([t.astype(BF16), xb], axis=0), xb)
                t_s[idx] = t + prod[:c2]
                x_s[idx] = prod[c2:].astype(BF16)
        for idx, _, _, _ in todo:
            t = t_s[idx]
            t = t + _dot(t.astype(BF16), x_s[idx])
            sol = _dot(t.astype(BF16), rhs_s[idx])
            u_s[idx] = sol[:, :PAIR]
            wq_s[idx, :c2, :] = sol[:, PAIR:].astype(BF16)

    def recur_group(g, slot):
        sts = {(p, d): st_s[d * npair + p] for p in range(npair) for d in range(2)}
        for idx, p, d, ci in chains(g, slot):
            sl = pl.ds(pl.multiple_of(ci * c, c), c)
            st = sts[(p, d)]
            ws = _dot(wq_s[idx], st.astype(BF16))
            v_new = (u_s[idx] - ws[:c2]).astype(BF16)
            av = _dot(ak_s[idx], v_new)
            o_st = ws[c2:] + av[:c2]
            oacc[sl, p * PAIR:(p + 1) * PAIR] += o_st[:c] + o_st[c:]
            sts[(p, d)] = st * egl_s[idx][0:1, :] + av[c2:]
        for (p, d), st in sts.items():
            st_s[d * npair + p] = st

    def prep(src_ref, r0):
        slabs = [(k * D_GDN + s * LANES, dst, l2, scale)
                 for s in range(D_GDN // LANES)
                 for k, (dst, l2, scale) in enumerate(((qn, True, HEAD_DIM ** -0.5), (kn, True, 1.0), (vn, False, 1.0)))]
        _gdn_prep(src_ref, slabs, cw_ref, r0)

    def finish(src_ref, out_ref, r0, n):
        for p in range(npair):
            lanes = slice(p * PAIR, (p + 1) * PAIR)
            o = oacc[r0:r0 + n, lanes]
            on = o * lax.rsqrt(_head_sumsq(o) * (1.0 / HEAD_DIM) + EPS)
            gate = src_ref[0, :, 3 * D_GDN + p * PAIR:3 * D_GDN + (p + 1) * PAIR].astype(F32)
            out_ref[0, :, lanes] = (on * ng_ref[...] * _silu(gate)).astype(out_ref.dtype)

    st_s[...] = jnp.zeros_like(st_s)
    oacc[...] = jnp.zeros_like(oacc)
    prep(xc_ref, 0)
    prep(x_ref, lc)

    ng = nt // grp
    stage_group(0, 0)

    def two_groups(kk, _):
        g = 2 * kk
        recur_group(g, 0)
        stage_group(g + 1, 1)
        recur_group(g + 1, 1)
        stage_group(g + 2, 0)
        return 0

    lax.fori_loop(0, (ng - 1) // 2, two_groups, 0)
    if ng % 2 == 0:
        recur_group(ng - 2, 0)
        stage_group(ng - 1, 1)
        recur_group(ng - 1, 1)
    else:
        recur_group(ng - 1, 0)
    finish(xc_ref, oc_ref, 0, lc)
    finish(x_ref, o_ref, lc, l)


def _gdn_tables(conv_w, a_log, dt_bias, norm_g):
    npair = D_GDN // PAIR
    nea = -jnp.exp(a_log.astype(F32)).reshape(2, npair, 2)
    dtb = dt_bias.astype(F32).reshape(2, npair, 2)
    rows = jnp.stack([jnp.transpose(nea, (1, 0, 2)).reshape(npair, 4), jnp.transpose(dtb, (1, 0, 2)).reshape(npair, 4)], axis=1)
    cst = jnp.pad(jnp.transpose(jnp.pad(rows, ((0, 0), (0, 0), (0, 4))), (1, 0, 2)).reshape(2, 8 * npair),
                  ((0, 6), (0, LANES - 8 * npair)))
    cstt = jnp.broadcast_to(jnp.pad(rows, ((0, 0), (0, 0), (0, 4)))[..., None], (npair, 2, 8, LANES))
    cw = jnp.pad(conv_w.astype(F32), ((0, 8 - SHORT_CONV), (0, 0)))
    ng = jnp.tile(norm_g.astype(F32), 2).reshape(1, PAIR)
    return cw, cst, cstt, ng


def _gdn(gdn_l, ab_l, abt_l, gdn_c, ab_c, abt_c, tables):
    b, l, _ = gdn_l.shape
    lc = gdn_c.shape[1]
    npair = D_GDN // PAIR
    lt = l + lc
    assert (lt // GDN_CHUNK) % GDN_GROUP == 0 and lc % LANES == 0
    cw, cst, cstt, ng = tables

    nstage = 2 * GDN_GROUP * 2 * npair

    def whole(shape):
        nd = len(shape)
        return pl.BlockSpec(shape, lambda i: (0,) * nd)

    def per_batch(shape):
        nd = len(shape)
        return pl.BlockSpec((1,) + shape, lambda i: (i,) + (0,) * nd)

    sq = (nstage, PAIR, PAIR)
    return pl.pallas_call(
        _gdn_body,
        grid=(b,),
        in_specs=[per_batch((l, 4 * D_GDN)), per_batch((lc, 4 * D_GDN)), per_batch((lt, LANES)),
                  per_batch((npair, lt // LANES, 8, LANES)),
                  whole((8, 3 * D_GDN)), whole((8, LANES)), whole((npair, 2, 8, LANES)), whole((1, PAIR))],
        out_specs=[per_batch((l, D_GDN)), per_batch((lc, D_GDN))],
        out_shape=[jax.ShapeDtypeStruct((b, l, D_GDN), BF16), jax.ShapeDtypeStruct((b, lc, D_GDN), BF16)],
        scratch_shapes=[pltpu.VMEM((lt, D_GDN), BF16)] * 3 + [pltpu.VMEM((lt, D_GDN), F32)]
        + [pltpu.VMEM((2 * npair, PAIR, PAIR), F32), pltpu.VMEM(sq, BF16), pltpu.VMEM(sq, F32),
           pltpu.VMEM((nstage, PAIR, 2 * PAIR), BF16), pltpu.VMEM(sq, F32),
           pltpu.VMEM((nstage, 2 * PAIR, PAIR), BF16), pltpu.VMEM((nstage, 2 * PAIR, PAIR), BF16),
           pltpu.VMEM((nstage, 8, PAIR), F32)],
        compiler_params=_cparams(("arbitrary",), VMEM_LIMIT),
        name="gated_deltanet",
    )(gdn_l, gdn_c, jnp.concatenate([ab_c, ab_l], axis=1), jnp.concatenate([abt_c, abt_l], axis=2), cw, cst, cstt, ng)


def _na_bias_tiles(rpb, rows):
    w = GRID_W
    ext = jnp.pad(rpb.astype(F32) * float(np.log2(np.e)), ((0, 0), (0, 0), (w - NA_KW, w - NA_KW)))
    cq = np.arange(w)[:, None]
    ck = np.arange(w)[None, :]
    tb = ext[:, :, jnp.asarray(ck - cq + w - 1)]
    ws = np.clip(cq - NA_KW // 2, 0, w - NA_KW)
    col_ok = (ck >= ws) & (ck < ws + NA_KW)
    tb = jnp.where(jnp.asarray(col_ok)[None, None], tb, NEG_INF)
    none = 2 * NA_KH - 1
    tb = jnp.concatenate([tb, jnp.full((tb.shape[0], 1, w, w), NEG_INF, F32)], axis=1)
    nt = rows // NA_QROWS
    idx = np.zeros((3, NA_QROWS, NA_KROWS), np.int32)
    for cls, t in enumerate((0, 1, nt - 1)):
        ks = int(np.clip(NA_QROWS * t - NA_KH // 2, 0, rows - NA_KROWS))
        for rl in range(NA_QROWS):
            r = NA_QROWS * t + rl
            r0 = int(np.clip(r - NA_KH // 2, 0, rows - NA_KH))
            for j in range(NA_KROWS):
                kr = ks + j
                idx[cls, rl, j] = kr - r + NA_KH - 1 if r0 <= kr < r0 + NA_KH else none
    pairs = idx.reshape(3, NA_QROWS, NA_KROWS // 2, 2)
    uniq, inv = np.unique(pairs.reshape(-1, 2), axis=0, return_inverse=True)
    wide = jnp.concatenate([tb[:, jnp.asarray(uniq[:, 0])], tb[:, jnp.asarray(uniq[:, 1])]], axis=-1)
    tiles = wide[:, jnp.asarray(inv.reshape(3, NA_QROWS, NA_KROWS // 2))]
    tiles = jnp.transpose(tiles, (1, 0, 3, 2, 4, 5))
    return tiles.reshape(3, tb.shape[0], NA_KROWS // 2, NA_QROWS * w, 2 * w)


def _softmax_pv(s_parts, v_parts):
    m = functools.reduce(jnp.maximum, [jnp.max(s, axis=-1, keepdims=True) for s in s_parts])
    ps = [jnp.exp2(s - m) for s in s_parts]
    den = functools.reduce(lambda x, y: x + y, [jnp.sum(p, axis=-1, keepdims=True) for p in ps])
    o = _dot(jnp.concatenate([p.astype(BF16) for p in ps], axis=1), jnp.concatenate(v_parts, axis=0))
    return o / den


def _na_body(q_ref, k_ref, v_ref, qc_ref, kc_ref, vc_ref, bias_ref, o_ref, oc_ref, *, rows):
    w = GRID_W
    tq = NA_QROWS * w
    tk = NA_KROWS * w
    nt = rows // NA_QROWS
    m0, m1 = _head_masks((tq, PAIR))
    masks = (m0, m1)

    def tile(t, _):
        cls = jnp.where(t > 0, 1, 0) + jnp.where(t == nt - 1, 1, 0)
        ks = jnp.clip(NA_QROWS * t - NA_KH // 2, 0, rows - NA_KROWS) * w
        ksl = pl.ds(pl.multiple_of(ks, w), tk)
        qsl = pl.ds(pl.multiple_of(t * tq, tq), tq)
        for p in range(NA_PAIRS):
            lanes = slice(p * PAIR, (p + 1) * PAIR)
            q = q_ref[0, qsl, lanes]
            kb = k_ref[0, ksl, lanes]
            vb = v_ref[0, ksl, lanes]
            kc = kc_ref[0, :, lanes]
            vc = vc_ref[0, :, lanes]
            s_all = _dot_nt(_stack_heads(q), jnp.concatenate([kb, kc], axis=0))
            v_all = jnp.concatenate([vb, vc], axis=0)
            acc = jnp.zeros((tq, PAIR), F32)
            for hh in range(2):
                bias = jnp.concatenate([bias_ref[0, cls, 2 * p + hh, s] for s in range(NA_KROWS // 2)]
                                       + [jnp.zeros((tq, kc.shape[0]), F32)], axis=1)
                o_h = _softmax_pv([s_all[hh * tq:(hh + 1) * tq] + bias], [v_all])
                acc = acc + jnp.where(masks[hh], o_h, 0.0)
            o_ref[0, qsl, lanes] = acc.astype(o_ref.dtype)
        return 0

    lax.fori_loop(0, nt, tile, 0)
    for p in range(NA_PAIRS):
        lanes = slice(p * PAIR, (p + 1) * PAIR)
        qc = qc_ref[0, :, lanes]
        kc = kc_ref[0, :, lanes]
        vc = vc_ref[0, :, lanes]
        mc0, mc1 = _head_masks(qc.shape)
        accc = jnp.zeros(qc.shape, F32)
        for mk in (mc0, mc1):
            qm = jnp.where(mk, qc, jnp.zeros_like(qc))
            accc = accc + jnp.where(mk, _softmax_pv([_dot_nt(qm, kc)], [vc]), 0.0)
        oc_ref[0, :, lanes] = accc.astype(oc_ref.dtype)


def _neighbourhood_attention(na_l, na_c, bias, layer):
    b, l, _ = na_l.shape
    lc = na_c.shape[1]
    rows = l // GRID_W
    wide = NA_PAIRS * PAIR
    ngrp = D_NA // wide

    def col(k):
        return lambda p, i: (i, 0, k * ngrp + p)

    lat = [pl.BlockSpec((1, l, wide), col(k)) for k in range(3)]
    ctx = [pl.BlockSpec((1, lc, wide), col(k)) for k in range(3)]
    return pl.pallas_call(
        functools.partial(_na_body, rows=rows),
        grid=(ngrp, b),
        in_specs=lat + ctx + [pl.BlockSpec((1, 3, 2 * NA_PAIRS, NA_KROWS // 2, NA_QROWS * GRID_W, 2 * GRID_W),
                                           lambda p, i: (layer, 0, p, 0, 0, 0))],
        out_specs=[pl.BlockSpec((1, l, wide), lambda p, i: (i, 0, p)),
                   pl.BlockSpec((1, lc, wide), lambda p, i: (i, 0, p))],
        out_shape=[jax.ShapeDtypeStruct((b, l, D_NA), BF16), jax.ShapeDtypeStruct((b, lc, D_NA), BF16)],
        compiler_params=_cparams(("arbitrary", "arbitrary"), VMEM_LIMIT),
        name="neighbourhood_attention",
    )(na_l, na_l, na_l, na_c, na_c, na_c, bias)


def _outproj_residual(r_ref, g_ref, n_ref, x_ref, mod_ref, wo_ref):
    mix = jnp.concatenate([r_ref[0], g_ref[0], n_ref[0]], axis=1)
    return x_ref[0] + mod_ref[0, 2:3, :] * _dot(mix, wo_ref[...])


def _swiglu_chunk(hb, w1_ref, w3_ref, w2_ref, j):
    cols = slice(j * FF_CHUNK, (j + 1) * FF_CHUNK)
    t = (_silu(_dot(hb, w1_ref[:, cols])) * _dot(hb, w3_ref[:, cols])).astype(BF16)
    return _dot(t, w2_ref[cols, :])


def _swiglu_chunks(hb, w1_ref, w3_ref, w2_ref, acc_ref):
    for j in range(w2_ref.shape[0] // FF_CHUNK):
        if j == 0:
            acc_ref[...] = _swiglu_chunk(hb, w1_ref, w3_ref, w2_ref, j)
        else:
            acc_ref[...] += _swiglu_chunk(hb, w1_ref, w3_ref, w2_ref, j)


def _dense_body(r_ref, g_ref, n_ref, x_ref, mod_ref, g2_ref, wo_ref, w1_ref, w3_ref, w2_ref, o_ref, acc_ref):
    x1 = _outproj_residual(r_ref, g_ref, n_ref, x_ref, mod_ref, wo_ref)
    hb = _norm_mod(x1, g2_ref[...], mod_ref[0, 3:4, :], mod_ref[0, 4:5, :]).astype(BF16)
    _swiglu_chunks(hb, w1_ref, w3_ref, w2_ref, acc_ref)
    o_ref[0] = x1 + mod_ref[0, 5:6, :] * acc_ref[...]


def _pack_bf16_pairs(h):
    m = h.shape[1] // 2
    bits = lax.bitcast_convert_type(h.astype(BF16).astype(F32), jnp.uint32)
    return (bits[:, :m] >> 16) | (bits[:, m:] & jnp.uint32(0xFFFF0000))


def _unpack_bf16_pairs(u):
    lo = lax.bitcast_convert_type(u << 16, F32)
    hi = lax.bitcast_convert_type(u & jnp.uint32(0xFFFF0000), F32)
    return jnp.concatenate([lo, hi], axis=1).astype(BF16)


def _router_body(r_ref, g_ref, n_ref, x_ref, mod_ref, g2_ref, wo_ref, wr_ref, x1_ref, h_ref, lg_ref):
    x1 = _outproj_residual(r_ref, g_ref, n_ref, x_ref, mod_ref, wo_ref)
    h = _norm_mod(x1, g2_ref[...], mod_ref[0, 3:4, :], mod_ref[0, 4:5, :])
    x1_ref[0] = x1
    h_ref[0] = _pack_bf16_pairs(h)
    lg_ref[0] = _dot_3pass(h, wr_ref[...])


def _resident(shape):
    nd = len(shape)
    return pl.BlockSpec(shape, lambda i, j: (0,) * nd, pipeline_mode=pl.Buffered(1))


def _mixer_specs(tm, d):
    return [pl.BlockSpec((1, tm, D_RET), lambda i, j: (i, j, 0)),
            pl.BlockSpec((1, tm, D_GDN), lambda i, j: (i, j, 0)),
            pl.BlockSpec((1, tm, D_NA), lambda i, j: (i, j, 0)),
            pl.BlockSpec((1, tm, d), lambda i, j: (i, j, 0)),
            pl.BlockSpec((1, 6, d), lambda i, j: (i, 0, 0)),
            pl.BlockSpec((1, d), lambda i, j: (0, 0))]


def _dense_block(r, g, n, x, mod, g2, wo, w1, w3, w2):
    b, l, d = x.shape
    tm = min(ROW_TILE, l)
    dff = w1.shape[1]
    return pl.pallas_call(
        _dense_body,
        grid=(b, l // tm),
        in_specs=_mixer_specs(tm, d) + [_resident((d, d)), _resident((d, dff)), _resident((d, dff)), _resident((dff, d))],
        out_specs=pl.BlockSpec((1, tm, d), lambda i, j: (i, j, 0)),
        out_shape=jax.ShapeDtypeStruct((b, l, d), F32),
        scratch_shapes=[pltpu.VMEM((tm, d), F32)],
        compiler_params=_cparams(("arbitrary", "arbitrary"), VMEM_LIMIT),
        name="outproj_swiglu",
    )(r, g, n, x, mod, g2, wo, w1, w3, w2)


def _router_block(r, g, n, x, mod, g2, wo, wr):
    b, l, d = x.shape
    tm = min(ROW_TILE, l)
    blk = pl.BlockSpec((1, tm, d), lambda i, j: (i, j, 0))
    return pl.pallas_call(
        _router_body,
        grid=(b, l // tm),
        in_specs=_mixer_specs(tm, d) + [_resident((d, d)), _resident((d, LANES))],
        out_specs=[blk, pl.BlockSpec((1, tm, d // 2), lambda i, j: (i, j, 0)), pl.BlockSpec((1, tm, LANES), lambda i, j: (i, j, 0))],
        out_shape=[jax.ShapeDtypeStruct((b, l, d), F32), jax.ShapeDtypeStruct((b, l, d // 2), jnp.uint32),
                   jax.ShapeDtypeStruct((b, l, LANES), F32)],
        compiler_params=_cparams(("arbitrary", "arbitrary"), VMEM_LIMIT),
        name="outproj_router",
    )(r, g, n, x, mod, g2, wo, wr)


def _row_copy_in(idx_ref, r, src_hbm, dst_ref, sem):
    return pltpu.make_async_copy(src_hbm.at[pl.ds(idx_ref[0, 0, r], 1), :], dst_ref.at[pl.ds(r, 1), :], sem)


def _row_copy_out(idx_ref, r, src_ref, dst_hbm, sem):
    return pltpu.make_async_copy(src_ref.at[pl.ds(r, 1), :], dst_hbm.at[pl.ds(idx_ref[0, 0, r], 1), :], sem)


def _moe_body(bval_ref, bexp_ref, tok_ref, tokn_ref, slotp_ref, h_hbm, w1_ref, w3_ref, w2_ref, y_hbm,
              xbuf, ybuf, gsem, ssem, *, n_asg):
    i = pl.program_id(0)
    tb = xbuf.shape[1]
    nch = w2_ref.shape[1] // FF_CHUNK
    cur = i % 2
    per = -(-tb // nch)
    valid = bval_ref[i] == 1
    prev_valid = bval_ref[jnp.maximum(i - 1, 0)] == 1

    def wait_gather(slot):
        pltpu.make_async_copy(h_hbm.at[pl.ds(0, tb), :], xbuf.at[slot], gsem).wait()

    def wait_scatter(slot):
        pltpu.make_async_copy(ybuf.at[slot], y_hbm.at[pl.ds(0, tb), :], ssem).wait()

    @pl.when(i == 0)
    def _():
        ybuf[1] = jnp.zeros((tb, ybuf.shape[2]), F32)
        fills = [pltpu.make_async_copy(ybuf.at[1], y_hbm.at[pl.ds(r0, tb), :], ssem)
                 for r0 in range(n_asg, y_hbm.shape[0], tb)]
        for f in fills:
            f.start()
        for f in fills:
            f.wait()

        def issue(r, _):
            _row_copy_in(tok_ref, r, h_hbm, xbuf.at[0], gsem).start()
            return 0

        lax.fori_loop(0, tb, issue, 0)

    @pl.when(valid)
    def _():
        wait_gather(cur)
        hb = _unpack_bf16_pairs(xbuf[cur])
        acc = ybuf.at[cur]
        prv = ybuf.at[1 - cur]
        nxt = xbuf.at[1 - cur]
        for j in range(nch):
            for r in range(j * per, min(tb, (j + 1) * per)):
                _row_copy_in(tokn_ref, r, h_hbm, nxt, gsem).start()
                _row_copy_out(slotp_ref, r, prv, y_hbm, ssem).start()
            part = _swiglu_chunk(hb, w1_ref.at[0], w3_ref.at[0], w2_ref.at[0], j)
            if j == 0:
                acc[...] = part
            else:
                acc[...] += part
        wait_scatter(1 - cur)

    @pl.when(jnp.logical_not(valid) & prev_valid & (i > 0))
    def _():
        wait_gather(cur)

        def issue(r, _):
            _row_copy_out(slotp_ref, r, ybuf.at[1 - cur], y_hbm, ssem).start()
            return 0

        lax.fori_loop(0, tb, issue, 0)
        wait_scatter(1 - cur)


def _moe_experts(h_rows, row_tok, row_slot, block_expert, block_valid, w1, w3, w2, n_out_rows, n_spare):
    d = w2.shape[-1]
    dff = w2.shape[1]
    nb = block_expert.shape[0]
    tb = MOE_ROWS
    smem = functools.partial(pl.BlockSpec, (1, 1, tb), memory_space=pltpu.SMEM)
    grid_spec = pltpu.PrefetchScalarGridSpec(
        num_scalar_prefetch=2,
        grid=(nb,),
        in_specs=[smem(lambda i, bv, be: (i, 0, 0)),
                  smem(lambda i, bv, be: (jnp.minimum(i + 1, nb - 1), 0, 0)),
                  smem(lambda i, bv, be: (i, 0, 0)),
                  pl.BlockSpec(memory_space=pl.ANY),
                  pl.BlockSpec((1, d, dff), lambda i, bv, be: (be[i], 0, 0)),
                  pl.BlockSpec((1, d, dff), lambda i, bv, be: (be[i], 0, 0)),
                  pl.BlockSpec((1, dff, d), lambda i, bv, be: (be[i], 0, 0))],
        out_specs=pl.BlockSpec(memory_space=pl.ANY),
        scratch_shapes=[pltpu.VMEM((2, tb, d // 2), jnp.uint32), pltpu.VMEM((2, tb, d), F32),
                        pltpu.SemaphoreType.DMA(()), pltpu.SemaphoreType.DMA(())],
    )
    return pl.pallas_call(
        functools.partial(_moe_body, n_asg=n_out_rows - n_spare),
        grid_spec=grid_spec,
        out_shape=jax.ShapeDtypeStruct((n_out_rows, d), F32),
        compiler_params=_cparams(("arbitrary",), VMEM_LIMIT),
        name="moe_experts",
    )(block_valid, block_expert, row_tok.reshape(nb, 1, tb), row_tok.reshape(nb, 1, tb), row_slot.reshape(nb + 1, 1, tb),
      h_rows, w1, w3, w2)


def _combine_body(y0_ref, y1_ref, x1_ref, gate_ref, gm_ref, fg_ref, o_ref, *, final):
    moe = gate_ref[:, 0:1] * y0_ref[...] + gate_ref[:, 1:2] * y1_ref[...]
    x2 = x1_ref[...] + gm_ref[0] * moe
    if final:
        ms = jnp.mean(x2 * x2, axis=-1, keepdims=True)
        x2 = x2 * lax.rsqrt(ms + EPS) * fg_ref[...]
    o_ref[...] = x2


def _moe_combine(y_rows, x1, gates, gate_mlp, final_g, final):
    b, l, d = x1.shape
    n_tok = b * l
    tm = min(ROW_TILE // 2, l)
    per_b = l // tm
    out = pl.pallas_call(
        functools.partial(_combine_body, final=final),
        grid=(n_tok // tm,),
        in_specs=[pl.BlockSpec((tm, d), lambda i: (i, 0)),
                  pl.BlockSpec((tm, d), lambda i: (n_tok // tm + i, 0)),
                  pl.BlockSpec((tm, d), lambda i: (i, 0)),
                  pl.BlockSpec((tm, 2), lambda i: (i, 0)),
                  pl.BlockSpec((1, 1, d), lambda i: (i // per_b, 0, 0)),
                  pl.BlockSpec((1, d), lambda i: (0, 0))],
        out_specs=pl.BlockSpec((tm, d), lambda i: (i, 0)),
        out_shape=jax.ShapeDtypeStruct((n_tok, d), F32),
        compiler_params=_cparams(("arbitrary",), VMEM_LIMIT),
        name="moe_combine_final_norm",
    )(y_rows, y_rows, x1.reshape(n_tok, d), gates, gate_mlp, final_g.reshape(1, d))
    return out.reshape(b, l, d)


def _route(logits):
    n_tok = logits.shape[0]
    tb = MOE_ROWS
    n_asg = n_tok * TOP_K
    top_logit, top_e = lax.top_k(logits[:, :N_EXPERTS], TOP_K)
    gates = jax.nn.softmax(top_logit, axis=-1)
    e_flat = top_e.reshape(-1).astype(jnp.int32)
    asg = jnp.arange(n_asg, dtype=jnp.int32)
    by_expert = lax.sort(e_flat * n_asg + asg)
    counts = jnp.sum((e_flat[:, None] == jnp.arange(N_EXPERTS, dtype=jnp.int32)[None, :]).astype(jnp.int32), axis=0)
    starts = jnp.cumsum(counts) - counts
    padded = (counts + tb - 1) // tb * tb
    pad_ends = jnp.cumsum(padded)
    pad_starts = pad_ends - padded
    nb = (n_asg + tb - 1) // tb + N_EXPERTS + 1
    n_rows = nb * tb
    block_start = jnp.arange(nb, dtype=jnp.int32) * tb
    block_expert = jnp.minimum(jnp.sum((block_start[:, None] >= pad_ends[None, :]).astype(jnp.int32), axis=1), N_EXPERTS - 1)
    block_valid = (block_start < pad_ends[-1]).astype(jnp.int32)
    row = jnp.arange(n_rows, dtype=jnp.int32)
    row_e = jnp.repeat(block_expert, tb)
    off = row - pad_starts[row_e]
    is_pad = (off >= counts[row_e]) | (jnp.repeat(block_valid, tb) == 0)
    src = by_expert[jnp.clip(starts[row_e] + off, 0, n_asg - 1)] - row_e * n_asg
    row_asg = jnp.where(is_pad, -1, src)
    spare = tb + n_asg + jnp.cumsum(is_pad.astype(jnp.int32)) - 1
    row_slot = jnp.where(is_pad, spare, (row_asg % TOP_K) * n_tok + row_asg // TOP_K)
    row_tok = jnp.where(is_pad, 0, row_asg // TOP_K)
    row_slot = jnp.concatenate([n_asg + jnp.arange(tb, dtype=jnp.int32), row_slot])
    last_e = block_expert[jnp.maximum(pad_ends[-1] // tb - 1, 0)]
    block_expert = jnp.where(block_valid == 1, block_expert, last_e)
    n_spare = (tb + n_rows - n_asg + tb - 1) // tb * tb
    return row_tok, row_slot, gates, block_expert, block_valid, n_asg + n_spare, n_spare


def _final_norm_body(x_ref, g_ref, o_ref):
    x = x_ref[...]
    ms = jnp.mean(x * x, axis=-1, keepdims=True)
    o_ref[...] = x * lax.rsqrt(ms + EPS) * g_ref[...]


def _final_norm(x, g):
    b, l, d = x.shape
    n = b * l
    tm = min(ROW_TILE, n)
    out = pl.pallas_call(
        _final_norm_body,
        grid=(n // tm,),
        in_specs=[pl.BlockSpec((tm, d), lambda i: (i, 0)), pl.BlockSpec((1, d), lambda i: (0, 0))],
        out_specs=pl.BlockSpec((tm, d), lambda i: (i, 0)),
        out_shape=jax.ShapeDtypeStruct((n, d), F32),
        compiler_params=_cparams(("arbitrary",)),
        name="final_norm",
    )(x.reshape(n, d), g.reshape(1, d))
    return out.reshape(b, l, d)


def _rope_tables(n_tok):
    t = jnp.arange(n_tok, dtype=jnp.int32)
    row = (t // GRID_W).astype(F32)
    col = (t % GRID_W).astype(F32)
    inv_freq = ROPE_BASE ** (-jnp.arange(N_FREQ, dtype=F32) / N_FREQ)
    ang = jnp.concatenate([row[:, None] * inv_freq, col[:, None] * inv_freq], axis=-1)
    cos, sin = jnp.cos(ang), jnp.sin(ang)
    cosf = jnp.concatenate([cos, cos, cos, cos], axis=-1)
    sins = jnp.concatenate([-sin, sin, -sin, sin], axis=-1)
    return cosf, sins


def kernel(x, c, ctx, c_ctx, ada_w, ada_b, norm1_g, norm2_g, w_in, w_out, conv_w, ret_decay, gdn_a_log, gdn_dt_bias,
           gdn_norm_g, na_rpb, ffn_w1, ffn_w3, ffn_w2, moe_router, moe_w1, moe_w3, moe_w2, final_g):
    b, l, d = x.shape
    lc = ctx.shape[1]
    depth = ada_w.shape[0]
    cosf, sins = _rope_tables(l)
    ones_c = jnp.ones((lc, LANES), F32)
    zeros_c = jnp.zeros((lc, LANES), F32)

    rows = ((b + 1 + 7) // 8) * 8
    c_all = jnp.zeros((rows, d), F32).at[:b].set(c).at[b].set(c_ctx)
    mod = _ada_vectors(c_all, ada_w, ada_b).reshape(depth, rows, 6, d)

    w1p_all = jax.vmap(_pack_w_in)(w_in)
    wo_all = w_out.astype(BF16)
    log_gamma_all = jnp.log1p(-jnp.exp2(-ret_decay.astype(F32)))
    gdn_tables = jax.vmap(_gdn_tables)(conv_w, gdn_a_log, gdn_dt_bias, gdn_norm_g)
    na_bias = jax.vmap(lambda r: _na_bias_tiles(r, l // GRID_W))(na_rpb)

    y = ctx
    for layer in range(depth):
        need_ctx = layer < depth - 1
        mod_l = mod[layer, :b]
        mod_c = jnp.broadcast_to(mod[layer, b][None], (b, 6, d))
        w1p = w1p_all[layer]
        g1 = norm1_g[layer].reshape(1, d)
        ret_l, gdn_l, ab_l, abt_l, na_l = _in_projection(x, mod_l, g1, cosf, sins, w1p, rope=True)
        ret_c, gdn_c, ab_c, abt_c, na_c = _in_projection(y, mod_c, g1, ones_c, zeros_c, w1p, rope=False)

        r_l, r_c = _retention(ret_l, ret_c, log_gamma_all[layer])
        g_l, g_c = _gdn(gdn_l, ab_l, abt_l, gdn_c, ab_c, abt_c, [t[layer] for t in gdn_tables])
        n_l, n_c = _neighbourhood_attention(na_l, na_c, na_bias, layer)

        wo = wo_all[layer]
        g2 = norm2_g[layer].reshape(1, d)
        j = layer // 2
        if layer % 2 == 0:
            w1, w3, w2 = ffn_w1[j].astype(BF16), ffn_w3[j].astype(BF16), ffn_w2[j].astype(BF16)
            x = _dense_block(r_l, g_l, n_l, x, mod_l, g2, wo, w1, w3, w2)
            if need_ctx:
                y = _dense_block(r_c, g_c, n_c, y, mod_c, g2, wo, w1, w3, w2)
            if layer == depth - 1:
                x = _final_norm(x, final_g)
        else:
            wr = jnp.pad(moe_router[j].astype(F32), ((0, 0), (0, LANES - N_EXPERTS)))
            w1, w3, w2 = moe_w1[j].astype(BF16), moe_w3[j].astype(BF16), moe_w2[j].astype(BF16)

            def moe_ffn(r, g, n, xin, m, last):
                bb, ll, _ = xin.shape
                x1, h, logits = _router_block(r, g, n, xin, m, g2, wo, wr)
                row_tok, row_slot, gates, bexp, bval, n_out, n_spare = _route(logits.reshape(bb * ll, LANES))
                y_rows = _moe_experts(h.reshape(bb * ll, d // 2), row_tok, row_slot, bexp, bval, w1, w3, w2, n_out, n_spare)
                return _moe_combine(y_rows, x1, gates, m[:, 5:6, :], final_g, last)

            x = moe_ffn(r_l, g_l, n_l, x, mod_l, layer == depth - 1)
            if need_ctx:
                y = moe_ffn(r_c, g_c, n_c, y, mod_c, False)
    return x
```

```python
import functools

import numpy as np
import jax
import jax.numpy as jnp
from jax import lax
from jax.experimental import pallas as pl
from jax.experimental.pallas import tpu as pltpu

F32 = jnp.float32
BF16 = jnp.bfloat16
HIGHEST = lax.Precision.HIGHEST

LANES = 128
BF16_ROWS = 16
ROW_TILE = 512
PROJ_COLS = 512
HEAD_DIM = 64
PAIR = 2 * HEAD_DIM
GRID_W = 64
H_RET, H_GDN, H_NA = 4, 4, 8
D_RET, D_GDN, D_NA = H_RET * HEAD_DIM, H_GDN * HEAD_DIM, H_NA * HEAD_DIM
RET_CHUNK = 128
RET_UNROLL = 4
GDN_CHUNK = 64
GDN_GROUP = 4
GDN_PREP_ROWS = 512
SHORT_CONV = 5
NA_KH, NA_KW = 8, 16
NA_QROWS = 4
NA_KROWS = NA_QROWS + NA_KH
NA_PAIRS = 1
NA_UNROLL = 4
NA_QSCALE = float(np.log2(np.e)) * HEAD_DIM ** -0.5
N_FREQ = HEAD_DIM // 4
ROPE_BASE = 10000.0
N_EXPERTS = 8
TOP_K = 2
MOE_ROWS = 512
FF_CHUNK = 256
EPS = 1e-6
NEG_INF = -1e30
VMEM_LIMIT = 56 * 1024 * 1024

C_RET = 0
C_GDN = C_RET + 4 * D_RET
C_AB = C_GDN + 4 * D_GDN
C_NA = C_AB + LANES
C_END = C_NA + 3 * D_NA


def _cparams(sem, vmem=None):
    return pltpu.CompilerParams(dimension_semantics=sem, vmem_limit_bytes=vmem)


def _silu(x):
    return x * jax.nn.sigmoid(x)


def _dot(a, b):
    return jnp.dot(a, b, preferred_element_type=F32)


def _dot_nt(a, b):
    return lax.dot_general(a, b, (((1,), (1,)), ((), ())), preferred_element_type=F32)


def _dot_tn(a, b):
    return lax.dot_general(a, b, (((0,), (0,)), ((), ())), preferred_element_type=F32)


def _split_bf16(x):
    hi = x.astype(BF16)
    return hi, (x - hi.astype(F32)).astype(BF16)


def _dot_3pass(a, b):
    ah, al = _split_bf16(a)
    bh, bl = _split_bf16(b)
    return _dot(ah, bh) + (_dot(ah, bl) + _dot(al, bh))


def _dot_hi(a, b):
    return jnp.dot(a, b, preferred_element_type=F32, precision=HIGHEST)


def _ada_body(c_ref, w_ref, b_ref, o_ref):
    s = _silu(c_ref[...])
    o_ref[0] = _dot_hi(s, w_ref[0]) + b_ref[0]


def _ada_vectors(c_all, ada_w, ada_b):
    depth, d, d6 = ada_w.shape
    rows = c_all.shape[0]
    tn = d
    return pl.pallas_call(
        _ada_body,
        grid=(depth, d6 // tn),
        in_specs=[pl.BlockSpec((rows, d), lambda l, j: (0, 0)),
                  pl.BlockSpec((1, d, tn), lambda l, j: (l, 0, j)),
                  pl.BlockSpec((1, 1, tn), lambda l, j: (l, 0, j))],
        out_specs=pl.BlockSpec((1, rows, tn), lambda l, j: (l, 0, j)),
        out_shape=jax.ShapeDtypeStruct((depth, rows, d6), F32),
        compiler_params=_cparams(("arbitrary", "arbitrary")),
        name="ada_vectors",
    )(c_all, ada_w, ada_b.reshape(depth, 1, d6))


def _norm_mod(x, g, shift, scale):
    ms = jnp.mean(x * x, axis=-1, keepdims=True)
    return (x * lax.rsqrt(ms + EPS) * g) * (1.0 + scale) + shift


def _rope_slab(t, cosf, sins):
    lane = lax.broadcasted_iota(jnp.int32, t.shape, 1)
    first = (lane % HEAD_DIM) < (HEAD_DIM // 2)
    partner = jnp.where(first, pltpu.roll(t, LANES - HEAD_DIM // 2, 1), pltpu.roll(t, HEAD_DIM // 2, 1))
    return t * cosf + partner * sins


def _inproj_body(x_ref, mod_ref, g_ref, cos_ref, sin_ref, w_ref, ret_ref, gdn_ref, ab_ref, abt_ref, na_ref, *, rope):
    h = _norm_mod(x_ref[0], g_ref[...], mod_ref[0, 0:1, :], mod_ref[0, 1:2, :]).astype(BF16)
    qk = _dot(h, w_ref[:, C_RET:C_RET + 2 * D_RET])
    slabs = []
    for s in range(2 * D_RET // LANES):
        t = qk[:, s * LANES:(s + 1) * LANES]
        if rope:
            t = _rope_slab(t, cos_ref[...], sin_ref[...])
        if s >= D_RET // LANES:
            t = t * HEAD_DIM ** -0.5
        slabs.append(t)
    ret_ref[0, :, 0:2 * D_RET] = jnp.concatenate(slabs, axis=1).astype(BF16)
    ret_ref[0, :, 2 * D_RET:] = _dot(h, w_ref[:, C_RET + 2 * D_RET:C_GDN]).astype(BF16)
    for j in range(4 * D_GDN // PROJ_COLS):
        cols = slice(PROJ_COLS * j, PROJ_COLS * (j + 1))
        gdn_ref[0, :, cols] = _dot(h, w_ref[:, C_GDN + cols.start:C_GDN + cols.stop]).astype(BF16)
    ab = _dot(h, w_ref[:, C_AB:C_NA])
    ab_ref[0] = ab
    for grp in range(ab.shape[0] // LANES):
        t = ab[grp * LANES:(grp + 1) * LANES, :].T
        for p in range(D_GDN // PAIR):
            abt_ref[0, p, grp] = t[8 * p:8 * p + 8, :]
    for j in range(3 * D_NA // PROJ_COLS):
        cols = slice(PROJ_COLS * j, PROJ_COLS * (j + 1))
        na_ref[0, :, cols] = _dot(h, w_ref[:, C_NA + cols.start:C_NA + cols.stop]).astype(BF16)


def _in_projection(x, mod, g, cosf, sins, w, rope):
    b, l, d = x.shape
    tm = min(ROW_TILE, l)
    body = functools.partial(_inproj_body, rope=rope)
    return pl.pallas_call(
        body,
        grid=(b, l // tm),
        in_specs=[pl.BlockSpec((1, tm, d), lambda i, j: (i, j, 0)),
                  pl.BlockSpec((1, 6, d), lambda i, j: (i, 0, 0)),
                  pl.BlockSpec((1, d), lambda i, j: (0, 0)),
                  pl.BlockSpec((tm, LANES), lambda i, j: (j, 0)),
                  pl.BlockSpec((tm, LANES), lambda i, j: (j, 0)),
                  pl.BlockSpec((d, C_END), lambda i, j: (0, 0))],
        out_specs=[pl.BlockSpec((1, tm, 4 * D_RET), lambda i, j: (i, j, 0)),
                   pl.BlockSpec((1, tm, 4 * D_GDN), lambda i, j: (i, j, 0)),
                   pl.BlockSpec((1, tm, LANES), lambda i, j: (i, j, 0)),
                   pl.BlockSpec((1, D_GDN // PAIR, tm // LANES, 8, LANES), lambda i, j: (i, 0, j, 0, 0)),
                   pl.BlockSpec((1, tm, 3 * D_NA), lambda i, j: (i, j, 0))],
        out_shape=[jax.ShapeDtypeStruct((b, l, 4 * D_RET), BF16),
                   jax.ShapeDtypeStruct((b, l, 4 * D_GDN), BF16),
                   jax.ShapeDtypeStruct((b, l, LANES), F32),
                   jax.ShapeDtypeStruct((b, D_GDN // PAIR, l // LANES, 8, LANES), F32),
                   jax.ShapeDtypeStruct((b, l, 3 * D_NA), BF16)],
        compiler_params=_cparams(("arbitrary", "arbitrary"), VMEM_LIMIT),
        name="in_projection",
    )(x, mod, g, cosf, sins, w)


def _pack_w_in(w_in):
    d = w_in.shape[0]
    c1 = 4 * D_RET
    c2 = c1 + 4 * D_GDN
    q0 = c2 + 4 * H_GDN
    col = jnp.arange(w_in.shape[1])
    w_in = jnp.where((col >= q0) & (col < q0 + D_NA), w_in * NA_QSCALE, w_in).astype(BF16)
    ab = w_in[:, c2:c2 + 4 * H_GDN]
    ab = ab.reshape(d, 2, 2, H_GDN // 2, 2)
    ab = jnp.transpose(ab, (0, 3, 1, 2, 4)).reshape(d, 4 * H_GDN)
    ab = jnp.pad(ab, ((0, 0), (0, LANES - 4 * H_GDN)))
    return jnp.concatenate([w_in[:, :c2], ab, w_in[:, c2 + 4 * H_GDN:]], axis=1)


def _head_masks(shape):
    lane = lax.broadcasted_iota(jnp.int32, shape, len(shape) - 1)
    return lane < HEAD_DIM, lane >= HEAD_DIM


def _per_head(lo, hi, shape):
    m0, _ = _head_masks(shape)
    return jnp.where(m0, lo, hi)


def _head_sumsq(o):
    m0, m1 = _head_masks(o.shape)
    sq = o * o
    s0 = jnp.sum(jnp.where(m0, sq, 0.0), axis=-1, keepdims=True)
    s1 = jnp.sum(jnp.where(m1, sq, 0.0), axis=-1, keepdims=True)
    return jnp.where(m0, s0, s1)


def _ret_body(lg_ref, ql, kl, vl, gl, qc, kc, vc, gc, r_ref, rc_ref, sb_ref):
    c = RET_CHUNK
    npair = D_RET // PAIR
    lane_shape = (c, PAIR)
    pos = lax.broadcasted_iota(jnp.int32, lane_shape, 0).astype(F32)
    ii = lax.broadcasted_iota(jnp.int32, (c, c), 0)
    jj = lax.broadcasted_iota(jnp.int32, (c, c), 1)
    diff = (ii - jj).astype(F32)
    m0, m1 = _head_masks(lane_shape)
    masks = (m0, m1)
    bi = lax.broadcasted_iota(jnp.int32, (PAIR, PAIR), 0) // HEAD_DIM
    bj = lax.broadcasted_iota(jnp.int32, (PAIR, PAIR), 1) // HEAD_DIM
    bd = bi == bj
    cst = []
    for p in range(npair):
        lgf = _per_head(lg_ref[0, 2 * p], lg_ref[0, 2 * p + 1], lane_shape)
        lgb = _per_head(lg_ref[1, 2 * p], lg_ref[1, 2 * p + 1], lane_shape)
        dmats = [jnp.where(diff > 0, jnp.exp(lg_ref[0, 2 * p + hh] * diff),
                           jnp.where(diff < 0, jnp.exp(-lg_ref[1, 2 * p + hh] * diff), 2.0)) for hh in range(2)]
        cst.append(dict(qdf=jnp.exp(lgf * (pos + 1.0)), kdf=jnp.exp(lgf * (c - 1.0 - pos)),
                        qdb=jnp.exp(lgb * (c - pos)), kdb=jnp.exp(lgb * pos),
                        cdf=jnp.exp(lgf[0:1] * c), cdb=jnp.exp(lgb[0:1] * c), dmats=dmats))

    def sweep(q_ref, k_ref, v_ref, g_ref, o_ref, n, sf0, sb0):
        unroll = min(RET_UNROLL, n)

        def bchunk(ci, sbs):
            sl = pl.ds(pl.multiple_of(ci * c, c), c)
            out = []
            for p in range(npair):
                lanes = slice(p * PAIR, (p + 1) * PAIR)
                sb_ref[p, ci] = sbs[p]
                kd = (k_ref[0, sl, lanes].astype(F32) * cst[p]["kdb"]).astype(BF16)
                out.append(sbs[p] * cst[p]["cdb"] + jnp.where(bd, _dot_tn(kd, v_ref[0, sl, lanes]), 0.0))
            return tuple(out)

        def bstep(t, sbs):
            for u in range(unroll):
                sbs = bchunk(n - 1 - (t * unroll + u), sbs)
            return sbs

        sb_fin = lax.fori_loop(0, n // unroll, bstep, sb0)

        def fchunk(ci, sfs):
            sl = pl.ds(pl.multiple_of(ci * c, c), c)
            out = []
            for p in range(npair):
                lanes = slice(p * PAIR, (p + 1) * PAIR)
                k_ = cst[p]
                q = q_ref[0, sl, lanes]
                k = k_ref[0, sl, lanes]
                v = v_ref[0, sl, lanes]
                qf = q.astype(F32)
                qd = jnp.concatenate([(qf * k_["qdf"]).astype(BF16), (qf * k_["qdb"]).astype(BF16)], axis=1)
                o = _dot(qd, jnp.concatenate([sfs[p].astype(BF16), sb_ref[p, ci].astype(BF16)], axis=0))
                s2 = _dot_nt(_stack_heads(q), k)
                pm = jnp.concatenate([s2[:c] * k_["dmats"][0], s2[c:] * k_["dmats"][1]], axis=0).astype(BF16)
                pv = _dot(pm, v)
                o = o + jnp.where(masks[0], pv[:c], pv[c:])
                on = o * lax.rsqrt(_head_sumsq(o) * (1.0 / HEAD_DIM) + EPS)
                o_ref[0, sl, lanes] = (on * _silu(g_ref[0, sl, lanes].astype(F32))).astype(o_ref.dtype)
                kd = (k.astype(F32) * k_["kdf"]).astype(BF16)
                out.append(sfs[p] * k_["cdf"] + jnp.where(bd, _dot_tn(kd, v), 0.0))
            return tuple(out)

        def fstep(t, sfs):
            for u in range(unroll):
                sfs = fchunk(t * unroll + u, sfs)
            return sfs

        sf_fin = lax.fori_loop(0, n // unroll, fstep, sf0)
        return sf_fin, sb_fin

    z = tuple(jnp.zeros((PAIR, PAIR), F32) for _ in range(npair))
    sfc, sbc = sweep(qc, kc, vc, gc, rc_ref, qc.shape[1] // c, z, z)
    sweep(ql, kl, vl, gl, r_ref, ql.shape[1] // c, sfc, sbc)


def _retention(ret_l, ret_c, log_gamma):
    b, l, _ = ret_l.shape
    lc = ret_c.shape[1]
    npair = D_RET // PAIR

    def col(k):
        return lambda i: (i, 0, k)

    lat = [pl.BlockSpec((1, l, D_RET), col(k)) for k in range(4)]
    ctx = [pl.BlockSpec((1, lc, D_RET), col(k)) for k in range(4)]
    return pl.pallas_call(
        _ret_body,
        grid=(b,),
        in_specs=[pl.BlockSpec(memory_space=pltpu.SMEM)] + lat + ctx,
        out_specs=[pl.BlockSpec((1, l, D_RET), lambda i: (i, 0, 0)),
                   pl.BlockSpec((1, lc, D_RET), lambda i: (i, 0, 0))],
        out_shape=[jax.ShapeDtypeStruct((b, l, D_RET), BF16), jax.ShapeDtypeStruct((b, lc, D_RET), BF16)],
        scratch_shapes=[pltpu.VMEM((npair, max(l, lc) // RET_CHUNK, PAIR, PAIR), F32)],
        compiler_params=_cparams(("arbitrary",), VMEM_LIMIT),
        name="retention",
    )(log_gamma, ret_l, ret_l, ret_l, ret_l, ret_c, ret_c, ret_c, ret_c)


def _stack_heads(t):
    m0, m1 = _head_masks(t.shape)
    z = jnp.zeros_like(t)
    return jnp.concatenate([jnp.where(m0, t, z), jnp.where(m1, t, z)], axis=0)


def _scan_sum(x, axis, n, reverse):
    size = x.shape[axis]
    pos = lax.broadcasted_iota(jnp.int32, x.shape, axis)
    s = 1
    while s < n:
        if reverse:
            x = x + jnp.where(pos + s < n, pltpu.roll(x, size - s, axis), 0.0)
        else:
            x = x + jnp.where(pos >= s, pltpu.roll(x, s, axis), 0.0)
        s *= 2
    return x


def _gdn_prep(src_ref, slabs, cw_ref, r0):
    n = src_ref.shape[1]
    blk = min(GDN_PREP_ROWS, n)
    halo = BF16_ROWS

    def block(i, _):
        b0 = pl.multiple_of(i * blk, blk)
        lo_rows = pl.ds(pl.multiple_of(jnp.maximum(b0 - halo, 0), halo), halo)
        hi_rows = pl.ds(pl.multiple_of(jnp.minimum(b0 + blk, n - halo), halo), halo)
        for col, dst_ref, l2, scale in slabs:
            cols = slice(col, col + LANES)
            cw = cw_ref[:, cols]
            x = src_ref[0, pl.ds(b0, blk), cols].astype(F32)
            lo = jnp.where(b0 > 0, src_ref[0, lo_rows, cols].astype(F32), 0.0)
            hi = jnp.where(b0 + blk < n, src_ref[0, hi_rows, cols].astype(F32), 0.0)
            ext = jnp.concatenate([lo, x, hi], axis=0)
            acc = x * cw[SHORT_CONV // 2:SHORT_CONV // 2 + 1, :]
            for j in range(SHORT_CONV):
                s = j - SHORT_CONV // 2
                if s != 0:
                    sh = pltpu.roll(ext, (-s) % (blk + 2 * halo), 0)
                    acc = acc + sh[halo:halo + blk] * cw[j:j + 1, :]
            y = _silu(acc)
            if l2:
                y = y * lax.rsqrt(_head_sumsq(y) + EPS)
            if scale != 1.0:
                y = y * scale
            dst_ref[pl.ds(pl.multiple_of(r0 + b0, halo), blk), col % D_GDN:col % D_GDN + LANES] = y.astype(dst_ref.dtype)
        return 0

    lax.fori_loop(0, n // blk, block, 0)


def _gdn_body(x_ref, xc_ref, ab_ref, abt_ref, cw_ref, cst_ref, cstt_ref, ng_ref,
              o_ref, oc_ref, qn, kn, vn, oacc, st_s, x_s, t_s, rhs_s, u_s, wq_s, ak_s, egl_s):
    c = GDN_CHUNK
    c2 = 2 * c
    grp = GDN_GROUP
    npair = D_GDN // PAIR
    lc = xc_ref.shape[1]
    l = x_ref.shape[1]
    nc = lc // c
    nt = (lc + l) // c
    ii = lax.broadcasted_iota(jnp.int32, (c2, c2), 0)
    jj = lax.broadcasted_iota(jnp.int32, (c2, c2), 1)
    same = (ii // c) == (jj // c)
    eye = (ii == jj).astype(F32)
    dirs = ((same & (ii >= jj), same & (ii > jj), c - 1),
            (same & (ii <= jj), same & (ii < jj), 0))

    def chains(g, slot):
        out = []
        for gi in range(grp):
            t = g * grp + gi
            cf = t
            cb = jnp.where(t < nc, nc - 1 - t, nt + nc - 1 - t)
            for p in range(npair):
                for d in range(2):
                    out.append((((slot * grp + gi) * 2 + d) * npair + p, p, d, cf if d == 0 else cb))
        return out

    def stage_inputs(idx, p, d, ci):
        incl, strict, last = dirs[d]
        lanes = slice(p * PAIR, (p + 1) * PAIR)
        nea, dtb = cst_ref[0:1, :], cst_ref[1:2, :]
        neat, dtbt = cstt_ref[p, 0], cstt_ref[p, 1]
        sl = pl.ds(pl.multiple_of(ci * c, c), c)
        abv = ab_ref[0, sl, :]
        gall = nea * jax.nn.softplus(abv + dtb)
        ball = jax.nn.sigmoid(abv)
        shape = (c, PAIR)
        c0 = 8 * p + 2 * d
        gl = _per_head(gall[:, c0:c0 + 1], gall[:, c0 + 1:c0 + 2], shape)
        bl = _per_head(ball[:, c0 + 4:c0 + 5], ball[:, c0 + 5:c0 + 6], shape)
        gcum = _scan_sum(gl, 0, c, d == 1)
        abt = abt_ref[0, p, ci // 2]
        abt = jnp.where(ci % 2 == 1, pltpu.roll(abt, c, 1), abt)
        gt = neat * jax.nn.softplus(abt + dtbt)
        gtc = _scan_sum(gt, 1, c, d == 1)
        grow = jnp.concatenate([gtc[2 * d:2 * d + 1, 0:c], gtc[2 * d + 1:2 * d + 2, 0:c]], axis=1)
        gcol = jnp.concatenate([gcum[:, 0:1], gcum[:, HEAD_DIM:HEAD_DIM + 1]], axis=0)
        dec = jnp.where(incl, jnp.exp(jnp.where(incl, gcol - grow, 0.0)), 0.0)
        dec = jnp.where(ii == jj, 1.0, dec)
        q = qn[sl, lanes]
        k = kn[sl, lanes]
        v = vn[sl, lanes]
        kf = k.astype(F32)
        eg = jnp.exp(gcum)
        kb = kf * bl
        k_st = _stack_heads(k)
        kq = _dot_nt(jnp.concatenate([_stack_heads(kb.astype(BF16)), _stack_heads(q)], axis=0), k_st)
        a = jnp.where(strict, kq[:c2] * dec, 0.0)
        attn = (kq[c2:] * dec).astype(BF16)
        x_s[idx] = (-a).astype(BF16)
        t_s[idx] = eye - a
        rhs_s[idx] = jnp.concatenate([_stack_heads((v.astype(F32) * bl).astype(BF16)),
                                      _stack_heads((kb * eg).astype(BF16))], axis=1)
        glast = gcum[last:last + 1, :]
        wq_s[idx, c2:, :] = _stack_heads((q.astype(F32) * eg).astype(BF16))
        ak_s[idx] = jnp.concatenate([attn, _stack_heads(kf * jnp.exp(glast - gcum)).T.astype(BF16)], axis=0)
        egl_s[idx] = jnp.broadcast_to(jnp.exp(glast), (8, PAIR))

    def stage_group(g, slot):
        todo = chains(g, slot)
        for idx, p, d, ci in todo:
            stage_inputs(idx, p, d, ci)
        for idx, _, _, _ in todo:
            xb = x_s[idx]
            x_s[idx] = _dot(xb, xb).astype(BF16)
        for _ in range(4):
            for idx, _, _, _ in todo:
                xb = x_s[idx]
                t = t_s[idx]
                prod = _dot(jnp.concatenate([t.astype(BF16), xb], axis=0), xb)
                t_s[idx] = t + prod[:c2]
                x_s[idx] = prod[c2:].astype(BF16)
        for idx, _, _, _ in todo:
            t = t_s[idx]
            t = t + _dot(t.astype(BF16), x_s[idx])
            sol = _dot(t.astype(BF16), rhs_s[idx])
            u_s[idx] = sol[:, :PAIR]
            wq_s[idx, :c2, :] = sol[:, PAIR:].astype(BF16)

    def recur_group(g, slot):
        sts = {(p, d): st_s[d * npair + p] for p in range(npair) for d in range(2)}
        for idx, p, d, ci in chains(g, slot):
            sl = pl.ds(pl.multiple_of(ci * c, c), c)
            st = sts[(p, d)]
            ws = _dot(wq_s[idx], st.astype(BF16))
            v_new = (u_s[idx] - ws[:c2]).astype(BF16)
            av = _dot(ak_s[idx], v_new)
            o_st = ws[c2:] + av[:c2]
            oacc[sl, p * PAIR:(p + 1) * PAIR] += o_st[:c] + o_st[c:]
            sts[(p, d)] = st * egl_s[idx][0:1, :] + av[c2:]
        for (p, d), st in sts.items():
            st_s[d * npair + p] = st

    def prep(src_ref, r0):
        slabs = [(k * D_GDN + s * LANES, dst, l2, scale)
                 for s in range(D_GDN // LANES)
                 for k, (dst, l2, scale) in enumerate(((qn, True, HEAD_DIM ** -0.5), (kn, True, 1.0), (vn, False, 1.0)))]
        _gdn_prep(src_ref, slabs, cw_ref, r0)

    def finish(src_ref, out_ref, r0, n):
        for p in range(npair):
            lanes = slice(p * PAIR, (p + 1) * PAIR)
            o = oacc[r0:r0 + n, lanes]
            on = o * lax.rsqrt(_head_sumsq(o) * (1.0 / HEAD_DIM) + EPS)
            gate = src_ref[0, :, 3 * D_GDN + p * PAIR:3 * D_GDN + (p + 1) * PAIR].astype(F32)
            out_ref[0, :, lanes] = (on * ng_ref[...] * _silu(gate)).astype(out_ref.dtype)

    st_s[...] = jnp.zeros_like(st_s)
    oacc[...] = jnp.zeros_like(oacc)
    prep(xc_ref, 0)
    prep(x_ref, lc)

    ng = nt // grp
    stage_group(0, 0)

    def two_groups(kk, _):
        g = 2 * kk
        recur_group(g, 0)
        stage_group(g + 1, 1)
        recur_group(g + 1, 1)
        stage_group(g + 2, 0)
        return 0

    lax.fori_loop(0, (ng - 1) // 2, two_groups, 0)
    if ng % 2 == 0:
        recur_group(ng - 2, 0)
        stage_group(ng - 1, 1)
        recur_group(ng - 1, 1)
    else:
        recur_group(ng - 1, 0)
    finish(xc_ref, oc_ref, 0, lc)
    finish(x_ref, o_ref, lc, l)


def _gdn_tables(conv_w, a_log, dt_bias, norm_g):
    npair = D_GDN // PAIR
    nea = -jnp.exp(a_log.astype(F32)).reshape(2, npair, 2)
    dtb = dt_bias.astype(F32).reshape(2, npair, 2)
    rows = jnp.stack([jnp.transpose(nea, (1, 0, 2)).reshape(npair, 4), jnp.transpose(dtb, (1, 0, 2)).reshape(npair, 4)], axis=1)
    cst = jnp.pad(jnp.transpose(jnp.pad(rows, ((0, 0), (0, 0), (0, 4))), (1, 0, 2)).reshape(2, 8 * npair),
                  ((0, 6), (0, LANES - 8 * npair)))
    cstt = jnp.broadcast_to(jnp.pad(rows, ((0, 0), (0, 0), (0, 4)))[..., None], (npair, 2, 8, LANES))
    cw = jnp.pad(conv_w.astype(F32), ((0, 8 - SHORT_CONV), (0, 0)))
    ng = jnp.tile(norm_g.astype(F32), 2).reshape(1, PAIR)
    return cw, cst, cstt, ng


def _gdn(gdn_l, ab_l, abt_l, gdn_c, ab_c, abt_c, tables):
    b, l, _ = gdn_l.shape
    lc = gdn_c.shape[1]
    npair = D_GDN // PAIR
    lt = l + lc
    assert (lt // GDN_CHUNK) % GDN_GROUP == 0 and lc % LANES == 0
    cw, cst, cstt, ng = tables

    nstage = 2 * GDN_GROUP * 2 * npair

    def whole(shape):
        nd = len(shape)
        return pl.BlockSpec(shape, lambda i: (0,) * nd)

    def per_batch(shape):
        nd = len(shape)
        return pl.BlockSpec((1,) + shape, lambda i: (i,) + (0,) * nd)

    sq = (nstage, PAIR, PAIR)
    return pl.pallas_call(
        _gdn_body,
        grid=(b,),
        in_specs=[per_batch((l, 4 * D_GDN)), per_batch((lc, 4 * D_GDN)), per_batch((lt, LANES)),
                  per_batch((npair, lt // LANES, 8, LANES)),
                  whole((8, 3 * D_GDN)), whole((8, LANES)), whole((npair, 2, 8, LANES)), whole((1, PAIR))],
        out_specs=[per_batch((l, D_GDN)), per_batch((lc, D_GDN))],
        out_shape=[jax.ShapeDtypeStruct((b, l, D_GDN), BF16), jax.ShapeDtypeStruct((b, lc, D_GDN), BF16)],
        scratch_shapes=[pltpu.VMEM((lt, D_GDN), BF16)] * 3 + [pltpu.VMEM((lt, D_GDN), F32)]
        + [pltpu.VMEM((2 * npair, PAIR, PAIR), F32), pltpu.VMEM(sq, BF16), pltpu.VMEM(sq, F32),
           pltpu.VMEM((nstage, PAIR, 2 * PAIR), BF16), pltpu.VMEM(sq, F32),
           pltpu.VMEM((nstage, 2 * PAIR, PAIR), BF16), pltpu.VMEM((nstage, 2 * PAIR, PAIR), BF16),
           pltpu.VMEM((nstage, 8, PAIR), F32)],
        compiler_params=_cparams(("arbitrary",), VMEM_LIMIT),
        name="gated_deltanet",
    )(gdn_l, gdn_c, jnp.concatenate([ab_c, ab_l], axis=1), jnp.concatenate([abt_c, abt_l], axis=2), cw, cst, cstt, ng)


def _na_bias_tiles(rpb, rows):
    w = GRID_W
    ext = jnp.pad(rpb.astype(F32) * float(np.log2(np.e)), ((0, 0), (0, 0), (w - NA_KW, w - NA_KW)))
    cq = np.arange(w)[:, None]
    ck = np.arange(w)[None, :]
    tb = ext[:, :, jnp.asarray(ck - cq + w - 1)]
    ws = np.clip(cq - NA_KW // 2, 0, w - NA_KW)
    col_ok = (ck >= ws) & (ck < ws + NA_KW)
    tb = jnp.where(jnp.asarray(col_ok)[None, None], tb, NEG_INF)
    none = 2 * NA_KH - 1
    tb = jnp.concatenate([tb, jnp.full((tb.shape[0], 1, w, w), NEG_INF, F32)], axis=1)
    nt = rows // NA_QROWS
    idx = np.zeros((3, NA_QROWS, NA_KROWS), np.int32)
    for cls, t in enumerate((0, 1, nt - 1)):
        ks = int(np.clip(NA_QROWS * t - NA_KH // 2, 0, rows - NA_KROWS))
        for rl in range(NA_QROWS):
            r = NA_QROWS * t + rl
            r0 = int(np.clip(r - NA_KH // 2, 0, rows - NA_KH))
            for j in range(NA_KROWS):
                kr = ks + j
                idx[cls, rl, j] = kr - r + NA_KH - 1 if r0 <= kr < r0 + NA_KH else none
    pairs = idx.reshape(3, NA_QROWS, NA_KROWS // 2, 2)
    uniq, inv = np.unique(pairs.reshape(-1, 2), axis=0, return_inverse=True)
    wide = jnp.concatenate([tb[:, jnp.asarray(uniq[:, 0])], tb[:, jnp.asarray(uniq[:, 1])]], axis=-1)
    tiles = wide[:, jnp.asarray(inv.reshape(3, NA_QROWS, NA_KROWS // 2))]
    tiles = jnp.transpose(tiles, (1, 0, 3, 2, 4, 5))
    return tiles.reshape(3, tb.shape[0], NA_KROWS // 2, NA_QROWS * w, 2 * w)


def _softmax_pv(s_parts, v_parts):
    m = functools.reduce(jnp.maximum, [jnp.max(s, axis=-1, keepdims=True) for s in s_parts])
    ps = [jnp.exp2(s - m) for s in s_parts]
    den = functools.reduce(lambda x, y: x + y, [jnp.sum(p, axis=-1, keepdims=True) for p in ps])
    o = _dot(jnp.concatenate([p.astype(BF16) for p in ps], axis=1), jnp.concatenate(v_parts, axis=0))
    return o / den


def _na_body(q_ref, k_ref, v_ref, qc_ref, kc_ref, vc_ref, bias_ref, o_ref, oc_ref, *, rows):
    w = GRID_W
    tq = NA_QROWS * w
    tk = NA_KROWS * w
    nt = rows // NA_QROWS
    m0, m1 = _head_masks((tq, PAIR))
    masks = (m0, m1)

    def one_tile(t):
        cls = jnp.where(t > 0, 1, 0) + jnp.where(t == nt - 1, 1, 0)
        ks = jnp.clip(NA_QROWS * t - NA_KH // 2, 0, rows - NA_KROWS) * w
        ksl = pl.ds(pl.multiple_of(ks, w), tk)
        qsl = pl.ds(pl.multiple_of(t * tq, tq), tq)
        for p in range(NA_PAIRS):
            lanes = slice(p * PAIR, (p + 1) * PAIR)
            q = q_ref[0, qsl, lanes]
            kb = k_ref[0, ksl, lanes]
            vb = v_ref[0, ksl, lanes]
            kc = kc_ref[0, :, lanes]
            vc = vc_ref[0, :, lanes]
            s_all = _dot_nt(_stack_heads(q), jnp.concatenate([kb, kc], axis=0))
            v_all = jnp.concatenate([vb, vc], axis=0)
            acc = jnp.zeros((tq, PAIR), F32)
            for hh in range(2):
                bias = jnp.concatenate([bias_ref[0, cls, 2 * p + hh, s] for s in range(NA_KROWS // 2)]
                                       + [jnp.zeros((tq, kc.shape[0]), F32)], axis=1)
                o_h = _softmax_pv([s_all[hh * tq:(hh + 1) * tq] + bias], [v_all])
                acc = acc + jnp.where(masks[hh], o_h, 0.0)
            o_ref[0, qsl, lanes] = acc.astype(o_ref.dtype)

    unroll = min(NA_UNROLL, nt)

    def tiles(i, _):
        for u in range(unroll):
            one_tile(i * unroll + u)
        return 0

    lax.fori_loop(0, nt // unroll, tiles, 0)
    for p in range(NA_PAIRS):
        lanes = slice(p * PAIR, (p + 1) * PAIR)
        qc = qc_ref[0, :, lanes]
        kc = kc_ref[0, :, lanes]
        vc = vc_ref[0, :, lanes]
        mc0, mc1 = _head_masks(qc.shape)
        accc = jnp.zeros(qc.shape, F32)
        for mk in (mc0, mc1):
            qm = jnp.where(mk, qc, jnp.zeros_like(qc))
            accc = accc + jnp.where(mk, _softmax_pv([_dot_nt(qm, kc)], [vc]), 0.0)
        oc_ref[0, :, lanes] = accc.astype(oc_ref.dtype)


def _neighbourhood_attention(na_l, na_c, bias, layer):
    b, l, _ = na_l.shape
    lc = na_c.shape[1]
    rows = l // GRID_W
    wide = NA_PAIRS * PAIR
    ngrp = D_NA // wide

    def col(k):
        return lambda p, i: (i, 0, k * ngrp + p)

    lat = [pl.BlockSpec((1, l, wide), col(k)) for k in range(3)]
    ctx = [pl.BlockSpec((1, lc, wide), col(k)) for k in range(3)]
    return pl.pallas_call(
        functools.partial(_na_body, rows=rows),
        grid=(ngrp, b),
        in_specs=lat + ctx + [pl.BlockSpec((1, 3, 2 * NA_PAIRS, NA_KROWS // 2, NA_QROWS * GRID_W, 2 * GRID_W),
                                           lambda p, i: (layer, 0, p, 0, 0, 0))],
        out_specs=[pl.BlockSpec((1, l, wide), lambda p, i: (i, 0, p)),
                   pl.BlockSpec((1, lc, wide), lambda p, i: (i, 0, p))],
        out_shape=[jax.ShapeDtypeStruct((b, l, D_NA), BF16), jax.ShapeDtypeStruct((b, lc, D_NA), BF16)],
        compiler_params=_cparams(("arbitrary", "arbitrary"), VMEM_LIMIT),
        name="neighbourhood_attention",
    )(na_l, na_l, na_l, na_c, na_c, na_c, bias)


def _outproj_residual(r_ref, g_ref, n_ref, x_ref, mod_ref, wo_ref):
    mix = jnp.concatenate([r_ref[0], g_ref[0], n_ref[0]], axis=1)
    return x_ref[0] + mod_ref[0, 2:3, :] * _dot(mix, wo_ref[...])


def _swiglu_chunk(hb, w1_ref, w3_ref, w2_ref, j):
    cols = slice(j * FF_CHUNK, (j + 1) * FF_CHUNK)
    t = (_silu(_dot(hb, w1_ref[:, cols])) * _dot(hb, w3_ref[:, cols])).astype(BF16)
    return _dot(t, w2_ref[cols, :])


def _swiglu_chunks(hb, w1_ref, w3_ref, w2_ref, acc_ref):
    for j in range(w2_ref.shape[0] // FF_CHUNK):
        if j == 0:
            acc_ref[...] = _swiglu_chunk(hb, w1_ref, w3_ref, w2_ref, j)
        else:
            acc_ref[...] += _swiglu_chunk(hb, w1_ref, w3_ref, w2_ref, j)


def _dense_body(r_ref, g_ref, n_ref, x_ref, mod_ref, g2_ref, wo_ref, w1_ref, w3_ref, w2_ref, o_ref, acc_ref):
    x1 = _outproj_residual(r_ref, g_ref, n_ref, x_ref, mod_ref, wo_ref)
    hb = _norm_mod(x1, g2_ref[...], mod_ref[0, 3:4, :], mod_ref[0, 4:5, :]).astype(BF16)
    _swiglu_chunks(hb, w1_ref, w3_ref, w2_ref, acc_ref)
    o_ref[0] = x1 + mod_ref[0, 5:6, :] * acc_ref[...]


def _pack_bf16_pairs(h):
    m = h.shape[1] // 2
    bits = lax.bitcast_convert_type(h.astype(BF16).astype(F32), jnp.uint32)
    return (bits[:, :m] >> 16) | (bits[:, m:] & jnp.uint32(0xFFFF0000))


def _unpack_bf16_pairs(u):
    lo = lax.bitcast_convert_type(u << 16, F32)
    hi = lax.bitcast_convert_type(u & jnp.uint32(0xFFFF0000), F32)
    return jnp.concatenate([lo, hi], axis=1).astype(BF16)


def _router_body(r_ref, g_ref, n_ref, x_ref, mod_ref, g2_ref, wo_ref, wr_ref, x1_ref, h_ref, lg_ref):
    x1 = _outproj_residual(r_ref, g_ref, n_ref, x_ref, mod_ref, wo_ref)
    h = _norm_mod(x1, g2_ref[...], mod_ref[0, 3:4, :], mod_ref[0, 4:5, :])
    x1_ref[0] = x1
    h_ref[0] = _pack_bf16_pairs(h)
    lg_ref[0] = _dot_3pass(h, wr_ref[...])


def _resident(shape):
    nd = len(shape)
    return pl.BlockSpec(shape, lambda i, j: (0,) * nd, pipeline_mode=pl.Buffered(1))


def _mixer_specs(tm, d):
    return [pl.BlockSpec((1, tm, D_RET), lambda i, j: (i, j, 0)),
            pl.BlockSpec((1, tm, D_GDN), lambda i, j: (i, j, 0)),
            pl.BlockSpec((1, tm, D_NA), lambda i, j: (i, j, 0)),
            pl.BlockSpec((1, tm, d), lambda i, j: (i, j, 0)),
            pl.BlockSpec((1, 6, d), lambda i, j: (i, 0, 0)),
            pl.BlockSpec((1, d), lambda i, j: (0, 0))]


def _dense_block(r, g, n, x, mod, g2, wo, w1, w3, w2):
    b, l, d = x.shape
    tm = min(ROW_TILE, l)
    dff = w1.shape[1]
    return pl.pallas_call(
        _dense_body,
        grid=(b, l // tm),
        in_specs=_mixer_specs(tm, d) + [_resident((d, d)), _resident((d, dff)), _resident((d, dff)), _resident((dff, d))],
        out_specs=pl.BlockSpec((1, tm, d), lambda i, j: (i, j, 0)),
        out_shape=jax.ShapeDtypeStruct((b, l, d), F32),
        scratch_shapes=[pltpu.VMEM((tm, d), F32)],
        compiler_params=_cparams(("arbitrary", "arbitrary"), VMEM_LIMIT),
        name="outproj_swiglu",
    )(r, g, n, x, mod, g2, wo, w1, w3, w2)


def _router_block(r, g, n, x, mod, g2, wo, wr):
    b, l, d = x.shape
    tm = min(ROW_TILE, l)
    blk = pl.BlockSpec((1, tm, d), lambda i, j: (i, j, 0))
    return pl.pallas_call(
        _router_body,
        grid=(b, l // tm),
        in_specs=_mixer_specs(tm, d) + [_resident((d, d)), _resident((d, LANES))],
        out_specs=[blk, pl.BlockSpec((1, tm, d // 2), lambda i, j: (i, j, 0)), pl.BlockSpec((1, tm, LANES), lambda i, j: (i, j, 0))],
        out_shape=[jax.ShapeDtypeStruct((b, l, d), F32), jax.ShapeDtypeStruct((b, l, d // 2), jnp.uint32),
                   jax.ShapeDtypeStruct((b, l, LANES), F32)],
        compiler_params=_cparams(("arbitrary", "arbitrary"), VMEM_LIMIT),
        name="outproj_router",
    )(r, g, n, x, mod, g2, wo, wr)


def _row_copy_in(idx_ref, r, src_hbm, dst_ref, sem):
    return pltpu.make_async_copy(src_hbm.at[pl.ds(idx_ref[0, 0, r], 1), :], dst_ref.at[pl.ds(r, 1), :], sem)


def _row_copy_out(idx_ref, r, src_ref, dst_hbm, sem):
    return pltpu.make_async_copy(src_ref.at[pl.ds(r, 1), :], dst_hbm.at[pl.ds(idx_ref[0, 0, r], 1), :], sem)


def _moe_body(bval_ref, bexp_ref, tok_ref, tokn_ref, slotp_ref, h_hbm, w1_ref, w3_ref, w2_ref, y_hbm,
              xbuf, ybuf, gsem, ssem, *, n_asg):
    i = pl.program_id(0)
    tb = xbuf.shape[1]
    nch = w2_ref.shape[1] // FF_CHUNK
    cur = i % 2
    per = -(-tb // nch)
    valid = bval_ref[i] == 1
    prev_valid = bval_ref[jnp.maximum(i - 1, 0)] == 1

    def wait_gather(slot):
        pltpu.make_async_copy(h_hbm.at[pl.ds(0, tb), :], xbuf.at[slot], gsem).wait()

    def wait_scatter(slot):
        pltpu.make_async_copy(ybuf.at[slot], y_hbm.at[pl.ds(0, tb), :], ssem).wait()

    @pl.when(i == 0)
    def _():
        ybuf[1] = jnp.zeros((tb, ybuf.shape[2]), F32)
        fills = [pltpu.make_async_copy(ybuf.at[1], y_hbm.at[pl.ds(r0, tb), :], ssem)
                 for r0 in range(n_asg, y_hbm.shape[0], tb)]
        for f in fills:
            f.start()
        for f in fills:
            f.wait()

        def issue(r, _):
            _row_copy_in(tok_ref, r, h_hbm, xbuf.at[0], gsem).start()
            return 0

        lax.fori_loop(0, tb, issue, 0)

    @pl.when(valid)
    def _():
        wait_gather(cur)
        hb = _unpack_bf16_pairs(xbuf[cur])
        acc = ybuf.at[cur]
        prv = ybuf.at[1 - cur]
        nxt = xbuf.at[1 - cur]
        for j in range(nch):
            for r in range(j * per, min(tb, (j + 1) * per)):
                _row_copy_in(tokn_ref, r, h_hbm, nxt, gsem).start()
                _row_copy_out(slotp_ref, r, prv, y_hbm, ssem).start()
            part = _swiglu_chunk(hb, w1_ref.at[0], w3_ref.at[0], w2_ref.at[0], j)
            if j == 0:
                acc[...] = part
            else:
                acc[...] += part
        wait_scatter(1 - cur)

    @pl.when(jnp.logical_not(valid) & prev_valid & (i > 0))
    def _():
        wait_gather(cur)

        def issue(r, _):
            _row_copy_out(slotp_ref, r, ybuf.at[1 - cur], y_hbm, ssem).start()
            return 0

        lax.fori_loop(0, tb, issue, 0)
        wait_scatter(1 - cur)


def _moe_experts(h_rows, row_tok, row_slot, block_expert, block_valid, w1, w3, w2, n_out_rows, n_spare):
    d = w2.shape[-1]
    dff = w2.shape[1]
    nb = block_expert.shape[0]
    tb = MOE_ROWS
    smem = functools.partial(pl.BlockSpec, (1, 1, tb), memory_space=pltpu.SMEM)
    grid_spec = pltpu.PrefetchScalarGridSpec(
        num_scalar_prefetch=2,
        grid=(nb,),
        in_specs=[smem(lambda i, bv, be: (i, 0, 0)),
                  smem(lambda i, bv, be: (jnp.minimum(i + 1, nb - 1), 0, 0)),
                  smem(lambda i, bv, be: (i, 0, 0)),
                  pl.BlockSpec(memory_space=pl.ANY),
                  pl.BlockSpec((1, d, dff), lambda i, bv, be: (be[i], 0, 0)),
                  pl.BlockSpec((1, d, dff), lambda i, bv, be: (be[i], 0, 0)),
                  pl.BlockSpec((1, dff, d), lambda i, bv, be: (be[i], 0, 0))],
        out_specs=pl.BlockSpec(memory_space=pl.ANY),
        scratch_shapes=[pltpu.VMEM((2, tb, d // 2), jnp.uint32), pltpu.VMEM((2, tb, d), F32),
                        pltpu.SemaphoreType.DMA(()), pltpu.SemaphoreType.DMA(())],
    )
    return pl.pallas_call(
        functools.partial(_moe_body, n_asg=n_out_rows - n_spare),
        grid_spec=grid_spec,
        out_shape=jax.ShapeDtypeStruct((n_out_rows, d), F32),
        compiler_params=_cparams(("arbitrary",), VMEM_LIMIT),
        name="moe_experts",
    )(block_valid, block_expert, row_tok.reshape(nb, 1, tb), row_tok.reshape(nb, 1, tb), row_slot.reshape(nb + 1, 1, tb),
      h_rows, w1, w3, w2)


def _combine_body(y0_ref, y1_ref, x1_ref, gate_ref, gm_ref, fg_ref, o_ref, *, final):
    moe = gate_ref[:, 0:1] * y0_ref[...] + gate_ref[:, 1:2] * y1_ref[...]
    x2 = x1_ref[...] + gm_ref[0] * moe
    if final:
        ms = jnp.mean(x2 * x2, axis=-1, keepdims=True)
        x2 = x2 * lax.rsqrt(ms + EPS) * fg_ref[...]
    o_ref[...] = x2


def _moe_combine(y_rows, x1, gates, gate_mlp, final_g, final):
    b, l, d = x1.shape
    n_tok = b * l
    tm = min(ROW_TILE // 2, l)
    per_b = l // tm
    out = pl.pallas_call(
        functools.partial(_combine_body, final=final),
        grid=(n_tok // tm,),
        in_specs=[pl.BlockSpec((tm, d), lambda i: (i, 0)),
                  pl.BlockSpec((tm, d), lambda i: (n_tok // tm + i, 0)),
                  pl.BlockSpec((tm, d), lambda i: (i, 0)),
                  pl.BlockSpec((tm, 2), lambda i: (i, 0)),
                  pl.BlockSpec((1, 1, d), lambda i: (i // per_b, 0, 0)),
                  pl.BlockSpec((1, d), lambda i: (0, 0))],
        out_specs=pl.BlockSpec((tm, d), lambda i: (i, 0)),
        out_shape=jax.ShapeDtypeStruct((n_tok, d), F32),
        compiler_params=_cparams(("arbitrary",), VMEM_LIMIT),
        name="moe_combine_final_norm",
    )(y_rows, y_rows, x1.reshape(n_tok, d), gates, gate_mlp, final_g.reshape(1, d))
    return out.reshape(b, l, d)


def _route(logits):
    n_tok = logits.shape[0]
    tb = MOE_ROWS
    n_asg = n_tok * TOP_K
    top_logit, top_e = lax.top_k(logits[:, :N_EXPERTS], TOP_K)
    gates = jax.nn.softmax(top_logit, axis=-1)
    e_flat = top_e.reshape(-1).astype(jnp.int32)
    asg = jnp.arange(n_asg, dtype=jnp.int32)
    by_expert = lax.sort(e_flat * n_asg + asg)
    counts = jnp.sum((e_flat[:, None] == jnp.arange(N_EXPERTS, dtype=jnp.int32)[None, :]).astype(jnp.int32), axis=0)
    starts = jnp.cumsum(counts) - counts
    padded = (counts + tb - 1) // tb * tb
    pad_ends = jnp.cumsum(padded)
    pad_starts = pad_ends - padded
    nb = (n_asg + tb - 1) // tb + N_EXPERTS + 1
    n_rows = nb * tb
    block_start = jnp.arange(nb, dtype=jnp.int32) * tb
    block_expert = jnp.minimum(jnp.sum((block_start[:, None] >= pad_ends[None, :]).astype(jnp.int32), axis=1), N_EXPERTS - 1)
    block_valid = (block_start < pad_ends[-1]).astype(jnp.int32)
    row = jnp.arange(n_rows, dtype=jnp.int32)
    row_e = jnp.repeat(block_expert, tb)
    off = row - pad_starts[row_e]
    is_pad = (off >= counts[row_e]) | (jnp.repeat(block_valid, tb) == 0)
    src = by_expert[jnp.clip(starts[row_e] + off, 0, n_asg - 1)] - row_e * n_asg
    row_asg = jnp.where(is_pad, -1, src)
    spare = tb + n_asg + jnp.cumsum(is_pad.astype(jnp.int32)) - 1
    row_slot = jnp.where(is_pad, spare, (row_asg % TOP_K) * n_tok + row_asg // TOP_K)
    row_tok = jnp.where(is_pad, 0, row_asg // TOP_K)
    row_slot = jnp.concatenate([n_asg + jnp.arange(tb, dtype=jnp.int32), row_slot])
    last_e = block_expert[jnp.maximum(pad_ends[-1] // tb - 1, 0)]
    block_expert = jnp.where(block_valid == 1, block_expert, last_e)
    n_spare = (tb + n_rows - n_asg + tb - 1) // tb * tb
    return row_tok, row_slot, gates, block_expert, block_valid, n_asg + n_spare, n_spare


def _final_norm_body(x_ref, g_ref, o_ref):
    x = x_ref[...]
    ms = jnp.mean(x * x, axis=-1, keepdims=True)
    o_ref[...] = x * lax.rsqrt(ms + EPS) * g_ref[...]


def _final_norm(x, g):
    b, l, d = x.shape
    n = b * l
    tm = min(ROW_TILE, n)
    out = pl.pallas_call(
        _final_norm_body,
        grid=(n // tm,),
        in_specs=[pl.BlockSpec((tm, d), lambda i: (i, 0)), pl.BlockSpec((1, d), lambda i: (0, 0))],
        out_specs=pl.BlockSpec((tm, d), lambda i: (i, 0)),
        out_shape=jax.ShapeDtypeStruct((n, d), F32),
        compiler_params=_cparams(("arbitrary",)),
        name="final_norm",
    )(x.reshape(n, d), g.reshape(1, d))
    return out.reshape(b, l, d)


def _rope_tables(n_tok):
    t = jnp.arange(n_tok, dtype=jnp.int32)
    row = (t // GRID_W).astype(F32)
    col = (t % GRID_W).astype(F32)
    inv_freq = ROPE_BASE ** (-jnp.arange(N_FREQ, dtype=F32) / N_FREQ)
    ang = jnp.concatenate([row[:, None] * inv_freq, col[:, None] * inv_freq], axis=-1)
    cos, sin = jnp.cos(ang), jnp.sin(ang)
    cosf = jnp.concatenate([cos, cos, cos, cos], axis=-1)
    sins = jnp.concatenate([-sin, sin, -sin, sin], axis=-1)
    return cosf, sins


def kernel(x, c, ctx, c_ctx, ada_w, ada_b, norm1_g, norm2_g, w_in, w_out, conv_w, ret_decay, gdn_a_log, gdn_dt_bias,
           gdn_norm_g, na_rpb, ffn_w1, ffn_w3, ffn_w2, moe_router, moe_w1, moe_w3, moe_w2, final_g):
    b, l, d = x.shape
    lc = ctx.shape[1]
    depth = ada_w.shape[0]
    cosf, sins = _rope_tables(l)
    ones_c = jnp.ones((lc, LANES), F32)
    zeros_c = jnp.zeros((lc, LANES), F32)

    rows = ((b + 1 + 7) // 8) * 8
    c_all = jnp.zeros((rows, d), F32).at[:b].set(c).at[b].set(c_ctx)
    mod = _ada_vectors(c_all, ada_w, ada_b).reshape(depth, rows, 6, d)

    w1p_all = jax.vmap(_pack_w_in)(w_in)
    wo_all = w_out.astype(BF16)
    log_gamma_all = jnp.log1p(-jnp.exp2(-ret_decay.astype(F32)))
    gdn_tables = jax.vmap(_gdn_tables)(conv_w, gdn_a_log, gdn_dt_bias, gdn_norm_g)
    na_bias = jax.vmap(lambda r: _na_bias_tiles(r, l // GRID_W))(na_rpb)

    y = ctx
    for layer in range(depth):
        need_ctx = layer < depth - 1
        mod_l = mod[layer, :b]
        mod_c = jnp.broadcast_to(mod[layer, b][None], (b, 6, d))
        w1p = w1p_all[layer]
        g1 = norm1_g[layer].reshape(1, d)
        ret_l, gdn_l, ab_l, abt_l, na_l = _in_projection(x, mod_l, g1, cosf, sins, w1p, rope=True)
        ret_c, gdn_c, ab_c, abt_c, na_c = _in_projection(y, mod_c, g1, ones_c, zeros_c, w1p, rope=False)

        r_l, r_c = _retention(ret_l, ret_c, log_gamma_all[layer])
        g_l, g_c = _gdn(gdn_l, ab_l, abt_l, gdn_c, ab_c, abt_c, [t[layer] for t in gdn_tables])
        n_l, n_c = _neighbourhood_attention(na_l, na_c, na_bias, layer)

        wo = wo_all[layer]
        g2 = norm2_g[layer].reshape(1, d)
        j = layer // 2
        if layer % 2 == 0:
            w1, w3, w2 = ffn_w1[j].astype(BF16), ffn_w3[j].astype(BF16), ffn_w2[j].astype(BF16)
            x = _dense_block(r_l, g_l, n_l, x, mod_l, g2, wo, w1, w3, w2)
            if need_ctx:
                y = _dense_block(r_c, g_c, n_c, y, mod_c, g2, wo, w1, w3, w2)
            if layer == depth - 1:
                x = _final_norm(x, final_g)
        else:
            wr = jnp.pad(moe_router[j].astype(F32), ((0, 0), (0, LANES - N_EXPERTS)))
            w1, w3, w2 = moe_w1[j].astype(BF16), moe_w3[j].astype(BF16), moe_w2[j].astype(BF16)

            def moe_ffn(r, g, n, xin, m, last):
                bb, ll, _ = xin.shape
                x1, h, logits = _router_block(r, g, n, xin, m, g2, wo, wr)
                row_tok, row_slot, gates, bexp, bval, n_out, n_spare = _route(logits.reshape(bb * ll, LANES))
                y_rows = _moe_experts(h.reshape(bb * ll, d // 2), row_tok, row_slot, bexp, bval, w1, w3, w2, n_out, n_spare)
                return _moe_combine(y_rows, x1, gates, m[:, 5:6, :], final_g, last)

            x = moe_ffn(r_l, g_l, n_l, x, mod_l, layer == depth - 1)
            if need_ctx:
                y = moe_ffn(r_c, g_c, n_c, y, mod_c, False)
    return x
```

```python
import functools

import numpy as np
import jax
import jax.numpy as jnp
from jax import lax
from jax.experimental import pallas as pl
from jax.experimental.pallas import tpu as pltpu

F32 = jnp.float32
BF16 = jnp.bfloat16
HIGHEST = lax.Precision.HIGHEST

LANES = 128
BF16_ROWS = 16
ROW_TILE = 512
PROJ_COLS = 512
ROW_PARTS = 2
HEAD_DIM = 64
PAIR = 2 * HEAD_DIM
GRID_W = 64
H_RET, H_GDN, H_NA = 4, 4, 8
D_RET, D_GDN, D_NA = H_RET * HEAD_DIM, H_GDN * HEAD_DIM, H_NA * HEAD_DIM
RET_CHUNK = 128
RET_UNROLL = 4
GDN_CHUNK = 64
GDN_GROUP = 4
GDN_PREP_ROWS = 512
SHORT_CONV = 5
NA_KH, NA_KW = 8, 16
NA_QROWS = 4
NA_KROWS = NA_QROWS + NA_KH
NA_PAIRS = 1
NA_UNROLL = 4
NA_QSCALE = float(np.log2(np.e)) * HEAD_DIM ** -0.5
N_FREQ = HEAD_DIM // 4
ROPE_BASE = 10000.0
N_EXPERTS = 8
TOP_K = 2
MOE_ROWS = 512
FF_CHUNK = 256
EPS = 1e-6
NEG_INF = -1e30
VMEM_LIMIT = 56 * 1024 * 1024

C_RET = 0
C_GDN = C_RET + 4 * D_RET
C_AB = C_GDN + 4 * D_GDN
C_NA = C_AB + LANES
C_END = C_NA + 3 * D_NA


def _cparams(sem, vmem=None):
    return pltpu.CompilerParams(dimension_semantics=sem, vmem_limit_bytes=vmem)


def _silu(x):
    return x * jax.nn.sigmoid(x)


def _dot(a, b):
    return jnp.dot(a, b, preferred_element_type=F32)


def _dot_nt(a, b):
    return lax.dot_general(a, b, (((1,), (1,)), ((), ())), preferred_element_type=F32)


def _dot_tn(a, b):
    return lax.dot_general(a, b, (((0,), (0,)), ((), ())), preferred_element_type=F32)


def _split_bf16(x):
    hi = x.astype(BF16)
    return hi, (x - hi.astype(F32)).astype(BF16)


def _dot_3pass(a, b):
    ah, al = _split_bf16(a)
    bh, bl = _split_bf16(b)
    return _dot(ah, bh) + (_dot(ah, bl) + _dot(al, bh))


def _dot_hi(a, b):
    return jnp.dot(a, b, preferred_element_type=F32, precision=HIGHEST)


def _ada_body(c_ref, w_ref, b_ref, o_ref):
    s = _silu(c_ref[...])
    o_ref[0] = _dot_hi(s, w_ref[0]) + b_ref[0]


def _ada_vectors(c_all, ada_w, ada_b):
    depth, d, d6 = ada_w.shape
    rows = c_all.shape[0]
    tn = d
    return pl.pallas_call(
        _ada_body,
        grid=(depth, d6 // tn),
        in_specs=[pl.BlockSpec((rows, d), lambda l, j: (0, 0)),
                  pl.BlockSpec((1, d, tn), lambda l, j: (l, 0, j)),
                  pl.BlockSpec((1, 1, tn), lambda l, j: (l, 0, j))],
        out_specs=pl.BlockSpec((1, rows, tn), lambda l, j: (l, 0, j)),
        out_shape=jax.ShapeDtypeStruct((depth, rows, d6), F32),
        compiler_params=_cparams(("arbitrary", "arbitrary")),
        name="ada_vectors",
    )(c_all, ada_w, ada_b.reshape(depth, 1, d6))


def _norm_mod(x, g, shift, scale):
    ms = jnp.mean(x * x, axis=-1, keepdims=True)
    return (x * lax.rsqrt(ms + EPS) * g) * (1.0 + scale) + shift


def _rope_slab(t, cosf, sins):
    lane = lax.broadcasted_iota(jnp.int32, t.shape, 1)
    first = (lane % HEAD_DIM) < (HEAD_DIM // 2)
    partner = jnp.where(first, pltpu.roll(t, LANES - HEAD_DIM // 2, 1), pltpu.roll(t, HEAD_DIM // 2, 1))
    return t * cosf + partner * sins


def _inproj_body(x_ref, mod_ref, g_ref, cos_ref, sin_ref, w_ref, ret_ref, gdn_ref, ab_ref, abt_ref, na_ref, *, rope):
    h = _norm_mod(x_ref[0], g_ref[...], mod_ref[0, 0:1, :], mod_ref[0, 1:2, :]).astype(BF16)
    qk = _dot(h, w_ref[:, C_RET:C_RET + 2 * D_RET])
    slabs = []
    for s in range(2 * D_RET // LANES):
        t = qk[:, s * LANES:(s + 1) * LANES]
        if rope:
            t = _rope_slab(t, cos_ref[...], sin_ref[...])
        if s >= D_RET // LANES:
            t = t * HEAD_DIM ** -0.5
        slabs.append(t)
    ret_ref[0, :, 0:2 * D_RET] = jnp.concatenate(slabs, axis=1).astype(BF16)
    ret_ref[0, :, 2 * D_RET:] = _dot(h, w_ref[:, C_RET + 2 * D_RET:C_GDN]).astype(BF16)
    for j in range(4 * D_GDN // PROJ_COLS):
        cols = slice(PROJ_COLS * j, PROJ_COLS * (j + 1))
        gdn_ref[0, :, cols] = _dot(h, w_ref[:, C_GDN + cols.start:C_GDN + cols.stop]).astype(BF16)
    ab = _dot(h, w_ref[:, C_AB:C_NA])
    ab_ref[0] = ab
    for grp in range(ab.shape[0] // LANES):
        t = ab[grp * LANES:(grp + 1) * LANES, :].T
        for p in range(D_GDN // PAIR):
            abt_ref[0, p, grp] = t[8 * p:8 * p + 8, :]
    for j in range(3 * D_NA // PROJ_COLS):
        cols = slice(PROJ_COLS * j, PROJ_COLS * (j + 1))
        na_ref[0, :, cols] = _dot(h, w_ref[:, C_NA + cols.start:C_NA + cols.stop]).astype(BF16)


def _in_projection(x, mod, g, cosf, sins, w, rope):
    b, l, d = x.shape
    tm = min(ROW_TILE, l)
    body = functools.partial(_inproj_body, rope=rope)
    return pl.pallas_call(
        body,
        grid=(b, l // tm),
        in_specs=[pl.BlockSpec((1, tm, d), lambda i, j: (i, j, 0)),
                  pl.BlockSpec((1, 6, d), lambda i, j: (i, 0, 0)),
                  pl.BlockSpec((1, d), lambda i, j: (0, 0)),
                  pl.BlockSpec((tm, LANES), lambda i, j: (j, 0)),
                  pl.BlockSpec((tm, LANES), lambda i, j: (j, 0)),
                  pl.BlockSpec((d, C_END), lambda i, j: (0, 0))],
        out_specs=[pl.BlockSpec((1, tm, 4 * D_RET), lambda i, j: (i, j, 0)),
                   pl.BlockSpec((1, tm, 4 * D_GDN), lambda i, j: (i, j, 0)),
                   pl.BlockSpec((1, tm, LANES), lambda i, j: (i, j, 0)),
                   pl.BlockSpec((1, D_GDN // PAIR, tm // LANES, 8, LANES), lambda i, j: (i, 0, j, 0, 0)),
                   pl.BlockSpec((1, tm, 3 * D_NA), lambda i, j: (i, j, 0))],
        out_shape=[jax.ShapeDtypeStruct((b, l, 4 * D_RET), BF16),
                   jax.ShapeDtypeStruct((b, l, 4 * D_GDN), BF16),
                   jax.ShapeDtypeStruct((b, l, LANES), F32),
                   jax.ShapeDtypeStruct((b, D_GDN // PAIR, l // LANES, 8, LANES), F32),
                   jax.ShapeDtypeStruct((b, l, 3 * D_NA), BF16)],
        compiler_params=_cparams(("arbitrary", "arbitrary"), VMEM_LIMIT),
        name="in_projection",
    )(x, mod, g, cosf, sins, w)


def _pack_w_in(w_in):
    d = w_in.shape[0]
    c1 = 4 * D_RET
    c2 = c1 + 4 * D_GDN
    q0 = c2 + 4 * H_GDN
    col = jnp.arange(w_in.shape[1])
    w_in = jnp.where((col >= q0) & (col < q0 + D_NA), w_in * NA_QSCALE, w_in).astype(BF16)
    ab = w_in[:, c2:c2 + 4 * H_GDN]
    ab = ab.reshape(d, 2, 2, H_GDN // 2, 2)
    ab = jnp.transpose(ab, (0, 3, 1, 2, 4)).reshape(d, 4 * H_GDN)
    ab = jnp.pad(ab, ((0, 0), (0, LANES - 4 * H_GDN)))
    return jnp.concatenate([w_in[:, :c2], ab, w_in[:, c2 + 4 * H_GDN:]], axis=1)


def _head_masks(shape):
    lane = lax.broadcasted_iota(jnp.int32, shape, len(shape) - 1)
    return lane < HEAD_DIM, lane >= HEAD_DIM


def _per_head(lo, hi, shape):
    m0, _ = _head_masks(shape)
    return jnp.where(m0, lo, hi)


def _head_sumsq(o):
    m0, m1 = _head_masks(o.shape)
    sq = o * o
    s0 = jnp.sum(jnp.where(m0, sq, 0.0), axis=-1, keepdims=True)
    s1 = jnp.sum(jnp.where(m1, sq, 0.0), axis=-1, keepdims=True)
    return jnp.where(m0, s0, s1)


def _ret_body(lg_ref, ql, kl, vl, gl, qc, kc, vc, gc, r_ref, rc_ref, sb_ref):
    c = RET_CHUNK
    npair = D_RET // PAIR
    lane_shape = (c, PAIR)
    pos = lax.broadcasted_iota(jnp.int32, lane_shape, 0).astype(F32)
    ii = lax.broadcasted_iota(jnp.int32, (c, c), 0)
    jj = lax.broadcasted_iota(jnp.int32, (c, c), 1)
    diff = (ii - jj).astype(F32)
    m0, m1 = _head_masks(lane_shape)
    masks = (m0, m1)
    bi = lax.broadcasted_iota(jnp.int32, (PAIR, PAIR), 0) // HEAD_DIM
    bj = lax.broadcasted_iota(jnp.int32, (PAIR, PAIR), 1) // HEAD_DIM
    bd = bi == bj
    cst = []
    for p in range(npair):
        lgf = _per_head(lg_ref[0, 2 * p], lg_ref[0, 2 * p + 1], lane_shape)
        lgb = _per_head(lg_ref[1, 2 * p], lg_ref[1, 2 * p + 1], lane_shape)
        dmats = [jnp.where(diff > 0, jnp.exp(lg_ref[0, 2 * p + hh] * diff),
                           jnp.where(diff < 0, jnp.exp(-lg_ref[1, 2 * p + hh] * diff), 2.0)) for hh in range(2)]
        cst.append(dict(qdf=jnp.exp(lgf * (pos + 1.0)), kdf=jnp.exp(lgf * (c - 1.0 - pos)),
                        qdb=jnp.exp(lgb * (c - pos)), kdb=jnp.exp(lgb * pos),
                        cdf=jnp.exp(lgf[0:1] * c), cdb=jnp.exp(lgb[0:1] * c), dmats=dmats))

    def sweep(q_ref, k_ref, v_ref, g_ref, o_ref, n, sf0, sb0):
        unroll = min(RET_UNROLL, n)

        def bchunk(ci, sbs):
            sl = pl.ds(pl.multiple_of(ci * c, c), c)
            out = []
            for p in range(npair):
                lanes = slice(p * PAIR, (p + 1) * PAIR)
                sb_ref[p, ci] = sbs[p]
                kd = (k_ref[0, sl, lanes].astype(F32) * cst[p]["kdb"]).astype(BF16)
                out.append(sbs[p] * cst[p]["cdb"] + jnp.where(bd, _dot_tn(kd, v_ref[0, sl, lanes]), 0.0))
            return tuple(out)

        def bstep(t, sbs):
            for u in range(unroll):
                sbs = bchunk(n - 1 - (t * unroll + u), sbs)
            return sbs

        sb_fin = lax.fori_loop(0, n // unroll, bstep, sb0)

        def fchunk(ci, sfs):
            sl = pl.ds(pl.multiple_of(ci * c, c), c)
            out = []
            for p in range(npair):
                lanes = slice(p * PAIR, (p + 1) * PAIR)
                k_ = cst[p]
                q = q_ref[0, sl, lanes]
                k = k_ref[0, sl, lanes]
                v = v_ref[0, sl, lanes]
                qf = q.astype(F32)
                qd = jnp.concatenate([(qf * k_["qdf"]).astype(BF16), (qf * k_["qdb"]).astype(BF16)], axis=1)
                o = _dot(qd, jnp.concatenate([sfs[p].astype(BF16), sb_ref[p, ci].astype(BF16)], axis=0))
                s2 = _dot_nt(_stack_heads(q), k)
                pm = jnp.concatenate([s2[:c] * k_["dmats"][0], s2[c:] * k_["dmats"][1]], axis=0).astype(BF16)
                pv = _dot(pm, v)
                o = o + jnp.where(masks[0], pv[:c], pv[c:])
                on = o * lax.rsqrt(_head_sumsq(o) * (1.0 / HEAD_DIM) + EPS)
                o_ref[0, sl, lanes] = (on * _silu(g_ref[0, sl, lanes].astype(F32))).astype(o_ref.dtype)
                kd = (k.astype(F32) * k_["kdf"]).astype(BF16)
                out.append(sfs[p] * k_["cdf"] + jnp.where(bd, _dot_tn(kd, v), 0.0))
            return tuple(out)

        def fstep(t, sfs):
            for u in range(unroll):
                sfs = fchunk(t * unroll + u, sfs)
            return sfs

        sf_fin = lax.fori_loop(0, n // unroll, fstep, sf0)
        return sf_fin, sb_fin

    z = tuple(jnp.zeros((PAIR, PAIR), F32) for _ in range(npair))
    sfc, sbc = sweep(qc, kc, vc, gc, rc_ref, qc.shape[1] // c, z, z)
    sweep(ql, kl, vl, gl, r_ref, ql.shape[1] // c, sfc, sbc)


def _retention(ret_l, ret_c, log_gamma):
    b, l, _ = ret_l.shape
    lc = ret_c.shape[1]
    npair = D_RET // PAIR

    def col(k):
        return lambda i: (i, 0, k)

    lat = [pl.BlockSpec((1, l, D_RET), col(k)) for k in range(4)]
    ctx = [pl.BlockSpec((1, lc, D_RET), col(k)) for k in range(4)]
    return pl.pallas_call(
        _ret_body,
        grid=(b,),
        in_specs=[pl.BlockSpec(memory_space=pltpu.SMEM)] + lat + ctx,
        out_specs=[pl.BlockSpec((1, l, D_RET), lambda i: (i, 0, 0)),
                   pl.BlockSpec((1, lc, D_RET), lambda i: (i, 0, 0))],
        out_shape=[jax.ShapeDtypeStruct((b, l, D_RET), BF16), jax.ShapeDtypeStruct((b, lc, D_RET), BF16)],
        scratch_shapes=[pltpu.VMEM((npair, max(l, lc) // RET_CHUNK, PAIR, PAIR), F32)],
        compiler_params=_cparams(("arbitrary",), VMEM_LIMIT),
        name="retention",
    )(log_gamma, ret_l, ret_l, ret_l, ret_l, ret_c, ret_c, ret_c, ret_c)


def _stack_heads(t):
    m0, m1 = _head_masks(t.shape)
    z = jnp.zeros_like(t)
    return jnp.concatenate([jnp.where(m0, t, z), jnp.where(m1, t, z)], axis=0)


def _scan_sum(x, axis, n, reverse):
    size = x.shape[axis]
    pos = lax.broadcasted_iota(jnp.int32, x.shape, axis)
    s = 1
    while s < n:
        if reverse:
            x = x + jnp.where(pos + s < n, pltpu.roll(x, size - s, axis), 0.0)
        else:
            x = x + jnp.where(pos >= s, pltpu.roll(x, s, axis), 0.0)
        s *= 2
    return x


def _gdn_prep(src_ref, slabs, cw_ref, r0):
    n = src_ref.shape[1]
    blk = min(GDN_PREP_ROWS, n)
    halo = BF16_ROWS

    def block(i, _):
        b0 = pl.multiple_of(i * blk, blk)
        lo_rows = pl.ds(pl.multiple_of(jnp.maximum(b0 - halo, 0), halo), halo)
        hi_rows = pl.ds(pl.multiple_of(jnp.minimum(b0 + blk, n - halo), halo), halo)
        for col, dst_ref, l2, scale in slabs:
            cols = slice(col, col + LANES)
            cw = cw_ref[:, cols]
            x = src_ref[0, pl.ds(b0, blk), cols].astype(F32)
            lo = jnp.where(b0 > 0, src_ref[0, lo_rows, cols].astype(F32), 0.0)
            hi = jnp.where(b0 + blk < n, src_ref[0, hi_rows, cols].astype(F32), 0.0)
            ext = jnp.concatenate([lo, x, hi], axis=0)
            acc = x * cw[SHORT_CONV // 2:SHORT_CONV // 2 + 1, :]
            for j in range(SHORT_CONV):
                s = j - SHORT_CONV // 2
                if s != 0:
                    sh = pltpu.roll(ext, (-s) % (blk + 2 * halo), 0)
                    acc = acc + sh[halo:halo + blk] * cw[j:j + 1, :]
            y = _silu(acc)
            if l2:
                y = y * lax.rsqrt(_head_sumsq(y) + EPS)
            if scale != 1.0:
                y = y * scale
            dst_ref[pl.ds(pl.multiple_of(r0 + b0, halo), blk), col % D_GDN:col % D_GDN + LANES] = y.astype(dst_ref.dtype)
        return 0

    lax.fori_loop(0, n // blk, block, 0)


def _gdn_body(x_ref, xc_ref, ab_ref, abt_ref, cw_ref, cst_ref, cstt_ref, ng_ref,
              o_ref, oc_ref, qn, kn, vn, oacc, st_s, x_s, t_s, rhs_s, u_s, wq_s, ak_s, egl_s):
    c = GDN_CHUNK
    c2 = 2 * c
    grp = GDN_GROUP
    npair = D_GDN // PAIR
    lc = xc_ref.shape[1]
    l = x_ref.shape[1]
    nc = lc // c
    nt = (lc + l) // c
    ii = lax.broadcasted_iota(jnp.int32, (c2, c2), 0)
    jj = lax.broadcasted_iota(jnp.int32, (c2, c2), 1)
    same = (ii // c) == (jj // c)
    eye = (ii == jj).astype(F32)
    dirs = ((same & (ii >= jj), same & (ii > jj), c - 1),
            (same & (ii <= jj), same & (ii < jj), 0))

    def chains(g, slot):
        out = []
        for gi in range(grp):
            t = g * grp + gi
            cf = t
            cb = jnp.where(t < nc, nc - 1 - t, nt + nc - 1 - t)
            for p in range(npair):
                for d in range(2):
                    out.append((((slot * grp + gi) * 2 + d) * npair + p, p, d, cf if d == 0 else cb))
        return out

    def stage_inputs(idx, p, d, ci):
        incl, strict, last = dirs[d]
        lanes = slice(p * PAIR, (p + 1) * PAIR)
        nea, dtb = cst_ref[0:1, :], cst_ref[1:2, :]
        neat, dtbt = cstt_ref[p, 0], cstt_ref[p, 1]
        sl = pl.ds(pl.multiple_of(ci * c, c), c)
        abv = ab_ref[0, sl, :]
        gall = nea * jax.nn.softplus(abv + dtb)
        ball = jax.nn.sigmoid(abv)
        shape = (c, PAIR)
        c0 = 8 * p + 2 * d
        gl = _per_head(gall[:, c0:c0 + 1], gall[:, c0 + 1:c0 + 2], shape)
        bl = _per_head(ball[:, c0 + 4:c0 + 5], ball[:, c0 + 5:c0 + 6], shape)
        gcum = _scan_sum(gl, 0, c, d == 1)
        abt = abt_ref[0, p, ci // 2]
        abt = jnp.where(ci % 2 == 1, pltpu.roll(abt, c, 1), abt)
        gt = neat * jax.nn.softplus(abt + dtbt)
        gtc = _scan_sum(gt, 1, c, d == 1)
        grow = jnp.concatenate([gtc[2 * d:2 * d + 1, 0:c], gtc[2 * d + 1:2 * d + 2, 0:c]], axis=1)
        gcol = jnp.concatenate([gcum[:, 0:1], gcum[:, HEAD_DIM:HEAD_DIM + 1]], axis=0)
        dec = jnp.where(incl, jnp.exp(jnp.where(incl, gcol - grow, 0.0)), 0.0)
        dec = jnp.where(ii == jj, 1.0, dec)
        q = qn[sl, lanes]
        k = kn[sl, lanes]
        v = vn[sl, lanes]
        kf = k.astype(F32)
        eg = jnp.exp(gcum)
        kb = kf * bl
        k_st = _stack_heads(k)
        kq = _dot_nt(jnp.concatenate([_stack_heads(kb.astype(BF16)), _stack_heads(q)], axis=0), k_st)
        a = jnp.where(strict, kq[:c2] * dec, 0.0)
        attn = (kq[c2:] * dec).astype(BF16)
        x_s[idx] = (-a).astype(BF16)
        t_s[idx] = eye - a
        rhs_s[idx] = jnp.concatenate([_stack_heads((v.astype(F32) * bl).astype(BF16)),
                                      _stack_heads((kb * eg).astype(BF16))], axis=1)
        glast = gcum[last:last + 1, :]
        wq_s[idx, c2:, :] = _stack_heads((q.astype(F32) * eg).astype(BF16))
        ak_s[idx] = jnp.concatenate([attn, _stack_heads(kf * jnp.exp(glast - gcum)).T.astype(BF16)], axis=0)
        egl_s[idx] = jnp.broadcast_to(jnp.exp(glast), (8, PAIR))

    def stage_group(g, slot):
        todo = chains(g, slot)
        for idx, p, d, ci in todo:
            stage_inputs(idx, p, d, ci)
        for idx, _, _, _ in todo:
            xb = x_s[idx]
            x_s[idx] = _dot(xb, xb).astype(BF16)
        for _ in range(4):
            for idx, _, _, _ in todo:
                xb = x_s[idx]
                t = t_s[idx]
                prod = _dot(jnp.concatenate([t.astype(BF16), xb], axis=0), xb)
                t_s[idx] = t + prod[:c2]
                x_s[idx] = prod[c2:].astype(BF16)
        for idx, _, _, _ in todo:
            t = t_s[idx]
            t = t + _dot(t.astype(BF16), x_s[idx])
            sol = _dot(t.astype(BF16), rhs_s[idx])
            u_s[idx] = sol[:, :PAIR]
            wq_s[idx, :c2, :] = sol[:, PAIR:].astype(BF16)

    def recur_group(g, slot):
        sts = {(p, d): st_s[d * npair + p] for p in range(npair) for d in range(2)}
        for idx, p, d, ci in chains(g, slot):
            sl = pl.ds(pl.multiple_of(ci * c, c), c)
            st = sts[(p, d)]
            ws = _dot(wq_s[idx], st.astype(BF16))
            v_new = (u_s[idx] - ws[:c2]).astype(BF16)
            av = _dot(ak_s[idx], v_new)
            o_st = ws[c2:] + av[:c2]
            oacc[sl, p * PAIR:(p + 1) * PAIR] += o_st[:c] + o_st[c:]
            sts[(p, d)] = st * egl_s[idx][0:1, :] + av[c2:]
        for (p, d), st in sts.items():
            st_s[d * npair + p] = st

    def prep(src_ref, r0):
        slabs = [(k * D_GDN + s * LANES, dst, l2, scale)
                 for s in range(D_GDN // LANES)
                 for k, (dst, l2, scale) in enumerate(((qn, True, HEAD_DIM ** -0.5), (kn, True, 1.0), (vn, False, 1.0)))]
        _gdn_prep(src_ref, slabs, cw_ref, r0)

    def finish(src_ref, out_ref, r0, n):
        for p in range(npair):
            lanes = slice(p * PAIR, (p + 1) * PAIR)
            o = oacc[r0:r0 + n, lanes]
            on = o * lax.rsqrt(_head_sumsq(o) * (1.0 / HEAD_DIM) + EPS)
            gate = src_ref[0, :, 3 * D_GDN + p * PAIR:3 * D_GDN + (p + 1) * PAIR].astype(F32)
            out_ref[0, :, lanes] = (on * ng_ref[...] * _silu(gate)).astype(out_ref.dtype)

    st_s[...] = jnp.zeros_like(st_s)
    oacc[...] = jnp.zeros_like(oacc)
    prep(xc_ref, 0)
    prep(x_ref, lc)

    ng = nt // grp
    stage_group(0, 0)

    def two_groups(kk, _):
        g = 2 * kk
        recur_group(g, 0)
        stage_group(g + 1, 1)
        recur_group(g + 1, 1)
        stage_group(g + 2, 0)
        return 0

    lax.fori_loop(0, (ng - 1) // 2, two_groups, 0)
    if ng % 2 == 0:
        recur_group(ng - 2, 0)
        stage_group(ng - 1, 1)
        recur_group(ng - 1, 1)
    else:
        recur_group(ng - 1, 0)
    finish(xc_ref, oc_ref, 0, lc)
    finish(x_ref, o_ref, lc, l)


def _gdn_tables(conv_w, a_log, dt_bias, norm_g):
    npair = D_GDN // PAIR
    nea = -jnp.exp(a_log.astype(F32)).reshape(2, npair, 2)
    dtb = dt_bias.astype(F32).reshape(2, npair, 2)
    rows = jnp.stack([jnp.transpose(nea, (1, 0, 2)).reshape(npair, 4), jnp.transpose(dtb, (1, 0, 2)).reshape(npair, 4)], axis=1)
    cst = jnp.pad(jnp.transpose(jnp.pad(rows, ((0, 0), (0, 0), (0, 4))), (1, 0, 2)).reshape(2, 8 * npair),
                  ((0, 6), (0, LANES - 8 * npair)))
    cstt = jnp.broadcast_to(jnp.pad(rows, ((0, 0), (0, 0), (0, 4)))[..., None], (npair, 2, 8, LANES))
    cw = jnp.pad(conv_w.astype(F32), ((0, 8 - SHORT_CONV), (0, 0)))
    ng = jnp.tile(norm_g.astype(F32), 2).reshape(1, PAIR)
    return cw, cst, cstt, ng


def _gdn(gdn_l, ab_l, abt_l, gdn_c, ab_c, abt_c, tables):
    b, l, _ = gdn_l.shape
    lc = gdn_c.shape[1]
    npair = D_GDN // PAIR
    lt = l + lc
    assert (lt // GDN_CHUNK) % GDN_GROUP == 0 and lc % LANES == 0
    cw, cst, cstt, ng = tables

    nstage = 2 * GDN_GROUP * 2 * npair

    def whole(shape):
        nd = len(shape)
        return pl.BlockSpec(shape, lambda i: (0,) * nd)

    def per_batch(shape):
        nd = len(shape)
        return pl.BlockSpec((1,) + shape, lambda i: (i,) + (0,) * nd)

    sq = (nstage, PAIR, PAIR)
    return pl.pallas_call(
        _gdn_body,
        grid=(b,),
        in_specs=[per_batch((l, 4 * D_GDN)), per_batch((lc, 4 * D_GDN)), per_batch((lt, LANES)),
                  per_batch((npair, lt // LANES, 8, LANES)),
                  whole((8, 3 * D_GDN)), whole((8, LANES)), whole((npair, 2, 8, LANES)), whole((1, PAIR))],
        out_specs=[per_batch((l, D_GDN)), per_batch((lc, D_GDN))],
        out_shape=[jax.ShapeDtypeStruct((b, l, D_GDN), BF16), jax.ShapeDtypeStruct((b, lc, D_GDN), BF16)],
        scratch_shapes=[pltpu.VMEM((lt, D_GDN), BF16)] * 3 + [pltpu.VMEM((lt, D_GDN), F32)]
        + [pltpu.VMEM((2 * npair, PAIR, PAIR), F32), pltpu.VMEM(sq, BF16), pltpu.VMEM(sq, F32),
           pltpu.VMEM((nstage, PAIR, 2 * PAIR), BF16), pltpu.VMEM(sq, F32),
           pltpu.VMEM((nstage, 2 * PAIR, PAIR), BF16), pltpu.VMEM((nstage, 2 * PAIR, PAIR), BF16),
           pltpu.VMEM((nstage, 8, PAIR), F32)],
        compiler_params=_cparams(("arbitrary",), VMEM_LIMIT),
        name="gated_deltanet",
    )(gdn_l, gdn_c, jnp.concatenate([ab_c, ab_l], axis=1), jnp.concatenate([abt_c, abt_l], axis=2), cw, cst, cstt, ng)


def _na_bias_tiles(rpb, rows):
    w = GRID_W
    ext = jnp.pad(rpb.astype(F32) * float(np.log2(np.e)), ((0, 0), (0, 0), (w - NA_KW, w - NA_KW)))
    cq = np.arange(w)[:, None]
    ck = np.arange(w)[None, :]
    tb = ext[:, :, jnp.asarray(ck - cq + w - 1)]
    ws = np.clip(cq - NA_KW // 2, 0, w - NA_KW)
    col_ok = (ck >= ws) & (ck < ws + NA_KW)
    tb = jnp.where(jnp.asarray(col_ok)[None, None], tb, NEG_INF)
    none = 2 * NA_KH - 1
    tb = jnp.concatenate([tb, jnp.full((tb.shape[0], 1, w, w), NEG_INF, F32)], axis=1)
    nt = rows // NA_QROWS
    idx = np.zeros((3, NA_QROWS, NA_KROWS), np.int32)
    for cls, t in enumerate((0, 1, nt - 1)):
        ks = int(np.clip(NA_QROWS * t - NA_KH // 2, 0, rows - NA_KROWS))
        for rl in range(NA_QROWS):
            r = NA_QROWS * t + rl
            r0 = int(np.clip(r - NA_KH // 2, 0, rows - NA_KH))
            for j in range(NA_KROWS):
                kr = ks + j
                idx[cls, rl, j] = kr - r + NA_KH - 1 if r0 <= kr < r0 + NA_KH else none
    pairs = idx.reshape(3, NA_QROWS, NA_KROWS // 2, 2)
    uniq, inv = np.unique(pairs.reshape(-1, 2), axis=0, return_inverse=True)
    wide = jnp.concatenate([tb[:, jnp.asarray(uniq[:, 0])], tb[:, jnp.asarray(uniq[:, 1])]], axis=-1)
    tiles = wide[:, jnp.asarray(inv.reshape(3, NA_QROWS, NA_KROWS // 2))]
    tiles = jnp.transpose(tiles, (1, 0, 3, 2, 4, 5))
    return tiles.reshape(3, tb.shape[0], NA_KROWS // 2, NA_QROWS * w, 2 * w)


def _softmax_pv(s_parts, v_parts):
    m = functools.reduce(jnp.maximum, [jnp.max(s, axis=-1, keepdims=True) for s in s_parts])
    ps = [jnp.exp2(s - m) for s in s_parts]
    den = functools.reduce(lambda x, y: x + y, [jnp.sum(p, axis=-1, keepdims=True) for p in ps])
    o = _dot(jnp.concatenate([p.astype(BF16) for p in ps], axis=1), jnp.concatenate(v_parts, axis=0))
    return o / den


def _na_body(q_ref, k_ref, v_ref, qc_ref, kc_ref, vc_ref, bias_ref, o_ref, oc_ref, *, rows):
    w = GRID_W
    tq = NA_QROWS * w
    tk = NA_KROWS * w
    nt = rows // NA_QROWS
    m0, m1 = _head_masks((tq, PAIR))
    masks = (m0, m1)

    def one_tile(t):
        cls = jnp.where(t > 0, 1, 0) + jnp.where(t == nt - 1, 1, 0)
        ks = jnp.clip(NA_QROWS * t - NA_KH // 2, 0, rows - NA_KROWS) * w
        ksl = pl.ds(pl.multiple_of(ks, w), tk)
        qsl = pl.ds(pl.multiple_of(t * tq, tq), tq)
        for p in range(NA_PAIRS):
            lanes = slice(p * PAIR, (p + 1) * PAIR)
            q = q_ref[0, qsl, lanes]
            kb = k_ref[0, ksl, lanes]
            vb = v_ref[0, ksl, lanes]
            kc = kc_ref[0, :, lanes]
            vc = vc_ref[0, :, lanes]
            s_all = _dot_nt(_stack_heads(q), jnp.concatenate([kb, kc], axis=0))
            v_all = jnp.concatenate([vb, vc], axis=0)
            acc = jnp.zeros((tq, PAIR), F32)
            for hh in range(2):
                bias = jnp.concatenate([bias_ref[0, cls, 2 * p + hh, s] for s in range(NA_KROWS // 2)]
                                       + [jnp.zeros((tq, kc.shape[0]), F32)], axis=1)
                o_h = _softmax_pv([s_all[hh * tq:(hh + 1) * tq] + bias], [v_all])
                acc = acc + jnp.where(masks[hh], o_h, 0.0)
            o_ref[0, qsl, lanes] = acc.astype(o_ref.dtype)

    unroll = min(NA_UNROLL, nt)

    def tiles(i, _):
        for u in range(unroll):
            one_tile(i * unroll + u)
        return 0

    lax.fori_loop(0, nt // unroll, tiles, 0)
    for p in range(NA_PAIRS):
        lanes = slice(p * PAIR, (p + 1) * PAIR)
        qc = qc_ref[0, :, lanes]
        kc = kc_ref[0, :, lanes]
        vc = vc_ref[0, :, lanes]
        mc0, mc1 = _head_masks(qc.shape)
        accc = jnp.zeros(qc.shape, F32)
        for mk in (mc0, mc1):
            qm = jnp.where(mk, qc, jnp.zeros_like(qc))
            accc = accc + jnp.where(mk, _softmax_pv([_dot_nt(qm, kc)], [vc]), 0.0)
        oc_ref[0, :, lanes] = accc.astype(oc_ref.dtype)


def _neighbourhood_attention(na_l, na_c, bias, layer):
    b, l, _ = na_l.shape
    lc = na_c.shape[1]
    rows = l // GRID_W
    wide = NA_PAIRS * PAIR
    ngrp = D_NA // wide

    def col(k):
        return lambda p, i: (i, 0, k * ngrp + p)

    lat = [pl.BlockSpec((1, l, wide), col(k)) for k in range(3)]
    ctx = [pl.BlockSpec((1, lc, wide), col(k)) for k in range(3)]
    return pl.pallas_call(
        functools.partial(_na_body, rows=rows),
        grid=(ngrp, b),
        in_specs=lat + ctx + [pl.BlockSpec((1, 3, 2 * NA_PAIRS, NA_KROWS // 2, NA_QROWS * GRID_W, 2 * GRID_W),
                                           lambda p, i: (layer, 0, p, 0, 0, 0))],
        out_specs=[pl.BlockSpec((1, l, wide), lambda p, i: (i, 0, p)),
                   pl.BlockSpec((1, lc, wide), lambda p, i: (i, 0, p))],
        out_shape=[jax.ShapeDtypeStruct((b, l, D_NA), BF16), jax.ShapeDtypeStruct((b, lc, D_NA), BF16)],
        compiler_params=_cparams(("arbitrary", "arbitrary"), VMEM_LIMIT),
        name="neighbourhood_attention",
    )(na_l, na_l, na_l, na_c, na_c, na_c, bias)


def _row_parts(tm):
    return [slice(i * (tm // ROW_PARTS), (i + 1) * (tm // ROW_PARTS)) for i in range(ROW_PARTS)]


def _outproj_residual(r_ref, g_ref, n_ref, x_ref, mod_ref, wo_ref, rows):
    mix = jnp.concatenate([r_ref[0, rows, :], g_ref[0, rows, :], n_ref[0, rows, :]], axis=1)
    return x_ref[0, rows, :] + mod_ref[0, 2:3, :] * _dot(mix, wo_ref[...])


def _swiglu_chunk(hb, w1_ref, w3_ref, w2_ref, j):
    cols = slice(j * FF_CHUNK, (j + 1) * FF_CHUNK)
    t = (_silu(_dot(hb, w1_ref[:, cols])) * _dot(hb, w3_ref[:, cols])).astype(BF16)
    return _dot(t, w2_ref[cols, :])


def _swiglu_chunks(hb, w1_ref, w3_ref, w2_ref, acc_ref):
    for j in range(w2_ref.shape[0] // FF_CHUNK):
        if j == 0:
            acc_ref[...] = _swiglu_chunk(hb, w1_ref, w3_ref, w2_ref, j)
        else:
            acc_ref[...] += _swiglu_chunk(hb, w1_ref, w3_ref, w2_ref, j)


def _dense_body(r_ref, g_ref, n_ref, x_ref, mod_ref, g2_ref, wo_ref, w1_ref, w3_ref, w2_ref, o_ref, acc_ref):
    x1s, hbs = [], []
    for rows in _row_parts(x_ref.shape[1]):
        x1s.append(_outproj_residual(r_ref, g_ref, n_ref, x_ref, mod_ref, wo_ref, rows))
        hbs.append(_norm_mod(x1s[-1], g2_ref[...], mod_ref[0, 3:4, :], mod_ref[0, 4:5, :]).astype(BF16))
    _swiglu_chunks(jnp.concatenate(hbs, axis=0), w1_ref, w3_ref, w2_ref, acc_ref)
    o_ref[0] = jnp.concatenate(x1s, axis=0) + mod_ref[0, 5:6, :] * acc_ref[...]


def _pack_bf16_pairs(h):
    m = h.shape[1] // 2
    bits = lax.bitcast_convert_type(h.astype(BF16).astype(F32), jnp.uint32)
    return (bits[:, :m] >> 16) | (bits[:, m:] & jnp.uint32(0xFFFF0000))


def _unpack_bf16_pairs(u):
    lo = lax.bitcast_convert_type(u << 16, F32)
    hi = lax.bitcast_convert_type(u & jnp.uint32(0xFFFF0000), F32)
    return jnp.concatenate([lo, hi], axis=1).astype(BF16)


def _router_body(r_ref, g_ref, n_ref, x_ref, mod_ref, g2_ref, wo_ref, wr_ref, x1_ref, h_ref, lg_ref):
    for rows in _row_parts(x_ref.shape[1]):
        x1 = _outproj_residual(r_ref, g_ref, n_ref, x_ref, mod_ref, wo_ref, rows)
        h = _norm_mod(x1, g2_ref[...], mod_ref[0, 3:4, :], mod_ref[0, 4:5, :])
        x1_ref[0, rows, :] = x1
        h_ref[0, rows, :] = _pack_bf16_pairs(h)
        lg_ref[0, rows, :] = _dot_3pass(h, wr_ref[...])


def _resident(shape):
    nd = len(shape)
    return pl.BlockSpec(shape, lambda i, j: (0,) * nd, pipeline_mode=pl.Buffered(1))


def _mixer_specs(tm, d):
    return [pl.BlockSpec((1, tm, D_RET), lambda i, j: (i, j, 0)),
            pl.BlockSpec((1, tm, D_GDN), lambda i, j: (i, j, 0)),
            pl.BlockSpec((1, tm, D_NA), lambda i, j: (i, j, 0)),
            pl.BlockSpec((1, tm, d), lambda i, j: (i, j, 0)),
            pl.BlockSpec((1, 6, d), lambda i, j: (i, 0, 0)),
            pl.BlockSpec((1, d), lambda i, j: (0, 0))]


def _dense_block(r, g, n, x, mod, g2, wo, w1, w3, w2):
    b, l, d = x.shape
    tm = min(ROW_TILE, l)
    dff = w1.shape[1]
    return pl.pallas_call(
        _dense_body,
        grid=(b, l // tm),
        in_specs=_mixer_specs(tm, d) + [_resident((d, d)), _resident((d, dff)), _resident((d, dff)), _resident((dff, d))],
        out_specs=pl.BlockSpec((1, tm, d), lambda i, j: (i, j, 0)),
        out_shape=jax.ShapeDtypeStruct((b, l, d), F32),
        scratch_shapes=[pltpu.VMEM((tm, d), F32)],
        compiler_params=_cparams(("arbitrary", "arbitrary"), VMEM_LIMIT),
        name="outproj_swiglu",
    )(r, g, n, x, mod, g2, wo, w1, w3, w2)


def _router_block(r, g, n, x, mod, g2, wo, wr):
    b, l, d = x.shape
    tm = min(ROW_TILE, l)
    blk = pl.BlockSpec((1, tm, d), lambda i, j: (i, j, 0))
    return pl.pallas_call(
        _router_body,
        grid=(b, l // tm),
        in_specs=_mixer_specs(tm, d) + [_resident((d, d)), _resident((d, LANES))],
        out_specs=[blk, pl.BlockSpec((1, tm, d // 2), lambda i, j: (i, j, 0)), pl.BlockSpec((1, tm, LANES), lambda i, j: (i, j, 0))],
        out_shape=[jax.ShapeDtypeStruct((b, l, d), F32), jax.ShapeDtypeStruct((b, l, d // 2), jnp.uint32),
                   jax.ShapeDtypeStruct((b, l, LANES), F32)],
        compiler_params=_cparams(("arbitrary", "arbitrary"), VMEM_LIMIT),
        name="outproj_router",
    )(r, g, n, x, mod, g2, wo, wr)


def _row_copy_in(idx_ref, r, src_hbm, dst_ref, sem):
    return pltpu.make_async_copy(src_hbm.at[pl.ds(idx_ref[0, 0, r], 1), :], dst_ref.at[pl.ds(r, 1), :], sem)


def _row_copy_out(idx_ref, r, src_ref, dst_hbm, sem):
    return pltpu.make_async_copy(src_ref.at[pl.ds(r, 1), :], dst_hbm.at[pl.ds(idx_ref[0, 0, r], 1), :], sem)


def _moe_body(bval_ref, bexp_ref, tok_ref, tokn_ref, slotp_ref, h_hbm, w1_ref, w3_ref, w2_ref, y_hbm,
              xbuf, ybuf, gsem, ssem, *, n_asg):
    i = pl.program_id(0)
    tb = xbuf.shape[1]
    nch = w2_ref.shape[1] // FF_CHUNK
    cur = i % 2
    per = -(-tb // nch)
    valid = bval_ref[i] == 1
    prev_valid = bval_ref[jnp.maximum(i - 1, 0)] == 1

    def wait_gather(slot):
        pltpu.make_async_copy(h_hbm.at[pl.ds(0, tb), :], xbuf.at[slot], gsem).wait()

    def wait_scatter(slot):
        pltpu.make_async_copy(ybuf.at[slot], y_hbm.at[pl.ds(0, tb), :], ssem).wait()

    @pl.when(i == 0)
    def _():
        ybuf[1] = jnp.zeros((tb, ybuf.shape[2]), F32)
        fills = [pltpu.make_async_copy(ybuf.at[1], y_hbm.at[pl.ds(r0, tb), :], ssem)
                 for r0 in range(n_asg, y_hbm.shape[0], tb)]
        for f in fills:
            f.start()
        for f in fills:
            f.wait()

        def issue(r, _):
            _row_copy_in(tok_ref, r, h_hbm, xbuf.at[0], gsem).start()
            return 0

        lax.fori_loop(0, tb, issue, 0)

    @pl.when(valid)
    def _():
        wait_gather(cur)
        hb = _unpack_bf16_pairs(xbuf[cur])
        acc = ybuf.at[cur]
        prv = ybuf.at[1 - cur]
        nxt = xbuf.at[1 - cur]
        for j in range(nch):
            for r in range(j * per, min(tb, (j + 1) * per)):
                _row_copy_in(tokn_ref, r, h_hbm, nxt, gsem).start()
                _row_copy_out(slotp_ref, r, prv, y_hbm, ssem).start()
            part = _swiglu_chunk(hb, w1_ref.at[0], w3_ref.at[0], w2_ref.at[0], j)
            if j == 0:
                acc[...] = part
            else:
                acc[...] += part
        wait_scatter(1 - cur)

    @pl.when(jnp.logical_not(valid) & prev_valid & (i > 0))
    def _():
        wait_gather(cur)

        def issue(r, _):
            _row_copy_out(slotp_ref, r, ybuf.at[1 - cur], y_hbm, ssem).start()
            return 0

        lax.fori_loop(0, tb, issue, 0)
        wait_scatter(1 - cur)


def _moe_experts(h_rows, row_tok, row_slot, block_expert, block_valid, w1, w3, w2, n_out_rows, n_spare):
    d = w2.shape[-1]
    dff = w2.shape[1]
    nb = block_expert.shape[0]
    tb = MOE_ROWS
    smem = functools.partial(pl.BlockSpec, (1, 1, tb), memory_space=pltpu.SMEM)
    grid_spec = pltpu.PrefetchScalarGridSpec(
        num_scalar_prefetch=2,
        grid=(nb,),
        in_specs=[smem(lambda i, bv, be: (i, 0, 0)),
                  smem(lambda i, bv, be: (jnp.minimum(i + 1, nb - 1), 0, 0)),
                  smem(lambda i, bv, be: (i, 0, 0)),
                  pl.BlockSpec(memory_space=pl.ANY),
                  pl.BlockSpec((1, d, dff), lambda i, bv, be: (be[i], 0, 0)),
                  pl.BlockSpec((1, d, dff), lambda i, bv, be: (be[i], 0, 0)),
                  pl.BlockSpec((1, dff, d), lambda i, bv, be: (be[i], 0, 0))],
        out_specs=pl.BlockSpec(memory_space=pl.ANY),
        scratch_shapes=[pltpu.VMEM((2, tb, d // 2), jnp.uint32), pltpu.VMEM((2, tb, d), F32),
                        pltpu.SemaphoreType.DMA(()), pltpu.SemaphoreType.DMA(())],
    )
    return pl.pallas_call(
        functools.partial(_moe_body, n_asg=n_out_rows - n_spare),
        grid_spec=grid_spec,
        out_shape=jax.ShapeDtypeStruct((n_out_rows, d), F32),
        compiler_params=_cparams(("arbitrary",), VMEM_LIMIT),
        name="moe_experts",
    )(block_valid, block_expert, row_tok.reshape(nb, 1, tb), row_tok.reshape(nb, 1, tb), row_slot.reshape(nb + 1, 1, tb),
      h_rows, w1, w3, w2)


def _combine_body(y0_ref, y1_ref, x1_ref, gate_ref, gm_ref, fg_ref, o_ref, *, final):
    moe = gate_ref[:, 0:1] * y0_ref[...] + gate_ref[:, 1:2] * y1_ref[...]
    x2 = x1_ref[...] + gm_ref[0] * moe
    if final:
        ms = jnp.mean(x2 * x2, axis=-1, keepdims=True)
        x2 = x2 * lax.rsqrt(ms + EPS) * fg_ref[...]
    o_ref[...] = x2


def _moe_combine(y_rows, x1, gates, gate_mlp, final_g, final):
    b, l, d = x1.shape
    n_tok = b * l
    tm = min(ROW_TILE // 2, l)
    per_b = l // tm
    out = pl.pallas_call(
        functools.partial(_combine_body, final=final),
        grid=(n_tok // tm,),
        in_specs=[pl.BlockSpec((tm, d), lambda i: (i, 0)),
                  pl.BlockSpec((tm, d), lambda i: (n_tok // tm + i, 0)),
                  pl.BlockSpec((tm, d), lambda i: (i, 0)),
                  pl.BlockSpec((tm, 2), lambda i: (i, 0)),
                  pl.BlockSpec((1, 1, d), lambda i: (i // per_b, 0, 0)),
                  pl.BlockSpec((1, d), lambda i: (0, 0))],
        out_specs=pl.BlockSpec((tm, d), lambda i: (i, 0)),
        out_shape=jax.ShapeDtypeStruct((n_tok, d), F32),
        compiler_params=_cparams(("arbitrary",), VMEM_LIMIT),
        name="moe_combine_final_norm",
    )(y_rows, y_rows, x1.reshape(n_tok, d), gates, gate_mlp, final_g.reshape(1, d))
    return out.reshape(b, l, d)


def _route(logits):
    n_tok = logits.shape[0]
    tb = MOE_ROWS
    n_asg = n_tok * TOP_K
    top_logit, top_e = lax.top_k(logits[:, :N_EXPERTS], TOP_K)
    gates = jax.nn.softmax(top_logit, axis=-1)
    e_flat = top_e.reshape(-1).astype(jnp.int32)
    asg = jnp.arange(n_asg, dtype=jnp.int32)
    by_expert = lax.sort(e_flat * n_asg + asg)
    counts = jnp.sum((e_flat[:, None] == jnp.arange(N_EXPERTS, dtype=jnp.int32)[None, :]).astype(jnp.int32), axis=0)
    starts = jnp.cumsum(counts) - counts
    padded = (counts + tb - 1) // tb * tb
    pad_ends = jnp.cumsum(padded)
    pad_starts = pad_ends - padded
    nb = (n_asg + tb - 1) // tb + N_EXPERTS + 1
    n_rows = nb * tb
    block_start = jnp.arange(nb, dtype=jnp.int32) * tb
    block_expert = jnp.minimum(jnp.sum((block_start[:, None] >= pad_ends[None, :]).astype(jnp.int32), axis=1), N_EXPERTS - 1)
    block_valid = (block_start < pad_ends[-1]).astype(jnp.int32)
    row = jnp.arange(n_rows, dtype=jnp.int32)
    row_e = jnp.repeat(block_expert, tb)
    off = row - pad_starts[row_e]
    is_pad = (off >= counts[row_e]) | (jnp.repeat(block_valid, tb) == 0)
    src = by_expert[jnp.clip(starts[row_e] + off, 0, n_asg - 1)] - row_e * n_asg
    row_asg = jnp.where(is_pad, -1, src)
    spare = tb + n_asg + jnp.cumsum(is_pad.astype(jnp.int32)) - 1
    row_slot = jnp.where(is_pad, spare, (row_asg % TOP_K) * n_tok + row_asg // TOP_K)
    row_tok = jnp.where(is_pad, 0, row_asg // TOP_K)
    row_slot = jnp.concatenate([n_asg + jnp.arange(tb, dtype=jnp.int32), row_slot])
    last_e = block_expert[jnp.maximum(pad_ends[-1] // tb - 1, 0)]
    block_expert = jnp.where(block_valid == 1, block_expert, last_e)
    n_spare = (tb + n_rows - n_asg + tb - 1) // tb * tb
    return row_tok, row_slot, gates, block_expert, block_valid, n_asg + n_spare, n_spare


def _final_norm_body(x_ref, g_ref, o_ref):
    x = x_ref[...]
    ms = jnp.mean(x * x, axis=-1, keepdims=True)
    o_ref[...] = x * lax.rsqrt(ms + EPS) * g_ref[...]


def _final_norm(x, g):
    b, l, d = x.shape
    n = b * l
    tm = min(ROW_TILE, n)
    out = pl.pallas_call(
        _final_norm_body,
        grid=(n // tm,),
        in_specs=[pl.BlockSpec((tm, d), lambda i: (i, 0)), pl.BlockSpec((1, d), lambda i: (0, 0))],
        out_specs=pl.BlockSpec((tm, d), lambda i: (i, 0)),
        out_shape=jax.ShapeDtypeStruct((n, d), F32),
        compiler_params=_cparams(("arbitrary",)),
        name="final_norm",
    )(x.reshape(n, d), g.reshape(1, d))
    return out.reshape(b, l, d)


def _rope_tables(n_tok):
    t = jnp.arange(n_tok, dtype=jnp.int32)
    row = (t // GRID_W).astype(F32)
    col = (t % GRID_W).astype(F32)
    inv_freq = ROPE_BASE ** (-jnp.arange(N_FREQ, dtype=F32) / N_FREQ)
    ang = jnp.concatenate([row[:, None] * inv_freq, col[:, None] * inv_freq], axis=-1)
    cos, sin = jnp.cos(ang), jnp.sin(ang)
    cosf = jnp.concatenate([cos, cos, cos, cos], axis=-1)
    sins = jnp.concatenate([-sin, sin, -sin, sin], axis=-1)
    return cosf, sins


def kernel(x, c, ctx, c_ctx, ada_w, ada_b, norm1_g, norm2_g, w_in, w_out, conv_w, ret_decay, gdn_a_log, gdn_dt_bias,
           gdn_norm_g, na_rpb, ffn_w1, ffn_w3, ffn_w2, moe_router, moe_w1, moe_w3, moe_w2, final_g):
    b, l, d = x.shape
    lc = ctx.shape[1]
    depth = ada_w.shape[0]
    cosf, sins = _rope_tables(l)
    ones_c = jnp.ones((lc, LANES), F32)
    zeros_c = jnp.zeros((lc, LANES), F32)

    rows = ((b + 1 + 7) // 8) * 8
    c_all = jnp.zeros((rows, d), F32).at[:b].set(c).at[b].set(c_ctx)
    mod = _ada_vectors(c_all, ada_w, ada_b).reshape(depth, rows, 6, d)

    w1p_all = jax.vmap(_pack_w_in)(w_in)
    wo_all = w_out.astype(BF16)
    log_gamma_all = jnp.log1p(-jnp.exp2(-ret_decay.astype(F32)))
    gdn_tables = jax.vmap(_gdn_tables)(conv_w, gdn_a_log, gdn_dt_bias, gdn_norm_g)
    na_bias = jax.vmap(lambda r: _na_bias_tiles(r, l // GRID_W))(na_rpb)

    y = ctx
    for layer in range(depth):
        need_ctx = layer < depth - 1
        mod_l = mod[layer, :b]
        mod_c = jnp.broadcast_to(mod[layer, b][None], (b, 6, d))
        w1p = w1p_all[layer]
        g1 = norm1_g[layer].reshape(1, d)
        ret_l, gdn_l, ab_l, abt_l, na_l = _in_projection(x, mod_l, g1, cosf, sins, w1p, rope=True)
        ret_c, gdn_c, ab_c, abt_c, na_c = _in_projection(y, mod_c, g1, ones_c, zeros_c, w1p, rope=False)

        r_l, r_c = _retention(ret_l, ret_c, log_gamma_all[layer])
        g_l, g_c = _gdn(gdn_l, ab_l, abt_l, gdn_c, ab_c, abt_c, [t[layer] for t in gdn_tables])
        n_l, n_c = _neighbourhood_attention(na_l, na_c, na_bias, layer)

        wo = wo_all[layer]
        g2 = norm2_g[layer].reshape(1, d)
        j = layer // 2
        if layer % 2 == 0:
            w1, w3, w2 = ffn_w1[j].astype(BF16), ffn_w3[j].astype(BF16), ffn_w2[j].astype(BF16)
            x = _dense_block(r_l, g_l, n_l, x, mod_l, g2, wo, w1, w3, w2)
            if need_ctx:
                y = _dense_block(r_c, g_c, n_c, y, mod_c, g2, wo, w1, w3, w2)
            if layer == depth - 1:
                x = _final_norm(x, final_g)
        else:
            wr = jnp.pad(moe_router[j].astype(F32), ((0, 0), (0, LANES - N_EXPERTS)))
            w1, w3, w2 = moe_w1[j].astype(BF16), moe_w3[j].astype(BF16), moe_w2[j].astype(BF16)

            def moe_ffn(r, g, n, xin, m, last):
                bb, ll, _ = xin.shape
                x1, h, logits = _router_block(r, g, n, xin, m, g2, wo, wr)
                row_tok, row_slot, gates, bexp, bval, n_out, n_spare = _route(logits.reshape(bb * ll, LANES))
                y_rows = _moe_experts(h.reshape(bb * ll, d // 2), row_tok, row_slot, bexp, bval, w1, w3, w2, n_out, n_spare)
                return _moe_combine(y_rows, x1, gates, m[:, 5:6, :], final_g, last)

            x = moe_ffn(r_l, g_l, n_l, x, mod_l, layer == depth - 1)
            if need_ctx:
                y = moe_ffn(r_c, g_c, n_c, y, mod_c, False)
    return x
```
